```python
import jax, jax.numpy as jnp
from jax import lax
import numpy as np

D_MODEL = 1024
BATCH = 16
SEQ = 2048
DEPTH = 2

HEAD_DIM = 64
N_HEADS = D_MODEL // HEAD_DIM
FOX_HEADS = N_HEADS // 2
DIL_HEADS = N_HEADS - FOX_HEADS
FOX_WIDTH = FOX_HEADS * HEAD_DIM
DIL_WIDTH = DIL_HEADS * HEAD_DIM
DIL_PATTERNS = ((128, 1), (512, 4), (2048, 16))
Q_BLOCK = 128
EVEN_IN = 3 * FOX_WIDTH + FOX_HEADS + 3 * DIL_WIDTH

NSA_HEADS = N_HEADS
NSA_GROUPS = 4
NSA_HPG = NSA_HEADS // NSA_GROUPS
NSA_KV = NSA_GROUPS * HEAD_DIM
NSA_CMP_LEN = 32
NSA_CMP_STRIDE = 16
NSA_CMP_HIDDEN = 4 * HEAD_DIM
NSA_SEL_LEN = 64
NSA_TOP_N = 8
NSA_WINDOW = 512
NSA_CHUNK = 32
ODD_IN = NSA_HEADS * HEAD_DIM + 6 * NSA_KV + 3 * NSA_HEADS

D_FF = 3584
N_EXPERTS = 8
TOP_K = 2
D_FF_EXPERT = 3584
PLE_DIM = 256
ROPE_THETA = 10000.0
RMS_EPS = 1e-6
NEG_INF = -1e30

kernel_name = 'hybrid_fox_dilated_nsa_moe_ple'


def rms_norm(x, g):
    xf = x.astype(jnp.float32)
    y = xf * lax.rsqrt(jnp.mean(xf * xf, axis=-1, keepdims=True) + RMS_EPS)
    return (y * g.astype(jnp.float32)).astype(x.dtype)


def to_heads(t, n):
    b, s, _ = t.shape
    return t.reshape(b, s, n, HEAD_DIM).transpose(0, 2, 1, 3)


def from_heads(t):
    b, n, s, d = t.shape
    return t.transpose(0, 2, 1, 3).reshape(b, s, n * d)


def rotary(t, positions):
    half = HEAD_DIM // 2
    inv_freq = ROPE_THETA ** (-jnp.arange(half, dtype=jnp.float32) / half)
    ang = positions.astype(jnp.float32)[:, None, :, None] * inv_freq
    cos, sin = jnp.cos(ang), jnp.sin(ang)
    tf = t.astype(jnp.float32)
    t1, t2 = tf[..., :half], tf[..., half:]
    return jnp.concatenate([t1 * cos - t2 * sin, t2 * cos + t1 * sin], axis=-1).astype(t.dtype)


def swiglu(x, w_gate, w_up, w_down):
    return (jax.nn.silu(x @ w_gate) * (x @ w_up)) @ w_down


def forgetting_attention(q, k, v, log_f):
    b, h, s, d = q.shape
    c = jnp.cumsum(log_f, axis=-1)
    nb = s // Q_BLOCK
    qb = q.reshape(b, h, nb, Q_BLOCK, d).transpose(2, 0, 1, 3, 4)
    cb = c.reshape(b, h, nb, Q_BLOCK).transpose(2, 0, 1, 3)
    kpos = jnp.arange(s)
    scale = d ** -0.5

    def block(args):
        qi, ci, bi = args
        sc = jnp.einsum('bhqd,bhkd->bhqk', qi, k).astype(jnp.float32) * scale
        sc = sc + ci[..., None] - c[:, :, None, :]
        qpos = bi * Q_BLOCK + jnp.arange(Q_BLOCK)
        sc = jnp.where(kpos[None, :] <= qpos[:, None], sc, NEG_INF)
        pr = jax.nn.softmax(sc, axis=-1).astype(v.dtype)
        return jnp.einsum('bhqk,bhkd->bhqd', pr, v)

    o = lax.map(block, (qb, cb, jnp.arange(nb)))
    return o.transpose(1, 2, 0, 3, 4).reshape(b, h, s, d)


def dilated_branch(q, k, v, window, dilation):
    b, h, s, d = q.shape
    span = window // dilation
    n = s // dilation
    nb = -(-n // Q_BLOCK)
    pad = nb * Q_BLOCK - n

    def residues(t):
        return t.reshape(b, h, n, dilation, d).transpose(0, 1, 3, 2, 4)

    def key_blocks(t):
        tp = jnp.pad(residues(t), ((0, 0), (0, 0), (0, 0), (Q_BLOCK, pad), (0, 0)))
        prev = tp[:, :, :, :nb * Q_BLOCK].reshape(b, h, dilation, nb, Q_BLOCK, d)
        cur = tp[:, :, :, Q_BLOCK:].reshape(b, h, dilation, nb, Q_BLOCK, d)
        return jnp.concatenate([prev, cur], axis=4)

    qb = jnp.pad(residues(q), ((0, 0), (0, 0), (0, 0), (0, pad), (0, 0))).reshape(b, h, dilation, nb, Q_BLOCK, d)
    kb, vb = key_blocks(k), key_blocks(v)
    qi = jnp.arange(Q_BLOCK)[:, None]
    kj = jnp.arange(2 * Q_BLOCK)[None, :]
    dist = qi + Q_BLOCK - kj
    band = (dist >= 0) & (dist <= span)
    after_start = (jnp.arange(nb)[:, None, None] > 0) | (kj >= Q_BLOCK)[None]
    mask = band[None] & after_start
    sc = jnp.einsum('bhrnqd,bhrnkd->bhrnqk', qb, kb).astype(jnp.float32) * (d ** -0.5)
    sc = jnp.where(mask, sc, NEG_INF)
    m = jnp.max(sc, axis=-1, keepdims=True)
    e = jnp.exp(sc - m)
    den = jnp.sum(e, axis=-1, keepdims=True)
    o = jnp.einsum('bhrnqk,bhrnkd->bhrnqd', (e / den).astype(v.dtype), vb)
    lse = (m + jnp.log(den))[..., 0]
    o = o.reshape(b, h, dilation, nb * Q_BLOCK, d)[:, :, :, :n].transpose(0, 1, 3, 2, 4).reshape(b, h, s, d)
    lse = lse.reshape(b, h, dilation, nb * Q_BLOCK)[..., :n].transpose(0, 1, 3, 2).reshape(b, h, s)
    return o, lse


def dilated_attention(q, k, v):
    outs, lses = [], []
    for window, dilation in DIL_PATTERNS:
        o, lse = dilated_branch(q, k, v, window, dilation)
        outs.append(o)
        lses.append(lse)
    w = jax.nn.softmax(jnp.stack(lses), axis=0)
    return jnp.einsum('pbhs,pbhsd->bhsd', w.astype(q.dtype), jnp.stack(outs))


def fox_dilated_mixer(hn, positions, w_in, forget_b, w_out):
    proj = hn @ w_in
    fw, dw = FOX_WIDTH, DIL_WIDTH
    cuts = [fw, 2 * fw, 3 * fw, 3 * fw + dw, 3 * fw + 2 * dw, 3 * fw + 3 * dw]
    fq, fk, fv, dq, dk, dv, f_logit = jnp.split(proj, cuts, axis=-1)
    log_f = jax.nn.log_sigmoid(f_logit.astype(jnp.float32) + forget_b.astype(jnp.float32)).transpose(0, 2, 1)
    o_fox = forgetting_attention(to_heads(fq, FOX_HEADS), to_heads(fk, FOX_HEADS), to_heads(fv, FOX_HEADS), log_f)
    o_dil = dilated_attention(rotary(to_heads(dq, DIL_HEADS), positions),
                              rotary(to_heads(dk, DIL_HEADS), positions),
                              to_heads(dv, DIL_HEADS))
    return jnp.concatenate([from_heads(o_fox), from_heads(o_dil)], axis=-1) @ w_out


def compress_blocks(t, pos, w1, b1, w2):
    b, g, s, d = t.shape
    nc = (s - NSA_CMP_LEN) // NSA_CMP_STRIDE + 1
    idx = jnp.arange(nc)[:, None] * NSA_CMP_STRIDE + jnp.arange(NSA_CMP_LEN)[None, :]
    blk = (t[:, :, idx, :] + pos).reshape(b, g, nc, NSA_CMP_LEN * d)
    return jax.nn.gelu(blk @ w1 + b1) @ w2


def native_sparse_attention(q, kc, vc, ks, vs, kw, vw, gates):
    b, h, s, d = q.shape
    g = NSA_GROUPS
    nc = kc.shape[2]
    ns = s // NSA_SEL_LEN
    n_sel = min(NSA_TOP_N, ns)
    n_ch = s // NSA_CHUNK
    scale = d ** -0.5
    cmp_start = jnp.arange(nc) * NSA_CMP_STRIDE
    cmp_end = cmp_start + NSA_CMP_LEN - 1
    sel_ids = jnp.arange(ns)
    overlap = ((cmp_start[:, None] < (sel_ids[None, :] + 1) * NSA_SEL_LEN)
               & (cmp_start[:, None] + NSA_CMP_LEN > sel_ids[None, :] * NSA_SEL_LEN)).astype(jnp.float32)
    ks_blk = ks.reshape(b, g, ns, NSA_SEL_LEN, d)
    vs_blk = vs.reshape(b, g, ns, NSA_SEL_LEN, d)
    kw_pad = jnp.pad(kw, ((0, 0), (0, 0), (NSA_WINDOW, 0), (0, 0)))
    vw_pad = jnp.pad(vw, ((0, 0), (0, 0), (NSA_WINDOW, 0), (0, 0)))
    qc = q.reshape(b, g, NSA_HPG, n_ch, NSA_CHUNK, d).transpose(3, 0, 1, 2, 4, 5)
    bi = jnp.arange(b)[:, None, None, None]
    gi = jnp.arange(g)[None, :, None, None]
    n_key_sel = n_sel * NSA_SEL_LEN

    def chunk(args):
        qi, ci = args
        tq = ci * NSA_CHUNK + jnp.arange(NSA_CHUNK)
        sc = jnp.einsum('bghqd,bgnd->bghqn', qi, kc).astype(jnp.float32) * scale
        valid_c = cmp_end[None, :] <= tq[:, None]
        pc = jax.nn.softmax(jnp.where(valid_c, sc, NEG_INF), axis=-1) * jnp.any(valid_c, axis=-1)[:, None]
        o_c = jnp.einsum('bghqn,bgnd->bghqd', pc.astype(vc.dtype), vc)
        imp = jnp.einsum('bghqn,nj->bgqj', pc, overlap)
        cur = tq // NSA_SEL_LEN
        forced = (sel_ids[None, :] == 0) | (sel_ids[None, :] == cur[:, None]) | (sel_ids[None, :] == cur[:, None] - 1)
        imp = jnp.where(sel_ids[None, :] > cur[:, None], -1.0, jnp.where(forced, 1e6, imp))
        _, top = lax.top_k(imp, n_sel)
        kg = ks_blk[bi, gi, top].reshape(b, g, NSA_CHUNK, n_key_sel, d)
        vg = vs_blk[bi, gi, top].reshape(b, g, NSA_CHUNK, n_key_sel, d)
        kpos = top[..., None] * NSA_SEL_LEN + jnp.arange(NSA_SEL_LEN)
        valid_s = (kpos <= tq[:, None, None]).reshape(b, g, 1, NSA_CHUNK, n_key_sel)
        ss = jnp.einsum('bghqd,bgqkd->bghqk', qi, kg).astype(jnp.float32) * scale
        ps = jax.nn.softmax(jnp.where(valid_s, ss, NEG_INF), axis=-1)
        o_s = jnp.einsum('bghqk,bgqkd->bghqd', ps.astype(vs.dtype), vg)
        start = ci * NSA_CHUNK
        kwin = lax.dynamic_slice_in_dim(kw_pad, start, NSA_WINDOW + NSA_CHUNK, axis=2)
        vwin = lax.dynamic_slice_in_dim(vw_pad, start, NSA_WINDOW + NSA_CHUNK, axis=2)
        wpos = start - NSA_WINDOW + jnp.arange(NSA_WINDOW + NSA_CHUNK)
        delta = tq[:, None] - wpos[None, :]
        valid_w = (delta >= 0) & (delta < NSA_WINDOW) & (wpos[None, :] >= 0)
        sw = jnp.einsum('bghqd,bgkd->bghqk', qi, kwin).astype(jnp.float32) * scale
        pw = jax.nn.softmax(jnp.where(valid_w, sw, NEG_INF), axis=-1)
        o_w = jnp.einsum('bghqk,bgkd->bghqd', pw.astype(vw.dtype), vwin)
        return jnp.stack([o_c, o_s, o_w], axis=-1)

    o = lax.map(chunk, (qc, jnp.arange(n_ch)))
    o = o.transpose(1, 0, 4, 2, 3, 5, 6).reshape(b, s, h, d, 3)
    return jnp.einsum('bshdc,bshc->bshd', o, gates).reshape(b, s, h * d)


def nsa_mixer(hn, positions, w_in, pos_k, w1_k, b1_k, w2_k, pos_v, w1_v, b1_v, w2_v, w_out):
    b, s, _ = hn.shape
    proj = hn @ w_in
    qw = NSA_HEADS * HEAD_DIM
    cuts = [qw + i * NSA_KV for i in range(7)]
    q, kc, vc, ks, vs, kw, vw, g = jnp.split(proj, cuts, axis=-1)
    q = rotary(to_heads(q, NSA_HEADS), positions)
    kc = rotary(to_heads(kc, NSA_GROUPS), positions)
    ks = rotary(to_heads(ks, NSA_GROUPS), positions)
    kw = rotary(to_heads(kw, NSA_GROUPS), positions)
    kc_cmp = compress_blocks(kc, pos_k, w1_k, b1_k, w2_k)
    vc_cmp = compress_blocks(to_heads(vc, NSA_GROUPS), pos_v, w1_v, b1_v, w2_v)
    gates = jax.nn.sigmoid(g.reshape(b, s, NSA_HEADS, 3))
    o = native_sparse_attention(q, kc_cmp, vc_cmp, ks, to_heads(vs, NSA_GROUPS), kw, to_heads(vw, NSA_GROUPS), gates)
    return o @ w_out


def moe_swiglu(x, w_router, b_router, w_gate, w_up, w_down):
    b, s, dm = x.shape
    xt = x.reshape(b * s, dm)
    logits = (xt @ w_router).astype(jnp.float32) + b_router.astype(jnp.float32)
    top_val, top_idx = lax.top_k(logits, TOP_K)
    top_w = jax.nn.softmax(top_val, axis=-1)
    combine = jnp.sum(jax.nn.one_hot(top_idx, N_EXPERTS, dtype=jnp.float32) * top_w[..., None], axis=1)
    y = jnp.zeros_like(xt)
    for e in range(N_EXPERTS):
        y = y + combine[:, e:e + 1].astype(x.dtype) * swiglu(xt, w_gate[e], w_up[e], w_down[e])
    return y.reshape(b, s, dm)


def setup_inputs(seed: int = 0) -> dict:
    key = jax.random.key(seed)
    keys = jax.random.split(key, 40)
    counter = [0]
    n_even = (DEPTH + 1) // 2
    n_odd = DEPTH // 2

    def next_key():
        k = keys[counter[0]]
        counter[0] += 1
        return k

    def normal(shape, scale):
        return jax.random.normal(next_key(), shape, jnp.float32) * scale

    def gain(shape):
        return 1.0 + normal(shape, 0.05)

    cmp_in = NSA_CMP_LEN * HEAD_DIM
    x = normal((BATCH, SEQ, D_MODEL), 1.0)
    p = normal((DEPTH, BATCH, SEQ, PLE_DIM), 1.0)
    offset = jax.random.randint(next_key(), (BATCH,), 0, 4096, dtype=jnp.int32)
    positions = offset[:, None] + jnp.arange(SEQ, dtype=jnp.int32)[None, :]
    return {
        'x': x,
        'p': p,
        'positions': positions,
        'mix_norm': gain((DEPTH, D_MODEL)),
        'ffn_norm': gain((DEPTH, D_MODEL)),
        'ple_norm': gain((DEPTH, D_MODEL)),
        'ple_gate_w': normal((DEPTH, D_MODEL, D_MODEL), D_MODEL ** -0.5),
        'ple_proj_w': normal((DEPTH, PLE_DIM, D_MODEL), 0.5 * PLE_DIM ** -0.5),
        'fd_w_in': normal((n_even, D_MODEL, EVEN_IN), D_MODEL ** -0.5),
        'fd_forget_b': 3.0 + normal((n_even, FOX_HEADS), 0.5),
        'fd_w_out': normal((n_even, D_MODEL, D_MODEL), D_MODEL ** -0.5),
        'dense_w_gate': normal((n_even, D_MODEL, D_FF), D_MODEL ** -0.5),
        'dense_w_up': normal((n_even, D_MODEL, D_FF), D_MODEL ** -0.5),
        'dense_w_down': normal((n_even, D_FF, D_MODEL), D_FF ** -0.5),
        'nsa_w_in': normal((n_odd, D_MODEL, ODD_IN), D_MODEL ** -0.5),
        'nsa_pos_k': normal((n_odd, NSA_CMP_LEN, HEAD_DIM), 0.02),
        'nsa_w1_k': normal((n_odd, cmp_in, NSA_CMP_HIDDEN), cmp_in ** -0.5),
        'nsa_b1_k': normal((n_odd, NSA_CMP_HIDDEN), 0.01),
        'nsa_w2_k': normal((n_odd, NSA_CMP_HIDDEN, HEAD_DIM), NSA_CMP_HIDDEN ** -0.5),
        'nsa_pos_v': normal((n_odd, NSA_CMP_LEN, HEAD_DIM), 0.02),
        'nsa_w1_v': normal((n_odd, cmp_in, NSA_CMP_HIDDEN), cmp_in ** -0.5),
        'nsa_b1_v': normal((n_odd, NSA_CMP_HIDDEN), 0.01),
        'nsa_w2_v': normal((n_odd, NSA_CMP_HIDDEN, HEAD_DIM), NSA_CMP_HIDDEN ** -0.5),
        'nsa_w_out': normal((n_odd, D_MODEL, D_MODEL), D_MODEL ** -0.5),
        'moe_w_router': normal((n_odd, D_MODEL, N_EXPERTS), D_MODEL ** -0.5),
        'moe_b_router': normal((n_odd, N_EXPERTS), 0.01),
        'moe_w_gate': normal((n_odd, N_EXPERTS, D_MODEL, D_FF_EXPERT), D_MODEL ** -0.5),
        'moe_w_up': normal((n_odd, N_EXPERTS, D_MODEL, D_FF_EXPERT), D_MODEL ** -0.5),
        'moe_w_down': normal((n_odd, N_EXPERTS, D_FF_EXPERT, D_MODEL), D_FF_EXPERT ** -0.5),
        'final_norm': gain((D_MODEL,)),
    }


def reference(x, p, positions, mix_norm, ffn_norm, ple_norm, ple_gate_w, ple_proj_w,
              fd_w_in, fd_forget_b, fd_w_out, dense_w_gate, dense_w_up, dense_w_down,
              nsa_w_in, nsa_pos_k, nsa_w1_k, nsa_b1_k, nsa_w2_k,
              nsa_pos_v, nsa_w1_v, nsa_b1_v, nsa_w2_v, nsa_w_out,
              moe_w_router, moe_b_router, moe_w_gate, moe_w_up, moe_w_down, final_norm):
    h = x
    for i in range(DEPTH):
        j = i // 2
        hn = rms_norm(h, mix_norm[i])
        if i % 2 == 0:
            h = h + fox_dilated_mixer(hn, positions, fd_w_in[j], fd_forget_b[j], fd_w_out[j])
            h = h + swiglu(rms_norm(h, ffn_norm[i]), dense_w_gate[j], dense_w_up[j], dense_w_down[j])
        else:
            h = h + nsa_mixer(hn, positions, nsa_w_in[j], nsa_pos_k[j], nsa_w1_k[j], nsa_b1_k[j], nsa_w2_k[j],
                              nsa_pos_v[j], nsa_w1_v[j], nsa_b1_v[j], nsa_w2_v[j], nsa_w_out[j])
            h = h + moe_swiglu(rms_norm(h, ffn_norm[i]), moe_w_router[j], moe_b_router[j],
                               moe_w_gate[j], moe_w_up[j], moe_w_down[j])
        gate = jax.nn.sigmoid(rms_norm(h, ple_norm[i]) @ ple_gate_w[i])
        h = h + gate * (p[i] @ ple_proj_w[i])
    return rms_norm(h, final_norm)
```

```python
import functools

import jax
import jax.numpy as jnp
from jax import lax
from jax.experimental import pallas as pl
from jax.experimental.pallas import tpu as pltpu

F32 = jnp.float32
BF16 = jnp.bfloat16

D_MODEL = 1024
HEAD_DIM = 64
LANES = 128
FOX_HEADS = 8
DIL_HEADS = 8
FOX_WIDTH = FOX_HEADS * HEAD_DIM
DIL_WIDTH = DIL_HEADS * HEAD_DIM
DIL_PATTERNS = ((128, 1), (512, 4), (2048, 16))
Q_BLOCK = 128
NSA_HEADS = 16
NSA_GROUPS = 4
NSA_HPG = NSA_HEADS // NSA_GROUPS
NSA_KV = NSA_GROUPS * HEAD_DIM
NSA_CMP_LEN = 32
NSA_CMP_STRIDE = 16
NSA_SEL_LEN = 64
NSA_TOP_N = 8
NSA_WINDOW = 512
N_EXPERTS = 8
ROPE_THETA = 10000.0
RMS_EPS = 1e-6
NEG_INF = -1e30
SCALE = HEAD_DIM ** -0.5

VMEM_LIMIT_BYTES = 52 * 1024 * 1024


def _params(*sem):
    return pltpu.CompilerParams(dimension_semantics=sem, vmem_limit_bytes=VMEM_LIMIT_BYTES)


def _rms(x, g):
    return x * lax.rsqrt(jnp.mean(x * x, axis=-1, keepdims=True) + RMS_EPS) * g


def _dot(a, b):
    return jnp.dot(a, b, preferred_element_type=F32)


def _dot_nt(a, b):
    return lax.dot_general(a, b, (((1,), (1,)), ((), ())), preferred_element_type=F32)


def _split3(x):
    hi = x.astype(BF16)
    r = x - hi.astype(F32)
    mid = r.astype(BF16)
    lo = (r - mid.astype(F32)).astype(BF16)
    return hi, mid, lo


def _lane_col(x, idx):
    lane = lax.broadcasted_iota(jnp.int32, x.shape, 1)
    return jnp.sum(jnp.where(lane == idx, x, 0.0), axis=1, keepdims=True)


def _rope_table_kernel(pos_ref, inv_ref, cos_ref, sin_ref):
    ang = pos_ref[...] * inv_ref[...]
    lane = lax.broadcasted_iota(jnp.int32, ang.shape, 1)
    sign = jnp.where((lane & (HEAD_DIM - 1)) < HEAD_DIM // 2, -1.0, 1.0)
    cos_ref[...] = jnp.cos(ang)
    sin_ref[...] = jnp.sin(ang) * sign


def _rope_tables(positions):
    t = positions.size
    half = HEAD_DIM // 2
    inv_freq = ROPE_THETA ** (-jnp.arange(half, dtype=F32) / half)
    inv = jnp.tile(inv_freq, LANES // half).reshape(1, LANES)
    pos = jnp.broadcast_to(positions.astype(F32).reshape(t, 1), (t, LANES))
    tm = 1024
    return pl.pallas_call(
        _rope_table_kernel,
        grid=(t // tm,),
        in_specs=[pl.BlockSpec((tm, LANES), lambda i: (i, 0)),
                  pl.BlockSpec((1, LANES), lambda i: (0, 0))],
        out_specs=[pl.BlockSpec((tm, LANES), lambda i: (i, 0))] * 2,
        out_shape=[jax.ShapeDtypeStruct((t, LANES), F32)] * 2,
        compiler_params=_params("parallel"),
        name="rope_tables",
    )(pos, inv)


def _proj_kernel(h_ref, g_ref, w_ref, wa_ref, ba_ref, cos_ref, sin_ref, o_ref, oa_ref, *, tiles, aux_act):
    xn = _rms(h_ref[...], g_ref[...]).astype(BF16)
    for c0, width, rope in tiles:
        acc = _dot(xn, w_ref[:, c0:c0 + width])
        if rope:
            reps = width // LANES
            cos = jnp.tile(cos_ref[...], (1, reps))
            sin = jnp.tile(sin_ref[...], (1, reps))
            lane = lax.broadcasted_iota(jnp.int32, acc.shape, 1)
            first_half = (lane & (HEAD_DIM - 1)) < HEAD_DIM // 2
            partner = jnp.where(first_half,
                                pltpu.roll(acc, width - HEAD_DIM // 2, 1),
                                pltpu.roll(acc, HEAD_DIM // 2, 1))
            acc = acc * cos + partner * sin
        o_ref[:, c0:c0 + width] = acc
    aux = _dot(xn, wa_ref[...]) + ba_ref[...]
    if aux_act == "log_sigmoid":
        oa_ref[...] = jnp.minimum(aux, 0.0) - jnp.log1p(jnp.exp(-jnp.abs(aux)))
    else:
        oa_ref[...] = jax.nn.sigmoid(aux)


def _proj(h, gain, w, w_aux, b_aux, cos, sin, tiles, aux_act, tm=512):
    t, n = h.shape[0], w.shape[1]
    return pl.pallas_call(
        functools.partial(_proj_kernel, tiles=tiles, aux_act=aux_act),
        grid=(t // tm,),
        in_specs=[pl.BlockSpec((tm, D_MODEL), lambda i: (i, 0)),
                  pl.BlockSpec((1, D_MODEL), lambda i: (0, 0)),
                  pl.BlockSpec((D_MODEL, n), lambda i: (0, 0)),
                  pl.BlockSpec((D_MODEL, LANES), lambda i: (0, 0)),
                  pl.BlockSpec((1, LANES), lambda i: (0, 0)),
                  pl.BlockSpec((tm, LANES), lambda i: (i, 0)),
                  pl.BlockSpec((tm, LANES), lambda i: (i, 0))],
        out_specs=[pl.BlockSpec((tm, n), lambda i: (i, 0)),
                   pl.BlockSpec((tm, LANES), lambda i: (i, 0))],
        out_shape=[jax.ShapeDtypeStruct((t, n), F32), jax.ShapeDtypeStruct((t, LANES), F32)],
        compiler_params=_params("parallel"),
        name="in_proj",
    )(h, gain.reshape(1, D_MODEL), w, w_aux, b_aux, cos, sin)


CUM_BLOCK = 256


def _cumsum_kernel(x_ref, ccol_ref, crow_ref, xrow_ref, carry_ref):
    j = pl.program_id(1)

    @pl.when(j == 0)
    def _():
        carry_ref[...] = jnp.zeros_like(carry_ref)

    x = x_ref[...]
    r = lax.broadcasted_iota(jnp.int32, (CUM_BLOCK, CUM_BLOCK), 0)
    c = lax.broadcasted_iota(jnp.int32, (CUM_BLOCK, CUM_BLOCK), 1)
    tri = jnp.where(r >= c, 1.0, 0.0).astype(BF16)
    hi, mid, lo = _split3(x)
    cum = _dot(tri, hi) + _dot(tri, mid) + _dot(tri, lo) + carry_ref[0:1, :]
    ccol_ref[...] = cum
    crow_ref[0] = cum.T
    xrow_ref[0] = x.T
    carry_ref[...] = jnp.broadcast_to(cum[CUM_BLOCK - 1:CUM_BLOCK, :], carry_ref.shape)


def _cumsum(x, batch, seq):
    nb = seq // CUM_BLOCK
    return pl.pallas_call(
        _cumsum_kernel,
        grid=(batch, nb),
        in_specs=[pl.BlockSpec((CUM_BLOCK, LANES), lambda b, j: (b * nb + j, 0))],
        out_specs=[pl.BlockSpec((CUM_BLOCK, LANES), lambda b, j: (b * nb + j, 0)),
                   pl.BlockSpec((1, LANES, CUM_BLOCK), lambda b, j: (b, 0, j)),
                   pl.BlockSpec((1, LANES, CUM_BLOCK), lambda b, j: (b, 0, j))],
        out_shape=[jax.ShapeDtypeStruct((batch * seq, LANES), F32),
                   jax.ShapeDtypeStruct((batch, LANES, seq), F32),
                   jax.ShapeDtypeStruct((batch, LANES, seq), F32)],
        scratch_shapes=[pltpu.VMEM((8, LANES), F32)],
        compiler_params=_params("parallel", "arbitrary"),
        name="token_cumsum",
    )(x)


def _fox_kernel(q_ref, k_ref, v_ref, ccol_ref, crow_ref, o_ref, *, tq):
    pair = pl.program_id(1)
    i = pl.program_id(2)
    tk = tq
    lane = lax.broadcasted_iota(jnp.int32, (tq, LANES), 1)
    low = lane < HEAD_DIM
    q = q_ref[...] * SCALE
    qs = jnp.concatenate([jnp.where(low, q, 0.0), jnp.where(low, 0.0, q)], axis=0).astype(BF16)
    cc = ccol_ref[...]
    h0 = 2 * pair
    cq = jnp.concatenate([_lane_col(cc, h0), _lane_col(cc, h0 + 1)], axis=0)
    qpos = i * tq + (lax.broadcasted_iota(jnp.int32, (2 * tq, tk), 0) & (tq - 1))
    kcol = lax.broadcasted_iota(jnp.int32, (2 * tq, tk), 1)
    row8 = lax.broadcasted_iota(jnp.int32, (8, tk), 0)

    def body(kt, carry):
        m, l, acc = carry
        ks = pl.multiple_of(kt * tk, tk)
        kb = k_ref[pl.ds(ks, tk), :].astype(BF16)
        vb = v_ref[pl.ds(ks, tk), :].astype(BF16)
        cr8 = crow_ref[0, :, pl.ds(ks, tk)]
        cr0 = jnp.sum(jnp.where(row8 == h0, cr8, 0.0), axis=0, keepdims=True)
        cr1 = jnp.sum(jnp.where(row8 == h0 + 1, cr8, 0.0), axis=0, keepdims=True)
        ck = jnp.concatenate([jnp.broadcast_to(cr0, (tq, tk)), jnp.broadcast_to(cr1, (tq, tk))], axis=0)
        ok2 = (ks + kcol) <= qpos
        s = _dot_nt(qs, kb) + cq - ck
        s = jnp.where(ok2, s, NEG_INF)
        m_new = jnp.maximum(m, jnp.max(s, axis=1, keepdims=True))
        alpha = jnp.exp(m - m_new)
        p = jnp.where(ok2, jnp.exp(s - m_new), 0.0)
        l = alpha * l + jnp.sum(p, axis=1, keepdims=True)
        pv = _dot(p.astype(BF16), vb)
        acc = acc * jnp.where(low, alpha[:tq], alpha[tq:]) + jnp.where(low, pv[:tq], pv[tq:])
        return m_new, l, acc

    init = (jnp.full((2 * tq, 1), NEG_INF, F32), jnp.zeros((2 * tq, 1), F32), jnp.zeros((tq, LANES), F32))
    m, l, acc = lax.fori_loop(0, i + 1, body, init)
    o_ref[...] = acc / jnp.where(low, l[:tq], l[tq:])


def _fox_attention(proj, ccol, crow, batch, seq, tq=256):
    nq = seq // tq
    npair = FOX_WIDTH // LANES
    return pl.pallas_call(
        functools.partial(_fox_kernel, tq=tq),
        grid=(batch, npair, nq),
        in_specs=[pl.BlockSpec((tq, LANES), lambda b, p, i: (b * nq + i, p)),
                  pl.BlockSpec((seq, LANES), lambda b, p, i: (b, npair + p)),
                  pl.BlockSpec((seq, LANES), lambda b, p, i: (b, 2 * npair + p)),
                  pl.BlockSpec((tq, LANES), lambda b, p, i: (b * nq + i, 0)),
                  pl.BlockSpec((1, 8, seq), lambda b, p, i: (b, 0, 0))],
        out_specs=pl.BlockSpec((tq, LANES), lambda b, p, i: (b * nq + i, p)),
        out_shape=jax.ShapeDtypeStruct((batch * seq, FOX_WIDTH), F32),
        compiler_params=_params("parallel", "parallel", "arbitrary"),
        name="fox_attention",
    )(proj, proj, proj, ccol, crow)


def _dil_kernel(q_ref, k_ref, v_ref, o_ref, os_ref, ls_ref, *, seq):
    lane = lax.broadcasted_iota(jnp.int32, (Q_BLOCK, LANES), 1)
    low = lane < HEAD_DIM
    qi = lax.broadcasted_iota(jnp.int32, (2 * Q_BLOCK, 2 * Q_BLOCK), 0) & (Q_BLOCK - 1)
    kj = lax.broadcasted_iota(jnp.int32, (2 * Q_BLOCK, 2 * Q_BLOCK), 1)
    dist = qi + Q_BLOCK - kj

    for pi, (window, dil) in enumerate(DIL_PATTERNS):
        span = window // dil
        nb = (seq // dil) // Q_BLOCK
        band = (dist >= 0) & (dist <= span)

        def rows(start, dil=dil):
            return pl.ds(start, Q_BLOCK, stride=dil) if dil > 1 else pl.ds(start, Q_BLOCK)

        def unit(u, carry, pi=pi, dil=dil, nb=nb, band=band, rows=rows):
            r = u // nb
            blk = u % nb
            cur = r + blk * (Q_BLOCK * dil)
            prev = jnp.maximum(cur - Q_BLOCK * dil, r)
            q = q_ref[rows(cur), :] * SCALE
            qs = jnp.concatenate([jnp.where(low, q, 0.0), jnp.where(low, 0.0, q)], axis=0).astype(BF16)
            kk = jnp.concatenate([k_ref[rows(prev), :], k_ref[rows(cur), :]], axis=0).astype(BF16)
            vv = jnp.concatenate([v_ref[rows(prev), :], v_ref[rows(cur), :]], axis=0).astype(BF16)
            mask2 = band & (kj >= jnp.where(blk > 0, 0, Q_BLOCK))
            s = jnp.where(mask2, _dot_nt(qs, kk), NEG_INF)
            m = jnp.max(s, axis=1, keepdims=True)
            e = jnp.where(mask2, jnp.exp(s - m), 0.0)
            den = jnp.sum(e, axis=1, keepdims=True)
            o2 = _dot((e / den).astype(BF16), vv)
            lse = m + jnp.log(den)
            os_ref[pi, rows(cur), :] = jnp.where(low, o2[:Q_BLOCK], o2[Q_BLOCK:])
            ls_ref[pi, rows(cur), :] = jnp.where(low, lse[:Q_BLOCK], lse[Q_BLOCK:])
            return carry

        lax.fori_loop(0, dil * nb, unit, 0)

    chunk = 256

    def combine(ci, carry):
        rs = pl.ds(pl.multiple_of(ci * chunk, chunk), chunk)
        l0, l1, l2 = ls_ref[0, rs, :], ls_ref[1, rs, :], ls_ref[2, rs, :]
        m = jnp.maximum(jnp.maximum(l0, l1), l2)
        e0, e1, e2 = jnp.exp(l0 - m), jnp.exp(l1 - m), jnp.exp(l2 - m)
        tot = e0 + e1 + e2
        o_ref[rs, :] = ((e0 / tot) * os_ref[0, rs, :] + (e1 / tot) * os_ref[1, rs, :]
                        + (e2 / tot) * os_ref[2, rs, :])
        return carry

    lax.fori_loop(0, seq // chunk, combine, 0)


def _dilated_attention(proj, batch, seq):
    npair = DIL_WIDTH // LANES
    base = 3 * FOX_WIDTH // LANES
    return pl.pallas_call(
        functools.partial(_dil_kernel, seq=seq),
        grid=(batch, npair),
        in_specs=[pl.BlockSpec((seq, LANES), lambda b, p: (b, base + p)),
                  pl.BlockSpec((seq, LANES), lambda b, p: (b, base + npair + p)),
                  pl.BlockSpec((seq, LANES), lambda b, p: (b, base + 2 * npair + p))],
        out_specs=pl.BlockSpec((seq, LANES), lambda b, p: (b, p)),
        out_shape=jax.ShapeDtypeStruct((batch * seq, DIL_WIDTH), F32),
        scratch_shapes=[pltpu.VMEM((3, seq, LANES), F32), pltpu.VMEM((3, seq, LANES), F32)],
        compiler_params=_params("parallel", "parallel"),
        name="dilated_attention",
    )(proj, proj, proj)


def _outproj_kernel(*refs, n_in, tn):
    h_ref = refs[0]
    a_refs = refs[1:1 + n_in]
    w_ref = refs[1 + n_in]
    o_ref = refs[2 + n_in]
    acts = [a[...].astype(BF16) for a in a_refs]
    for c0 in range(0, D_MODEL, tn):
        acc = h_ref[:, c0:c0 + tn]
        k0 = 0
        for a in acts:
            acc = acc + _dot(a, w_ref[k0:k0 + a.shape[1], c0:c0 + tn])
            k0 += a.shape[1]
        o_ref[:, c0:c0 + tn] = acc


def _outproj(h, acts, w, tm=512, tn=512):
    t = h.shape[0]
    return pl.pallas_call(
        functools.partial(_outproj_kernel, n_in=len(acts), tn=tn),
        grid=(t // tm,),
        in_specs=([pl.BlockSpec((tm, D_MODEL), lambda i: (i, 0))]
                  + [pl.BlockSpec((tm, a.shape[1]), lambda i: (i, 0)) for a in acts]
                  + [pl.BlockSpec((D_MODEL, D_MODEL), lambda i: (0, 0))]),
        out_specs=pl.BlockSpec((tm, D_MODEL), lambda i: (i, 0)),
        out_shape=jax.ShapeDtypeStruct((t, D_MODEL), F32),
        compiler_params=_params("parallel"),
        name="out_proj",
    )(h, *acts, w)


def _ffn_kernel(*refs, n_f, weighted):
    if weighted:
        h_ref, g_ref, cw_ref, wg_ref, wu_ref, wd_ref, o_ref, xn_ref, acc_ref = refs
        e = pl.program_id(1)
        f = pl.program_id(2)
        first = (e == 0) & (f == 0)
        last = (e == N_EXPERTS - 1) & (f == n_f - 1)
    else:
        h_ref, g_ref, wg_ref, wu_ref, wd_ref, o_ref, xn_ref, acc_ref = refs
        f = pl.program_id(1)
        first = f == 0
        last = f == n_f - 1

    @pl.when(first)
    def _():
        xn_ref[...] = _rms(h_ref[...], g_ref[...]).astype(BF16)
        acc_ref[...] = jnp.zeros_like(acc_ref)

    xn = xn_ref[...]
    gate = _dot(xn, wg_ref[...])
    up = _dot(xn, wu_ref[...])
    act = gate * jax.nn.sigmoid(gate) * up
    if weighted:
        act = act * _lane_col(cw_ref[...], e)
    acc_ref[...] += _dot(act.astype(BF16), wd_ref[...])

    @pl.when(last)
    def _():
        o_ref[...] = h_ref[...] + acc_ref[...]


def _ffn(h, gain, wg, wu, wd, tm=1024, tf=512):
    t, dff = h.shape[0], wg.shape[1]
    n_f = dff // tf
    return pl.pallas_call(
        functools.partial(_ffn_kernel, n_f=n_f, weighted=False),
        grid=(t // tm, n_f),
        in_specs=[pl.BlockSpec((tm, D_MODEL), lambda i, f: (i, 0)),
                  pl.BlockSpec((1, D_MODEL), lambda i, f: (0, 0)),
                  pl.BlockSpec((D_MODEL, tf), lambda i, f: (0, f)),
                  pl.BlockSpec((D_MODEL, tf), lambda i, f: (0, f)),
                  pl.BlockSpec((tf, D_MODEL), lambda i, f: (f, 0))],
        out_specs=pl.BlockSpec((tm, D_MODEL), lambda i, f: (i, 0)),
        out_shape=jax.ShapeDtypeStruct((t, D_MODEL), F32),
        scratch_shapes=[pltpu.VMEM((tm, D_MODEL), BF16), pltpu.VMEM((tm, D_MODEL), F32)],
        compiler_params=_params("parallel", "arbitrary"),
        name="dense_swiglu",
    )(h, gain.reshape(1, D_MODEL), wg, wu, wd)


def _moe_ffn(h, gain, cw, wg, wu, wd, tm=1024, tf=512):
    t, dff = h.shape[0], wg.shape[2]
    n_f = dff // tf
    return pl.pallas_call(
        functools.partial(_ffn_kernel, n_f=n_f, weighted=True),
        grid=(t // tm, N_EXPERTS, n_f),
        in_specs=[pl.BlockSpec((tm, D_MODEL), lambda i, e, f: (i, 0)),
                  pl.BlockSpec((1, D_MODEL), lambda i, e, f: (0, 0)),
                  pl.BlockSpec((tm, LANES), lambda i, e, f: (i, 0)),
                  pl.BlockSpec((None, D_MODEL, tf), lambda i, e, f: (e, 0, f)),
                  pl.BlockSpec((None, D_MODEL, tf), lambda i, e, f: (e, 0, f)),
                  pl.BlockSpec((None, tf, D_MODEL), lambda i, e, f: (e, f, 0))],
        out_specs=pl.BlockSpec((tm, D_MODEL), lambda i, e, f: (i, 0)),
        out_shape=jax.ShapeDtypeStruct((t, D_MODEL), F32),
        scratch_shapes=[pltpu.VMEM((tm, D_MODEL), BF16), pltpu.VMEM((tm, D_MODEL), F32)],
        compiler_params=_params("parallel", "arbitrary", "arbitrary"),
        name="moe_swiglu",
    )(h, gain.reshape(1, D_MODEL), cw, wg, wu, wd)


def _router_kernel(h_ref, g_ref, w_ref, b_ref, cw_ref):
    xn = _rms(h_ref[...], g_ref[...]).astype(BF16)
    logits = _dot(xn, w_ref[...]) + b_ref[...]
    lane = lax.broadcasted_iota(jnp.int32, logits.shape, 1).astype(F32)
    logits = jnp.where(lane < N_EXPERTS, logits, -jnp.inf)
    m1 = jnp.max(logits, axis=1, keepdims=True)
    i1 = jnp.min(jnp.where(logits == m1, lane, float(LANES)), axis=1, keepdims=True)
    rest = jnp.where(lane == i1, -jnp.inf, logits)
    m2 = jnp.max(rest, axis=1, keepdims=True)
    i2 = jnp.min(jnp.where(rest == m2, lane, float(LANES)), axis=1, keepdims=True)
    e2 = jnp.exp(m2 - m1)
    w1 = 1.0 / (1.0 + e2)
    w2 = e2 / (1.0 + e2)
    cw_ref[...] = jnp.where(lane == i1, w1, jnp.where(lane == i2, w2, 0.0))


def _router(h, gain, w, b, tm=1024):
    t = h.shape[0]
    return pl.pallas_call(
        _router_kernel,
        grid=(t // tm,),
        in_specs=[pl.BlockSpec((tm, D_MODEL), lambda i: (i, 0)),
                  pl.BlockSpec((1, D_MODEL), lambda i: (0, 0)),
                  pl.BlockSpec((D_MODEL, LANES), lambda i: (0, 0)),
                  pl.BlockSpec((1, LANES), lambda i: (0, 0))],
        out_specs=pl.BlockSpec((tm, LANES), lambda i: (i, 0)),
        out_shape=jax.ShapeDtypeStruct((t, LANES), F32),
        compiler_params=_params("parallel"),
        name="moe_router",
    )(h, gain.reshape(1, D_MODEL), w, b)


def _ple_kernel(h_ref, g_ref, p_ref, wg_ref, wp_ref, fg_ref, o_ref, *, final, tn):
    h = h_ref[...]
    xn = _rms(h, g_ref[...]).astype(BF16)
    pe = p_ref[...].astype(BF16)
    outs = []
    for c0 in range(0, D_MODEL, tn):
        gate = jax.nn.sigmoid(_dot(xn, wg_ref[:, c0:c0 + tn]))
        outs.append(h[:, c0:c0 + tn] + gate * _dot(pe, wp_ref[:, c0:c0 + tn]))
    new = jnp.concatenate(outs, axis=1)
    o_ref[...] = _rms(new, fg_ref[...]) if final else new


def _ple(h, gain, p, wg, wp, final_gain, final, tm=512, tn=512):
    t, pd = p.shape
    return pl.pallas_call(
        functools.partial(_ple_kernel, final=final, tn=tn),
        grid=(t // tm,),
        in_specs=[pl.BlockSpec((tm, D_MODEL), lambda i: (i, 0)),
                  pl.BlockSpec((1, D_MODEL), lambda i: (0, 0)),
                  pl.BlockSpec((tm, pd), lambda i: (i, 0)),
                  pl.BlockSpec((D_MODEL, D_MODEL), lambda i: (0, 0)),
                  pl.BlockSpec((pd, D_MODEL), lambda i: (0, 0)),
                  pl.BlockSpec((1, D_MODEL), lambda i: (0, 0))],
        out_specs=pl.BlockSpec((tm, D_MODEL), lambda i: (i, 0)),
        out_shape=jax.ShapeDtypeStruct((t, D_MODEL), F32),
        compiler_params=_params("parallel"),
        name="ple",
    )(h, gain.reshape(1, D_MODEL), p, wg, wp, final_gain.reshape(1, D_MODEL))


CMP_ROWS = NSA_CMP_STRIDE * HEAD_DIM


def _compress_kernel(x_ref, pos_ref, w1_ref, b1_ref, w2_ref, o_ref, *, tm, blocks):
    x = x_ref[...]
    first = _dot((x + pos_ref[0:1, :]).astype(BF16), w1_ref[0:CMP_ROWS, :])
    second = _dot((x + pos_ref[1:2, :]).astype(BF16), w1_ref[CMP_ROWS:2 * CMP_ROWS, :])
    hid = first + pltpu.roll(second, tm - 1, 0) + b1_ref[...]
    out = _dot(jax.nn.gelu(hid, approximate=True).astype(BF16), w2_ref[...])
    row = lax.broadcasted_iota(jnp.int32, out.shape, 0)
    o_ref[...] = jnp.where((row & (blocks - 1)) == blocks - 1, 0.0, out)


def _compress(x, pos, w1, b1, w2, blocks, tm=1024):
    rows = x.shape[0]
    hidden = w1.shape[1]
    tm = min(tm, rows)
    assert blocks & (blocks - 1) == 0 and tm % blocks == 0
    return pl.pallas_call(
        functools.partial(_compress_kernel, tm=tm, blocks=blocks),
        grid=(rows // tm,),
        in_specs=[pl.BlockSpec((tm, CMP_ROWS), lambda i: (i, 0)),
                  pl.BlockSpec((2, CMP_ROWS), lambda i: (0, 0)),
                  pl.BlockSpec((2 * CMP_ROWS, hidden), lambda i: (0, 0)),
                  pl.BlockSpec((1, hidden), lambda i: (0, 0)),
                  pl.BlockSpec((hidden, NSA_WIDTH), lambda i: (0, 0))],
        out_specs=pl.BlockSpec((tm, NSA_WIDTH), lambda i: (i, 0)),
        out_shape=jax.ShapeDtypeStruct((rows, NSA_WIDTH), F32),
        compiler_params=_params("parallel"),
        name="nsa_compress",
    )(x, pos.reshape(2, CMP_ROWS), w1, b1.reshape(1, hidden), w2)


NSA_TQ = 128
NSA_TK = 256
NSA_WIDTH = NSA_HPG * HEAD_DIM
LOG_HEAD_DIM = HEAD_DIM.bit_length() - 1
LOG_SEL_LEN = NSA_SEL_LEN.bit_length() - 1


def _group_lanes(x, g):
    grp = lax.broadcasted_iota(jnp.int32, x.shape, 1) >> LOG_HEAD_DIM
    out = x
    for s in range(1, NSA_GROUPS):
        out = jnp.where(((grp - g) & (NSA_GROUPS - 1)) == s, pltpu.roll(x, s * HEAD_DIM, 1), out)
    return out


def _nsa_kernel(q_ref, kc_ref, vc_ref, ks_ref, vs_ref, kw_ref, vw_ref, gate_ref, o_ref,
                ks4, vs4, kw4, vw4, *, seq, n_cmp):
    g = pl.program_id(1)
    i = pl.program_id(2)
    tq, tk = NSA_TQ, NSA_TK
    rows = NSA_HPG * tq

    @pl.when(i == 0)
    def _():
        def fill(ci, carry):
            rs = pl.ds(pl.multiple_of(ci * tk, tk), tk)
            for src, dst in ((ks_ref, ks4), (vs_ref, vs4), (kw_ref, kw4), (vw_ref, vw4)):
                dst[rs, :] = _group_lanes(src[rs, :], g).astype(BF16)
            return carry

        lax.fori_loop(0, seq // tk, fill, 0)

    t0 = i * tq
    grp = lax.broadcasted_iota(jnp.int32, (tq, NSA_WIDTH), 1) >> LOG_HEAD_DIM
    q = q_ref[...] * SCALE
    qs = jnp.concatenate([jnp.where(grp == j, q, 0.0) for j in range(NSA_HPG)], axis=0).astype(BF16)

    def head_select(x4):
        out = x4[0:tq]
        for j in range(1, NSA_HPG):
            out = jnp.where(grp == j, x4[j * tq:(j + 1) * tq], out)
        return out

    def stack4(x):
        return jnp.concatenate([x] * NSA_HPG, axis=0)

    tpos_c4 = t0 + (lax.broadcasted_iota(jnp.int32, (rows, LANES), 0) & (tq - 1))
    ncol4 = lax.broadcasted_iota(jnp.int32, (rows, LANES), 1)
    valid_c4 = (ncol4 * NSA_CMP_STRIDE + NSA_CMP_LEN - 1 <= tpos_c4) & (ncol4 < n_cmp)
    sc = jnp.where(valid_c4, _dot_nt(qs, kc_ref[0].astype(BF16)), NEG_INF)
    mc = jnp.max(sc, axis=1, keepdims=True)
    ec = jnp.where(valid_c4, jnp.exp(sc - mc), 0.0)
    dc = jnp.sum(ec, axis=1, keepdims=True)
    pc = ec / jnp.where(dc > 0.0, dc, 1.0)
    o_cmp = head_select(_dot(pc.astype(BF16), vc_ref[0].astype(BF16)))
    tpos_c = t0 + lax.broadcasted_iota(jnp.int32, (tq, LANES), 0)
    ncol = lax.broadcasted_iota(jnp.int32, (tq, LANES), 1)
    pc_sum = pc[0:tq]
    for j in range(1, NSA_HPG):
        pc_sum = pc_sum + pc[j * tq:(j + 1) * tq]

    n_sel_blocks = seq // NSA_SEL_LEN
    nrow = lax.broadcasted_iota(jnp.int32, (LANES, LANES), 0)
    jcol = lax.broadcasted_iota(jnp.int32, (LANES, LANES), 1)
    overlap = ((nrow * NSA_CMP_STRIDE < (jcol + 1) * NSA_SEL_LEN)
               & (nrow * NSA_CMP_STRIDE + NSA_CMP_LEN > jcol * NSA_SEL_LEN)
               & (nrow < n_cmp) & (jcol < n_sel_blocks))
    overlap = jnp.where(overlap, 1.0, 0.0).astype(BF16)
    hi, mid, lo = _split3(pc_sum)
    imp = _dot(hi, overlap) + _dot(mid, overlap) + _dot(lo, overlap)
    cur = tpos_c >> LOG_SEL_LEN
    forced = (ncol == 0) | (ncol == cur) | (ncol == cur - 1)
    imp = jnp.where(ncol > cur, -1.0, jnp.where(forced, 1e6, imp))
    imp = jnp.where(ncol < n_sel_blocks, imp, -jnp.inf)
    beaten = jnp.zeros((tq, LANES), jnp.int32)
    for c in range(n_sel_blocks):
        col = imp[:, c:c + 1]
        wins = (col > imp) | ((col == imp) & (ncol > c))
        beaten = beaten + jnp.where(wins, 1, 0)
    selected = jnp.where((beaten < NSA_TOP_N) & (ncol < n_sel_blocks), 1.0, 0.0).astype(BF16)

    qpos = t0 + (lax.broadcasted_iota(jnp.int32, (rows, tk), 0) & (tq - 1))
    kcol = lax.broadcasted_iota(jnp.int32, (rows, tk), 1)
    erow = lax.broadcasted_iota(jnp.int32, (LANES, tk), 0)
    ecol = lax.broadcasted_iota(jnp.int32, (LANES, tk), 1)

    def flash(k4, v4, lo_tile, hi_tile, valid_fn):
        def body(kt, carry):
            m, l, acc = carry
            ks = pl.multiple_of(kt * tk, tk)
            valid4 = valid_fn(ks)
            s = jnp.where(valid4, _dot_nt(qs, k4[pl.ds(ks, tk), :]), NEG_INF)
            m_new = jnp.maximum(m, jnp.max(s, axis=1, keepdims=True))
            alpha = jnp.exp(m - m_new)
            p = jnp.where(valid4, jnp.exp(s - m_new), 0.0)
            l = alpha * l + jnp.sum(p, axis=1, keepdims=True)
            acc = acc * alpha + _dot(p.astype(BF16), v4[pl.ds(ks, tk), :])
            return m_new, l, acc

        init = (jnp.full((rows, 1), NEG_INF, F32), jnp.zeros((rows, 1), F32),
                jnp.zeros((rows, NSA_WIDTH), F32))
        m, l, acc = lax.fori_loop(lo_tile, hi_tile, body, init)
        return head_select(acc / l)

    last_tile = (t0 + tq - 1) // tk + 1

    def valid_sel(ks):
        expand = jnp.where(((ks + ecol) >> LOG_SEL_LEN) == erow, 1.0, 0.0).astype(BF16)
        chosen = stack4(_dot(selected, expand))
        return (chosen > 0.5) & (ks + kcol <= qpos)

    o_sel = flash(ks4, vs4, 0, last_tile, valid_sel)

    def valid_win(ks):
        kpos = ks + kcol
        return (kpos <= qpos) & (kpos > qpos - NSA_WINDOW)

    first_tile = jnp.maximum(t0 - NSA_WINDOW + 1, 0) // tk
    o_win = flash(kw4, vw4, first_tile, last_tile, valid_win)

    grow = lax.broadcasted_iota(jnp.int32, (LANES, NSA_WIDTH), 0)
    gcol = lax.broadcasted_iota(jnp.int32, (LANES, NSA_WIDTH), 1) >> LOG_HEAD_DIM
    ghi, gmid, glo = _split3(gate_ref[...])
    out = jnp.zeros((tq, NSA_WIDTH), F32)
    for c, branch in enumerate((o_cmp, o_sel, o_win)):
        pick = jnp.where(grow == c * NSA_HEADS + g * NSA_HPG + gcol, 1.0, 0.0).astype(BF16)
        out = out + (_dot(ghi, pick) + _dot(gmid, pick) + _dot(glo, pick)) * branch
    o_ref[...] = out


def _nsa_attention(proj, gates, kc_cmp, vc_cmp, batch, seq, n_cmp):
    nq = seq // NSA_TQ
    ncb = kc_cmp.shape[1]
    qcols = NSA_HEADS * HEAD_DIM // NSA_WIDTH
    kv = lambda c: pl.BlockSpec((seq, NSA_WIDTH), lambda b, g, i, c=c: (b, qcols + c))
    cmp_spec = pl.BlockSpec((1, ncb, NSA_WIDTH), lambda b, g, i: (b * NSA_GROUPS + g, 0, 0))
    return pl.pallas_call(
        functools.partial(_nsa_kernel, seq=seq, n_cmp=n_cmp),
        grid=(batch, NSA_GROUPS, nq),
        in_specs=[pl.BlockSpec((NSA_TQ, NSA_WIDTH), lambda b, g, i: (b * nq + i, g)),
                  cmp_spec, cmp_spec, kv(2), kv(3), kv(4), kv(5),
                  pl.BlockSpec((NSA_TQ, LANES), lambda b, g, i: (b * nq + i, 0))],
        out_specs=pl.BlockSpec((NSA_TQ, NSA_WIDTH), lambda b, g, i: (b * nq + i, g)),
        out_shape=jax.ShapeDtypeStruct((batch * seq, NSA_HEADS * HEAD_DIM), F32),
        scratch_shapes=[pltpu.VMEM((seq, NSA_WIDTH), BF16)] * 4,
        compiler_params=_params("parallel", "parallel", "arbitrary"),
        name="nsa_attention",
    )(proj, kc_cmp, vc_cmp, proj, proj, proj, proj, gates)


def _pad_cols(w, width=LANES):
    return jnp.pad(w, ((0, 0), (0, width - w.shape[1])))


def _group_rows(cols, batch, seq):
    x = cols.reshape(batch, seq, NSA_GROUPS, HEAD_DIM).transpose(0, 2, 1, 3)
    return x.reshape(batch * NSA_GROUPS * (seq // NSA_CMP_STRIDE), CMP_ROWS)


def kernel(x, p, positions, mix_norm, ffn_norm, ple_norm, ple_gate_w, ple_proj_w, fd_w_in, fd_forget_b, fd_w_out, dense_w_gate, dense_w_up, dense_w_down, nsa_w_in, nsa_pos_k, nsa_w1_k, nsa_b1_k, nsa_w2_k, nsa_pos_v, nsa_w1_v, nsa_b1_v, nsa_w2_v, nsa_w_out, moe_w_router, moe_b_router, moe_w_gate, moe_w_up, moe_w_down, final_norm):
    batch, seq, _ = x.shape
    t = batch * seq
    h = x.reshape(t, D_MODEL)
    cos, sin = _rope_tables(positions)

    n_main = 3 * FOX_WIDTH + 3 * DIL_WIDTH
    w_in = fd_w_in[0]
    tiles0 = tuple((c, 512, c in (3 * FOX_WIDTH, 3 * FOX_WIDTH + DIL_WIDTH)) for c in range(0, n_main, 512))
    proj0, log_f = _proj(h, mix_norm[0], w_in[:, :n_main].astype(BF16),
                         _pad_cols(w_in[:, n_main:]).astype(BF16),
                         _pad_cols(fd_forget_b[0].reshape(1, FOX_HEADS)).astype(F32),
                         cos, sin, tiles0, "log_sigmoid")
    ccol, crow, _ = _cumsum(log_f, batch, seq)
    o_fox = _fox_attention(proj0, ccol, crow, batch, seq)
    o_dil = _dilated_attention(proj0, batch, seq)
    h = _outproj(h, [o_fox, o_dil], fd_w_out[0].astype(BF16))
    h = _ffn(h, ffn_norm[0], dense_w_gate[0].astype(BF16), dense_w_up[0].astype(BF16),
             dense_w_down[0].astype(BF16))
    h = _ple(h, ple_norm[0], p[0].reshape(t, -1), ple_gate_w[0].astype(BF16), ple_proj_w[0].astype(BF16),
             final_norm, final=False)

    qw = NSA_HEADS * HEAD_DIM
    n_main1 = qw + 6 * NSA_KV
    w_in1 = nsa_w_in[0]
    rope_cols = set(range(0, qw, 256)) | {qw, qw + 2 * NSA_KV, qw + 4 * NSA_KV}
    tiles1 = tuple((c, 256, c in rope_cols) for c in range(0, n_main1, 256))
    w_gate = w_in1[:, n_main1:].reshape(D_MODEL, NSA_HEADS, 3).transpose(0, 2, 1).reshape(D_MODEL, 3 * NSA_HEADS)
    proj1, gates = _proj(h, mix_norm[1], w_in1[:, :n_main1].astype(BF16), _pad_cols(w_gate).astype(BF16),
                         jnp.zeros((1, LANES), F32), cos, sin, tiles1, "sigmoid")
    n_cmp = (seq - NSA_CMP_LEN) // NSA_CMP_STRIDE + 1
    blocks = seq // NSA_CMP_STRIDE
    kc_cmp = _compress(_group_rows(proj1[:, qw:qw + NSA_KV], batch, seq), nsa_pos_k[0],
                       nsa_w1_k[0].astype(BF16), nsa_b1_k[0],
                       jnp.tile(nsa_w2_k[0], (1, NSA_HPG)).astype(BF16), blocks)
    vc_cmp = _compress(_group_rows(proj1[:, qw + NSA_KV:qw + 2 * NSA_KV], batch, seq), nsa_pos_v[0],
                       nsa_w1_v[0].astype(BF16), nsa_b1_v[0],
                       jnp.tile(nsa_w2_v[0], (1, NSA_HPG)).astype(BF16), blocks)
    kc_cmp = kc_cmp.reshape(batch * NSA_GROUPS, blocks, NSA_WIDTH)
    vc_cmp = vc_cmp.reshape(batch * NSA_GROUPS, blocks, NSA_WIDTH)
    o_nsa = _nsa_attention(proj1, gates, kc_cmp, vc_cmp, batch, seq, n_cmp)
    h = _outproj(h, [o_nsa], nsa_w_out[0].astype(BF16))
    cw = _router(h, ffn_norm[1], _pad_cols(moe_w_router[0]).astype(BF16),
                 _pad_cols(moe_b_router[0].reshape(1, N_EXPERTS)).astype(F32))
    h = _moe_ffn(h, ffn_norm[1], cw, moe_w_gate[0].astype(BF16), moe_w_up[0].astype(BF16),
                 moe_w_down[0].astype(BF16))
    h = _ple(h, ple_norm[1], p[1].reshape(t, -1), ple_gate_w[1].astype(BF16), ple_proj_w[1].astype(BF16),
             final_norm, final=True)
    return h.reshape(batch, seq, D_MODEL)
```

```python
import functools

import jax
import jax.numpy as jnp
from jax import lax
from jax.experimental import pallas as pl
from jax.experimental.pallas import tpu as pltpu

F32 = jnp.float32
BF16 = jnp.bfloat16

D_MODEL = 1024
HEAD_DIM = 64
LANES = 128
FOX_HEADS = 8
DIL_HEADS = 8
FOX_WIDTH = FOX_HEADS * HEAD_DIM
DIL_WIDTH = DIL_HEADS * HEAD_DIM
DIL_PATTERNS = ((128, 1), (512, 4), (2048, 16))
Q_BLOCK = 128
NSA_HEADS = 16
NSA_GROUPS = 4
NSA_HPG = NSA_HEADS // NSA_GROUPS
NSA_KV = NSA_GROUPS * HEAD_DIM
NSA_CMP_LEN = 32
NSA_CMP_STRIDE = 16
NSA_SEL_LEN = 64
NSA_TOP_N = 8
NSA_WINDOW = 512
N_EXPERTS = 8
ROPE_THETA = 10000.0
RMS_EPS = 1e-6
NEG_INF = -1e30
SCALE = HEAD_DIM ** -0.5

VMEM_LIMIT_BYTES = 52 * 1024 * 1024


def _params(*sem):
    return pltpu.CompilerParams(dimension_semantics=sem, vmem_limit_bytes=VMEM_LIMIT_BYTES)


def _rms(x, g):
    return x * lax.rsqrt(jnp.mean(x * x, axis=-1, keepdims=True) + RMS_EPS) * g


def _dot(a, b):
    return jnp.dot(a, b, preferred_element_type=F32)


def _dot_nt(a, b):
    return lax.dot_general(a, b, (((1,), (1,)), ((), ())), preferred_element_type=F32)


def _split3(x):
    hi = x.astype(BF16)
    r = x - hi.astype(F32)
    mid = r.astype(BF16)
    lo = (r - mid.astype(F32)).astype(BF16)
    return hi, mid, lo


def _lane_col(x, idx):
    lane = lax.broadcasted_iota(jnp.int32, x.shape, 1)
    return jnp.sum(jnp.where(lane == idx, x, 0.0), axis=1, keepdims=True)


def _rope_table_kernel(pos_ref, inv_ref, cos_ref, sin_ref):
    ang = pos_ref[...] * inv_ref[...]
    lane = lax.broadcasted_iota(jnp.int32, ang.shape, 1)
    sign = jnp.where((lane & (HEAD_DIM - 1)) < HEAD_DIM // 2, -1.0, 1.0)
    cos_ref[...] = jnp.cos(ang)
    sin_ref[...] = jnp.sin(ang) * sign


def _rope_tables(positions):
    t = positions.size
    half = HEAD_DIM // 2
    inv_freq = ROPE_THETA ** (-jnp.arange(half, dtype=F32) / half)
    inv = jnp.tile(inv_freq, LANES // half).reshape(1, LANES)
    pos = jnp.broadcast_to(positions.astype(F32).reshape(t, 1), (t, LANES))
    tm = 1024
    return pl.pallas_call(
        _rope_table_kernel,
        grid=(t // tm,),
        in_specs=[pl.BlockSpec((tm, LANES), lambda i: (i, 0)),
                  pl.BlockSpec((1, LANES), lambda i: (0, 0))],
        out_specs=[pl.BlockSpec((tm, LANES), lambda i: (i, 0))] * 2,
        out_shape=[jax.ShapeDtypeStruct((t, LANES), F32)] * 2,
        compiler_params=_params("parallel"),
        name="rope_tables",
    )(pos, inv)


def _proj_kernel(h_ref, g_ref, w_ref, wa_ref, ba_ref, cos_ref, sin_ref, o_ref, oa_ref, *, tiles, aux_act):
    xn = _rms(h_ref[...], g_ref[...]).astype(BF16)
    for c0, width, rope in tiles:
        acc = _dot(xn, w_ref[:, c0:c0 + width])
        if rope:
            reps = width // LANES
            cos = jnp.tile(cos_ref[...], (1, reps))
            sin = jnp.tile(sin_ref[...], (1, reps))
            lane = lax.broadcasted_iota(jnp.int32, acc.shape, 1)
            first_half = (lane & (HEAD_DIM - 1)) < HEAD_DIM // 2
            partner = jnp.where(first_half,
                                pltpu.roll(acc, width - HEAD_DIM // 2, 1),
                                pltpu.roll(acc, HEAD_DIM // 2, 1))
            acc = acc * cos + partner * sin
        o_ref[:, c0:c0 + width] = acc
    aux = _dot(xn, wa_ref[...]) + ba_ref[...]
    if aux_act == "log_sigmoid":
        oa_ref[...] = jnp.minimum(aux, 0.0) - jnp.log1p(jnp.exp(-jnp.abs(aux)))
    else:
        oa_ref[...] = jax.nn.sigmoid(aux)


def _proj(h, gain, w, w_aux, b_aux, cos, sin, tiles, aux_act, tm=512):
    t, n = h.shape[0], w.shape[1]
    return pl.pallas_call(
        functools.partial(_proj_kernel, tiles=tiles, aux_act=aux_act),
        grid=(t // tm,),
        in_specs=[pl.BlockSpec((tm, D_MODEL), lambda i: (i, 0)),
                  pl.BlockSpec((1, D_MODEL), lambda i: (0, 0)),
                  pl.BlockSpec((D_MODEL, n), lambda i: (0, 0)),
                  pl.BlockSpec((D_MODEL, LANES), lambda i: (0, 0)),
                  pl.BlockSpec((1, LANES), lambda i: (0, 0)),
                  pl.BlockSpec((tm, LANES), lambda i: (i, 0)),
                  pl.BlockSpec((tm, LANES), lambda i: (i, 0))],
        out_specs=[pl.BlockSpec((tm, n), lambda i: (i, 0)),
                   pl.BlockSpec((tm, LANES), lambda i: (i, 0))],
        out_shape=[jax.ShapeDtypeStruct((t, n), F32), jax.ShapeDtypeStruct((t, LANES), F32)],
        compiler_params=_params("parallel"),
        name="in_proj",
    )(h, gain.reshape(1, D_MODEL), w, w_aux, b_aux, cos, sin)


CUM_BLOCK = 256


def _cumsum_kernel(x_ref, ccol_ref, crow_ref, xrow_ref, carry_ref):
    j = pl.program_id(1)

    @pl.when(j == 0)
    def _():
        carry_ref[...] = jnp.zeros_like(carry_ref)

    x = x_ref[...]
    r = lax.broadcasted_iota(jnp.int32, (CUM_BLOCK, CUM_BLOCK), 0)
    c = lax.broadcasted_iota(jnp.int32, (CUM_BLOCK, CUM_BLOCK), 1)
    tri = jnp.where(r >= c, 1.0, 0.0).astype(BF16)
    hi, mid, lo = _split3(x)
    cum = _dot(tri, hi) + _dot(tri, mid) + _dot(tri, lo) + carry_ref[0:1, :]
    ccol_ref[...] = cum
    crow_ref[0] = cum.T
    xrow_ref[0] = x.T
    carry_ref[...] = jnp.broadcast_to(cum[CUM_BLOCK - 1:CUM_BLOCK, :], carry_ref.shape)


def _cumsum(x, batch, seq):
    nb = seq // CUM_BLOCK
    return pl.pallas_call(
        _cumsum_kernel,
        grid=(batch, nb),
        in_specs=[pl.BlockSpec((CUM_BLOCK, LANES), lambda b, j: (b * nb + j, 0))],
        out_specs=[pl.BlockSpec((CUM_BLOCK, LANES), lambda b, j: (b * nb + j, 0)),
                   pl.BlockSpec((1, LANES, CUM_BLOCK), lambda b, j: (b, 0, j)),
                   pl.BlockSpec((1, LANES, CUM_BLOCK), lambda b, j: (b, 0, j))],
        out_shape=[jax.ShapeDtypeStruct((batch * seq, LANES), F32),
                   jax.ShapeDtypeStruct((batch, LANES, seq), F32),
                   jax.ShapeDtypeStruct((batch, LANES, seq), F32)],
        scratch_shapes=[pltpu.VMEM((8, LANES), F32)],
        compiler_params=_params("parallel", "arbitrary"),
        name="token_cumsum",
    )(x)


def _fox_kernel(q_ref, k_ref, v_ref, ccol_ref, crow_ref, o_ref, *, tq):
    pair = pl.program_id(1)
    i = pl.program_id(2)
    tk = tq
    lane = lax.broadcasted_iota(jnp.int32, (tq, LANES), 1)
    low = lane < HEAD_DIM
    q = q_ref[...] * SCALE
    qs = jnp.concatenate([jnp.where(low, q, 0.0), jnp.where(low, 0.0, q)], axis=0).astype(BF16)
    cc = ccol_ref[...]
    h0 = 2 * pair
    cq = jnp.concatenate([_lane_col(cc, h0), _lane_col(cc, h0 + 1)], axis=0)
    qpos = i * tq + (lax.broadcasted_iota(jnp.int32, (2 * tq, tk), 0) & (tq - 1))
    kcol = lax.broadcasted_iota(jnp.int32, (2 * tq, tk), 1)
    row8 = lax.broadcasted_iota(jnp.int32, (8, tk), 0)

    def body(kt, carry):
        m, l, acc = carry
        ks = pl.multiple_of(kt * tk, tk)
        kb = k_ref[pl.ds(ks, tk), :].astype(BF16)
        vb = v_ref[pl.ds(ks, tk), :].astype(BF16)
        cr8 = crow_ref[0, :, pl.ds(ks, tk)]
        cr0 = jnp.sum(jnp.where(row8 == h0, cr8, 0.0), axis=0, keepdims=True)
        cr1 = jnp.sum(jnp.where(row8 == h0 + 1, cr8, 0.0), axis=0, keepdims=True)
        ck = jnp.concatenate([jnp.broadcast_to(cr0, (tq, tk)), jnp.broadcast_to(cr1, (tq, tk))], axis=0)
        ok2 = (ks + kcol) <= qpos
        s = _dot_nt(qs, kb) + cq - ck
        s = jnp.where(ok2, s, NEG_INF)
        m_new = jnp.maximum(m, jnp.max(s, axis=1, keepdims=True))
        alpha = jnp.exp(m - m_new)
        p = jnp.where(ok2, jnp.exp(s - m_new), 0.0)
        l = alpha * l + jnp.sum(p, axis=1, keepdims=True)
        pv = _dot(p.astype(BF16), vb)
        acc = acc * jnp.where(low, alpha[:tq], alpha[tq:]) + jnp.where(low, pv[:tq], pv[tq:])
        return m_new, l, acc

    init = (jnp.full((2 * tq, 1), NEG_INF, F32), jnp.zeros((2 * tq, 1), F32), jnp.zeros((tq, LANES), F32))
    m, l, acc = lax.fori_loop(0, i + 1, body, init)
    o_ref[...] = acc / jnp.where(low, l[:tq], l[tq:])


def _fox_attention(proj, ccol, crow, batch, seq, tq=256):
    nq = seq // tq
    npair = FOX_WIDTH // LANES
    return pl.pallas_call(
        functools.partial(_fox_kernel, tq=tq),
        grid=(batch, npair, nq),
        in_specs=[pl.BlockSpec((tq, LANES), lambda b, p, i: (b * nq + i, p)),
                  pl.BlockSpec((seq, LANES), lambda b, p, i: (b, npair + p)),
                  pl.BlockSpec((seq, LANES), lambda b, p, i: (b, 2 * npair + p)),
                  pl.BlockSpec((tq, LANES), lambda b, p, i: (b * nq + i, 0)),
                  pl.BlockSpec((1, 8, seq), lambda b, p, i: (b, 0, 0))],
        out_specs=pl.BlockSpec((tq, LANES), lambda b, p, i: (b * nq + i, p)),
        out_shape=jax.ShapeDtypeStruct((batch * seq, FOX_WIDTH), F32),
        compiler_params=_params("parallel", "parallel", "arbitrary"),
        name="fox_attention",
    )(proj, proj, proj, ccol, crow)


def _dil_kernel(q_ref, k_ref, v_ref, o_ref, os_ref, ls_ref, *, seq):
    lane = lax.broadcasted_iota(jnp.int32, (Q_BLOCK, LANES), 1)
    low = lane < HEAD_DIM
    qi = lax.broadcasted_iota(jnp.int32, (2 * Q_BLOCK, 2 * Q_BLOCK), 0) & (Q_BLOCK - 1)
    kj = lax.broadcasted_iota(jnp.int32, (2 * Q_BLOCK, 2 * Q_BLOCK), 1)
    dist = qi + Q_BLOCK - kj

    for pi, (window, dil) in enumerate(DIL_PATTERNS):
        span = window // dil
        nb = (seq // dil) // Q_BLOCK
        band = (dist >= 0) & (dist <= span)

        def rows(start, dil=dil):
            return pl.ds(start, Q_BLOCK, stride=dil) if dil > 1 else pl.ds(start, Q_BLOCK)

        def unit(u, carry, pi=pi, dil=dil, nb=nb, band=band, rows=rows):
            r = u // nb
            blk = u % nb
            cur = r + blk * (Q_BLOCK * dil)
            prev = jnp.maximum(cur - Q_BLOCK * dil, r)
            q = q_ref[rows(cur), :] * SCALE
            qs = jnp.concatenate([jnp.where(low, q, 0.0), jnp.where(low, 0.0, q)], axis=0).astype(BF16)
            kk = jnp.concatenate([k_ref[rows(prev), :], k_ref[rows(cur), :]], axis=0).astype(BF16)
            vv = jnp.concatenate([v_ref[rows(prev), :], v_ref[rows(cur), :]], axis=0).astype(BF16)
            mask2 = band & (kj >= jnp.where(blk > 0, 0, Q_BLOCK))
            s = jnp.where(mask2, _dot_nt(qs, kk), NEG_INF)
            m = jnp.max(s, axis=1, keepdims=True)
            e = jnp.where(mask2, jnp.exp(s - m), 0.0)
            den = jnp.sum(e, axis=1, keepdims=True)
            o2 = _dot((e / den).astype(BF16), vv)
            lse = m + jnp.log(den)
            os_ref[pi, rows(cur), :] = jnp.where(low, o2[:Q_BLOCK], o2[Q_BLOCK:])
            ls_ref[pi, rows(cur), :] = jnp.where(low, lse[:Q_BLOCK], lse[Q_BLOCK:])
            return carry

        lax.fori_loop(0, dil * nb, unit, 0)

    chunk = 256

    def combine(ci, carry):
        rs = pl.ds(pl.multiple_of(ci * chunk, chunk), chunk)
        l0, l1, l2 = ls_ref[0, rs, :], ls_ref[1, rs, :], ls_ref[2, rs, :]
        m = jnp.maximum(jnp.maximum(l0, l1), l2)
        e0, e1, e2 = jnp.exp(l0 - m), jnp.exp(l1 - m), jnp.exp(l2 - m)
        tot = e0 + e1 + e2
        o_ref[rs, :] = ((e0 / tot) * os_ref[0, rs, :] + (e1 / tot) * os_ref[1, rs, :]
                        + (e2 / tot) * os_ref[2, rs, :])
        return carry

    lax.fori_loop(0, seq // chunk, combine, 0)


def _dilated_attention(proj, batch, seq):
    npair = DIL_WIDTH // LANES
    base = 3 * FOX_WIDTH // LANES
    return pl.pallas_call(
        functools.partial(_dil_kernel, seq=seq),
        grid=(batch, npair),
        in_specs=[pl.BlockSpec((seq, LANES), lambda b, p: (b, base + p)),
                  pl.BlockSpec((seq, LANES), lambda b, p: (b, base + npair + p)),
                  pl.BlockSpec((seq, LANES), lambda b, p: (b, base + 2 * npair + p))],
        out_specs=pl.BlockSpec((seq, LANES), lambda b, p: (b, p)),
        out_shape=jax.ShapeDtypeStruct((batch * seq, DIL_WIDTH), F32),
        scratch_shapes=[pltpu.VMEM((3, seq, LANES), F32), pltpu.VMEM((3, seq, LANES), F32)],
        compiler_params=_params("parallel", "parallel"),
        name="dilated_attention",
    )(proj, proj, proj)


def _outproj_kernel(*refs, n_in, tn):
    h_ref = refs[0]
    a_refs = refs[1:1 + n_in]
    w_ref = refs[1 + n_in]
    o_ref = refs[2 + n_in]
    acts = [a[...].astype(BF16) for a in a_refs]
    for c0 in range(0, D_MODEL, tn):
        acc = h_ref[:, c0:c0 + tn]
        k0 = 0
        for a in acts:
            acc = acc + _dot(a, w_ref[k0:k0 + a.shape[1], c0:c0 + tn])
            k0 += a.shape[1]
        o_ref[:, c0:c0 + tn] = acc


def _outproj(h, acts, w, tm=512, tn=512):
    t = h.shape[0]
    return pl.pallas_call(
        functools.partial(_outproj_kernel, n_in=len(acts), tn=tn),
        grid=(t // tm,),
        in_specs=([pl.BlockSpec((tm, D_MODEL), lambda i: (i, 0))]
                  + [pl.BlockSpec((tm, a.shape[1]), lambda i: (i, 0)) for a in acts]
                  + [pl.BlockSpec((D_MODEL, D_MODEL), lambda i: (0, 0))]),
        out_specs=pl.BlockSpec((tm, D_MODEL), lambda i: (i, 0)),
        out_shape=jax.ShapeDtypeStruct((t, D_MODEL), F32),
        compiler_params=_params("parallel"),
        name="out_proj",
    )(h, *acts, w)


def _ffn_kernel(*refs, n_f, weighted):
    if weighted:
        h_ref, g_ref, cw_ref, wg_ref, wu_ref, wd_ref, o_ref, xn_ref, acc_ref = refs
        e = pl.program_id(1)
        f = pl.program_id(2)
        first = (e == 0) & (f == 0)
        last = (e == N_EXPERTS - 1) & (f == n_f - 1)
    else:
        h_ref, g_ref, wg_ref, wu_ref, wd_ref, o_ref, xn_ref, acc_ref = refs
        f = pl.program_id(1)
        first = f == 0
        last = f == n_f - 1

    @pl.when(first)
    def _():
        xn_ref[...] = _rms(h_ref[...], g_ref[...]).astype(BF16)
        acc_ref[...] = jnp.zeros_like(acc_ref)

    xn = xn_ref[...]
    gate = _dot(xn, wg_ref[...])
    up = _dot(xn, wu_ref[...])
    act = gate * jax.nn.sigmoid(gate) * up
    if weighted:
        act = act * _lane_col(cw_ref[...], e)
    acc_ref[...] += _dot(act.astype(BF16), wd_ref[...])

    @pl.when(last)
    def _():
        o_ref[...] = h_ref[...] + acc_ref[...]


def _ffn(h, gain, wg, wu, wd, tm=1024, tf=512):
    t, dff = h.shape[0], wg.shape[1]
    n_f = dff // tf
    return pl.pallas_call(
        functools.partial(_ffn_kernel, n_f=n_f, weighted=False),
        grid=(t // tm, n_f),
        in_specs=[pl.BlockSpec((tm, D_MODEL), lambda i, f: (i, 0)),
                  pl.BlockSpec((1, D_MODEL), lambda i, f: (0, 0)),
                  pl.BlockSpec((D_MODEL, tf), lambda i, f: (0, f)),
                  pl.BlockSpec((D_MODEL, tf), lambda i, f: (0, f)),
                  pl.BlockSpec((tf, D_MODEL), lambda i, f: (f, 0))],
        out_specs=pl.BlockSpec((tm, D_MODEL), lambda i, f: (i, 0)),
        out_shape=jax.ShapeDtypeStruct((t, D_MODEL), F32),
        scratch_shapes=[pltpu.VMEM((tm, D_MODEL), BF16), pltpu.VMEM((tm, D_MODEL), F32)],
        compiler_params=_params("parallel", "arbitrary"),
        name="dense_swiglu",
    )(h, gain.reshape(1, D_MODEL), wg, wu, wd)


def _moe_ffn(h, gain, cw, wg, wu, wd, tm=1024, tf=512):
    t, dff = h.shape[0], wg.shape[2]
    n_f = dff // tf
    return pl.pallas_call(
        functools.partial(_ffn_kernel, n_f=n_f, weighted=True),
        grid=(t // tm, N_EXPERTS, n_f),
        in_specs=[pl.BlockSpec((tm, D_MODEL), lambda i, e, f: (i, 0)),
                  pl.BlockSpec((1, D_MODEL), lambda i, e, f: (0, 0)),
                  pl.BlockSpec((tm, LANES), lambda i, e, f: (i, 0)),
                  pl.BlockSpec((None, D_MODEL, tf), lambda i, e, f: (e, 0, f)),
                  pl.BlockSpec((None, D_MODEL, tf), lambda i, e, f: (e, 0, f)),
                  pl.BlockSpec((None, tf, D_MODEL), lambda i, e, f: (e, f, 0))],
        out_specs=pl.BlockSpec((tm, D_MODEL), lambda i, e, f: (i, 0)),
        out_shape=jax.ShapeDtypeStruct((t, D_MODEL), F32),
        scratch_shapes=[pltpu.VMEM((tm, D_MODEL), BF16), pltpu.VMEM((tm, D_MODEL), F32)],
        compiler_params=_params("parallel", "arbitrary", "arbitrary"),
        name="moe_swiglu",
    )(h, gain.reshape(1, D_MODEL), cw, wg, wu, wd)


def _router_kernel(h_ref, g_ref, w_ref, b_ref, cw_ref):
    xn = _rms(h_ref[...], g_ref[...]).astype(BF16)
    logits = _dot(xn, w_ref[...]) + b_ref[...]
    lane = lax.broadcasted_iota(jnp.int32, logits.shape, 1).astype(F32)
    logits = jnp.where(lane < N_EXPERTS, logits, -jnp.inf)
    m1 = jnp.max(logits, axis=1, keepdims=True)
    i1 = jnp.min(jnp.where(logits == m1, lane, float(LANES)), axis=1, keepdims=True)
    rest = jnp.where(lane == i1, -jnp.inf, logits)
    m2 = jnp.max(rest, axis=1, keepdims=True)
    i2 = jnp.min(jnp.where(rest == m2, lane, float(LANES)), axis=1, keepdims=True)
    e2 = jnp.exp(m2 - m1)
    w1 = 1.0 / (1.0 + e2)
    w2 = e2 / (1.0 + e2)
    cw_ref[...] = jnp.where(lane == i1, w1, jnp.where(lane == i2, w2, 0.0))


def _router(h, gain, w, b, tm=1024):
    t = h.shape[0]
    return pl.pallas_call(
        _router_kernel,
        grid=(t // tm,),
        in_specs=[pl.BlockSpec((tm, D_MODEL), lambda i: (i, 0)),
                  pl.BlockSpec((1, D_MODEL), lambda i: (0, 0)),
                  pl.BlockSpec((D_MODEL, LANES), lambda i: (0, 0)),
                  pl.BlockSpec((1, LANES), lambda i: (0, 0))],
        out_specs=pl.BlockSpec((tm, LANES), lambda i: (i, 0)),
        out_shape=jax.ShapeDtypeStruct((t, LANES), F32),
        compiler_params=_params("parallel"),
        name="moe_router",
    )(h, gain.reshape(1, D_MODEL), w, b)


def _ple_kernel(h_ref, g_ref, p_ref, wg_ref, wp_ref, fg_ref, o_ref, *, final, tn):
    h = h_ref[...]
    xn = _rms(h, g_ref[...]).astype(BF16)
    pe = p_ref[...].astype(BF16)
    outs = []
    for c0 in range(0, D_MODEL, tn):
        gate = jax.nn.sigmoid(_dot(xn, wg_ref[:, c0:c0 + tn]))
        outs.append(h[:, c0:c0 + tn] + gate * _dot(pe, wp_ref[:, c0:c0 + tn]))
    new = jnp.concatenate(outs, axis=1)
    o_ref[...] = _rms(new, fg_ref[...]) if final else new


def _ple(h, gain, p, wg, wp, final_gain, final, tm=512, tn=512):
    t, pd = p.shape
    return pl.pallas_call(
        functools.partial(_ple_kernel, final=final, tn=tn),
        grid=(t // tm,),
        in_specs=[pl.BlockSpec((tm, D_MODEL), lambda i: (i, 0)),
                  pl.BlockSpec((1, D_MODEL), lambda i: (0, 0)),
                  pl.BlockSpec((tm, pd), lambda i: (i, 0)),
                  pl.BlockSpec((D_MODEL, D_MODEL), lambda i: (0, 0)),
                  pl.BlockSpec((pd, D_MODEL), lambda i: (0, 0)),
                  pl.BlockSpec((1, D_MODEL), lambda i: (0, 0))],
        out_specs=pl.BlockSpec((tm, D_MODEL), lambda i: (i, 0)),
        out_shape=jax.ShapeDtypeStruct((t, D_MODEL), F32),
        compiler_params=_params("parallel"),
        name="ple",
    )(h, gain.reshape(1, D_MODEL), p, wg, wp, final_gain.reshape(1, D_MODEL))


CMP_ROWS = NSA_CMP_STRIDE * HEAD_DIM


def _compress_kernel(x_ref, pos_ref, w1_ref, b1_ref, w2_ref, o_ref, *, tm, blocks):
    x = x_ref[...]
    first = _dot((x + pos_ref[0:1, :]).astype(BF16), w1_ref[0:CMP_ROWS, :])
    second = _dot((x + pos_ref[1:2, :]).astype(BF16), w1_ref[CMP_ROWS:2 * CMP_ROWS, :])
    hid = first + pltpu.roll(second, tm - 1, 0) + b1_ref[...]
    out = _dot(jax.nn.gelu(hid, approximate=True).astype(BF16), w2_ref[...])
    row = lax.broadcasted_iota(jnp.int32, out.shape, 0)
    o_ref[...] = jnp.where((row & (blocks - 1)) == blocks - 1, 0.0, out)


def _compress(x, pos, w1, b1, w2, blocks, tm=1024):
    rows = x.shape[0]
    hidden = w1.shape[1]
    tm = min(tm, rows)
    assert blocks & (blocks - 1) == 0 and tm % blocks == 0
    return pl.pallas_call(
        functools.partial(_compress_kernel, tm=tm, blocks=blocks),
        grid=(rows // tm,),
        in_specs=[pl.BlockSpec((tm, CMP_ROWS), lambda i: (i, 0)),
                  pl.BlockSpec((2, CMP_ROWS), lambda i: (0, 0)),
                  pl.BlockSpec((2 * CMP_ROWS, hidden), lambda i: (0, 0)),
                  pl.BlockSpec((1, hidden), lambda i: (0, 0)),
                  pl.BlockSpec((hidden, NSA_WIDTH), lambda i: (0, 0))],
        out_specs=pl.BlockSpec((tm, NSA_WIDTH), lambda i: (i, 0)),
        out_shape=jax.ShapeDtypeStruct((rows, NSA_WIDTH), F32),
        compiler_params=_params("parallel"),
        name="nsa_compress",
    )(x, pos.reshape(2, CMP_ROWS), w1, b1.reshape(1, hidden), w2)


NSA_TQ = 128
NSA_TK = 512
NSA_WIDTH = NSA_HPG * HEAD_DIM
LOG_HEAD_DIM = HEAD_DIM.bit_length() - 1
LOG_SEL_LEN = NSA_SEL_LEN.bit_length() - 1


def _nsa_kernel(q_ref, kc_ref, vc_ref, ks_ref, vs_ref, kw_ref, vw_ref, gate_ref, o_ref,
                ksb, kwb, vst, vwt, vct, gt_ref, tmp_ref, *, seq, n_cmp):
    g = pl.program_id(1)
    i = pl.program_id(2)
    tq, tk = NSA_TQ, NSA_TK
    cols = NSA_HPG * tq
    grow = pl.multiple_of(g * HEAD_DIM, HEAD_DIM)

    @pl.when(i == 0)
    def _():
        def fill(ci, carry):
            rs = pl.ds(pl.multiple_of(ci * NSA_WIDTH, NSA_WIDTH), NSA_WIDTH)
            ksb[rs, :] = ks_ref[rs, :].astype(BF16)
            kwb[rs, :] = kw_ref[rs, :].astype(BF16)
            for src, dst in ((vs_ref, vst), (vw_ref, vwt)):
                tmp_ref[...] = src[rs, :].T
                dst[:, rs] = tmp_ref[pl.ds(grow, HEAD_DIM), :].astype(BF16)
            return carry

        lax.fori_loop(0, seq // NSA_WIDTH, fill, 0)
        vct[...] = vc_ref[0].T[0:HEAD_DIM, :].astype(BF16)

    t0 = i * tq
    lane_grp = lax.broadcasted_iota(jnp.int32, (tq, NSA_WIDTH), 1) >> LOG_HEAD_DIM
    q = q_ref[...] * SCALE
    rolled = [q] + [pltpu.roll(q, s * HEAD_DIM, 1) for s in range(1, NSA_HPG)]
    parts = []
    for j in range(NSA_HPG):
        shift = (g - j) & (NSA_HPG - 1)
        moved = jnp.where(shift == 0, rolled[0],
                          jnp.where(shift == 1, rolled[1], jnp.where(shift == 2, rolled[2], rolled[3])))
        parts.append(jnp.where(lane_grp == g, moved, 0.0))
    qs = jnp.concatenate(parts, axis=0).astype(BF16)

    def heads_sum(x):
        out = x[:, 0:tq]
        for j in range(1, NSA_HPG):
            out = out + x[:, j * tq:(j + 1) * tq]
        return out

    def lanes4(x):
        return jnp.concatenate([x] * NSA_HPG, axis=1)

    nrow = lax.broadcasted_iota(jnp.int32, (LANES, cols), 0)
    tcol = t0 + (lax.broadcasted_iota(jnp.int32, (LANES, cols), 1) & (tq - 1))
    valid_c = (nrow * NSA_CMP_STRIDE + NSA_CMP_LEN - 1 <= tcol) & (nrow < n_cmp)
    sc = jnp.where(valid_c, _dot_nt(kc_ref[0].astype(BF16), qs), NEG_INF)
    mc = jnp.max(sc, axis=0, keepdims=True)
    ec = jnp.where(valid_c, jnp.exp(sc - mc), 0.0)
    dc = jnp.sum(ec, axis=0, keepdims=True)
    pc = ec / jnp.where(dc > 0.0, dc, 1.0)
    o_cmp = _dot(vct[...], pc.astype(BF16))
    pc_sum = heads_sum(pc)

    n_sel_blocks = seq // NSA_SEL_LEN
    jrow = lax.broadcasted_iota(jnp.int32, (LANES, LANES), 0)
    ncol = lax.broadcasted_iota(jnp.int32, (LANES, LANES), 1)
    overlap = ((ncol * NSA_CMP_STRIDE < (jrow + 1) * NSA_SEL_LEN)
               & (ncol * NSA_CMP_STRIDE + NSA_CMP_LEN > jrow * NSA_SEL_LEN)
               & (ncol < n_cmp) & (jrow < n_sel_blocks))
    overlap = jnp.where(overlap, 1.0, 0.0).astype(BF16)
    hi, mid, lo = _split3(pc_sum)
    imp = (_dot(overlap, hi) + _dot(overlap, mid) + _dot(overlap, lo))[0:n_sel_blocks]
    blk = lax.broadcasted_iota(jnp.int32, (n_sel_blocks, tq), 0)
    cur = (t0 + lax.broadcasted_iota(jnp.int32, (n_sel_blocks, tq), 1)) >> LOG_SEL_LEN
    forced = (blk == 0) | (blk == cur) | (blk == cur - 1)
    imp = jnp.where(blk > cur, -1.0, jnp.where(forced, 1e6, imp))
    beaten = jnp.zeros((n_sel_blocks, tq), jnp.int32)
    for c in range(n_sel_blocks):
        row = imp[c:c + 1, :]
        wins = (row > imp) | ((row == imp) & (blk > c))
        beaten = beaten + jnp.where(wins, 1, 0)
    sel_bias = jnp.where(beaten < NSA_TOP_N, 0.0, NEG_INF)
    sel_bias = jnp.concatenate([sel_bias, jnp.zeros((LANES - n_sel_blocks, tq), F32)], axis=0).astype(BF16)

    krow = lax.broadcasted_iota(jnp.int32, (tk, tq), 0)
    qlane = t0 + lax.broadcasted_iota(jnp.int32, (tk, tq), 1)
    erow = lax.broadcasted_iota(jnp.int32, (tk, LANES), 0)
    ecol = lax.broadcasted_iota(jnp.int32, (tk, LANES), 1)

    def sel_body(kt, carry):
        m, l, acc = carry
        ks = pl.multiple_of(kt * tk, tk)
        expand = jnp.where(((ks + erow) >> LOG_SEL_LEN) == ecol, 1.0, 0.0).astype(BF16)
        bias = jnp.where(ks + krow <= qlane, _dot(expand, sel_bias), NEG_INF)
        s = _dot_nt(ksb[pl.ds(ks, tk), :], qs) + lanes4(bias)
        m_new = jnp.maximum(m, jnp.max(s, axis=0, keepdims=True))
        alpha = jnp.exp(m - m_new)
        p = jnp.exp(s - m_new)
        l = alpha * l + jnp.sum(p, axis=0, keepdims=True)
        acc = acc * alpha + _dot(vst[:, pl.ds(ks, tk)], p.astype(BF16))
        return m_new, l, acc

    init = (jnp.full((1, cols), NEG_INF, F32), jnp.zeros((1, cols), F32), jnp.zeros((HEAD_DIM, cols), F32))
    _, l_sel, acc_sel = lax.fori_loop(0, (t0 + tq - 1) // tk + 1, sel_body, init)
    o_sel = acc_sel / l_sel

    wk = NSA_WINDOW + tq
    ws = pl.multiple_of(jnp.maximum(t0 - NSA_WINDOW, 0), tq)
    kpos = ws + lax.broadcasted_iota(jnp.int32, (wk, tq), 0)
    qw = t0 + lax.broadcasted_iota(jnp.int32, (wk, tq), 1)
    bias_w = jnp.where((kpos <= qw) & (kpos > qw - NSA_WINDOW), 0.0, NEG_INF)
    sw = _dot_nt(kwb[pl.ds(ws, wk), :], qs) + lanes4(bias_w)
    pw = jnp.exp(sw - jnp.max(sw, axis=0, keepdims=True))
    o_win = _dot(vwt[:, pl.ds(ws, wk)], pw.astype(BF16)) / jnp.sum(pw, axis=0, keepdims=True)

    gt_ref[...] = gate_ref[...].T
    out = jnp.zeros((HEAD_DIM, cols), F32)
    for c, branch in enumerate((o_cmp, o_sel, o_win)):
        gate = jnp.concatenate([gt_ref[pl.ds(c * NSA_HEADS + g * NSA_HPG + j, 1), :] for j in range(NSA_HPG)],
                               axis=1)
        out = out + gate * branch
    o_ref[...] = jnp.concatenate([out[:, j * tq:(j + 1) * tq] for j in range(NSA_HPG)], axis=0).T


def _nsa_attention(proj, gates, kc_cmp, vc_cmp, batch, seq, n_cmp):
    nq = seq // NSA_TQ
    ncb = kc_cmp.shape[1]
    qcols = NSA_HEADS * HEAD_DIM // NSA_WIDTH
    kv = lambda c: pl.BlockSpec((seq, NSA_WIDTH), lambda b, g, i, c=c: (b, qcols + c))
    cmp_spec = pl.BlockSpec((1, ncb, NSA_WIDTH), lambda b, g, i: (b * NSA_GROUPS + g, 0, 0))
    return pl.pallas_call(
        functools.partial(_nsa_kernel, seq=seq, n_cmp=n_cmp),
        grid=(batch, NSA_GROUPS, nq),
        in_specs=[pl.BlockSpec((NSA_TQ, NSA_WIDTH), lambda b, g, i: (b * nq + i, g)),
                  cmp_spec, cmp_spec, kv(2), kv(3), kv(4), kv(5),
                  pl.BlockSpec((NSA_TQ, LANES), lambda b, g, i: (b * nq + i, 0))],
        out_specs=pl.BlockSpec((NSA_TQ, NSA_WIDTH), lambda b, g, i: (b * nq + i, g)),
        out_shape=jax.ShapeDtypeStruct((batch * seq, NSA_HEADS * HEAD_DIM), F32),
        scratch_shapes=[pltpu.VMEM((seq, NSA_WIDTH), BF16), pltpu.VMEM((seq, NSA_WIDTH), BF16),
                        pltpu.VMEM((HEAD_DIM, seq), BF16), pltpu.VMEM((HEAD_DIM, seq), BF16),
                        pltpu.VMEM((HEAD_DIM, ncb), BF16), pltpu.VMEM((LANES, NSA_TQ), F32),
                        pltpu.VMEM((NSA_WIDTH, NSA_WIDTH), F32)],
        compiler_params=_params("parallel", "parallel", "arbitrary"),
        name="nsa_attention",
    )(proj, kc_cmp, vc_cmp, proj, proj, proj, proj, gates)


def _pad_cols(w, width=LANES):
    return jnp.pad(w, ((0, 0), (0, width - w.shape[1])))


def _group_rows(cols, batch, seq):
    x = cols.reshape(batch, seq, NSA_GROUPS, HEAD_DIM).transpose(0, 2, 1, 3)
    return x.reshape(batch * NSA_GROUPS * (seq // NSA_CMP_STRIDE), CMP_ROWS)


def kernel(x, p, positions, mix_norm, ffn_norm, ple_norm, ple_gate_w, ple_proj_w, fd_w_in, fd_forget_b, fd_w_out, dense_w_gate, dense_w_up, dense_w_down, nsa_w_in, nsa_pos_k, nsa_w1_k, nsa_b1_k, nsa_w2_k, nsa_pos_v, nsa_w1_v, nsa_b1_v, nsa_w2_v, nsa_w_out, moe_w_router, moe_b_router, moe_w_gate, moe_w_up, moe_w_down, final_norm):
    batch, seq, _ = x.shape
    t = batch * seq
    h = x.reshape(t, D_MODEL)
    cos, sin = _rope_tables(positions)

    n_main = 3 * FOX_WIDTH + 3 * DIL_WIDTH
    w_in = fd_w_in[0]
    tiles0 = tuple((c, 512, c in (3 * FOX_WIDTH, 3 * FOX_WIDTH + DIL_WIDTH)) for c in range(0, n_main, 512))
    proj0, log_f = _proj(h, mix_norm[0], w_in[:, :n_main].astype(BF16),
                         _pad_cols(w_in[:, n_main:]).astype(BF16),
                         _pad_cols(fd_forget_b[0].reshape(1, FOX_HEADS)).astype(F32),
                         cos, sin, tiles0, "log_sigmoid")
    ccol, crow, _ = _cumsum(log_f, batch, seq)
    o_fox = _fox_attention(proj0, ccol, crow, batch, seq)
    o_dil = _dilated_attention(proj0, batch, seq)
    h = _outproj(h, [o_fox, o_dil], fd_w_out[0].astype(BF16))
    h = _ffn(h, ffn_norm[0], dense_w_gate[0].astype(BF16), dense_w_up[0].astype(BF16),
             dense_w_down[0].astype(BF16))
    h = _ple(h, ple_norm[0], p[0].reshape(t, -1), ple_gate_w[0].astype(BF16), ple_proj_w[0].astype(BF16),
             final_norm, final=False)

    qw = NSA_HEADS * HEAD_DIM
    n_main1 = qw + 6 * NSA_KV
    w_in1 = nsa_w_in[0]
    rope_cols = set(range(0, qw, 256)) | {qw, qw + 2 * NSA_KV, qw + 4 * NSA_KV}
    tiles1 = tuple((c, 256, c in rope_cols) for c in range(0, n_main1, 256))
    w_gate = w_in1[:, n_main1:].reshape(D_MODEL, NSA_HEADS, 3).transpose(0, 2, 1).reshape(D_MODEL, 3 * NSA_HEADS)
    proj1, gates = _proj(h, mix_norm[1], w_in1[:, :n_main1].astype(BF16), _pad_cols(w_gate).astype(BF16),
                         jnp.zeros((1, LANES), F32), cos, sin, tiles1, "sigmoid")
    n_cmp = (seq - NSA_CMP_LEN) // NSA_CMP_STRIDE + 1
    blocks = seq // NSA_CMP_STRIDE
    kc_cmp = _compress(_group_rows(proj1[:, qw:qw + NSA_KV], batch, seq), nsa_pos_k[0],
                       nsa_w1_k[0].astype(BF16), nsa_b1_k[0],
                       jnp.tile(nsa_w2_k[0], (1, NSA_HPG)).astype(BF16), blocks)
    vc_cmp = _compress(_group_rows(proj1[:, qw + NSA_KV:qw + 2 * NSA_KV], batch, seq), nsa_pos_v[0],
                       nsa_w1_v[0].astype(BF16), nsa_b1_v[0],
                       jnp.tile(nsa_w2_v[0], (1, NSA_HPG)).astype(BF16), blocks)
    kc_cmp = kc_cmp.reshape(batch * NSA_GROUPS, blocks, NSA_WIDTH)
    vc_cmp = vc_cmp.reshape(batch * NSA_GROUPS, blocks, NSA_WIDTH)
    o_nsa = _nsa_attention(proj1, gates, kc_cmp, vc_cmp, batch, seq, n_cmp)
    h = _outproj(h, [o_nsa], nsa_w_out[0].astype(BF16))
    cw = _router(h, ffn_norm[1], _pad_cols(moe_w_router[0]).astype(BF16),
                 _pad_cols(moe_b_router[0].reshape(1, N_EXPERTS)).astype(F32))
    h = _moe_ffn(h, ffn_norm[1], cw, moe_w_gate[0].astype(BF16), moe_w_up[0].astype(BF16),
                 moe_w_down[0].astype(BF16))
    h = _ple(h, ple_norm[1], p[1].reshape(t, -1), ple_gate_w[1].astype(BF16), ple_proj_w[1].astype(BF16),
             final_norm, final=True)
    return h.reshape(batch, seq, D_MODEL)
```

```python
import functools

import jax
import jax.numpy as jnp
from jax import lax
from jax.experimental import pallas as pl
from jax.experimental.pallas import tpu as pltpu

F32 = jnp.float32
BF16 = jnp.bfloat16

D_MODEL = 1024
HEAD_DIM = 64
LANES = 128
FOX_HEADS = 8
DIL_HEADS = 8
FOX_WIDTH = FOX_HEADS * HEAD_DIM
DIL_WIDTH = DIL_HEADS * HEAD_DIM
DIL_PATTERNS = ((128, 1), (512, 4), (2048, 16))
Q_BLOCK = 128
NSA_HEADS = 16
NSA_GROUPS = 4
NSA_HPG = NSA_HEADS // NSA_GROUPS
NSA_KV = NSA_GROUPS * HEAD_DIM
NSA_CMP_LEN = 32
NSA_CMP_STRIDE = 16
NSA_SEL_LEN = 64
NSA_TOP_N = 8
NSA_WINDOW = 512
N_EXPERTS = 8
ROPE_THETA = 10000.0
RMS_EPS = 1e-6
NEG_INF = -1e30
SCALE = HEAD_DIM ** -0.5

VMEM_LIMIT_BYTES = 52 * 1024 * 1024


def _params(*sem):
    return pltpu.CompilerParams(dimension_semantics=sem, vmem_limit_bytes=VMEM_LIMIT_BYTES)


def _rms(x, g):
    return x * lax.rsqrt(jnp.mean(x * x, axis=-1, keepdims=True) + RMS_EPS) * g


def _dot(a, b):
    return jnp.dot(a, b, preferred_element_type=F32)


def _dot_nt(a, b):
    return lax.dot_general(a, b, (((1,), (1,)), ((), ())), preferred_element_type=F32)


def _split3(x):
    hi = x.astype(BF16)
    r = x - hi.astype(F32)
    mid = r.astype(BF16)
    lo = (r - mid.astype(F32)).astype(BF16)
    return hi, mid, lo


def _lane_col(x, idx):
    lane = lax.broadcasted_iota(jnp.int32, x.shape, 1)
    return jnp.sum(jnp.where(lane == idx, x, 0.0), axis=1, keepdims=True)


def _rope_table_kernel(pos_ref, inv_ref, cos_ref, sin_ref):
    ang = pos_ref[...] * inv_ref[...]
    lane = lax.broadcasted_iota(jnp.int32, ang.shape, 1)
    sign = jnp.where((lane & (HEAD_DIM - 1)) < HEAD_DIM // 2, -1.0, 1.0)
    cos_ref[...] = jnp.cos(ang)
    sin_ref[...] = jnp.sin(ang) * sign


def _rope_tables(positions):
    t = positions.size
    half = HEAD_DIM // 2
    inv_freq = ROPE_THETA ** (-jnp.arange(half, dtype=F32) / half)
    inv = jnp.tile(inv_freq, LANES // half).reshape(1, LANES)
    pos = jnp.broadcast_to(positions.astype(F32).reshape(t, 1), (t, LANES))
    tm = 1024
    return pl.pallas_call(
        _rope_table_kernel,
        grid=(t // tm,),
        in_specs=[pl.BlockSpec((tm, LANES), lambda i: (i, 0)),
                  pl.BlockSpec((1, LANES), lambda i: (0, 0))],
        out_specs=[pl.BlockSpec((tm, LANES), lambda i: (i, 0))] * 2,
        out_shape=[jax.ShapeDtypeStruct((t, LANES), F32)] * 2,
        compiler_params=_params("parallel"),
        name="rope_tables",
    )(pos, inv)


def _proj_kernel(h_ref, g_ref, w_ref, wa_ref, ba_ref, cos_ref, sin_ref, o_ref, oa_ref, *, tiles, aux_act):
    xn = _rms(h_ref[...], g_ref[...]).astype(BF16)
    for c0, width, rope in tiles:
        acc = _dot(xn, w_ref[:, c0:c0 + width])
        if rope:
            reps = width // LANES
            cos = jnp.tile(cos_ref[...], (1, reps))
            sin = jnp.tile(sin_ref[...], (1, reps))
            lane = lax.broadcasted_iota(jnp.int32, acc.shape, 1)
            first_half = (lane & (HEAD_DIM - 1)) < HEAD_DIM // 2
            partner = jnp.where(first_half,
                                pltpu.roll(acc, width - HEAD_DIM // 2, 1),
                                pltpu.roll(acc, HEAD_DIM // 2, 1))
            acc = acc * cos + partner * sin
        o_ref[:, c0:c0 + width] = acc
    aux = _dot(xn, wa_ref[...]) + ba_ref[...]
    if aux_act == "log_sigmoid":
        oa_ref[...] = jnp.minimum(aux, 0.0) - jnp.log1p(jnp.exp(-jnp.abs(aux)))
    else:
        oa_ref[...] = jax.nn.sigmoid(aux)


def _proj(h, gain, w, w_aux, b_aux, cos, sin, tiles, aux_act, tm=512):
    t, n = h.shape[0], w.shape[1]
    return pl.pallas_call(
        functools.partial(_proj_kernel, tiles=tiles, aux_act=aux_act),
        grid=(t // tm,),
        in_specs=[pl.BlockSpec((tm, D_MODEL), lambda i: (i, 0)),
                  pl.BlockSpec((1, D_MODEL), lambda i: (0, 0)),
                  pl.BlockSpec((D_MODEL, n), lambda i: (0, 0)),
                  pl.BlockSpec((D_MODEL, LANES), lambda i: (0, 0)),
                  pl.BlockSpec((1, LANES), lambda i: (0, 0)),
                  pl.BlockSpec((tm, LANES), lambda i: (i, 0)),
                  pl.BlockSpec((tm, LANES), lambda i: (i, 0))],
        out_specs=[pl.BlockSpec((tm, n), lambda i: (i, 0)),
                   pl.BlockSpec((tm, LANES), lambda i: (i, 0))],
        out_shape=[jax.ShapeDtypeStruct((t, n), F32), jax.ShapeDtypeStruct((t, LANES), F32)],
        compiler_params=_params("parallel"),
        name="in_proj",
    )(h, gain.reshape(1, D_MODEL), w, w_aux, b_aux, cos, sin)


CUM_BLOCK = 256


def _cumsum_kernel(x_ref, ccol_ref, crow_ref, xrow_ref, carry_ref):
    j = pl.program_id(1)

    @pl.when(j == 0)
    def _():
        carry_ref[...] = jnp.zeros_like(carry_ref)

    x = x_ref[...]
    r = lax.broadcasted_iota(jnp.int32, (CUM_BLOCK, CUM_BLOCK), 0)
    c = lax.broadcasted_iota(jnp.int32, (CUM_BLOCK, CUM_BLOCK), 1)
    tri = jnp.where(r >= c, 1.0, 0.0).astype(BF16)
    hi, mid, lo = _split3(x)
    cum = _dot(tri, hi) + _dot(tri, mid) + _dot(tri, lo) + carry_ref[0:1, :]
    ccol_ref[...] = cum
    crow_ref[0] = cum.T
    xrow_ref[0] = x.T
    carry_ref[...] = jnp.broadcast_to(cum[CUM_BLOCK - 1:CUM_BLOCK, :], carry_ref.shape)


def _cumsum(x, batch, seq):
    nb = seq // CUM_BLOCK
    return pl.pallas_call(
        _cumsum_kernel,
        grid=(batch, nb),
        in_specs=[pl.BlockSpec((CUM_BLOCK, LANES), lambda b, j: (b * nb + j, 0))],
        out_specs=[pl.BlockSpec((CUM_BLOCK, LANES), lambda b, j: (b * nb + j, 0)),
                   pl.BlockSpec((1, LANES, CUM_BLOCK), lambda b, j: (b, 0, j)),
                   pl.BlockSpec((1, LANES, CUM_BLOCK), lambda b, j: (b, 0, j))],
        out_shape=[jax.ShapeDtypeStruct((batch * seq, LANES), F32),
                   jax.ShapeDtypeStruct((batch, LANES, seq), F32),
                   jax.ShapeDtypeStruct((batch, LANES, seq), F32)],
        scratch_shapes=[pltpu.VMEM((8, LANES), F32)],
        compiler_params=_params("parallel", "arbitrary"),
        name="token_cumsum",
    )(x)


def _fox_kernel(q_ref, k_ref, v_ref, ccol_ref, crow_ref, o_ref, *, tq):
    pair = pl.program_id(1)
    i = pl.program_id(2)
    tk = tq
    lane = lax.broadcasted_iota(jnp.int32, (tq, LANES), 1)
    low = lane < HEAD_DIM
    q = q_ref[...] * SCALE
    qs = jnp.concatenate([jnp.where(low, q, 0.0), jnp.where(low, 0.0, q)], axis=0).astype(BF16)
    cc = ccol_ref[...]
    h0 = 2 * pair
    cq = jnp.concatenate([_lane_col(cc, h0), _lane_col(cc, h0 + 1)], axis=0)
    qpos = i * tq + (lax.broadcasted_iota(jnp.int32, (2 * tq, tk), 0) & (tq - 1))
    kcol = lax.broadcasted_iota(jnp.int32, (2 * tq, tk), 1)
    row8 = lax.broadcasted_iota(jnp.int32, (8, tk), 0)

    def body(kt, carry):
        m, l, acc = carry
        ks = pl.multiple_of(kt * tk, tk)
        kb = k_ref[pl.ds(ks, tk), :].astype(BF16)
        vb = v_ref[pl.ds(ks, tk), :].astype(BF16)
        cr8 = crow_ref[0, :, pl.ds(ks, tk)]
        cr0 = jnp.sum(jnp.where(row8 == h0, cr8, 0.0), axis=0, keepdims=True)
        cr1 = jnp.sum(jnp.where(row8 == h0 + 1, cr8, 0.0), axis=0, keepdims=True)
        ck = jnp.concatenate([jnp.broadcast_to(cr0, (tq, tk)), jnp.broadcast_to(cr1, (tq, tk))], axis=0)
        ok2 = (ks + kcol) <= qpos
        s = _dot_nt(qs, kb) + cq - ck
        s = jnp.where(ok2, s, NEG_INF)
        m_new = jnp.maximum(m, jnp.max(s, axis=1, keepdims=True))
        alpha = jnp.exp(m - m_new)
        p = jnp.where(ok2, jnp.exp(s - m_new), 0.0)
        l = alpha * l + jnp.sum(p, axis=1, keepdims=True)
        pv = _dot(p.astype(BF16), vb)
        acc = acc * jnp.where(low, alpha[:tq], alpha[tq:]) + jnp.where(low, pv[:tq], pv[tq:])
        return m_new, l, acc

    init = (jnp.full((2 * tq, 1), NEG_INF, F32), jnp.zeros((2 * tq, 1), F32), jnp.zeros((tq, LANES), F32))
    m, l, acc = lax.fori_loop(0, i + 1, body, init)
    o_ref[...] = acc / jnp.where(low, l[:tq], l[tq:])


def _fox_attention(proj, ccol, crow, batch, seq, tq=256):
    nq = seq // tq
    npair = FOX_WIDTH // LANES
    return pl.pallas_call(
        functools.partial(_fox_kernel, tq=tq),
        grid=(batch, npair, nq),
        in_specs=[pl.BlockSpec((tq, LANES), lambda b, p, i: (b * nq + i, p)),
                  pl.BlockSpec((seq, LANES), lambda b, p, i: (b, npair + p)),
                  pl.BlockSpec((seq, LANES), lambda b, p, i: (b, 2 * npair + p)),
                  pl.BlockSpec((tq, LANES), lambda b, p, i: (b * nq + i, 0)),
                  pl.BlockSpec((1, 8, seq), lambda b, p, i: (b, 0, 0))],
        out_specs=pl.BlockSpec((tq, LANES), lambda b, p, i: (b * nq + i, p)),
        out_shape=jax.ShapeDtypeStruct((batch * seq, FOX_WIDTH), F32),
        compiler_params=_params("parallel", "parallel", "arbitrary"),
        name="fox_attention",
    )(proj, proj, proj, ccol, crow)


def _dil_kernel(q_ref, k_ref, v_ref, o_ref, os_ref, ls_ref, *, seq):
    lane = lax.broadcasted_iota(jnp.int32, (Q_BLOCK, LANES), 1)
    low = lane < HEAD_DIM
    qi = lax.broadcasted_iota(jnp.int32, (2 * Q_BLOCK, 2 * Q_BLOCK), 0) & (Q_BLOCK - 1)
    kj = lax.broadcasted_iota(jnp.int32, (2 * Q_BLOCK, 2 * Q_BLOCK), 1)
    dist = qi + Q_BLOCK - kj

    for pi, (window, dil) in enumerate(DIL_PATTERNS):
        span = window // dil
        nb = (seq // dil) // Q_BLOCK
        band = (dist >= 0) & (dist <= span)

        def rows(start, dil=dil):
            return pl.ds(start, Q_BLOCK, stride=dil) if dil > 1 else pl.ds(start, Q_BLOCK)

        def unit(u, carry, pi=pi, dil=dil, nb=nb, band=band, rows=rows):
            r = u // nb
            blk = u % nb
            cur = r + blk * (Q_BLOCK * dil)
            prev = jnp.maximum(cur - Q_BLOCK * dil, r)
            q = q_ref[rows(cur), :] * SCALE
            qs = jnp.concatenate([jnp.where(low, q, 0.0), jnp.where(low, 0.0, q)], axis=0).astype(BF16)
            kk = jnp.concatenate([k_ref[rows(prev), :], k_ref[rows(cur), :]], axis=0).astype(BF16)
            vv = jnp.concatenate([v_ref[rows(prev), :], v_ref[rows(cur), :]], axis=0).astype(BF16)
            mask2 = band & (kj >= jnp.where(blk > 0, 0, Q_BLOCK))
            s = jnp.where(mask2, _dot_nt(qs, kk), NEG_INF)
            m = jnp.max(s, axis=1, keepdims=True)
            e = jnp.where(mask2, jnp.exp(s - m), 0.0)
            den = jnp.sum(e, axis=1, keepdims=True)
            o2 = _dot((e / den).astype(BF16), vv)
            lse = m + jnp.log(den)
            os_ref[pi, rows(cur), :] = jnp.where(low, o2[:Q_BLOCK], o2[Q_BLOCK:])
            ls_ref[pi, rows(cur), :] = jnp.where(low, lse[:Q_BLOCK], lse[Q_BLOCK:])
            return carry

        lax.fori_loop(0, dil * nb, unit, 0)

    chunk = 256

    def combine(ci, carry):
        rs = pl.ds(pl.multiple_of(ci * chunk, chunk), chunk)
        l0, l1, l2 = ls_ref[0, rs, :], ls_ref[1, rs, :], ls_ref[2, rs, :]
        m = jnp.maximum(jnp.maximum(l0, l1), l2)
        e0, e1, e2 = jnp.exp(l0 - m), jnp.exp(l1 - m), jnp.exp(l2 - m)
        tot = e0 + e1 + e2
        o_ref[rs, :] = ((e0 / tot) * os_ref[0, rs, :] + (e1 / tot) * os_ref[1, rs, :]
                        + (e2 / tot) * os_ref[2, rs, :])
        return carry

    lax.fori_loop(0, seq // chunk, combine, 0)


def _dilated_attention(proj, batch, seq):
    npair = DIL_WIDTH // LANES
    base = 3 * FOX_WIDTH // LANES
    return pl.pallas_call(
        functools.partial(_dil_kernel, seq=seq),
        grid=(batch, npair),
        in_specs=[pl.BlockSpec((seq, LANES), lambda b, p: (b, base + p)),
                  pl.BlockSpec((seq, LANES), lambda b, p: (b, base + npair + p)),
                  pl.BlockSpec((seq, LANES), lambda b, p: (b, base + 2 * npair + p))],
        out_specs=pl.BlockSpec((seq, LANES), lambda b, p: (b, p)),
        out_shape=jax.ShapeDtypeStruct((batch * seq, DIL_WIDTH), F32),
        scratch_shapes=[pltpu.VMEM((3, seq, LANES), F32), pltpu.VMEM((3, seq, LANES), F32)],
        compiler_params=_params("parallel", "parallel"),
        name="dilated_attention",
    )(proj, proj, proj)


def _outproj_kernel(*refs, n_in, tn):
    h_ref = refs[0]
    a_refs = refs[1:1 + n_in]
    w_ref = refs[1 + n_in]
    o_ref = refs[2 + n_in]
    acts = [a[...].astype(BF16) for a in a_refs]
    for c0 in range(0, D_MODEL, tn):
        acc = h_ref[:, c0:c0 + tn]
        k0 = 0
        for a in acts:
            acc = acc + _dot(a, w_ref[k0:k0 + a.shape[1], c0:c0 + tn])
            k0 += a.shape[1]
        o_ref[:, c0:c0 + tn] = acc


def _outproj(h, acts, w, tm=512, tn=512):
    t = h.shape[0]
    return pl.pallas_call(
        functools.partial(_outproj_kernel, n_in=len(acts), tn=tn),
        grid=(t // tm,),
        in_specs=([pl.BlockSpec((tm, D_MODEL), lambda i: (i, 0))]
                  + [pl.BlockSpec((tm, a.shape[1]), lambda i: (i, 0)) for a in acts]
                  + [pl.BlockSpec((D_MODEL, D_MODEL), lambda i: (0, 0))]),
        out_specs=pl.BlockSpec((tm, D_MODEL), lambda i: (i, 0)),
        out_shape=jax.ShapeDtypeStruct((t, D_MODEL), F32),
        compiler_params=_params("parallel"),
        name="out_proj",
    )(h, *acts, w)


def _swiglu_tile(x, wg, wu, wd):
    gate = _dot(x, wg)
    up = _dot(x, wu)
    return _dot((gate * jax.nn.sigmoid(gate) * up).astype(BF16), wd)


def _ffn_kernel(h_ref, g_ref, wg_ref, wu_ref, wd_ref, o_ref, xn_ref, acc_ref, *, n_f):
    f = pl.program_id(1)

    @pl.when(f == 0)
    def _():
        xn_ref[...] = _rms(h_ref[...], g_ref[...]).astype(BF16)
        acc_ref[...] = jnp.zeros_like(acc_ref)

    acc_ref[...] += _swiglu_tile(xn_ref[...], wg_ref[...], wu_ref[...], wd_ref[...])

    @pl.when(f == n_f - 1)
    def _():
        o_ref[...] = h_ref[...] + acc_ref[...]


def _ffn(h, gain, wg, wu, wd, tm=1024, tf=512):
    t, dff = h.shape[0], wg.shape[1]
    n_f = dff // tf
    return pl.pallas_call(
        functools.partial(_ffn_kernel, n_f=n_f),
        grid=(t // tm, n_f),
        in_specs=[pl.BlockSpec((tm, D_MODEL), lambda i, f: (i, 0)),
                  pl.BlockSpec((1, D_MODEL), lambda i, f: (0, 0)),
                  pl.BlockSpec((D_MODEL, tf), lambda i, f: (0, f)),
                  pl.BlockSpec((D_MODEL, tf), lambda i, f: (0, f)),
                  pl.BlockSpec((tf, D_MODEL), lambda i, f: (f, 0))],
        out_specs=pl.BlockSpec((tm, D_MODEL), lambda i, f: (i, 0)),
        out_shape=jax.ShapeDtypeStruct((t, D_MODEL), F32),
        scratch_shapes=[pltpu.VMEM((tm, D_MODEL), BF16), pltpu.VMEM((tm, D_MODEL), F32)],
        compiler_params=_params("parallel", "arbitrary"),
        name="dense_swiglu",
    )(h, gain.reshape(1, D_MODEL), wg, wu, wd)


MOE_CHUNK = 1024
MOE_ROWS = 128
MOE_SCATTER = 256


def _moe_kernel(cnt_ref, h_ref, g_ref, cw_ref, m_ref, rcol_ref, rrow_ref, mrow_ref, wg_ref, wu_ref, wd_ref,
                o_ref, xn_ref, xe_ref, ye_ref, *, n_f):
    c = pl.program_id(0)
    e = pl.program_id(1)
    f = pl.program_id(2)
    chunk = MOE_CHUNK
    n = cnt_ref[c * N_EXPERTS + e]
    n_scatter = (n + MOE_SCATTER - 1) // MOE_SCATTER
    n_tiles = (n + MOE_ROWS - 1) // MOE_ROWS

    @pl.when((e == 0) & (f == 0))
    def _():
        h = h_ref[...]
        xn_ref[...] = _rms(h, g_ref[...]).astype(BF16)
        o_ref[...] = h

    @pl.when(f == 0)
    def _():
        rank = rrow_ref[pl.ds(e, 1), :] * mrow_ref[pl.ds(e, 1), :]
        slot = lax.broadcasted_iota(jnp.int32, (MOE_ROWS, chunk), 0) + 1

        def gather(i, carry):
            rows = pl.ds(pl.multiple_of(i * MOE_ROWS, MOE_ROWS), MOE_ROWS)
            onehot = jnp.where(rank == (slot + i * MOE_ROWS).astype(F32), 1.0, 0.0).astype(BF16)
            xe_ref[rows, :] = _dot(onehot, xn_ref[...]).astype(BF16)
            ye_ref[rows, :] = jnp.zeros((MOE_ROWS, D_MODEL), F32)
            return carry

        lax.fori_loop(0, n_scatter * (MOE_SCATTER // MOE_ROWS), gather, 0)

    def tile(i, carry):
        rows = pl.ds(pl.multiple_of(i * MOE_ROWS, MOE_ROWS), MOE_ROWS)
        ye_ref[rows, :] += _swiglu_tile(xe_ref[rows, :], wg_ref[...], wu_ref[...], wd_ref[...])
        return carry

    lax.fori_loop(0, n_tiles, tile, 0)

    @pl.when(f == n_f - 1)
    def _():
        rank = _lane_col(rcol_ref[...] * m_ref[...], e)
        weight = _lane_col(cw_ref[...], e)
        slot = lax.broadcasted_iota(jnp.int32, (chunk, MOE_SCATTER), 1) + 1

        def scatter(i, carry):
            rows = pl.ds(pl.multiple_of(i * MOE_SCATTER, MOE_SCATTER), MOE_SCATTER)
            onehot = jnp.where(rank == (slot + i * MOE_SCATTER).astype(F32), 1.0, 0.0).astype(BF16)
            o_ref[...] += weight * _dot(onehot, ye_ref[rows, :].astype(BF16))
            return carry

        lax.fori_loop(0, n_scatter, scatter, 0)


def _moe_ffn(h, gain, cw, mask, rcol, rrow, mrow, counts, wg, wu, wd, tf=512):
    t, dff = h.shape[0], wg.shape[2]
    n_f = dff // tf
    chunk = MOE_CHUNK
    tok = lambda width: pl.BlockSpec((chunk, width), lambda c, e, f, cnt: (c, 0))
    lane_major = pl.BlockSpec((None, N_EXPERTS, chunk), lambda c, e, f, cnt: (c, 0, 0))
    grid_spec = pltpu.PrefetchScalarGridSpec(
        num_scalar_prefetch=1,
        grid=(t // chunk, N_EXPERTS, n_f),
        in_specs=[tok(D_MODEL),
                  pl.BlockSpec((1, D_MODEL), lambda c, e, f, cnt: (0, 0)),
                  tok(LANES), tok(LANES), tok(LANES), lane_major, lane_major,
                  pl.BlockSpec((None, D_MODEL, tf), lambda c, e, f, cnt: (e, 0, f)),
                  pl.BlockSpec((None, D_MODEL, tf), lambda c, e, f, cnt: (e, 0, f)),
                  pl.BlockSpec((None, tf, D_MODEL), lambda c, e, f, cnt: (e, f, 0))],
        out_specs=tok(D_MODEL),
        scratch_shapes=[pltpu.VMEM((chunk, D_MODEL), BF16), pltpu.VMEM((chunk, D_MODEL), BF16),
                        pltpu.VMEM((chunk, D_MODEL), F32)],
    )
    return pl.pallas_call(
        functools.partial(_moe_kernel, n_f=n_f),
        grid_spec=grid_spec,
        out_shape=jax.ShapeDtypeStruct((t, D_MODEL), F32),
        compiler_params=_params("parallel", "arbitrary", "arbitrary"),
        name="moe_swiglu",
    )(counts, h, gain.reshape(1, D_MODEL), cw, mask, rcol, rrow, mrow, wg, wu, wd)


def _router_kernel(h_ref, g_ref, w_ref, b_ref, cw_ref, m_ref):
    xn = _rms(h_ref[...], g_ref[...]).astype(BF16)
    logits = _dot(xn, w_ref[...]) + b_ref[...]
    lane = lax.broadcasted_iota(jnp.int32, logits.shape, 1).astype(F32)
    logits = jnp.where(lane < N_EXPERTS, logits, -jnp.inf)
    m1 = jnp.max(logits, axis=1, keepdims=True)
    i1 = jnp.min(jnp.where(logits == m1, lane, float(LANES)), axis=1, keepdims=True)
    rest = jnp.where(lane == i1, -jnp.inf, logits)
    m2 = jnp.max(rest, axis=1, keepdims=True)
    i2 = jnp.min(jnp.where(rest == m2, lane, float(LANES)), axis=1, keepdims=True)
    e2 = jnp.exp(m2 - m1)
    w1 = 1.0 / (1.0 + e2)
    w2 = e2 / (1.0 + e2)
    cw_ref[...] = jnp.where(lane == i1, w1, jnp.where(lane == i2, w2, 0.0))
    m_ref[...] = jnp.where((lane == i1) | (lane == i2), 1.0, 0.0)


def _router(h, gain, w, b, tm=1024):
    t = h.shape[0]
    return pl.pallas_call(
        _router_kernel,
        grid=(t // tm,),
        in_specs=[pl.BlockSpec((tm, D_MODEL), lambda i: (i, 0)),
                  pl.BlockSpec((1, D_MODEL), lambda i: (0, 0)),
                  pl.BlockSpec((D_MODEL, LANES), lambda i: (0, 0)),
                  pl.BlockSpec((1, LANES), lambda i: (0, 0))],
        out_specs=[pl.BlockSpec((tm, LANES), lambda i: (i, 0))] * 2,
        out_shape=[jax.ShapeDtypeStruct((t, LANES), F32)] * 2,
        compiler_params=_params("parallel"),
        name="moe_router",
    )(h, gain.reshape(1, D_MODEL), w, b)


def _ple_kernel(h_ref, g_ref, p_ref, wg_ref, wp_ref, fg_ref, o_ref, *, final, tn):
    h = h_ref[...]
    xn = _rms(h, g_ref[...]).astype(BF16)
    pe = p_ref[...].astype(BF16)
    outs = []
    for c0 in range(0, D_MODEL, tn):
        gate = jax.nn.sigmoid(_dot(xn, wg_ref[:, c0:c0 + tn]))
        outs.append(h[:, c0:c0 + tn] + gate * _dot(pe, wp_ref[:, c0:c0 + tn]))
    new = jnp.concatenate(outs, axis=1)
    o_ref[...] = _rms(new, fg_ref[...]) if final else new


def _ple(h, gain, p, wg, wp, final_gain, final, tm=512, tn=512):
    t, pd = p.shape
    return pl.pallas_call(
        functools.partial(_ple_kernel, final=final, tn=tn),
        grid=(t // tm,),
        in_specs=[pl.BlockSpec((tm, D_MODEL), lambda i: (i, 0)),
                  pl.BlockSpec((1, D_MODEL), lambda i: (0, 0)),
                  pl.BlockSpec((tm, pd), lambda i: (i, 0)),
                  pl.BlockSpec((D_MODEL, D_MODEL), lambda i: (0, 0)),
                  pl.BlockSpec((pd, D_MODEL), lambda i: (0, 0)),
                  pl.BlockSpec((1, D_MODEL), lambda i: (0, 0))],
        out_specs=pl.BlockSpec((tm, D_MODEL), lambda i: (i, 0)),
        out_shape=jax.ShapeDtypeStruct((t, D_MODEL), F32),
        compiler_params=_params("parallel"),
        name="ple",
    )(h, gain.reshape(1, D_MODEL), p, wg, wp, final_gain.reshape(1, D_MODEL))


CMP_ROWS = NSA_CMP_STRIDE * HEAD_DIM


def _compress_kernel(x_ref, pos_ref, w1_ref, b1_ref, w2_ref, o_ref, *, tm, blocks):
    x = x_ref[...]
    first = _dot((x + pos_ref[0:1, :]).astype(BF16), w1_ref[0:CMP_ROWS, :])
    second = _dot((x + pos_ref[1:2, :]).astype(BF16), w1_ref[CMP_ROWS:2 * CMP_ROWS, :])
    hid = first + pltpu.roll(second, tm - 1, 0) + b1_ref[...]
    out = _dot(jax.nn.gelu(hid, approximate=True).astype(BF16), w2_ref[...])
    row = lax.broadcasted_iota(jnp.int32, out.shape, 0)
    o_ref[...] = jnp.where((row & (blocks - 1)) == blocks - 1, 0.0, out)


def _compress(x, pos, w1, b1, w2, blocks, tm=1024):
    rows = x.shape[0]
    hidden = w1.shape[1]
    tm = min(tm, rows)
    assert blocks & (blocks - 1) == 0 and tm % blocks == 0
    return pl.pallas_call(
        functools.partial(_compress_kernel, tm=tm, blocks=blocks),
        grid=(rows // tm,),
        in_specs=[pl.BlockSpec((tm, CMP_ROWS), lambda i: (i, 0)),
                  pl.BlockSpec((2, CMP_ROWS), lambda i: (0, 0)),
                  pl.BlockSpec((2 * CMP_ROWS, hidden), lambda i: (0, 0)),
                  pl.BlockSpec((1, hidden), lambda i: (0, 0)),
                  pl.BlockSpec((hidden, NSA_WIDTH), lambda i: (0, 0))],
        out_specs=pl.BlockSpec((tm, NSA_WIDTH), lambda i: (i, 0)),
        out_shape=jax.ShapeDtypeStruct((rows, NSA_WIDTH), F32),
        compiler_params=_params("parallel"),
        name="nsa_compress",
    )(x, pos.reshape(2, CMP_ROWS), w1, b1.reshape(1, hidden), w2)


NSA_TQ = 128
NSA_TK = 512
NSA_WIDTH = NSA_HPG * HEAD_DIM
LOG_HEAD_DIM = HEAD_DIM.bit_length() - 1
LOG_SEL_LEN = NSA_SEL_LEN.bit_length() - 1


def _nsa_kernel(q_ref, kc_ref, vc_ref, ks_ref, vs_ref, kw_ref, vw_ref, gate_ref, o_ref,
                ksb, kwb, vst, vwt, vct, gt_ref, tmp_ref, *, seq, n_cmp):
    g = pl.program_id(1)
    i = pl.program_id(2)
    tq, tk = NSA_TQ, NSA_TK
    cols = NSA_HPG * tq
    grow = pl.multiple_of(g * HEAD_DIM, HEAD_DIM)

    @pl.when(i == 0)
    def _():
        def fill(ci, carry):
            rs = pl.ds(pl.multiple_of(ci * NSA_WIDTH, NSA_WIDTH), NSA_WIDTH)
            ksb[rs, :] = ks_ref[rs, :].astype(BF16)
            kwb[rs, :] = kw_ref[rs, :].astype(BF16)
            for src, dst in ((vs_ref, vst), (vw_ref, vwt)):
                tmp_ref[...] = src[rs, :].T
                dst[:, rs] = tmp_ref[pl.ds(grow, HEAD_DIM), :].astype(BF16)
            return carry

        lax.fori_loop(0, seq // NSA_WIDTH, fill, 0)
        vct[...] = vc_ref[0].T[0:HEAD_DIM, :].astype(BF16)

    t0 = i * tq
    lane_grp = lax.broadcasted_iota(jnp.int32, (tq, NSA_WIDTH), 1) >> LOG_HEAD_DIM
    q = q_ref[...] * SCALE
    rolled = [q] + [pltpu.roll(q, s * HEAD_DIM, 1) for s in range(1, NSA_HPG)]
    parts = []
    for j in range(NSA_HPG):
        shift = (g - j) & (NSA_HPG - 1)
        moved = jnp.where(shift == 0, rolled[0],
                          jnp.where(shift == 1, rolled[1], jnp.where(shift == 2, rolled[2], rolled[3])))
        parts.append(jnp.where(lane_grp == g, moved, 0.0))
    qs = jnp.concatenate(parts, axis=0).astype(BF16)

    def heads_sum(x):
        out = x[:, 0:tq]
        for j in range(1, NSA_HPG):
            out = out + x[:, j * tq:(j + 1) * tq]
        return out

    def lanes4(x):
        return jnp.concatenate([x] * NSA_HPG, axis=1)

    nrow = lax.broadcasted_iota(jnp.int32, (LANES, cols), 0)
    tcol = t0 + (lax.broadcasted_iota(jnp.int32, (LANES, cols), 1) & (tq - 1))
    valid_c = (nrow * NSA_CMP_STRIDE + NSA_CMP_LEN - 1 <= tcol) & (nrow < n_cmp)
    sc = jnp.where(valid_c, _dot_nt(kc_ref[0].astype(BF16), qs), NEG_INF)
    mc = jnp.max(sc, axis=0, keepdims=True)
    ec = jnp.where(valid_c, jnp.exp(sc - mc), 0.0)
    dc = jnp.sum(ec, axis=0, keepdims=True)
    pc = ec / jnp.where(dc > 0.0, dc, 1.0)
    o_cmp = _dot(vct[...], pc.astype(BF16))
    pc_sum = heads_sum(pc)

    n_sel_blocks = seq // NSA_SEL_LEN
    jrow = lax.broadcasted_iota(jnp.int32, (LANES, LANES), 0)
    ncol = lax.broadcasted_iota(jnp.int32, (LANES, LANES), 1)
    overlap = ((ncol * NSA_CMP_STRIDE < (jrow + 1) * NSA_SEL_LEN)
               & (ncol * NSA_CMP_STRIDE + NSA_CMP_LEN > jrow * NSA_SEL_LEN)
               & (ncol < n_cmp) & (jrow < n_sel_blocks))
    overlap = jnp.where(overlap, 1.0, 0.0).astype(BF16)
    hi, mid, lo = _split3(pc_sum)
    imp = (_dot(overlap, hi) + _dot(overlap, mid) + _dot(overlap, lo))[0:n_sel_blocks]
    blk = lax.broadcasted_iota(jnp.int32, (n_sel_blocks, tq), 0)
    cur = (t0 + lax.broadcasted_iota(jnp.int32, (n_sel_blocks, tq), 1)) >> LOG_SEL_LEN
    forced = (blk == 0) | (blk == cur) | (blk == cur - 1)
    imp = jnp.where(blk > cur, -1.0, jnp.where(forced, 1e6, imp))
    beaten = jnp.zeros((n_sel_blocks, tq), jnp.int32)
    for c in range(n_sel_blocks):
        row = imp[c:c + 1, :]
        wins = (row > imp) | ((row == imp) & (blk > c))
        beaten = beaten + jnp.where(wins, 1, 0)
    sel_bias = jnp.where(beaten < NSA_TOP_N, 0.0, NEG_INF)
    sel_bias = jnp.concatenate([sel_bias, jnp.zeros((LANES - n_sel_blocks, tq), F32)], axis=0).astype(BF16)

    krow = lax.broadcasted_iota(jnp.int32, (tk, tq), 0)
    qlane = t0 + lax.broadcasted_iota(jnp.int32, (tk, tq), 1)
    erow = lax.broadcasted_iota(jnp.int32, (tk, LANES), 0)
    ecol = lax.broadcasted_iota(jnp.int32, (tk, LANES), 1)

    def sel_body(kt, carry):
        m, l, acc = carry
        ks = pl.multiple_of(kt * tk, tk)
        expand = jnp.where(((ks + erow) >> LOG_SEL_LEN) == ecol, 1.0, 0.0).astype(BF16)
        bias = jnp.where(ks + krow <= qlane, _dot(expand, sel_bias), NEG_INF)
        s = _dot_nt(ksb[pl.ds(ks, tk), :], qs) + lanes4(bias)
        m_new = jnp.maximum(m, jnp.max(s, axis=0, keepdims=True))
        alpha = jnp.exp(m - m_new)
        p = jnp.exp(s - m_new)
        l = alpha * l + jnp.sum(p, axis=0, keepdims=True)
        acc = acc * alpha + _dot(vst[:, pl.ds(ks, tk)], p.astype(BF16))
        return m_new, l, acc

    init = (jnp.full((1, cols), NEG_INF, F32), jnp.zeros((1, cols), F32), jnp.zeros((HEAD_DIM, cols), F32))
    _, l_sel, acc_sel = lax.fori_loop(0, (t0 + tq - 1) // tk + 1, sel_body, init)
    o_sel = acc_sel / l_sel

    wk = NSA_WINDOW + tq
    ws = pl.multiple_of(jnp.maximum(t0 - NSA_WINDOW, 0), tq)
    kpos = ws + lax.broadcasted_iota(jnp.int32, (wk, tq), 0)
    qw = t0 + lax.broadcasted_iota(jnp.int32, (wk, tq), 1)
    bias_w = jnp.where((kpos <= qw) & (kpos > qw - NSA_WINDOW), 0.0, NEG_INF)
    sw = _dot_nt(kwb[pl.ds(ws, wk), :], qs) + lanes4(bias_w)
    pw = jnp.exp(sw - jnp.max(sw, axis=0, keepdims=True))
    o_win = _dot(vwt[:, pl.ds(ws, wk)], pw.astype(BF16)) / jnp.sum(pw, axis=0, keepdims=True)

    gt_ref[...] = gate_ref[...].T
    out = jnp.zeros((HEAD_DIM, cols), F32)
    for c, branch in enumerate((o_cmp, o_sel, o_win)):
        gate = jnp.concatenate([gt_ref[pl.ds(c * NSA_HEADS + g * NSA_HPG + j, 1), :] for j in range(NSA_HPG)],
                               axis=1)
        out = out + gate * branch
    o_ref[...] = jnp.concatenate([out[:, j * tq:(j + 1) * tq] for j in range(NSA_HPG)], axis=0).T


def _nsa_attention(proj, gates, kc_cmp, vc_cmp, batch, seq, n_cmp):
    nq = seq // NSA_TQ
    ncb = kc_cmp.shape[1]
    qcols = NSA_HEADS * HEAD_DIM // NSA_WIDTH
    kv = lambda c: pl.BlockSpec((seq, NSA_WIDTH), lambda b, g, i, c=c: (b, qcols + c))
    cmp_spec = pl.BlockSpec((1, ncb, NSA_WIDTH), lambda b, g, i: (b * NSA_GROUPS + g, 0, 0))
    return pl.pallas_call(
        functools.partial(_nsa_kernel, seq=seq, n_cmp=n_cmp),
        grid=(batch, NSA_GROUPS, nq),
        in_specs=[pl.BlockSpec((NSA_TQ, NSA_WIDTH), lambda b, g, i: (b * nq + i, g)),
                  cmp_spec, cmp_spec, kv(2), kv(3), kv(4), kv(5),
                  pl.BlockSpec((NSA_TQ, LANES), lambda b, g, i: (b * nq + i, 0))],
        out_specs=pl.BlockSpec((NSA_TQ, NSA_WIDTH), lambda b, g, i: (b * nq + i, g)),
        out_shape=jax.ShapeDtypeStruct((batch * seq, NSA_HEADS * HEAD_DIM), F32),
        scratch_shapes=[pltpu.VMEM((seq, NSA_WIDTH), BF16), pltpu.VMEM((seq, NSA_WIDTH), BF16),
                        pltpu.VMEM((HEAD_DIM, seq), BF16), pltpu.VMEM((HEAD_DIM, seq), BF16),
                        pltpu.VMEM((HEAD_DIM, ncb), BF16), pltpu.VMEM((LANES, NSA_TQ), F32),
                        pltpu.VMEM((NSA_WIDTH, NSA_WIDTH), F32)],
        compiler_params=_params("parallel", "parallel", "arbitrary"),
        name="nsa_attention",
    )(proj, kc_cmp, vc_cmp, proj, proj, proj, proj, gates)


def _pad_cols(w, width=LANES):
    return jnp.pad(w, ((0, 0), (0, width - w.shape[1])))


def _group_rows(cols, batch, seq):
    x = cols.reshape(batch, seq, NSA_GROUPS, HEAD_DIM).transpose(0, 2, 1, 3)
    return x.reshape(batch * NSA_GROUPS * (seq // NSA_CMP_STRIDE), CMP_ROWS)


def kernel(x, p, positions, mix_norm, ffn_norm, ple_norm, ple_gate_w, ple_proj_w, fd_w_in, fd_forget_b, fd_w_out, dense_w_gate, dense_w_up, dense_w_down, nsa_w_in, nsa_pos_k, nsa_w1_k, nsa_b1_k, nsa_w2_k, nsa_pos_v, nsa_w1_v, nsa_b1_v, nsa_w2_v, nsa_w_out, moe_w_router, moe_b_router, moe_w_gate, moe_w_up, moe_w_down, final_norm):
    batch, seq, _ = x.shape
    t = batch * seq
    h = x.reshape(t, D_MODEL)
    cos, sin = _rope_tables(positions)

    n_main = 3 * FOX_WIDTH + 3 * DIL_WIDTH
    w_in = fd_w_in[0]
    tiles0 = tuple((c, 512, c in (3 * FOX_WIDTH, 3 * FOX_WIDTH + DIL_WIDTH)) for c in range(0, n_main, 512))
    proj0, log_f = _proj(h, mix_norm[0], w_in[:, :n_main].astype(BF16),
                         _pad_cols(w_in[:, n_main:]).astype(BF16),
                         _pad_cols(fd_forget_b[0].reshape(1, FOX_HEADS)).astype(F32),
                         cos, sin, tiles0, "log_sigmoid")
    ccol, crow, _ = _cumsum(log_f, batch, seq)
    o_fox = _fox_attention(proj0, ccol, crow, batch, seq)
    o_dil = _dilated_attention(proj0, batch, seq)
    h = _outproj(h, [o_fox, o_dil], fd_w_out[0].astype(BF16))
    h = _ffn(h, ffn_norm[0], dense_w_gate[0].astype(BF16), dense_w_up[0].astype(BF16),
             dense_w_down[0].astype(BF16))
    h = _ple(h, ple_norm[0], p[0].reshape(t, -1), ple_gate_w[0].astype(BF16), ple_proj_w[0].astype(BF16),
             final_norm, final=False)

    qw = NSA_HEADS * HEAD_DIM
    n_main1 = qw + 6 * NSA_KV
    w_in1 = nsa_w_in[0]
    rope_cols = set(range(0, qw, 256)) | {qw, qw + 2 * NSA_KV, qw + 4 * NSA_KV}
    tiles1 = tuple((c, 256, c in rope_cols) for c in range(0, n_main1, 256))
    w_gate = w_in1[:, n_main1:].reshape(D_MODEL, NSA_HEADS, 3).transpose(0, 2, 1).reshape(D_MODEL, 3 * NSA_HEADS)
    proj1, gates = _proj(h, mix_norm[1], w_in1[:, :n_main1].astype(BF16), _pad_cols(w_gate).astype(BF16),
                         jnp.zeros((1, LANES), F32), cos, sin, tiles1, "sigmoid")
    n_cmp = (seq - NSA_CMP_LEN) // NSA_CMP_STRIDE + 1
    blocks = seq // NSA_CMP_STRIDE
    kc_cmp = _compress(_group_rows(proj1[:, qw:qw + NSA_KV], batch, seq), nsa_pos_k[0],
                       nsa_w1_k[0].astype(BF16), nsa_b1_k[0],
                       jnp.tile(nsa_w2_k[0], (1, NSA_HPG)).astype(BF16), blocks)
    vc_cmp = _compress(_group_rows(proj1[:, qw + NSA_KV:qw + 2 * NSA_KV], batch, seq), nsa_pos_v[0],
                       nsa_w1_v[0].astype(BF16), nsa_b1_v[0],
                       jnp.tile(nsa_w2_v[0], (1, NSA_HPG)).astype(BF16), blocks)
    kc_cmp = kc_cmp.reshape(batch * NSA_GROUPS, blocks, NSA_WIDTH)
    vc_cmp = vc_cmp.reshape(batch * NSA_GROUPS, blocks, NSA_WIDTH)
    o_nsa = _nsa_attention(proj1, gates, kc_cmp, vc_cmp, batch, seq, n_cmp)
    h = _outproj(h, [o_nsa], nsa_w_out[0].astype(BF16))
    cw, routed = _router(h, ffn_norm[1], _pad_cols(moe_w_router[0]).astype(BF16),
                         _pad_cols(moe_b_router[0].reshape(1, N_EXPERTS)).astype(F32))
    n_chunks = t // MOE_CHUNK
    rank_col, rank_row, routed_row = _cumsum(routed, n_chunks, MOE_CHUNK)
    counts = rank_col.reshape(n_chunks, MOE_CHUNK, LANES)[:, -1, :N_EXPERTS].astype(jnp.int32).reshape(-1)
    h = _moe_ffn(h, ffn_norm[1], cw, routed, rank_col, rank_row, routed_row, counts,
                 moe_w_gate[0].astype(BF16), moe_w_up[0].astype(BF16), moe_w_down[0].astype(BF16))
    h = _ple(h, ple_norm[1], p[1].reshape(t, -1), ple_gate_w[1].astype(BF16), ple_proj_w[1].astype(BF16),
             final_norm, final=True)
    return h.reshape(batch, seq, D_MODEL)
```

```python
import functools

import jax
import jax.numpy as jnp
from jax import lax
from jax.experimental import pallas as pl
from jax.experimental.pallas import tpu as pltpu

F32 = jnp.float32
BF16 = jnp.bfloat16

D_MODEL = 1024
HEAD_DIM = 64
LANES = 128
FOX_HEADS = 8
DIL_HEADS = 8
FOX_WIDTH = FOX_HEADS * HEAD_DIM
DIL_WIDTH = DIL_HEADS * HEAD_DIM
DIL_PATTERNS = ((128, 1), (512, 4), (2048, 16))
Q_BLOCK = 128
NSA_HEADS = 16
NSA_GROUPS = 4
NSA_HPG = NSA_HEADS // NSA_GROUPS
NSA_KV = NSA_GROUPS * HEAD_DIM
NSA_CMP_LEN = 32
NSA_CMP_STRIDE = 16
NSA_SEL_LEN = 64
NSA_TOP_N = 8
NSA_WINDOW = 512
N_EXPERTS = 8
ROPE_THETA = 10000.0
RMS_EPS = 1e-6
NEG_INF = -1e30
SCALE = HEAD_DIM ** -0.5
LOG2E = 1.4426950408889634
LN2 = 0.6931471805599453
SCALE2 = SCALE * LOG2E
ONES_ROWS = 16

VMEM_LIMIT_BYTES = 52 * 1024 * 1024


def _params(*sem):
    return pltpu.CompilerParams(dimension_semantics=sem, vmem_limit_bytes=VMEM_LIMIT_BYTES)


def _rms(x, g):
    return x * lax.rsqrt(jnp.mean(x * x, axis=-1, keepdims=True) + RMS_EPS) * g


def _dot(a, b):
    return jnp.dot(a, b, preferred_element_type=F32)


def _dot_nt(a, b):
    return lax.dot_general(a, b, (((1,), (1,)), ((), ())), preferred_element_type=F32)


def _split3(x):
    hi = x.astype(BF16)
    r = x - hi.astype(F32)
    mid = r.astype(BF16)
    lo = (r - mid.astype(F32)).astype(BF16)
    return hi, mid, lo


def _lane_col(x, idx):
    lane = lax.broadcasted_iota(jnp.int32, x.shape, 1)
    return jnp.sum(jnp.where(lane == idx, x, 0.0), axis=1, keepdims=True)


def _rope_table_kernel(pos_ref, inv_ref, cos_ref, sin_ref):
    ang = pos_ref[...] * inv_ref[...]
    lane = lax.broadcasted_iota(jnp.int32, ang.shape, 1)
    sign = jnp.where((lane & (HEAD_DIM - 1)) < HEAD_DIM // 2, -1.0, 1.0)
    cos_ref[...] = jnp.cos(ang)
    sin_ref[...] = jnp.sin(ang) * sign


def _rope_tables(positions):
    t = positions.size
    half = HEAD_DIM // 2
    inv_freq = ROPE_THETA ** (-jnp.arange(half, dtype=F32) / half)
    inv = jnp.tile(inv_freq, LANES // half).reshape(1, LANES)
    pos = jnp.broadcast_to(positions.astype(F32).reshape(t, 1), (t, LANES))
    tm = 1024
    return pl.pallas_call(
        _rope_table_kernel,
        grid=(t // tm,),
        in_specs=[pl.BlockSpec((tm, LANES), lambda i: (i, 0)),
                  pl.BlockSpec((1, LANES), lambda i: (0, 0))],
        out_specs=[pl.BlockSpec((tm, LANES), lambda i: (i, 0))] * 2,
        out_shape=[jax.ShapeDtypeStruct((t, LANES), F32)] * 2,
        compiler_params=_params("parallel"),
        name="rope_tables",
    )(pos, inv)


def _proj_kernel(h_ref, g_ref, w_ref, wa_ref, ba_ref, cos_ref, sin_ref, o_ref, oa_ref, *, tiles, aux_act):
    xn = _rms(h_ref[...], g_ref[...]).astype(BF16)
    for c0, width, rope in tiles:
        acc = _dot(xn, w_ref[:, c0:c0 + width])
        if rope:
            reps = width // LANES
            cos = jnp.tile(cos_ref[...], (1, reps))
            sin = jnp.tile(sin_ref[...], (1, reps))
            lane = lax.broadcasted_iota(jnp.int32, acc.shape, 1)
            first_half = (lane & (HEAD_DIM - 1)) < HEAD_DIM // 2
            partner = jnp.where(first_half,
                                pltpu.roll(acc, width - HEAD_DIM // 2, 1),
                                pltpu.roll(acc, HEAD_DIM // 2, 1))
            acc = acc * cos + partner * sin
        o_ref[:, c0:c0 + width] = acc
    aux = _dot(xn, wa_ref[...]) + ba_ref[...]
    if aux_act == "log_sigmoid":
        oa_ref[...] = jnp.minimum(aux, 0.0) - jnp.log1p(jnp.exp(-jnp.abs(aux)))
    else:
        oa_ref[...] = jax.nn.sigmoid(aux)


def _proj(h, gain, w, w_aux, b_aux, cos, sin, tiles, aux_act, tm=512):
    t, n = h.shape[0], w.shape[1]
    return pl.pallas_call(
        functools.partial(_proj_kernel, tiles=tiles, aux_act=aux_act),
        grid=(t // tm,),
        in_specs=[pl.BlockSpec((tm, D_MODEL), lambda i: (i, 0)),
                  pl.BlockSpec((1, D_MODEL), lambda i: (0, 0)),
                  pl.BlockSpec((D_MODEL, n), lambda i: (0, 0)),
                  pl.BlockSpec((D_MODEL, LANES), lambda i: (0, 0)),
                  pl.BlockSpec((1, LANES), lambda i: (0, 0)),
                  pl.BlockSpec((tm, LANES), lambda i: (i, 0)),
                  pl.BlockSpec((tm, LANES), lambda i: (i, 0))],
        out_specs=[pl.BlockSpec((tm, n), lambda i: (i, 0)),
                   pl.BlockSpec((tm, LANES), lambda i: (i, 0))],
        out_shape=[jax.ShapeDtypeStruct((t, n), F32), jax.ShapeDtypeStruct((t, LANES), F32)],
        compiler_params=_params("parallel"),
        name="in_proj",
    )(h, gain.reshape(1, D_MODEL), w, w_aux, b_aux, cos, sin)


CUM_BLOCK = 512


def _cumsum_kernel(x_ref, ccol_ref, crow_ref, xrow_ref, carry_ref):
    j = pl.program_id(1)

    @pl.when(j == 0)
    def _():
        carry_ref[...] = jnp.zeros_like(carry_ref)

    x = x_ref[...]
    r = lax.broadcasted_iota(jnp.int32, (CUM_BLOCK, CUM_BLOCK), 0)
    c = lax.broadcasted_iota(jnp.int32, (CUM_BLOCK, CUM_BLOCK), 1)
    tri = jnp.where(r >= c, 1.0, 0.0).astype(BF16)
    hi, mid, lo = _split3(x)
    cum = _dot(tri, hi) + _dot(tri, mid) + _dot(tri, lo) + carry_ref[0:1, :]
    ccol_ref[...] = cum
    crow_ref[0] = cum.T
    xrow_ref[0] = x.T
    carry_ref[...] = jnp.broadcast_to(cum[CUM_BLOCK - 1:CUM_BLOCK, :], carry_ref.shape)


def _cumsum(x, batch, seq):
    nb = seq // CUM_BLOCK
    return pl.pallas_call(
        _cumsum_kernel,
        grid=(batch, nb),
        in_specs=[pl.BlockSpec((CUM_BLOCK, LANES), lambda b, j: (b * nb + j, 0))],
        out_specs=[pl.BlockSpec((CUM_BLOCK, LANES), lambda b, j: (b * nb + j, 0)),
                   pl.BlockSpec((1, LANES, CUM_BLOCK), lambda b, j: (b, 0, j)),
                   pl.BlockSpec((1, LANES, CUM_BLOCK), lambda b, j: (b, 0, j))],
        out_shape=[jax.ShapeDtypeStruct((batch * seq, LANES), F32),
                   jax.ShapeDtypeStruct((batch, LANES, seq), F32),
                   jax.ShapeDtypeStruct((batch, LANES, seq), F32)],
        scratch_shapes=[pltpu.VMEM((8, LANES), F32)],
        compiler_params=_params("parallel", "arbitrary"),
        name="token_cumsum",
    )(x)


def _fox_kernel(q_ref, k_ref, v_ref, ccol_ref, crow_ref, o_ref, kb_ref, vt_ref, ck_ref, *, tq, seq):
    pair = pl.program_id(1)
    i = pl.program_id(2)
    tk = tq
    cols = 2 * tq
    h0 = 2 * pair

    @pl.when(i == 0)
    def _():
        def fill(ci, carry):
            rs = pl.ds(pl.multiple_of(ci * tk, tk), tk)
            kb_ref[rs, :] = k_ref[rs, :].astype(BF16)
            vt_ref[0:LANES, rs] = v_ref[rs, :].T.astype(BF16)
            vt_ref[LANES:, rs] = jnp.ones((ONES_ROWS, tk), BF16)
            cc = ccol_ref[rs, :] * LOG2E
            ck_ref[0, rs, :] = jnp.broadcast_to(_lane_col(cc, h0), (tk, LANES))
            ck_ref[1, rs, :] = jnp.broadcast_to(_lane_col(cc, h0 + 1), (tk, LANES))
            return carry

        lax.fori_loop(0, seq // tk, fill, 0)

    t0 = pl.multiple_of(i * tq, tq)
    low = lax.broadcasted_iota(jnp.int32, (tq, LANES), 1) < HEAD_DIM
    q = q_ref[...] * SCALE2
    qs = jnp.concatenate([jnp.where(low, q, 0.0), jnp.where(low, 0.0, q)], axis=0).astype(BF16)
    cq = jnp.concatenate([crow_ref[0, pl.ds(h0, 1), pl.ds(t0, tq)],
                          crow_ref[0, pl.ds(h0 + 1, 1), pl.ds(t0, tq)]], axis=1) * LOG2E
    reps = tq // LANES

    def scores(k0, nk):
        ck = jnp.concatenate([ck_ref[0, k0:k0 + nk, :]] * reps + [ck_ref[1, k0:k0 + nk, :]] * reps, axis=1)
        return _dot_nt(kb_ref[k0:k0 + nk, :], qs) + cq - ck

    krow = lax.broadcasted_iota(jnp.int32, (tk, tq), 0)
    qlane = lax.broadcasted_iota(jnp.int32, (tk, tq), 1)
    causal = jnp.where(krow <= qlane, 0.0, NEG_INF)

    def branch(n):
        below = (n - 1) * tk
        s_diag = scores(below, tk) + jnp.concatenate([causal, causal], axis=1)
        m = jnp.max(s_diag, axis=0, keepdims=True)
        if below:
            s_below = scores(0, below)
            m = jnp.maximum(m, jnp.max(s_below, axis=0, keepdims=True))
            acc = _dot(vt_ref[:, 0:below], jnp.exp2(s_below - m).astype(BF16))
        else:
            acc = 0.0
        acc = acc + _dot(vt_ref[:, below:below + tk], jnp.exp2(s_diag - m).astype(BF16))
        out = acc[0:LANES] / acc[LANES:LANES + 1]
        o_ref[...] = jnp.concatenate([out[0:HEAD_DIM, 0:tq], out[HEAD_DIM:, tq:]], axis=0).T.astype(o_ref.dtype)

    for n in range(1, seq // tq + 1):
        pl.when(i == n - 1)(functools.partial(branch, n))


def _fox_attention(proj, ccol, crow, batch, seq, tq=512):
    nq = seq // tq
    npair = FOX_WIDTH // LANES
    return pl.pallas_call(
        functools.partial(_fox_kernel, tq=tq, seq=seq),
        grid=(batch, npair, nq),
        in_specs=[pl.BlockSpec((tq, LANES), lambda b, p, i: (b * nq + i, p)),
                  pl.BlockSpec((seq, LANES), lambda b, p, i: (b, npair + p)),
                  pl.BlockSpec((seq, LANES), lambda b, p, i: (b, 2 * npair + p)),
                  pl.BlockSpec((seq, LANES), lambda b, p, i: (b, 0)),
                  pl.BlockSpec((1, 8, seq), lambda b, p, i: (b, 0, 0))],
        out_specs=pl.BlockSpec((tq, LANES), lambda b, p, i: (b * nq + i, p)),
        out_shape=jax.ShapeDtypeStruct((batch * seq, FOX_WIDTH), BF16),
        scratch_shapes=[pltpu.VMEM((seq, LANES), BF16), pltpu.VMEM((LANES + ONES_ROWS, seq), BF16),
                        pltpu.VMEM((2, seq, LANES), F32)],
        compiler_params=_params("parallel", "parallel", "arbitrary"),
        name="fox_attention",
    )(proj, proj, proj, ccol, crow)


def _dil_kernel(q_ref, k_ref, v_ref, o_ref, os_ref, ls_ref, st_ref, *, seq):
    low = lax.broadcasted_iota(jnp.int32, (Q_BLOCK, LANES), 1) < HEAD_DIM
    kr = lax.broadcasted_iota(jnp.int32, (2 * Q_BLOCK, Q_BLOCK), 0)
    qc = lax.broadcasted_iota(jnp.int32, (2 * Q_BLOCK, Q_BLOCK), 1)
    dist = qc + Q_BLOCK - kr
    group = 8

    for pi, (window, dil) in enumerate(DIL_PATTERNS):
        span = window // dil
        nb = (seq // dil) // Q_BLOCK
        band_bias = jnp.where((dist >= 0) & (dist <= span), 0.0, NEG_INF)
        first_bias = jnp.where(kr >= Q_BLOCK, band_bias, NEG_INF)

        def rows(start, dil=dil):
            return pl.ds(start, Q_BLOCK, stride=dil) if dil > 1 else pl.ds(start, Q_BLOCK)

        def unit(u, pi=pi, dil=dil, nb=nb, band_bias=band_bias, first_bias=first_bias, rows=rows):
            r = u // nb
            blk = u % nb
            cur = r + blk * (Q_BLOCK * dil)
            q = q_ref[rows(cur), :] * SCALE2
            qs = jnp.concatenate([jnp.where(low, q, 0.0), jnp.where(low, 0.0, q)], axis=0).astype(BF16)
            if nb == 1:
                kk = k_ref[rows(cur), :].astype(BF16)
                vt = v_ref[rows(cur), :].T.astype(BF16)
                bias = band_bias[Q_BLOCK:]
            else:
                prev = jnp.maximum(cur - Q_BLOCK * dil, r)
                kk = jnp.concatenate([k_ref[rows(prev), :], k_ref[rows(cur), :]], axis=0).astype(BF16)
                vt = jnp.concatenate([v_ref[rows(prev), :].T, v_ref[rows(cur), :].T], axis=1).astype(BF16)
                bias = jnp.where(blk > 0, band_bias, first_bias)
            vt = jnp.concatenate([vt, jnp.ones((ONES_ROWS, vt.shape[1]), BF16)], axis=0)
            s = _dot_nt(kk, qs) + jnp.concatenate([bias, bias], axis=1)
            return s, vt

        def softmax(s):
            m = jnp.max(s, axis=0, keepdims=True)
            return jnp.exp2(s - m).astype(BF16), m

        def finish(vt, p, m):
            ot = _dot(vt, p)
            den = ot[LANES:LANES + 1]
            ot = ot[0:LANES] * (1.0 / den)
            lse = m * LN2 + jnp.log(den)
            lse_t = jnp.concatenate([jnp.broadcast_to(lse[:, 0:Q_BLOCK], (HEAD_DIM, Q_BLOCK)),
                                     jnp.broadcast_to(lse[:, Q_BLOCK:], (HEAD_DIM, Q_BLOCK))], axis=0)
            return jnp.concatenate([ot[0:HEAD_DIM, 0:Q_BLOCK], ot[HEAD_DIM:, Q_BLOCK:]], axis=0).T, lse_t.T

        def units(gi, carry, pi=pi, dil=dil, unit=unit, softmax=softmax, finish=finish):
            scored = [unit(gi * group + j) for j in range(group)]
            probs = [softmax(s) for s, _ in scored]
            outs = [finish(vt, p, lse) for (_, vt), (p, lse) in zip(scored, probs)]
            rs = pl.ds(pl.multiple_of(gi * (group * Q_BLOCK), group * Q_BLOCK), group * Q_BLOCK)
            dst_o, dst_l = (os_ref.at[pi], ls_ref.at[pi]) if dil == 1 else (st_ref.at[0], st_ref.at[1])
            dst_o[rs, :] = jnp.concatenate([o for o, _ in outs], axis=0)
            dst_l[rs, :] = jnp.concatenate([l for _, l in outs], axis=0)
            return carry

        lax.fori_loop(0, dil * nb // group, units, 0)

        if dil > 1:
            n = seq // dil

            def unstage(r, carry, pi=pi, dil=dil, n=n):
                src = pl.ds(pl.multiple_of(r * n, n), n)
                os_ref[pi, pl.ds(r, n, stride=dil), :] = st_ref[0, src, :]
                ls_ref[pi, pl.ds(r, n, stride=dil), :] = st_ref[1, src, :]
                return carry

            lax.fori_loop(0, dil, unstage, 0)

    chunk = 256

    def combine(ci, carry):
        rs = pl.ds(pl.multiple_of(ci * chunk, chunk), chunk)
        l0, l1, l2 = ls_ref[0, rs, :], ls_ref[1, rs, :], ls_ref[2, rs, :]
        m = jnp.maximum(jnp.maximum(l0, l1), l2)
        e0, e1, e2 = jnp.exp(l0 - m), jnp.exp(l1 - m), jnp.exp(l2 - m)
        tot = e0 + e1 + e2
        o_ref[rs, :] = ((e0 / tot) * os_ref[0, rs, :] + (e1 / tot) * os_ref[1, rs, :]
                        + (e2 / tot) * os_ref[2, rs, :]).astype(o_ref.dtype)
        return carry

    lax.fori_loop(0, seq // chunk, combine, 0)


def _dilated_attention(proj, batch, seq):
    npair = DIL_WIDTH // LANES
    base = 3 * FOX_WIDTH // LANES
    return pl.pallas_call(
        functools.partial(_dil_kernel, seq=seq),
        grid=(batch, npair),
        in_specs=[pl.BlockSpec((seq, LANES), lambda b, p: (b, base + p)),
                  pl.BlockSpec((seq, LANES), lambda b, p: (b, base + npair + p)),
                  pl.BlockSpec((seq, LANES), lambda b, p: (b, base + 2 * npair + p))],
        out_specs=pl.BlockSpec((seq, LANES), lambda b, p: (b, p)),
        out_shape=jax.ShapeDtypeStruct((batch * seq, DIL_WIDTH), BF16),
        scratch_shapes=[pltpu.VMEM((3, seq, LANES), F32), pltpu.VMEM((3, seq, LANES), F32),
                        pltpu.VMEM((2, seq, LANES), F32)],
        compiler_params=_params("parallel", "parallel"),
        name="dilated_attention",
    )(proj, proj, proj)


def _outproj_kernel(*refs, n_in, tn):
    h_ref = refs[0]
    a_refs = refs[1:1 + n_in]
    w_ref = refs[1 + n_in]
    o_ref = refs[2 + n_in]
    acts = [a[...].astype(BF16) for a in a_refs]
    for c0 in range(0, D_MODEL, tn):
        acc = h_ref[:, c0:c0 + tn]
        k0 = 0
        for a in acts:
            acc = acc + _dot(a, w_ref[k0:k0 + a.shape[1], c0:c0 + tn])
            k0 += a.shape[1]
        o_ref[:, c0:c0 + tn] = acc


def _outproj(h, acts, w, tm=512, tn=512):
    t = h.shape[0]
    return pl.pallas_call(
        functools.partial(_outproj_kernel, n_in=len(acts), tn=tn),
        grid=(t // tm,),
        in_specs=([pl.BlockSpec((tm, D_MODEL), lambda i: (i, 0))]
                  + [pl.BlockSpec((tm, a.shape[1]), lambda i: (i, 0)) for a in acts]
                  + [pl.BlockSpec((D_MODEL, D_MODEL), lambda i: (0, 0))]),
        out_specs=pl.BlockSpec((tm, D_MODEL), lambda i: (i, 0)),
        out_shape=jax.ShapeDtypeStruct((t, D_MODEL), F32),
        compiler_params=_params("parallel"),
        name="out_proj",
    )(h, *acts, w)


def _swiglu_tile(x, wg, wu, wd):
    gate = _dot(x, wg)
    up = _dot(x, wu)
    return _dot((gate * jax.nn.sigmoid(gate) * up).astype(BF16), wd)


def _ffn_kernel(h_ref, g_ref, wg_ref, wu_ref, wd_ref, o_ref, xn_ref, acc_ref, *, n_f):
    f = pl.program_id(1)

    @pl.when(f == 0)
    def _():
        xn_ref[...] = _rms(h_ref[...], g_ref[...]).astype(BF16)
        acc_ref[...] = jnp.zeros_like(acc_ref)

    acc_ref[...] += _swiglu_tile(xn_ref[...], wg_ref[...], wu_ref[...], wd_ref[...])

    @pl.when(f == n_f - 1)
    def _():
        o_ref[...] = h_ref[...] + acc_ref[...]


def _ffn(h, gain, wg, wu, wd, tm=512, tf=1792):
    t, dff = h.shape[0], wg.shape[1]
    n_f = dff // tf
    return pl.pallas_call(
        functools.partial(_ffn_kernel, n_f=n_f),
        grid=(t // tm, n_f),
        in_specs=[pl.BlockSpec((tm, D_MODEL), lambda i, f: (i, 0)),
                  pl.BlockSpec((1, D_MODEL), lambda i, f: (0, 0)),
                  pl.BlockSpec((D_MODEL, tf), lambda i, f: (0, f)),
                  pl.BlockSpec((D_MODEL, tf), lambda i, f: (0, f)),
                  pl.BlockSpec((tf, D_MODEL), lambda i, f: (f, 0))],
        out_specs=pl.BlockSpec((tm, D_MODEL), lambda i, f: (i, 0)),
        out_shape=jax.ShapeDtypeStruct((t, D_MODEL), F32),
        scratch_shapes=[pltpu.VMEM((tm, D_MODEL), BF16), pltpu.VMEM((tm, D_MODEL), F32)],
        compiler_params=_params("parallel", "arbitrary"),
        name="dense_swiglu",
    )(h, gain.reshape(1, D_MODEL), wg, wu, wd)


MOE_CHUNK = 1024
MOE_ROWS = 128
MOE_SCATTER = 256


def _moe_kernel(cnt_ref, h_ref, g_ref, cw_ref, m_ref, rcol_ref, rrow_ref, mrow_ref, wg_ref, wu_ref, wd_ref,
                o_ref, xn_ref, xe_ref, ye_ref, *, n_f):
    c = pl.program_id(0)
    e = pl.program_id(1)
    f = pl.program_id(2)
    chunk = MOE_CHUNK
    n = cnt_ref[c * N_EXPERTS + e]
    n_scatter = (n + MOE_SCATTER - 1) // MOE_SCATTER
    n_tiles = (n + MOE_ROWS - 1) // MOE_ROWS

    @pl.when((e == 0) & (f == 0))
    def _():
        h = h_ref[...]
        xn_ref[...] = _rms(h, g_ref[...]).astype(BF16)
        o_ref[...] = h

    @pl.when(f == 0)
    def _():
        rank = rrow_ref[pl.ds(e, 1), :] * mrow_ref[pl.ds(e, 1), :]
        slot = lax.broadcasted_iota(jnp.int32, (MOE_ROWS, chunk), 0) + 1

        def gather(i, carry):
            rows = pl.ds(pl.multiple_of(i * MOE_ROWS, MOE_ROWS), MOE_ROWS)
            onehot = jnp.where(rank == (slot + i * MOE_ROWS).astype(F32), 1.0, 0.0).astype(BF16)
            xe_ref[rows, :] = _dot(onehot, xn_ref[...]).astype(BF16)
            ye_ref[rows, :] = jnp.zeros((MOE_ROWS, D_MODEL), F32)
            return carry

        def clear(i, carry):
            rows = pl.ds(pl.multiple_of(i * MOE_ROWS, MOE_ROWS), MOE_ROWS)
            ye_ref[rows, :] = jnp.zeros((MOE_ROWS, D_MODEL), F32)
            return carry

        lax.fori_loop(0, n_tiles, gather, 0)
        lax.fori_loop(n_tiles, n_scatter * (MOE_SCATTER // MOE_ROWS), clear, 0)

    def tile(i, carry):
        rows = pl.ds(pl.multiple_of(i * MOE_ROWS, MOE_ROWS), MOE_ROWS)
        ye_ref[rows, :] += _swiglu_tile(xe_ref[rows, :], wg_ref[...], wu_ref[...], wd_ref[...])
        return carry

    lax.fori_loop(0, n_tiles, tile, 0)

    @pl.when(f == n_f - 1)
    def _():
        rank = _lane_col(rcol_ref[...] * m_ref[...], e)
        weight = _lane_col(cw_ref[...], e)
        slot = lax.broadcasted_iota(jnp.int32, (chunk, MOE_SCATTER), 1) + 1

        def scatter(i, carry):
            rows = pl.ds(pl.multiple_of(i * MOE_SCATTER, MOE_SCATTER), MOE_SCATTER)
            onehot = jnp.where(rank == (slot + i * MOE_SCATTER).astype(F32), 1.0, 0.0).astype(BF16)
            o_ref[...] += weight * _dot(onehot, ye_ref[rows, :].astype(BF16))
            return carry

        lax.fori_loop(0, n_scatter, scatter, 0)


def _moe_ffn(h, gain, cw, mask, rcol, rrow, mrow, counts, wg, wu, wd, tf=1792):
    t, dff = h.shape[0], wg.shape[2]
    n_f = dff // tf
    chunk = MOE_CHUNK
    tok = lambda width: pl.BlockSpec((chunk, width), lambda c, e, f, cnt: (c, 0))
    lane_major = pl.BlockSpec((None, N_EXPERTS, chunk), lambda c, e, f, cnt: (c, 0, 0))
    grid_spec = pltpu.PrefetchScalarGridSpec(
        num_scalar_prefetch=1,
        grid=(t // chunk, N_EXPERTS, n_f),
        in_specs=[tok(D_MODEL),
                  pl.BlockSpec((1, D_MODEL), lambda c, e, f, cnt: (0, 0)),
                  tok(LANES), tok(LANES), tok(LANES), lane_major, lane_major,
                  pl.BlockSpec((None, D_MODEL, tf), lambda c, e, f, cnt: (e, 0, f)),
                  pl.BlockSpec((None, D_MODEL, tf), lambda c, e, f, cnt: (e, 0, f)),
                  pl.BlockSpec((None, tf, D_MODEL), lambda c, e, f, cnt: (e, f, 0))],
        out_specs=tok(D_MODEL),
        scratch_shapes=[pltpu.VMEM((chunk, D_MODEL), BF16), pltpu.VMEM((chunk, D_MODEL), BF16),
                        pltpu.VMEM((chunk, D_MODEL), F32)],
    )
    return pl.pallas_call(
        functools.partial(_moe_kernel, n_f=n_f),
        grid_spec=grid_spec,
        out_shape=jax.ShapeDtypeStruct((t, D_MODEL), F32),
        compiler_params=_params("parallel", "arbitrary", "arbitrary"),
        name="moe_swiglu",
    )(counts, h, gain.reshape(1, D_MODEL), cw, mask, rcol, rrow, mrow, wg, wu, wd)


def _router_kernel(h_ref, g_ref, w_ref, b_ref, cw_ref, m_ref):
    xn = _rms(h_ref[...], g_ref[...]).astype(BF16)
    logits = _dot(xn, w_ref[...]) + b_ref[...]
    lane = lax.broadcasted_iota(jnp.int32, logits.shape, 1).astype(F32)
    logits = jnp.where(lane < N_EXPERTS, logits, -jnp.inf)
    m1 = jnp.max(logits, axis=1, keepdims=True)
    i1 = jnp.min(jnp.where(logits == m1, lane, float(LANES)), axis=1, keepdims=True)
    rest = jnp.where(lane == i1, -jnp.inf, logits)
    m2 = jnp.max(rest, axis=1, keepdims=True)
    i2 = jnp.min(jnp.where(rest == m2, lane, float(LANES)), axis=1, keepdims=True)
    e2 = jnp.exp(m2 - m1)
    w1 = 1.0 / (1.0 + e2)
    w2 = e2 / (1.0 + e2)
    cw_ref[...] = jnp.where(lane == i1, w1, jnp.where(lane == i2, w2, 0.0))
    m_ref[...] = jnp.where((lane == i1) | (lane == i2), 1.0, 0.0)


def _router(h, gain, w, b, tm=1024):
    t = h.shape[0]
    return pl.pallas_call(
        _router_kernel,
        grid=(t // tm,),
        in_specs=[pl.BlockSpec((tm, D_MODEL), lambda i: (i, 0)),
                  pl.BlockSpec((1, D_MODEL), lambda i: (0, 0)),
                  pl.BlockSpec((D_MODEL, LANES), lambda i: (0, 0)),
                  pl.BlockSpec((1, LANES), lambda i: (0, 0))],
        out_specs=[pl.BlockSpec((tm, LANES), lambda i: (i, 0))] * 2,
        out_shape=[jax.ShapeDtypeStruct((t, LANES), F32)] * 2,
        compiler_params=_params("parallel"),
        name="moe_router",
    )(h, gain.reshape(1, D_MODEL), w, b)


def _ple_kernel(h_ref, g_ref, p_ref, wg_ref, wp_ref, fg_ref, o_ref, *, final, tn):
    h = h_ref[...]
    xn = _rms(h, g_ref[...]).astype(BF16)
    pe = p_ref[...].astype(BF16)
    outs = []
    for c0 in range(0, D_MODEL, tn):
        gate = jax.nn.sigmoid(_dot(xn, wg_ref[:, c0:c0 + tn]))
        outs.append(h[:, c0:c0 + tn] + gate * _dot(pe, wp_ref[:, c0:c0 + tn]))
    new = jnp.concatenate(outs, axis=1)
    o_ref[...] = _rms(new, fg_ref[...]) if final else new


def _ple(h, gain, p, wg, wp, final_gain, final, tm=512, tn=512):
    t, pd = p.shape
    return pl.pallas_call(
        functools.partial(_ple_kernel, final=final, tn=tn),
        grid=(t // tm,),
        in_specs=[pl.BlockSpec((tm, D_MODEL), lambda i: (i, 0)),
                  pl.BlockSpec((1, D_MODEL), lambda i: (0, 0)),
                  pl.BlockSpec((tm, pd), lambda i: (i, 0)),
                  pl.BlockSpec((D_MODEL, D_MODEL), lambda i: (0, 0)),
                  pl.BlockSpec((pd, D_MODEL), lambda i: (0, 0)),
                  pl.BlockSpec((1, D_MODEL), lambda i: (0, 0))],
        out_specs=pl.BlockSpec((tm, D_MODEL), lambda i: (i, 0)),
        out_shape=jax.ShapeDtypeStruct((t, D_MODEL), F32),
        compiler_params=_params("parallel"),
        name="ple",
    )(h, gain.reshape(1, D_MODEL), p, wg, wp, final_gain.reshape(1, D_MODEL))


def _compress_kernel(x01_ref, x23_ref, pos_ref, w1_ref, b1_ref, w2_ref, o_ref, *, blocks):
    stride = NSA_CMP_STRIDE
    hidden = b1_ref.shape[1]
    low = lax.broadcasted_iota(jnp.int32, (blocks, LANES), 1) < HEAD_DIM
    first = [jnp.zeros((blocks, hidden), F32) for _ in range(NSA_GROUPS)]
    second = [jnp.zeros((blocks, hidden), F32) for _ in range(NSA_GROUPS)]
    for j in range(stride):
        for half, x_ref in enumerate((x01_ref, x23_ref)):
            xs = x_ref[pl.ds(j, blocks, stride=stride), :]
            xa = xs + pos_ref[j:j + 1, :]
            xb = xs + pos_ref[stride + j:stride + j + 1, :]
            for sub in range(2):
                g = 2 * half + sub
                keep = low if sub == 0 else jnp.logical_not(low)
                first[g] = first[g] + _dot(jnp.where(keep, xa, 0.0).astype(BF16), w1_ref[j])
                second[g] = second[g] + _dot(jnp.where(keep, xb, 0.0).astype(BF16), w1_ref[stride + j])
    row = lax.broadcasted_iota(jnp.int32, (blocks, NSA_WIDTH), 0)
    for g in range(NSA_GROUPS):
        hid = first[g] + pltpu.roll(second[g], blocks - 1, 0) + b1_ref[...]
        out = _dot(jax.nn.gelu(hid, approximate=True).astype(BF16), w2_ref[...])
        o_ref[g] = jnp.where(row == blocks - 1, 0.0, out)


def _compress(proj, col_block, pos, w1, b1, w2, batch, seq):
    blocks = seq // NSA_CMP_STRIDE
    hidden = w1.shape[1]
    w1_rep = jnp.tile(w1.reshape(NSA_CMP_LEN, HEAD_DIM, hidden), (1, 2, 1))
    pos_rep = jnp.tile(pos, (1, 2))
    halves = NSA_WIDTH // LANES
    return pl.pallas_call(
        functools.partial(_compress_kernel, blocks=blocks),
        grid=(batch,),
        in_specs=[pl.BlockSpec((seq, LANES), lambda b: (b, halves * col_block)),
                  pl.BlockSpec((seq, LANES), lambda b: (b, halves * col_block + 1)),
                  pl.BlockSpec((NSA_CMP_LEN, LANES), lambda b: (0, 0)),
                  pl.BlockSpec((NSA_CMP_LEN, LANES, hidden), lambda b: (0, 0, 0)),
                  pl.BlockSpec((1, hidden), lambda b: (0, 0)),
                  pl.BlockSpec((hidden, NSA_WIDTH), lambda b: (0, 0))],
        out_specs=pl.BlockSpec((NSA_GROUPS, blocks, NSA_WIDTH), lambda b: (b, 0, 0)),
        out_shape=jax.ShapeDtypeStruct((batch * NSA_GROUPS, blocks, NSA_WIDTH), F32),
        compiler_params=_params("parallel"),
        name="nsa_compress",
    )(proj, proj, pos_rep, w1_rep.astype(BF16), b1.reshape(1, hidden), w2)


NSA_TQ = 256
NSA_TK = 512
NSA_WIDTH = NSA_HPG * HEAD_DIM
LOG_HEAD_DIM = HEAD_DIM.bit_length() - 1
LOG_SEL_LEN = NSA_SEL_LEN.bit_length() - 1


def _nsa_kernel(q_ref, kc_ref, vc_ref, ks_ref, vs_ref, kw_ref, vw_ref, gate_ref, o_ref,
                ksb, kwb, vst_all, vwt_all, vct, gt_ref, osel_ref, *, seq, n_cmp):
    g = pl.program_id(1)
    i = pl.program_id(2)
    tq, tk = NSA_TQ, NSA_TK
    cols = NSA_HPG * tq
    vrows = HEAD_DIM + ONES_ROWS
    grow = pl.multiple_of(g * vrows, vrows)

    @pl.when((i == 0) & (g == 0))
    def _():
        def fill(ci, carry):
            rs = pl.ds(pl.multiple_of(ci * NSA_WIDTH, NSA_WIDTH), NSA_WIDTH)
            ksb[rs, :] = ks_ref[rs, :].astype(BF16)
            kwb[rs, :] = kw_ref[rs, :].astype(BF16)
            for src, dst in ((vs_ref, vst_all), (vw_ref, vwt_all)):
                vt = src[rs, :].T.astype(BF16)
                for grp in range(NSA_GROUPS):
                    dst[grp * vrows:grp * vrows + HEAD_DIM, rs] = vt[grp * HEAD_DIM:(grp + 1) * HEAD_DIM]
                    dst[grp * vrows + HEAD_DIM:(grp + 1) * vrows, rs] = jnp.ones((ONES_ROWS, NSA_WIDTH), BF16)
            return carry

        lax.fori_loop(0, seq // NSA_WIDTH, fill, 0)

    @pl.when(i == 0)
    def _():
        vct[...] = vc_ref[0].T[0:HEAD_DIM, :].astype(BF16)

    vst = vst_all.at[pl.ds(grow, vrows)]
    vwt = vwt_all.at[pl.ds(grow, vrows)]

    t0 = i * tq
    lane_grp = lax.broadcasted_iota(jnp.int32, (tq, NSA_WIDTH), 1) >> LOG_HEAD_DIM
    q = q_ref[...] * SCALE2
    rolled = [q] + [pltpu.roll(q, s * HEAD_DIM, 1) for s in range(1, NSA_HPG)]
    parts = []
    for j in range(NSA_HPG):
        shift = (g - j) & (NSA_HPG - 1)
        moved = jnp.where(shift == 0, rolled[0],
                          jnp.where(shift == 1, rolled[1], jnp.where(shift == 2, rolled[2], rolled[3])))
        parts.append(jnp.where(lane_grp == g, moved, 0.0))
    qs = jnp.concatenate(parts, axis=0).astype(BF16)

    def heads_sum(x):
        out = x[:, 0:tq]
        for j in range(1, NSA_HPG):
            out = out + x[:, j * tq:(j + 1) * tq]
        return out

    def lanes4(x):
        return jnp.concatenate([x] * NSA_HPG, axis=1)

    wk = NSA_WINDOW + tq
    ws = pl.multiple_of(jnp.maximum(t0 - NSA_WINDOW, 0), tq)
    kpos = ws + lax.broadcasted_iota(jnp.int32, (wk, tq), 0)
    qw = t0 + lax.broadcasted_iota(jnp.int32, (wk, tq), 1)
    bias_w = jnp.where((kpos <= qw) & (kpos > qw - NSA_WINDOW), 0.0, NEG_INF)
    sw = _dot_nt(kwb[pl.ds(ws, wk), :], qs) + lanes4(bias_w)

    nrow = lax.broadcasted_iota(jnp.int32, (LANES, cols), 0)
    tcol = t0 + (lax.broadcasted_iota(jnp.int32, (LANES, cols), 1) & (tq - 1))
    valid_c = (nrow * NSA_CMP_STRIDE + NSA_CMP_LEN - 1 <= tcol) & (nrow < n_cmp)
    sc = jnp.where(valid_c, _dot_nt(kc_ref[0].astype(BF16), qs), NEG_INF)
    mc = jnp.max(sc, axis=0, keepdims=True)
    ec = jnp.where(valid_c, jnp.exp2(sc - mc), 0.0)
    dc = jnp.sum(ec, axis=0, keepdims=True)
    pc = ec / jnp.where(dc > 0.0, dc, 1.0)
    o_cmp = _dot(vct[...], pc.astype(BF16))
    pc_sum = heads_sum(pc)

    n_sel_blocks = seq // NSA_SEL_LEN
    jrow = lax.broadcasted_iota(jnp.int32, (LANES, LANES), 0)
    ncol = lax.broadcasted_iota(jnp.int32, (LANES, LANES), 1)
    overlap = ((ncol * NSA_CMP_STRIDE < (jrow + 1) * NSA_SEL_LEN)
               & (ncol * NSA_CMP_STRIDE + NSA_CMP_LEN > jrow * NSA_SEL_LEN)
               & (ncol < n_cmp) & (jrow < n_sel_blocks))
    overlap = jnp.where(overlap, 1.0, 0.0).astype(BF16)
    hi, mid, lo = _split3(pc_sum)
    imp = (_dot(overlap, hi) + _dot(overlap, mid) + _dot(overlap, lo))[0:n_sel_blocks]
    pw = jnp.exp2(sw - jnp.max(sw, axis=0, keepdims=True))
    o_win = _dot(vwt[:, pl.ds(ws, wk)], pw.astype(BF16))
    o_win = o_win[0:HEAD_DIM] / o_win[HEAD_DIM:HEAD_DIM + 1]
    blk = lax.broadcasted_iota(jnp.int32, (n_sel_blocks, tq), 0)
    cur = (t0 + lax.broadcasted_iota(jnp.int32, (n_sel_blocks, tq), 1)) >> LOG_SEL_LEN
    forced = (blk == 0) | (blk == cur) | (blk == cur - 1)
    imp = jnp.where(blk > cur, -1.0, jnp.where(forced, 1e6, imp))
    beaten = jnp.zeros((n_sel_blocks, tq), jnp.int32)
    for c in range(n_sel_blocks):
        row = imp[c:c + 1, :]
        wins = (row > imp) | ((row == imp) & (blk > c))
        beaten = beaten + jnp.where(wins, 1, 0)
    sel_bias = jnp.where(beaten < NSA_TOP_N, 0.0, NEG_INF)
    sel_bias = jnp.concatenate([sel_bias, jnp.zeros((LANES - n_sel_blocks, tq), F32)], axis=0).astype(BF16)

    n_tiles = (t0 + tq - 1) // tk + 1

    def sel_branch(n):
        nk = n * tk
        erow = lax.broadcasted_iota(jnp.int32, (nk, LANES), 0)
        ecol = lax.broadcasted_iota(jnp.int32, (nk, LANES), 1)
        expand = jnp.where((erow >> LOG_SEL_LEN) == ecol, 1.0, 0.0).astype(BF16)
        krow = lax.broadcasted_iota(jnp.int32, (nk, tq), 0)
        qlane = t0 + lax.broadcasted_iota(jnp.int32, (nk, tq), 1)
        bias = jnp.where(krow <= qlane, _dot(expand, sel_bias), NEG_INF)
        s = _dot_nt(ksb[0:nk, :], qs) + lanes4(bias)
        p = jnp.exp2(s - jnp.max(s, axis=0, keepdims=True))
        acc = _dot(vst[:, 0:nk], p.astype(BF16))
        osel_ref[...] = acc[0:HEAD_DIM] / acc[HEAD_DIM:HEAD_DIM + 1]

    for n in range(1, seq // tk + 1):
        pl.when(n_tiles == n)(functools.partial(sel_branch, n))
    o_sel = osel_ref[...]

    gt_ref[...] = gate_ref[...].T
    out = jnp.zeros((HEAD_DIM, cols), F32)
    for c, branch in enumerate((o_cmp, o_sel, o_win)):
        gate = jnp.concatenate([gt_ref[pl.ds(c * NSA_HEADS + g * NSA_HPG + j, 1), :] for j in range(NSA_HPG)],
                               axis=1)
        out = out + gate * branch
    o_ref[...] = jnp.concatenate([out[:, j * tq:(j + 1) * tq] for j in range(NSA_HPG)], axis=0).T.astype(o_ref.dtype)


def _nsa_attention(proj, gates, kc_cmp, vc_cmp, batch, seq, n_cmp):
    nq = seq // NSA_TQ
    ncb = kc_cmp.shape[1]
    qcols = NSA_HEADS * HEAD_DIM // NSA_WIDTH
    kv = lambda c: pl.BlockSpec((seq, NSA_WIDTH), lambda b, g, i, c=c: (b, qcols + c))
    cmp_spec = pl.BlockSpec((1, ncb, NSA_WIDTH), lambda b, g, i: (b * NSA_GROUPS + g, 0, 0))
    return pl.pallas_call(
        functools.partial(_nsa_kernel, seq=seq, n_cmp=n_cmp),
        grid=(batch, NSA_GROUPS, nq),
        in_specs=[pl.BlockSpec((NSA_TQ, NSA_WIDTH), lambda b, g, i: (b * nq + i, g)),
                  cmp_spec, cmp_spec, kv(2), kv(3), kv(4), kv(5),
                  pl.BlockSpec((NSA_TQ, LANES), lambda b, g, i: (b * nq + i, 0))],
        out_specs=pl.BlockSpec((NSA_TQ, NSA_WIDTH), lambda b, g, i: (b * nq + i, g)),
        out_shape=jax.ShapeDtypeStruct((batch * seq, NSA_HEADS * HEAD_DIM), BF16),
        scratch_shapes=[pltpu.VMEM((seq, NSA_WIDTH), BF16), pltpu.VMEM((seq, NSA_WIDTH), BF16),
                        pltpu.VMEM((NSA_GROUPS * (HEAD_DIM + ONES_ROWS), seq), BF16),
                        pltpu.VMEM((NSA_GROUPS * (HEAD_DIM + ONES_ROWS), seq), BF16),
                        pltpu.VMEM((HEAD_DIM, ncb), BF16), pltpu.VMEM((LANES, NSA_TQ), F32),
                        pltpu.VMEM((HEAD_DIM, NSA_HPG * NSA_TQ), F32)],
        compiler_params=_params("parallel", "arbitrary", "arbitrary"),
        name="nsa_attention",
    )(proj, kc_cmp, vc_cmp, proj, proj, proj, proj, gates)


def _pad_cols(w, width=LANES):
    return jnp.pad(w, ((0, 0), (0, width - w.shape[1])))


def kernel(x, p, positions, mix_norm, ffn_norm, ple_norm, ple_gate_w, ple_proj_w, fd_w_in, fd_forget_b, fd_w_out, dense_w_gate, dense_w_up, dense_w_down, nsa_w_in, nsa_pos_k, nsa_w1_k, nsa_b1_k, nsa_w2_k, nsa_pos_v, nsa_w1_v, nsa_b1_v, nsa_w2_v, nsa_w_out, moe_w_router, moe_b_router, moe_w_gate, moe_w_up, moe_w_down, final_norm):
    batch, seq, _ = x.shape
    t = batch * seq
    h = x.reshape(t, D_MODEL)
    cos, sin = _rope_tables(positions)

    n_main = 3 * FOX_WIDTH + 3 * DIL_WIDTH
    w_in = fd_w_in[0]
    tiles0 = tuple((c, 512, c in (3 * FOX_WIDTH, 3 * FOX_WIDTH + DIL_WIDTH)) for c in range(0, n_main, 512))
    proj0, log_f = _proj(h, mix_norm[0], w_in[:, :n_main].astype(BF16),
                         _pad_cols(w_in[:, n_main:]).astype(BF16),
                         _pad_cols(fd_forget_b[0].reshape(1, FOX_HEADS)).astype(F32),
                         cos, sin, tiles0, "log_sigmoid")
    ccol, crow, _ = _cumsum(log_f, batch, seq)
    o_fox = _fox_attention(proj0, ccol, crow, batch, seq)
    o_dil = _dilated_attention(proj0, batch, seq)
    h = _outproj(h, [o_fox, o_dil], fd_w_out[0].astype(BF16))
    h = _ffn(h, ffn_norm[0], dense_w_gate[0].astype(BF16), dense_w_up[0].astype(BF16),
             dense_w_down[0].astype(BF16))
    h = _ple(h, ple_norm[0], p[0].reshape(t, -1), ple_gate_w[0].astype(BF16), ple_proj_w[0].astype(BF16),
             final_norm, final=False)

    qw = NSA_HEADS * HEAD_DIM
    n_main1 = qw + 6 * NSA_KV
    w_in1 = nsa_w_in[0]
    rope_cols = set(range(0, qw, 256)) | {qw, qw + 2 * NSA_KV, qw + 4 * NSA_KV}
    tiles1 = tuple((c, 256, c in rope_cols) for c in range(0, n_main1, 256))
    w_gate = w_in1[:, n_main1:].reshape(D_MODEL, NSA_HEADS, 3).transpose(0, 2, 1).reshape(D_MODEL, 3 * NSA_HEADS)
    proj1, gates = _proj(h, mix_norm[1], w_in1[:, :n_main1].astype(BF16), _pad_cols(w_gate).astype(BF16),
                         jnp.zeros((1, LANES), F32), cos, sin, tiles1, "sigmoid")
    n_cmp = (seq - NSA_CMP_LEN) // NSA_CMP_STRIDE + 1
    kc_cmp = _compress(proj1, qw // NSA_WIDTH, nsa_pos_k[0], nsa_w1_k[0], nsa_b1_k[0],
                       jnp.tile(nsa_w2_k[0], (1, NSA_HPG)).astype(BF16), batch, seq)
    vc_cmp = _compress(proj1, qw // NSA_WIDTH + 1, nsa_pos_v[0], nsa_w1_v[0], nsa_b1_v[0],
                       jnp.tile(nsa_w2_v[0], (1, NSA_HPG)).astype(BF16), batch, seq)
    o_nsa = _nsa_attention(proj1, gates, kc_cmp, vc_cmp, batch, seq, n_cmp)
    h = _outproj(h, [o_nsa], nsa_w_out[0].astype(BF16))
    cw, routed = _router(h, ffn_norm[1], _pad_cols(moe_w_router[0]).astype(BF16),
                         _pad_cols(moe_b_router[0].reshape(1, N_EXPERTS)).astype(F32))
    n_chunks = t // MOE_CHUNK
    rank_col, rank_row, routed_row = _cumsum(routed, n_chunks, MOE_CHUNK)
    counts = rank_col.reshape(n_chunks, MOE_CHUNK, LANES)[:, -1, :N_EXPERTS].astype(jnp.int32).reshape(-1)
    h = _moe_ffn(h, ffn_norm[1], cw, routed, rank_col, rank_row, routed_row, counts,
                 moe_w_gate[0].astype(BF16), moe_w_up[0].astype(BF16), moe_w_down[0].astype(BF16))
    h = _ple(h, ple_norm[1], p[1].reshape(t, -1), ple_gate_w[1].astype(BF16), ple_proj_w[1].astype(BF16),
             final_norm, final=True)
    return h.reshape(batch, seq, D_MODEL)
```

```python
import functools

import jax
import jax.numpy as jnp
from jax import lax
from jax.experimental import pallas as pl
from jax.experimental.pallas import tpu as pltpu

F32 = jnp.float32
BF16 = jnp.bfloat16

D_MODEL = 1024
HEAD_DIM = 64
LANES = 128
FOX_HEADS = 8
DIL_HEADS = 8
FOX_WIDTH = FOX_HEADS * HEAD_DIM
DIL_WIDTH = DIL_HEADS * HEAD_DIM
DIL_PATTERNS = ((128, 1), (512, 4), (2048, 16))
Q_BLOCK = 128
NSA_HEADS = 16
NSA_GROUPS = 4
NSA_HPG = NSA_HEADS // NSA_GROUPS
NSA_KV = NSA_GROUPS * HEAD_DIM
NSA_CMP_LEN = 32
NSA_CMP_STRIDE = 16
NSA_SEL_LEN = 64
NSA_TOP_N = 8
NSA_WINDOW = 512
N_EXPERTS = 8
ROPE_THETA = 10000.0
RMS_EPS = 1e-6
NEG_INF = -1e30
SCALE = HEAD_DIM ** -0.5
LOG2E = 1.4426950408889634
LN2 = 0.6931471805599453
SCALE2 = SCALE * LOG2E
ONES_ROWS = 16

VMEM_LIMIT_BYTES = 52 * 1024 * 1024


def _params(*sem):
    return pltpu.CompilerParams(dimension_semantics=sem, vmem_limit_bytes=VMEM_LIMIT_BYTES)


def _rms(x, g):
    return x * lax.rsqrt(jnp.mean(x * x, axis=-1, keepdims=True) + RMS_EPS) * g


def _dot(a, b):
    return jnp.dot(a, b, preferred_element_type=F32)


def _dot_nt(a, b):
    return lax.dot_general(a, b, (((1,), (1,)), ((), ())), preferred_element_type=F32)


def _split3(x):
    hi = x.astype(BF16)
    r = x - hi.astype(F32)
    mid = r.astype(BF16)
    lo = (r - mid.astype(F32)).astype(BF16)
    return hi, mid, lo


def _lane_col(x, idx):
    lane = lax.broadcasted_iota(jnp.int32, x.shape, 1)
    return jnp.sum(jnp.where(lane == idx, x, 0.0), axis=1, keepdims=True)


def _rope_table_kernel(pos_ref, inv_ref, cos_ref, sin_ref):
    ang = pos_ref[...] * inv_ref[...]
    lane = lax.broadcasted_iota(jnp.int32, ang.shape, 1)
    sign = jnp.where((lane & (HEAD_DIM - 1)) < HEAD_DIM // 2, -1.0, 1.0)
    cos_ref[...] = jnp.cos(ang)
    sin_ref[...] = jnp.sin(ang) * sign


def _rope_tables(positions):
    t = positions.size
    half = HEAD_DIM // 2
    inv_freq = ROPE_THETA ** (-jnp.arange(half, dtype=F32) / half)
    inv = jnp.tile(inv_freq, LANES // half).reshape(1, LANES)
    pos = jnp.broadcast_to(positions.astype(F32).reshape(t, 1), (t, LANES))
    tm = 1024
    return pl.pallas_call(
        _rope_table_kernel,
        grid=(t // tm,),
        in_specs=[pl.BlockSpec((tm, LANES), lambda i: (i, 0)),
                  pl.BlockSpec((1, LANES), lambda i: (0, 0))],
        out_specs=[pl.BlockSpec((tm, LANES), lambda i: (i, 0))] * 2,
        out_shape=[jax.ShapeDtypeStruct((t, LANES), F32)] * 2,
        compiler_params=_params("parallel"),
        name="rope_tables",
    )(pos, inv)


def _proj_kernel(h_ref, g_ref, w_ref, wa_ref, ba_ref, cos_ref, sin_ref, o_ref, oa_ref, *, tiles, aux_act):
    xn = _rms(h_ref[...], g_ref[...]).astype(BF16)
    for c0, width, rope in tiles:
        acc = _dot(xn, w_ref[:, c0:c0 + width])
        if rope:
            reps = width // LANES
            cos = jnp.tile(cos_ref[...], (1, reps))
            sin = jnp.tile(sin_ref[...], (1, reps))
            lane = lax.broadcasted_iota(jnp.int32, acc.shape, 1)
            first_half = (lane & (HEAD_DIM - 1)) < HEAD_DIM // 2
            partner = jnp.where(first_half,
                                pltpu.roll(acc, width - HEAD_DIM // 2, 1),
                                pltpu.roll(acc, HEAD_DIM // 2, 1))
            acc = acc * cos + partner * sin
        o_ref[:, c0:c0 + width] = acc
    aux = _dot(xn, wa_ref[...]) + ba_ref[...]
    if aux_act == "log_sigmoid":
        oa_ref[...] = jnp.minimum(aux, 0.0) - jnp.log1p(jnp.exp(-jnp.abs(aux)))
    else:
        oa_ref[...] = jax.nn.sigmoid(aux)


def _proj(h, gain, w, w_aux, b_aux, cos, sin, tiles, aux_act, tm=512):
    t, n = h.shape[0], w.shape[1]
    return pl.pallas_call(
        functools.partial(_proj_kernel, tiles=tiles, aux_act=aux_act),
        grid=(t // tm,),
        in_specs=[pl.BlockSpec((tm, D_MODEL), lambda i: (i, 0)),
                  pl.BlockSpec((1, D_MODEL), lambda i: (0, 0)),
                  pl.BlockSpec((D_MODEL, n), lambda i: (0, 0)),
                  pl.BlockSpec((D_MODEL, LANES), lambda i: (0, 0)),
                  pl.BlockSpec((1, LANES), lambda i: (0, 0)),
                  pl.BlockSpec((tm, LANES), lambda i: (i, 0)),
                  pl.BlockSpec((tm, LANES), lambda i: (i, 0))],
        out_specs=[pl.BlockSpec((tm, n), lambda i: (i, 0)),
                   pl.BlockSpec((tm, LANES), lambda i: (i, 0))],
        out_shape=[jax.ShapeDtypeStruct((t, n), F32), jax.ShapeDtypeStruct((t, LANES), F32)],
        compiler_params=_params("parallel"),
        name="in_proj",
    )(h, gain.reshape(1, D_MODEL), w, w_aux, b_aux, cos, sin)


CUM_BLOCK = 512


def _cumsum_kernel(x_ref, ccol_ref, crow_ref, xrow_ref, carry_ref):
    j = pl.program_id(1)

    @pl.when(j == 0)
    def _():
        carry_ref[...] = jnp.zeros_like(carry_ref)

    x = x_ref[...]
    r = lax.broadcasted_iota(jnp.int32, (CUM_BLOCK, CUM_BLOCK), 0)
    c = lax.broadcasted_iota(jnp.int32, (CUM_BLOCK, CUM_BLOCK), 1)
    tri = jnp.where(r >= c, 1.0, 0.0).astype(BF16)
    hi, mid, lo = _split3(x)
    cum = _dot(tri, hi) + _dot(tri, mid) + _dot(tri, lo) + carry_ref[0:1, :]
    ccol_ref[...] = cum
    crow_ref[0] = cum.T
    xrow_ref[0] = x.T
    carry_ref[...] = jnp.broadcast_to(cum[CUM_BLOCK - 1:CUM_BLOCK, :], carry_ref.shape)


def _cumsum(x, batch, seq):
    nb = seq // CUM_BLOCK
    return pl.pallas_call(
        _cumsum_kernel,
        grid=(batch, nb),
        in_specs=[pl.BlockSpec((CUM_BLOCK, LANES), lambda b, j: (b * nb + j, 0))],
        out_specs=[pl.BlockSpec((CUM_BLOCK, LANES), lambda b, j: (b * nb + j, 0)),
                   pl.BlockSpec((1, LANES, CUM_BLOCK), lambda b, j: (b, 0, j)),
                   pl.BlockSpec((1, LANES, CUM_BLOCK), lambda b, j: (b, 0, j))],
        out_shape=[jax.ShapeDtypeStruct((batch * seq, LANES), F32),
                   jax.ShapeDtypeStruct((batch, LANES, seq), F32),
                   jax.ShapeDtypeStruct((batch, LANES, seq), F32)],
        scratch_shapes=[pltpu.VMEM((8, LANES), F32)],
        compiler_params=_params("parallel", "arbitrary"),
        name="token_cumsum",
    )(x)


def _fox_kernel(q_ref, k_ref, v_ref, ccol_ref, crow_ref, o_ref, kb_ref, vt_ref, ck_ref, *, tq, seq):
    pair = pl.program_id(1)
    i = pl.program_id(2)
    tk = tq
    cols = 2 * tq
    h0 = 2 * pair

    @pl.when(i == 0)
    def _():
        def fill(ci, carry):
            rs = pl.ds(pl.multiple_of(ci * tk, tk), tk)
            kb_ref[rs, :] = k_ref[rs, :].astype(BF16)
            vt_ref[0:LANES, rs] = v_ref[rs, :].T.astype(BF16)
            vt_ref[LANES:, rs] = jnp.ones((ONES_ROWS, tk), BF16)
            cc = ccol_ref[rs, :] * LOG2E
            ck_ref[0, rs, :] = jnp.broadcast_to(_lane_col(cc, h0), (tk, LANES))
            ck_ref[1, rs, :] = jnp.broadcast_to(_lane_col(cc, h0 + 1), (tk, LANES))
            return carry

        lax.fori_loop(0, seq // tk, fill, 0)

    t0 = pl.multiple_of(i * tq, tq)
    low = lax.broadcasted_iota(jnp.int32, (tq, LANES), 1) < HEAD_DIM
    q = q_ref[...] * SCALE2
    qs = jnp.concatenate([jnp.where(low, q, 0.0), jnp.where(low, 0.0, q)], axis=0).astype(BF16)
    cq = jnp.concatenate([crow_ref[0, pl.ds(h0, 1), pl.ds(t0, tq)],
                          crow_ref[0, pl.ds(h0 + 1, 1), pl.ds(t0, tq)]], axis=1) * LOG2E
    reps = tq // LANES

    def scores(k0, nk):
        ck = jnp.concatenate([ck_ref[0, k0:k0 + nk, :]] * reps + [ck_ref[1, k0:k0 + nk, :]] * reps, axis=1)
        return _dot_nt(kb_ref[k0:k0 + nk, :], qs) + cq - ck

    krow = lax.broadcasted_iota(jnp.int32, (tk, tq), 0)
    qlane = lax.broadcasted_iota(jnp.int32, (tk, tq), 1)
    causal = jnp.where(krow <= qlane, 0.0, NEG_INF)

    def branch(n):
        below = (n - 1) * tk
        s_diag = scores(below, tk) + jnp.concatenate([causal, causal], axis=1)
        m = jnp.max(s_diag, axis=0, keepdims=True)
        if below:
            s_below = scores(0, below)
            m = jnp.maximum(m, jnp.max(s_below, axis=0, keepdims=True))
            acc = _dot(vt_ref[:, 0:below], jnp.exp2(s_below - m).astype(BF16))
        else:
            acc = 0.0
        acc = acc + _dot(vt_ref[:, below:below + tk], jnp.exp2(s_diag - m).astype(BF16))
        out = acc[0:LANES] / acc[LANES:LANES + 1]
        o_ref[...] = jnp.concatenate([out[0:HEAD_DIM, 0:tq], out[HEAD_DIM:, tq:]], axis=0).T.astype(o_ref.dtype)

    for n in range(1, seq // tq + 1):
        pl.when(i == n - 1)(functools.partial(branch, n))


def _fox_attention(proj, ccol, crow, batch, seq, tq=512):
    nq = seq // tq
    npair = FOX_WIDTH // LANES
    return pl.pallas_call(
        functools.partial(_fox_kernel, tq=tq, seq=seq),
        grid=(batch, npair, nq),
        in_specs=[pl.BlockSpec((tq, LANES), lambda b, p, i: (b * nq + i, p)),
                  pl.BlockSpec((seq, LANES), lambda b, p, i: (b, npair + p)),
                  pl.BlockSpec((seq, LANES), lambda b, p, i: (b, 2 * npair + p)),
                  pl.BlockSpec((seq, LANES), lambda b, p, i: (b, 0)),
                  pl.BlockSpec((1, 8, seq), lambda b, p, i: (b, 0, 0))],
        out_specs=pl.BlockSpec((tq, LANES), lambda b, p, i: (b * nq + i, p)),
        out_shape=jax.ShapeDtypeStruct((batch * seq, FOX_WIDTH), BF16),
        scratch_shapes=[pltpu.VMEM((seq, LANES), BF16), pltpu.VMEM((LANES + ONES_ROWS, seq), BF16),
                        pltpu.VMEM((2, seq, LANES), F32)],
        compiler_params=_params("parallel", "parallel", "arbitrary"),
        name="fox_attention",
    )(proj, proj, proj, ccol, crow)


def _dil_kernel(q_ref, k_ref, v_ref, o_ref, os_ref, ls_ref, *, seq):
    low = lax.broadcasted_iota(jnp.int32, (Q_BLOCK, LANES), 1) < HEAD_DIM
    kr = lax.broadcasted_iota(jnp.int32, (2 * Q_BLOCK, Q_BLOCK), 0)
    qc = lax.broadcasted_iota(jnp.int32, (2 * Q_BLOCK, Q_BLOCK), 1)
    dist = qc + Q_BLOCK - kr
    group = 8

    for pi, (window, dil) in enumerate(DIL_PATTERNS):
        span = window // dil
        nb = (seq // dil) // Q_BLOCK
        band_bias = jnp.where((dist >= 0) & (dist <= span), 0.0, NEG_INF)
        first_bias = jnp.where(kr >= Q_BLOCK, band_bias, NEG_INF)

        def rows(start, dil=dil):
            return pl.ds(start, Q_BLOCK, stride=dil) if dil > 1 else pl.ds(start, Q_BLOCK)

        def unit(u, pi=pi, dil=dil, nb=nb, band_bias=band_bias, first_bias=first_bias, rows=rows):
            r = u // nb
            blk = u % nb
            cur = r + blk * (Q_BLOCK * dil)
            q = q_ref[rows(cur), :] * SCALE2
            qs = jnp.concatenate([jnp.where(low, q, 0.0), jnp.where(low, 0.0, q)], axis=0).astype(BF16)
            if nb == 1:
                kk = k_ref[rows(cur), :].astype(BF16)
                vt = v_ref[rows(cur), :].T.astype(BF16)
                bias = band_bias[Q_BLOCK:]
            else:
                prev = jnp.maximum(cur - Q_BLOCK * dil, r)
                kk = jnp.concatenate([k_ref[rows(prev), :], k_ref[rows(cur), :]], axis=0).astype(BF16)
                vt = jnp.concatenate([v_ref[rows(prev), :].T, v_ref[rows(cur), :].T], axis=1).astype(BF16)
                bias = jnp.where(blk > 0, band_bias, first_bias)
            vt = jnp.concatenate([vt, jnp.ones((ONES_ROWS, vt.shape[1]), BF16)], axis=0)
            s = _dot_nt(kk, qs) + jnp.concatenate([bias, bias], axis=1)
            return s, vt

        def softmax(s):
            m = jnp.max(s, axis=0, keepdims=True)
            return jnp.exp2(s - m).astype(BF16), m

        def finish(vt, p, m):
            ot = _dot(vt, p)
            den = ot[LANES:LANES + 1]
            ot = ot[0:LANES] * (1.0 / den)
            lse = m * LN2 + jnp.log(den)
            lse_t = jnp.concatenate([jnp.broadcast_to(lse[:, 0:Q_BLOCK], (HEAD_DIM, Q_BLOCK)),
                                     jnp.broadcast_to(lse[:, Q_BLOCK:], (HEAD_DIM, Q_BLOCK))], axis=0)
            return jnp.concatenate([ot[0:HEAD_DIM, 0:Q_BLOCK], ot[HEAD_DIM:, Q_BLOCK:]], axis=0).T, lse_t.T

        def units(gi, carry, pi=pi, nb=nb, dil=dil, rows=rows, unit=unit, softmax=softmax, finish=finish):
            ids = [gi * group + j for j in range(group)]
            scored = [unit(u) for u in ids]
            probs = [softmax(s) for s, _ in scored]
            outs = [finish(vt, p, lse) for (_, vt), (p, lse) in zip(scored, probs)]
            for u, (o, l) in zip(ids, outs):
                cur = u // nb + (u % nb) * (Q_BLOCK * dil)
                os_ref[pi, rows(cur), :] = o
                ls_ref[pi, rows(cur), :] = l
            return carry

        lax.fori_loop(0, dil * nb // group, units, 0)

    chunk = 256

    def combine(ci, carry):
        rs = pl.ds(pl.multiple_of(ci * chunk, chunk), chunk)
        l0, l1, l2 = ls_ref[0, rs, :], ls_ref[1, rs, :], ls_ref[2, rs, :]
        m = jnp.maximum(jnp.maximum(l0, l1), l2)
        e0, e1, e2 = jnp.exp(l0 - m), jnp.exp(l1 - m), jnp.exp(l2 - m)
        tot = e0 + e1 + e2
        o_ref[rs, :] = ((e0 / tot) * os_ref[0, rs, :] + (e1 / tot) * os_ref[1, rs, :]
                        + (e2 / tot) * os_ref[2, rs, :]).astype(o_ref.dtype)
        return carry

    lax.fori_loop(0, seq // chunk, combine, 0)


def _dilated_attention(proj, batch, seq):
    npair = DIL_WIDTH // LANES
    base = 3 * FOX_WIDTH // LANES
    return pl.pallas_call(
        functools.partial(_dil_kernel, seq=seq),
        grid=(batch, npair),
        in_specs=[pl.BlockSpec((seq, LANES), lambda b, p: (b, base + p)),
                  pl.BlockSpec((seq, LANES), lambda b, p: (b, base + npair + p)),
                  pl.BlockSpec((seq, LANES), lambda b, p: (b, base + 2 * npair + p))],
        out_specs=pl.BlockSpec((seq, LANES), lambda b, p: (b, p)),
        out_shape=jax.ShapeDtypeStruct((batch * seq, DIL_WIDTH), BF16),
        scratch_shapes=[pltpu.VMEM((3, seq, LANES), F32), pltpu.VMEM((3, seq, LANES), F32)],
        compiler_params=_params("parallel", "parallel"),
        name="dilated_attention",
    )(proj, proj, proj)


def _outproj_kernel(*refs, n_in, tn):
    h_ref = refs[0]
    a_refs = refs[1:1 + n_in]
    w_ref = refs[1 + n_in]
    o_ref = refs[2 + n_in]
    acts = [a[...].astype(BF16) for a in a_refs]
    for c0 in range(0, D_MODEL, tn):
        acc = h_ref[:, c0:c0 + tn]
        k0 = 0
        for a in acts:
            acc = acc + _dot(a, w_ref[k0:k0 + a.shape[1], c0:c0 + tn])
            k0 += a.shape[1]
        o_ref[:, c0:c0 + tn] = acc


def _outproj(h, acts, w, tm=1024, tn=512):
    t = h.shape[0]
    return pl.pallas_call(
        functools.partial(_outproj_kernel, n_in=len(acts), tn=tn),
        grid=(t // tm,),
        in_specs=([pl.BlockSpec((tm, D_MODEL), lambda i: (i, 0))]
                  + [pl.BlockSpec((tm, a.shape[1]), lambda i: (i, 0)) for a in acts]
                  + [pl.BlockSpec((D_MODEL, D_MODEL), lambda i: (0, 0))]),
        out_specs=pl.BlockSpec((tm, D_MODEL), lambda i: (i, 0)),
        out_shape=jax.ShapeDtypeStruct((t, D_MODEL), F32),
        compiler_params=_params("parallel"),
        name="out_proj",
    )(h, *acts, w)


def _swiglu_tile(x, wg, wu, wd):
    gate = _dot(x, wg)
    up = _dot(x, wu)
    return _dot((gate * jax.nn.sigmoid(gate) * up).astype(BF16), wd)


def _ffn_kernel(h_ref, g_ref, wg_ref, wu_ref, wd_ref, o_ref, xn_ref, acc_ref, *, n_f):
    f = pl.program_id(1)

    @pl.when(f == 0)
    def _():
        xn_ref[...] = _rms(h_ref[...], g_ref[...]).astype(BF16)
        acc_ref[...] = jnp.zeros_like(acc_ref)

    acc_ref[...] += _swiglu_tile(xn_ref[...], wg_ref[...], wu_ref[...], wd_ref[...])

    @pl.when(f == n_f - 1)
    def _():
        o_ref[...] = h_ref[...] + acc_ref[...]


def _ffn(h, gain, wg, wu, wd, tm=512, tf=1792):
    t, dff = h.shape[0], wg.shape[1]
    n_f = dff // tf
    return pl.pallas_call(
        functools.partial(_ffn_kernel, n_f=n_f),
        grid=(t // tm, n_f),
        in_specs=[pl.BlockSpec((tm, D_MODEL), lambda i, f: (i, 0)),
                  pl.BlockSpec((1, D_MODEL), lambda i, f: (0, 0)),
                  pl.BlockSpec((D_MODEL, tf), lambda i, f: (0, f)),
                  pl.BlockSpec((D_MODEL, tf), lambda i, f: (0, f)),
                  pl.BlockSpec((tf, D_MODEL), lambda i, f: (f, 0))],
        out_specs=pl.BlockSpec((tm, D_MODEL), lambda i, f: (i, 0)),
        out_shape=jax.ShapeDtypeStruct((t, D_MODEL), F32),
        scratch_shapes=[pltpu.VMEM((tm, D_MODEL), BF16), pltpu.VMEM((tm, D_MODEL), F32)],
        compiler_params=_params("parallel", "arbitrary"),
        name="dense_swiglu",
    )(h, gain.reshape(1, D_MODEL), wg, wu, wd)


MOE_CHUNK = 1024
MOE_ROWS = 128
MOE_SCATTER = 256


def _moe_kernel(cnt_ref, h_ref, g_ref, cw_ref, m_ref, rcol_ref, rrow_ref, mrow_ref, wg_ref, wu_ref, wd_ref,
                o_ref, xn_ref, xe_ref, ye_ref, *, n_f):
    c = pl.program_id(0)
    e = pl.program_id(1)
    f = pl.program_id(2)
    chunk = MOE_CHUNK
    n = cnt_ref[c * N_EXPERTS + e]
    n_scatter = (n + MOE_SCATTER - 1) // MOE_SCATTER
    n_tiles = (n + MOE_ROWS - 1) // MOE_ROWS

    @pl.when((e == 0) & (f == 0))
    def _():
        h = h_ref[...]
        xn_ref[...] = _rms(h, g_ref[...]).astype(BF16)
        o_ref[...] = h

    @pl.when(f == 0)
    def _():
        rank = rrow_ref[pl.ds(e, 1), :] * mrow_ref[pl.ds(e, 1), :]
        slot = lax.broadcasted_iota(jnp.int32, (MOE_ROWS, chunk), 0) + 1

        def gather(i, carry):
            rows = pl.ds(pl.multiple_of(i * MOE_ROWS, MOE_ROWS), MOE_ROWS)
            onehot = jnp.where(rank == (slot + i * MOE_ROWS).astype(F32), 1.0, 0.0).astype(BF16)
            xe_ref[rows, :] = _dot(onehot, xn_ref[...]).astype(BF16)
            ye_ref[rows, :] = jnp.zeros((MOE_ROWS, D_MODEL), F32)
            return carry

        def clear(i, carry):
            rows = pl.ds(pl.multiple_of(i * MOE_ROWS, MOE_ROWS), MOE_ROWS)
            ye_ref[rows, :] = jnp.zeros((MOE_ROWS, D_MODEL), F32)
            return carry

        lax.fori_loop(0, n_tiles, gather, 0)
        lax.fori_loop(n_tiles, n_scatter * (MOE_SCATTER // MOE_ROWS), clear, 0)

    def tile(i, carry):
        rows = pl.ds(pl.multiple_of(i * MOE_ROWS, MOE_ROWS), MOE_ROWS)
        ye_ref[rows, :] += _swiglu_tile(xe_ref[rows, :], wg_ref[...], wu_ref[...], wd_ref[...])
        return carry

    lax.fori_loop(0, n_tiles, tile, 0)

    @pl.when(f == n_f - 1)
    def _():
        rank = _lane_col(rcol_ref[...] * m_ref[...], e)
        weight = _lane_col(cw_ref[...], e)
        slot = lax.broadcasted_iota(jnp.int32, (chunk, MOE_SCATTER), 1) + 1

        def scatter(i, carry):
            rows = pl.ds(pl.multiple_of(i * MOE_SCATTER, MOE_SCATTER), MOE_SCATTER)
            onehot = jnp.where(rank == (slot + i * MOE_SCATTER).astype(F32), 1.0, 0.0).astype(BF16)
            o_ref[...] += weight * _dot(onehot, ye_ref[rows, :].astype(BF16))
            return carry

        lax.fori_loop(0, n_scatter, scatter, 0)


def _moe_ffn(h, gain, cw, mask, rcol, rrow, mrow, counts, wg, wu, wd, tf=1792):
    t, dff = h.shape[0], wg.shape[2]
    n_f = dff // tf
    chunk = MOE_CHUNK
    tok = lambda width: pl.BlockSpec((chunk, width), lambda c, e, f, cnt: (c, 0))
    lane_major = pl.BlockSpec((None, N_EXPERTS, chunk), lambda c, e, f, cnt: (c, 0, 0))
    grid_spec = pltpu.PrefetchScalarGridSpec(
        num_scalar_prefetch=1,
        grid=(t // chunk, N_EXPERTS, n_f),
        in_specs=[tok(D_MODEL),
                  pl.BlockSpec((1, D_MODEL), lambda c, e, f, cnt: (0, 0)),
                  tok(LANES), tok(LANES), tok(LANES), lane_major, lane_major,
                  pl.BlockSpec((None, D_MODEL, tf), lambda c, e, f, cnt: (e, 0, f)),
                  pl.BlockSpec((None, D_MODEL, tf), lambda c, e, f, cnt: (e, 0, f)),
                  pl.BlockSpec((None, tf, D_MODEL), lambda c, e, f, cnt: (e, f, 0))],
        out_specs=tok(D_MODEL),
        scratch_shapes=[pltpu.VMEM((chunk, D_MODEL), BF16), pltpu.VMEM((chunk, D_MODEL), BF16),
                        pltpu.VMEM((chunk, D_MODEL), F32)],
    )
    return pl.pallas_call(
        functools.partial(_moe_kernel, n_f=n_f),
        grid_spec=grid_spec,
        out_shape=jax.ShapeDtypeStruct((t, D_MODEL), F32),
        compiler_params=_params("parallel", "arbitrary", "arbitrary"),
        name="moe_swiglu",
    )(counts, h, gain.reshape(1, D_MODEL), cw, mask, rcol, rrow, mrow, wg, wu, wd)


def _router_kernel(h_ref, g_ref, w_ref, b_ref, cw_ref, m_ref):
    xn = _rms(h_ref[...], g_ref[...]).astype(BF16)
    logits = _dot(xn, w_ref[...]) + b_ref[...]
    lane = lax.broadcasted_iota(jnp.int32, logits.shape, 1).astype(F32)
    logits = jnp.where(lane < N_EXPERTS, logits, -jnp.inf)
    m1 = jnp.max(logits, axis=1, keepdims=True)
    i1 = jnp.min(jnp.where(logits == m1, lane, float(LANES)), axis=1, keepdims=True)
    rest = jnp.where(lane == i1, -jnp.inf, logits)
    m2 = jnp.max(rest, axis=1, keepdims=True)
    i2 = jnp.min(jnp.where(rest == m2, lane, float(LANES)), axis=1, keepdims=True)
    e2 = jnp.exp(m2 - m1)
    w1 = 1.0 / (1.0 + e2)
    w2 = e2 / (1.0 + e2)
    cw_ref[...] = jnp.where(lane == i1, w1, jnp.where(lane == i2, w2, 0.0))
    m_ref[...] = jnp.where((lane == i1) | (lane == i2), 1.0, 0.0)


def _router(h, gain, w, b, tm=1024):
    t = h.shape[0]
    return pl.pallas_call(
        _router_kernel,
        grid=(t // tm,),
        in_specs=[pl.BlockSpec((tm, D_MODEL), lambda i: (i, 0)),
                  pl.BlockSpec((1, D_MODEL), lambda i: (0, 0)),
                  pl.BlockSpec((D_MODEL, LANES), lambda i: (0, 0)),
                  pl.BlockSpec((1, LANES), lambda i: (0, 0))],
        out_specs=[pl.BlockSpec((tm, LANES), lambda i: (i, 0))] * 2,
        out_shape=[jax.ShapeDtypeStruct((t, LANES), F32)] * 2,
        compiler_params=_params("parallel"),
        name="moe_router",
    )(h, gain.reshape(1, D_MODEL), w, b)


def _ple_kernel(h_ref, g_ref, p_ref, wg_ref, wp_ref, fg_ref, o_ref, *, final, tn):
    h = h_ref[...]
    xn = _rms(h, g_ref[...]).astype(BF16)
    pe = p_ref[...].astype(BF16)
    outs = []
    for c0 in range(0, D_MODEL, tn):
        gate = jax.nn.sigmoid(_dot(xn, wg_ref[:, c0:c0 + tn]))
        outs.append(h[:, c0:c0 + tn] + gate * _dot(pe, wp_ref[:, c0:c0 + tn]))
    new = jnp.concatenate(outs, axis=1)
    o_ref[...] = _rms(new, fg_ref[...]) if final else new


def _ple(h, gain, p, wg, wp, final_gain, final, tm=1024, tn=512):
    t, pd = p.shape
    return pl.pallas_call(
        functools.partial(_ple_kernel, final=final, tn=tn),
        grid=(t // tm,),
        in_specs=[pl.BlockSpec((tm, D_MODEL), lambda i: (i, 0)),
                  pl.BlockSpec((1, D_MODEL), lambda i: (0, 0)),
                  pl.BlockSpec((tm, pd), lambda i: (i, 0)),
                  pl.BlockSpec((D_MODEL, D_MODEL), lambda i: (0, 0)),
                  pl.BlockSpec((pd, D_MODEL), lambda i: (0, 0)),
                  pl.BlockSpec((1, D_MODEL), lambda i: (0, 0))],
        out_specs=pl.BlockSpec((tm, D_MODEL), lambda i: (i, 0)),
        out_shape=jax.ShapeDtypeStruct((t, D_MODEL), F32),
        compiler_params=_params("parallel"),
        name="ple",
    )(h, gain.reshape(1, D_MODEL), p, wg, wp, final_gain.reshape(1, D_MODEL))


def _compress_kernel(x01_ref, x23_ref, pos_ref, w1_ref, b1_ref, w2_ref, o_ref, *, blocks):
    stride = NSA_CMP_STRIDE
    hidden = b1_ref.shape[1]
    low = lax.broadcasted_iota(jnp.int32, (blocks, LANES), 1) < HEAD_DIM
    first = [jnp.zeros((blocks, hidden), F32) for _ in range(NSA_GROUPS)]
    second = [jnp.zeros((blocks, hidden), F32) for _ in range(NSA_GROUPS)]
    for j in range(stride):
        for half, x_ref in enumerate((x01_ref, x23_ref)):
            xs = x_ref[pl.ds(j, blocks, stride=stride), :]
            xa = xs + pos_ref[j:j + 1, :]
            xb = xs + pos_ref[stride + j:stride + j + 1, :]
            for sub in range(2):
                g = 2 * half + sub
                keep = low if sub == 0 else jnp.logical_not(low)
                first[g] = first[g] + _dot(jnp.where(keep, xa, 0.0).astype(BF16), w1_ref[j])
                second[g] = second[g] + _dot(jnp.where(keep, xb, 0.0).astype(BF16), w1_ref[stride + j])
    row = lax.broadcasted_iota(jnp.int32, (blocks, NSA_WIDTH), 0)
    for g in range(NSA_GROUPS):
        hid = first[g] + pltpu.roll(second[g], blocks - 1, 0) + b1_ref[...]
        out = _dot(jax.nn.gelu(hid, approximate=True).astype(BF16), w2_ref[...])
        o_ref[g] = jnp.where(row == blocks - 1, 0.0, out)


def _compress(proj, col_block, pos, w1, b1, w2, batch, seq):
    blocks = seq // NSA_CMP_STRIDE
    hidden = w1.shape[1]
    w1_rep = jnp.tile(w1.reshape(NSA_CMP_LEN, HEAD_DIM, hidden), (1, 2, 1))
    pos_rep = jnp.tile(pos, (1, 2))
    halves = NSA_WIDTH // LANES
    return pl.pallas_call(
        functools.partial(_compress_kernel, blocks=blocks),
        grid=(batch,),
        in_specs=[pl.BlockSpec((seq, LANES), lambda b: (b, halves * col_block)),
                  pl.BlockSpec((seq, LANES), lambda b: (b, halves * col_block + 1)),
                  pl.BlockSpec((NSA_CMP_LEN, LANES), lambda b: (0, 0)),
                  pl.BlockSpec((NSA_CMP_LEN, LANES, hidden), lambda b: (0, 0, 0)),
                  pl.BlockSpec((1, hidden), lambda b: (0, 0)),
                  pl.BlockSpec((hidden, NSA_WIDTH), lambda b: (0, 0))],
        out_specs=pl.BlockSpec((NSA_GROUPS, blocks, NSA_WIDTH), lambda b: (b, 0, 0)),
        out_shape=jax.ShapeDtypeStruct((batch * NSA_GROUPS, blocks, NSA_WIDTH), F32),
        compiler_params=_params("parallel"),
        name="nsa_compress",
    )(proj, proj, pos_rep, w1_rep.astype(BF16), b1.reshape(1, hidden), w2)


NSA_TQ = 256
NSA_TK = 512
NSA_WIDTH = NSA_HPG * HEAD_DIM
LOG_HEAD_DIM = HEAD_DIM.bit_length() - 1
LOG_SEL_LEN = NSA_SEL_LEN.bit_length() - 1


def _nsa_kernel(q_ref, kc_ref, vc_ref, ks_ref, vs_ref, kw_ref, vw_ref, gate_ref, o_ref,
                ksb, kwb, vst_all, vwt_all, vct, gt_ref, osel_ref, *, seq, n_cmp):
    g = pl.program_id(1)
    i = pl.program_id(2)
    tq, tk = NSA_TQ, NSA_TK
    cols = NSA_HPG * tq
    vrows = HEAD_DIM + ONES_ROWS
    grow = pl.multiple_of(g * vrows, vrows)

    @pl.when((i == 0) & (g == 0))
    def _():
        def fill(ci, carry):
            rs = pl.ds(pl.multiple_of(ci * NSA_WIDTH, NSA_WIDTH), NSA_WIDTH)
            ksb[rs, :] = ks_ref[rs, :].astype(BF16)
            kwb[rs, :] = kw_ref[rs, :].astype(BF16)
            for src, dst in ((vs_ref, vst_all), (vw_ref, vwt_all)):
                vt = src[rs, :].T.astype(BF16)
                for grp in range(NSA_GROUPS):
                    dst[grp * vrows:grp * vrows + HEAD_DIM, rs] = vt[grp * HEAD_DIM:(grp + 1) * HEAD_DIM]
                    dst[grp * vrows + HEAD_DIM:(grp + 1) * vrows, rs] = jnp.ones((ONES_ROWS, NSA_WIDTH), BF16)
            return carry

        lax.fori_loop(0, seq // NSA_WIDTH, fill, 0)

    @pl.when(i == 0)
    def _():
        vct[...] = vc_ref[0].T[0:HEAD_DIM, :].astype(BF16)

    vst = vst_all.at[pl.ds(grow, vrows)]
    vwt = vwt_all.at[pl.ds(grow, vrows)]

    t0 = i * tq
    lane_grp = lax.broadcasted_iota(jnp.int32, (tq, NSA_WIDTH), 1) >> LOG_HEAD_DIM
    q = q_ref[...] * SCALE2
    rolled = [q] + [pltpu.roll(q, s * HEAD_DIM, 1) for s in range(1, NSA_HPG)]
    parts = []
    for j in range(NSA_HPG):
        shift = (g - j) & (NSA_HPG - 1)
        moved = jnp.where(shift == 0, rolled[0],
                          jnp.where(shift == 1, rolled[1], jnp.where(shift == 2, rolled[2], rolled[3])))
        parts.append(jnp.where(lane_grp == g, moved, 0.0))
    qs = jnp.concatenate(parts, axis=0).astype(BF16)

    def heads_sum(x):
        out = x[:, 0:tq]
        for j in range(1, NSA_HPG):
            out = out + x[:, j * tq:(j + 1) * tq]
        return out

    def lanes4(x):
        return jnp.concatenate([x] * NSA_HPG, axis=1)

    wk = NSA_WINDOW + tq
    ws = pl.multiple_of(jnp.maximum(t0 - NSA_WINDOW, 0), tq)
    kpos = ws + lax.broadcasted_iota(jnp.int32, (wk, tq), 0)
    qw = t0 + lax.broadcasted_iota(jnp.int32, (wk, tq), 1)
    bias_w = jnp.where((kpos <= qw) & (kpos > qw - NSA_WINDOW), 0.0, NEG_INF)
    sw = _dot_nt(kwb[pl.ds(ws, wk), :], qs) + lanes4(bias_w)

    nrow = lax.broadcasted_iota(jnp.int32, (LANES, cols), 0)
    tcol = t0 + (lax.broadcasted_iota(jnp.int32, (LANES, cols), 1) & (tq - 1))
    valid_c = (nrow * NSA_CMP_STRIDE + NSA_CMP_LEN - 1 <= tcol) & (nrow < n_cmp)
    sc = jnp.where(valid_c, _dot_nt(kc_ref[0].astype(BF16), qs), NEG_INF)
    mc = jnp.max(sc, axis=0, keepdims=True)
    ec = jnp.where(valid_c, jnp.exp2(sc - mc), 0.0)
    dc = jnp.sum(ec, axis=0, keepdims=True)
    pc = ec / jnp.where(dc > 0.0, dc, 1.0)
    o_cmp = _dot(vct[...], pc.astype(BF16))
    pc_sum = heads_sum(pc)

    n_sel_blocks = seq // NSA_SEL_LEN
    jrow = lax.broadcasted_iota(jnp.int32, (LANES, LANES), 0)
    ncol = lax.broadcasted_iota(jnp.int32, (LANES, LANES), 1)
    overlap = ((ncol * NSA_CMP_STRIDE < (jrow + 1) * NSA_SEL_LEN)
               & (ncol * NSA_CMP_STRIDE + NSA_CMP_LEN > jrow * NSA_SEL_LEN)
               & (ncol < n_cmp) & (jrow < n_sel_blocks))
    overlap = jnp.where(overlap, 1.0, 0.0).astype(BF16)
    hi, mid, lo = _split3(pc_sum)
    imp = (_dot(overlap, hi) + _dot(overlap, mid) + _dot(overlap, lo))[0:n_sel_blocks]
    pw = jnp.exp2(sw - jnp.max(sw, axis=0, keepdims=True))
    o_win = _dot(vwt[:, pl.ds(ws, wk)], pw.astype(BF16))
    o_win = o_win[0:HEAD_DIM] / o_win[HEAD_DIM:HEAD_DIM + 1]
    blk = lax.broadcasted_iota(jnp.int32, (n_sel_blocks, tq), 0)
    cur = (t0 + lax.broadcasted_iota(jnp.int32, (n_sel_blocks, tq), 1)) >> LOG_SEL_LEN
    forced = (blk == 0) | (blk == cur) | (blk == cur - 1)
    imp = jnp.where(blk > cur, -1.0, jnp.where(forced, 1e6, imp))
    beaten = jnp.zeros((n_sel_blocks, tq), jnp.int32)
    for c in range(n_sel_blocks):
        row = imp[c:c + 1, :]
        wins = (row > imp) | ((row == imp) & (blk > c))
        beaten = beaten + jnp.where(wins, 1, 0)
    sel_bias = jnp.where(beaten < NSA_TOP_N, 0.0, NEG_INF)
    sel_bias = jnp.concatenate([sel_bias, jnp.zeros((LANES - n_sel_blocks, tq), F32)], axis=0).astype(BF16)

    n_tiles = (t0 + tq - 1) // tk + 1

    def sel_branch(n):
        nk = n * tk
        erow = lax.broadcasted_iota(jnp.int32, (nk, LANES), 0)
        ecol = lax.broadcasted_iota(jnp.int32, (nk, LANES), 1)
        expand = jnp.where((erow >> LOG_SEL_LEN) == ecol, 1.0, 0.0).astype(BF16)
        krow = lax.broadcasted_iota(jnp.int32, (nk, tq), 0)
        qlane = t0 + lax.broadcasted_iota(jnp.int32, (nk, tq), 1)
        bias = jnp.where(krow <= qlane, _dot(expand, sel_bias), NEG_INF)
        s = _dot_nt(ksb[0:nk, :], qs) + lanes4(bias)
        p = jnp.exp2(s - jnp.max(s, axis=0, keepdims=True))
        acc = _dot(vst[:, 0:nk], p.astype(BF16))
        osel_ref[...] = acc[0:HEAD_DIM] / acc[HEAD_DIM:HEAD_DIM + 1]

    for n in range(1, seq // tk + 1):
        pl.when(n_tiles == n)(functools.partial(sel_branch, n))
    o_sel = osel_ref[...]

    gt_ref[...] = gate_ref[...].T
    out = jnp.zeros((HEAD_DIM, cols), F32)
    for c, branch in enumerate((o_cmp, o_sel, o_win)):
        gate = jnp.concatenate([gt_ref[pl.ds(c * NSA_HEADS + g * NSA_HPG + j, 1), :] for j in range(NSA_HPG)],
                               axis=1)
        out = out + gate * branch
    o_ref[...] = jnp.concatenate([out[:, j * tq:(j + 1) * tq] for j in range(NSA_HPG)], axis=0).T.astype(o_ref.dtype)


def _nsa_attention(proj, gates, kc_cmp, vc_cmp, batch, seq, n_cmp):
    nq = seq // NSA_TQ
    ncb = kc_cmp.shape[1]
    qcols = NSA_HEADS * HEAD_DIM // NSA_WIDTH
    kv = lambda c: pl.BlockSpec((seq, NSA_WIDTH), lambda b, g, i, c=c: (b, qcols + c))
    cmp_spec = pl.BlockSpec((1, ncb, NSA_WIDTH), lambda b, g, i: (b * NSA_GROUPS + g, 0, 0))
    return pl.pallas_call(
        functools.partial(_nsa_kernel, seq=seq, n_cmp=n_cmp),
        grid=(batch, NSA_GROUPS, nq),
        in_specs=[pl.BlockSpec((NSA_TQ, NSA_WIDTH), lambda b, g, i: (b * nq + i, g)),
                  cmp_spec, cmp_spec, kv(2), kv(3), kv(4), kv(5),
                  pl.BlockSpec((NSA_TQ, LANES), lambda b, g, i: (b * nq + i, 0))],
        out_specs=pl.BlockSpec((NSA_TQ, NSA_WIDTH), lambda b, g, i: (b * nq + i, g)),
        out_shape=jax.ShapeDtypeStruct((batch * seq, NSA_HEADS * HEAD_DIM), BF16),
        scratch_shapes=[pltpu.VMEM((seq, NSA_WIDTH), BF16), pltpu.VMEM((seq, NSA_WIDTH), BF16),
                        pltpu.VMEM((NSA_GROUPS * (HEAD_DIM + ONES_ROWS), seq), BF16),
                        pltpu.VMEM((NSA_GROUPS * (HEAD_DIM + ONES_ROWS), seq), BF16),
                        pltpu.VMEM((HEAD_DIM, ncb), BF16), pltpu.VMEM((LANES, NSA_TQ), F32),
                        pltpu.VMEM((HEAD_DIM, NSA_HPG * NSA_TQ), F32)],
        compiler_params=_params("parallel", "arbitrary", "arbitrary"),
        name="nsa_attention",
    )(proj, kc_cmp, vc_cmp, proj, proj, proj, proj, gates)


def _pad_cols(w, width=LANES):
    return jnp.pad(w, ((0, 0), (0, width - w.shape[1])))


def kernel(x, p, positions, mix_norm, ffn_norm, ple_norm, ple_gate_w, ple_proj_w, fd_w_in, fd_forget_b, fd_w_out, dense_w_gate, dense_w_up, dense_w_down, nsa_w_in, nsa_pos_k, nsa_w1_k, nsa_b1_k, nsa_w2_k, nsa_pos_v, nsa_w1_v, nsa_b1_v, nsa_w2_v, nsa_w_out, moe_w_router, moe_b_router, moe_w_gate, moe_w_up, moe_w_down, final_norm):
    batch, seq, _ = x.shape
    t = batch * seq
    h = x.reshape(t, D_MODEL)
    cos, sin = _rope_tables(positions)

    n_main = 3 * FOX_WIDTH + 3 * DIL_WIDTH
    w_in = fd_w_in[0]
    tiles0 = tuple((c, 512, c in (3 * FOX_WIDTH, 3 * FOX_WIDTH + DIL_WIDTH)) for c in range(0, n_main, 512))
    proj0, log_f = _proj(h, mix_norm[0], w_in[:, :n_main].astype(BF16),
                         _pad_cols(w_in[:, n_main:]).astype(BF16),
                         _pad_cols(fd_forget_b[0].reshape(1, FOX_HEADS)).astype(F32),
                         cos, sin, tiles0, "log_sigmoid")
    ccol, crow, _ = _cumsum(log_f, batch, seq)
    o_fox = _fox_attention(proj0, ccol, crow, batch, seq)
    o_dil = _dilated_attention(proj0, batch, seq)
    h = _outproj(h, [o_fox, o_dil], fd_w_out[0].astype(BF16))
    h = _ffn(h, ffn_norm[0], dense_w_gate[0].astype(BF16), dense_w_up[0].astype(BF16),
             dense_w_down[0].astype(BF16))
    h = _ple(h, ple_norm[0], p[0].reshape(t, -1), ple_gate_w[0].astype(BF16), ple_proj_w[0].astype(BF16),
             final_norm, final=False)

    qw = NSA_HEADS * HEAD_DIM
    n_main1 = qw + 6 * NSA_KV
    w_in1 = nsa_w_in[0]
    rope_cols = set(range(0, qw, 256)) | {qw, qw + 2 * NSA_KV, qw + 4 * NSA_KV}
    tiles1 = tuple((c, 256, c in rope_cols) for c in range(0, n_main1, 256))
    w_gate = w_in1[:, n_main1:].reshape(D_MODEL, NSA_HEADS, 3).transpose(0, 2, 1).reshape(D_MODEL, 3 * NSA_HEADS)
    proj1, gates = _proj(h, mix_norm[1], w_in1[:, :n_main1].astype(BF16), _pad_cols(w_gate).astype(BF16),
                         jnp.zeros((1, LANES), F32), cos, sin, tiles1, "sigmoid")
    n_cmp = (seq - NSA_CMP_LEN) // NSA_CMP_STRIDE + 1
    kc_cmp = _compress(proj1, qw // NSA_WIDTH, nsa_pos_k[0], nsa_w1_k[0], nsa_b1_k[0],
                       jnp.tile(nsa_w2_k[0], (1, NSA_HPG)).astype(BF16), batch, seq)
    vc_cmp = _compress(proj1, qw // NSA_WIDTH + 1, nsa_pos_v[0], nsa_w1_v[0], nsa_b1_v[0],
                       jnp.tile(nsa_w2_v[0], (1, NSA_HPG)).astype(BF16), batch, seq)
    o_nsa = _nsa_attention(proj1, gates, kc_cmp, vc_cmp, batch, seq, n_cmp)
    h = _outproj(h, [o_nsa], nsa_w_out[0].astype(BF16))
    cw, routed = _router(h, ffn_norm[1], _pad_cols(moe_w_router[0]).astype(BF16),
                         _pad_cols(moe_b_router[0].reshape(1, N_EXPERTS)).astype(F32))
    n_chunks = t // MOE_CHUNK
    rank_col, rank_row, routed_row = _cumsum(routed, n_chunks, MOE_CHUNK)
    counts = rank_col.reshape(n_chunks, MOE_CHUNK, LANES)[:, -1, :N_EXPERTS].astype(jnp.int32).reshape(-1)
    h = _moe_ffn(h, ffn_norm[1], cw, routed, rank_col, rank_row, routed_row, counts,
                 moe_w_gate[0].astype(BF16), moe_w_up[0].astype(BF16), moe_w_down[0].astype(BF16))
    h = _ple(h, ple_norm[1], p[1].reshape(t, -1), ple_gate_w[1].astype(BF16), ple_proj_w[1].astype(BF16),
             final_norm, final=True)
    return h.reshape(batch, seq, D_MODEL)
```

```python
import functools

import jax
import jax.numpy as jnp
from jax import lax
from jax.experimental import pallas as pl
from jax.experimental.pallas import tpu as pltpu

F32 = jnp.float32
BF16 = jnp.bfloat16

D_MODEL = 1024
HEAD_DIM = 64
LANES = 128
FOX_HEADS = 8
DIL_HEADS = 8
FOX_WIDTH = FOX_HEADS * HEAD_DIM
DIL_WIDTH = DIL_HEADS * HEAD_DIM
DIL_PATTERNS = ((128, 1), (512, 4), (2048, 16))
Q_BLOCK = 128
NSA_HEADS = 16
NSA_GROUPS = 4
NSA_HPG = NSA_HEADS // NSA_GROUPS
NSA_KV = NSA_GROUPS * HEAD_DIM
NSA_CMP_LEN = 32
NSA_CMP_STRIDE = 16
NSA_SEL_LEN = 64
NSA_TOP_N = 8
NSA_WINDOW = 512
N_EXPERTS = 8
ROPE_THETA = 10000.0
RMS_EPS = 1e-6
NEG_INF = -1e30
SCALE = HEAD_DIM ** -0.5
LOG2E = 1.4426950408889634
LN2 = 0.6931471805599453
SCALE2 = SCALE * LOG2E
ONES_ROWS = 16

VMEM_LIMIT_BYTES = 52 * 1024 * 1024


def _params(*sem):
    return pltpu.CompilerParams(dimension_semantics=sem, vmem_limit_bytes=VMEM_LIMIT_BYTES)


def _rms(x, g):
    return x * lax.rsqrt(jnp.mean(x * x, axis=-1, keepdims=True) + RMS_EPS) * g


def _dot(a, b):
    return jnp.dot(a, b, preferred_element_type=F32)


def _dot_nt(a, b):
    return lax.dot_general(a, b, (((1,), (1,)), ((), ())), preferred_element_type=F32)


def _split3(x):
    hi = x.astype(BF16)
    r = x - hi.astype(F32)
    mid = r.astype(BF16)
    lo = (r - mid.astype(F32)).astype(BF16)
    return hi, mid, lo


def _lane_col(x, idx):
    lane = lax.broadcasted_iota(jnp.int32, x.shape, 1)
    return jnp.sum(jnp.where(lane == idx, x, 0.0), axis=1, keepdims=True)


def _rope_table_kernel(pos_ref, inv_ref, cos_ref, sin_ref):
    ang = pos_ref[...] * inv_ref[...]
    lane = lax.broadcasted_iota(jnp.int32, ang.shape, 1)
    sign = jnp.where((lane & (HEAD_DIM - 1)) < HEAD_DIM // 2, -1.0, 1.0)
    cos_ref[...] = jnp.cos(ang)
    sin_ref[...] = jnp.sin(ang) * sign


def _rope_tables(positions):
    t = positions.size
    half = HEAD_DIM // 2
    inv_freq = ROPE_THETA ** (-jnp.arange(half, dtype=F32) / half)
    inv = jnp.tile(inv_freq, LANES // half).reshape(1, LANES)
    pos = jnp.broadcast_to(positions.astype(F32).reshape(t, 1), (t, LANES))
    tm = 1024
    return pl.pallas_call(
        _rope_table_kernel,
        grid=(t // tm,),
        in_specs=[pl.BlockSpec((tm, LANES), lambda i: (i, 0)),
                  pl.BlockSpec((1, LANES), lambda i: (0, 0))],
        out_specs=[pl.BlockSpec((tm, LANES), lambda i: (i, 0))] * 2,
        out_shape=[jax.ShapeDtypeStruct((t, LANES), F32)] * 2,
        compiler_params=_params("parallel"),
        name="rope_tables",
    )(pos, inv)


def _proj_kernel(h_ref, g_ref, w_ref, wa_ref, ba_ref, cos_ref, sin_ref, o_ref, oa_ref, *, tiles, aux_act):
    xn = _rms(h_ref[...], g_ref[...]).astype(BF16)
    for c0, width, rope in tiles:
        acc = _dot(xn, w_ref[:, c0:c0 + width])
        if rope:
            reps = width // LANES
            cos = jnp.tile(cos_ref[...], (1, reps))
            sin = jnp.tile(sin_ref[...], (1, reps))
            lane = lax.broadcasted_iota(jnp.int32, acc.shape, 1)
            first_half = (lane & (HEAD_DIM - 1)) < HEAD_DIM // 2
            partner = jnp.where(first_half,
                                pltpu.roll(acc, width - HEAD_DIM // 2, 1),
                                pltpu.roll(acc, HEAD_DIM // 2, 1))
            acc = acc * cos + partner * sin
        o_ref[:, c0:c0 + width] = acc
    aux = _dot(xn, wa_ref[...]) + ba_ref[...]
    if aux_act == "log_sigmoid":
        oa_ref[...] = jnp.minimum(aux, 0.0) - jnp.log1p(jnp.exp(-jnp.abs(aux)))
    else:
        oa_ref[...] = jax.nn.sigmoid(aux)


def _proj(h, gain, w, w_aux, b_aux, cos, sin, tiles, aux_act, tm=512):
    t, n = h.shape[0], w.shape[1]
    return pl.pallas_call(
        functools.partial(_proj_kernel, tiles=tiles, aux_act=aux_act),
        grid=(t // tm,),
        in_specs=[pl.BlockSpec((tm, D_MODEL), lambda i: (i, 0)),
                  pl.BlockSpec((1, D_MODEL), lambda i: (0, 0)),
                  pl.BlockSpec((D_MODEL, n), lambda i: (0, 0)),
                  pl.BlockSpec((D_MODEL, LANES), lambda i: (0, 0)),
                  pl.BlockSpec((1, LANES), lambda i: (0, 0)),
                  pl.BlockSpec((tm, LANES), lambda i: (i, 0)),
                  pl.BlockSpec((tm, LANES), lambda i: (i, 0))],
        out_specs=[pl.BlockSpec((tm, n), lambda i: (i, 0)),
                   pl.BlockSpec((tm, LANES), lambda i: (i, 0))],
        out_shape=[jax.ShapeDtypeStruct((t, n), F32), jax.ShapeDtypeStruct((t, LANES), F32)],
        compiler_params=_params("parallel"),
        name="in_proj",
    )(h, gain.reshape(1, D_MODEL), w, w_aux, b_aux, cos, sin)


CUM_BLOCK = 512


def _cumsum_kernel(x_ref, ccol_ref, crow_ref, carry_ref):
    j = pl.program_id(1)

    @pl.when(j == 0)
    def _():
        carry_ref[...] = jnp.zeros_like(carry_ref)

    x = x_ref[...]
    r = lax.broadcasted_iota(jnp.int32, (CUM_BLOCK, CUM_BLOCK), 0)
    c = lax.broadcasted_iota(jnp.int32, (CUM_BLOCK, CUM_BLOCK), 1)
    tri = jnp.where(r >= c, 1.0, 0.0).astype(BF16)
    hi, mid, lo = _split3(x)
    cum = _dot(tri, hi) + _dot(tri, mid) + _dot(tri, lo) + carry_ref[0:1, :]
    ccol_ref[...] = cum
    crow_ref[0] = cum.T
    carry_ref[...] = jnp.broadcast_to(cum[CUM_BLOCK - 1:CUM_BLOCK, :], carry_ref.shape)


def _cumsum(x, batch, seq):
    nb = seq // CUM_BLOCK
    return pl.pallas_call(
        _cumsum_kernel,
        grid=(batch, nb),
        in_specs=[pl.BlockSpec((CUM_BLOCK, LANES), lambda b, j: (b * nb + j, 0))],
        out_specs=[pl.BlockSpec((CUM_BLOCK, LANES), lambda b, j: (b * nb + j, 0)),
                   pl.BlockSpec((1, LANES, CUM_BLOCK), lambda b, j: (b, 0, j))],
        out_shape=[jax.ShapeDtypeStruct((batch * seq, LANES), F32),
                   jax.ShapeDtypeStruct((batch, LANES, seq), F32)],
        scratch_shapes=[pltpu.VMEM((8, LANES), F32)],
        compiler_params=_params("parallel", "arbitrary"),
        name="token_cumsum",
    )(x)


def _fox_kernel(q_ref, k_ref, v_ref, ccol_ref, crow_ref, o_ref, kb_ref, vt_ref, ck_ref, *, tq, seq):
    pair = pl.program_id(1)
    i = pl.program_id(2)
    tk = tq
    h0 = 2 * pair

    @pl.when(i == 0)
    def _():
        def fill(ci, carry):
            rs = pl.ds(pl.multiple_of(ci * tk, tk), tk)
            kb_ref[rs, :] = k_ref[rs, :].astype(BF16)
            vt_ref[0:LANES, rs] = v_ref[rs, :].T.astype(BF16)
            vt_ref[LANES:, rs] = jnp.ones((ONES_ROWS, tk), BF16)
            cc = ccol_ref[rs, :] * LOG2E
            ck_ref[0, rs, :] = jnp.broadcast_to(_lane_col(cc, h0), (tk, LANES))
            ck_ref[1, rs, :] = jnp.broadcast_to(_lane_col(cc, h0 + 1), (tk, LANES))
            return carry

        lax.fori_loop(0, seq // tk, fill, 0)

    t0 = pl.multiple_of(i * tq, tq)
    low = lax.broadcasted_iota(jnp.int32, (tq, LANES), 1) < HEAD_DIM
    q = q_ref[...] * SCALE2
    qs = jnp.concatenate([jnp.where(low, q, 0.0), jnp.where(low, 0.0, q)], axis=0).astype(BF16)
    cq = jnp.concatenate([crow_ref[0, pl.ds(h0, 1), pl.ds(t0, tq)],
                          crow_ref[0, pl.ds(h0 + 1, 1), pl.ds(t0, tq)]], axis=1) * LOG2E
    reps = tq // LANES

    def scores(k0, nk):
        ck = jnp.concatenate([ck_ref[0, k0:k0 + nk, :]] * reps + [ck_ref[1, k0:k0 + nk, :]] * reps, axis=1)
        return _dot_nt(kb_ref[k0:k0 + nk, :], qs) + cq - ck

    krow = lax.broadcasted_iota(jnp.int32, (tk, tq), 0)
    qlane = lax.broadcasted_iota(jnp.int32, (tk, tq), 1)
    causal = jnp.where(krow <= qlane, 0.0, NEG_INF)

    def branch(n):
        below = (n - 1) * tk
        s_diag = scores(below, tk) + jnp.concatenate([causal, causal], axis=1)
        m = jnp.max(s_diag, axis=0, keepdims=True)
        if below:
            s_below = scores(0, below)
            m = jnp.maximum(m, jnp.max(s_below, axis=0, keepdims=True))
            acc = _dot(vt_ref[:, 0:below], jnp.exp2(s_below - m).astype(BF16))
        else:
            acc = 0.0
        acc = acc + _dot(vt_ref[:, below:below + tk], jnp.exp2(s_diag - m).astype(BF16))
        out = acc[0:LANES] / acc[LANES:LANES + 1]
        o_ref[...] = jnp.concatenate([out[0:HEAD_DIM, 0:tq], out[HEAD_DIM:, tq:]], axis=0).T.astype(o_ref.dtype)

    for n in range(1, seq // tq + 1):
        pl.when(i == n - 1)(functools.partial(branch, n))


def _fox_attention(proj, ccol, crow, batch, seq, tq=512):
    nq = seq // tq
    npair = FOX_WIDTH // LANES
    return pl.pallas_call(
        functools.partial(_fox_kernel, tq=tq, seq=seq),
        grid=(batch, npair, nq),
        in_specs=[pl.BlockSpec((tq, LANES), lambda b, p, i: (b * nq + i, p)),
                  pl.BlockSpec((seq, LANES), lambda b, p, i: (b, npair + p)),
                  pl.BlockSpec((seq, LANES), lambda b, p, i: (b, 2 * npair + p)),
                  pl.BlockSpec((seq, LANES), lambda b, p, i: (b, 0)),
                  pl.BlockSpec((1, 8, seq), lambda b, p, i: (b, 0, 0))],
        out_specs=pl.BlockSpec((tq, LANES), lambda b, p, i: (b * nq + i, p)),
        out_shape=jax.ShapeDtypeStruct((batch * seq, FOX_WIDTH), BF16),
        scratch_shapes=[pltpu.VMEM((seq, LANES), BF16), pltpu.VMEM((LANES + ONES_ROWS, seq), BF16),
                        pltpu.VMEM((2, seq, LANES), F32)],
        compiler_params=_params("parallel", "parallel", "arbitrary"),
        name="fox_attention",
    )(proj, proj, proj, ccol, crow)


def _dil_kernel(q_ref, k_ref, v_ref, o_ref, os_ref, ls_ref, *, seq):
    low = lax.broadcasted_iota(jnp.int32, (Q_BLOCK, LANES), 1) < HEAD_DIM
    kr = lax.broadcasted_iota(jnp.int32, (2 * Q_BLOCK, Q_BLOCK), 0)
    qc = lax.broadcasted_iota(jnp.int32, (2 * Q_BLOCK, Q_BLOCK), 1)
    dist = qc + Q_BLOCK - kr
    group = 8

    for pi, (window, dil) in enumerate(DIL_PATTERNS):
        span = window // dil
        nb = (seq // dil) // Q_BLOCK
        band_bias = jnp.where((dist >= 0) & (dist <= span), 0.0, NEG_INF)
        first_bias = jnp.where(kr >= Q_BLOCK, band_bias, NEG_INF)

        def rows(start, dil=dil):
            return pl.ds(start, Q_BLOCK, stride=dil) if dil > 1 else pl.ds(start, Q_BLOCK)

        def unit(u, pi=pi, dil=dil, nb=nb, band_bias=band_bias, first_bias=first_bias, rows=rows):
            r = u // nb
            blk = u % nb
            cur = r + blk * (Q_BLOCK * dil)
            q = q_ref[rows(cur), :] * SCALE2
            qs = jnp.concatenate([jnp.where(low, q, 0.0), jnp.where(low, 0.0, q)], axis=0).astype(BF16)
            if nb == 1:
                kk = k_ref[rows(cur), :].astype(BF16)
                vt = v_ref[rows(cur), :].T.astype(BF16)
                bias = band_bias[Q_BLOCK:]
            else:
                prev = jnp.maximum(cur - Q_BLOCK * dil, r)
                kk = jnp.concatenate([k_ref[rows(prev), :], k_ref[rows(cur), :]], axis=0).astype(BF16)
                vt = jnp.concatenate([v_ref[rows(prev), :].T, v_ref[rows(cur), :].T], axis=1).astype(BF16)
                bias = jnp.where(blk > 0, band_bias, first_bias)
            vt = jnp.concatenate([vt, jnp.ones((ONES_ROWS, vt.shape[1]), BF16)], axis=0)
            s = _dot_nt(kk, qs) + jnp.concatenate([bias, bias], axis=1)
            return s, vt

        def softmax(s):
            m = jnp.max(s, axis=0, keepdims=True)
            return jnp.exp2(s - m).astype(BF16), m

        def finish(vt, p, m):
            ot = _dot(vt, p)
            den = ot[LANES:LANES + 1]
            ot = ot[0:LANES] * (1.0 / den)
            lse = m * LN2 + jnp.log(den)
            lse_t = jnp.concatenate([jnp.broadcast_to(lse[:, 0:Q_BLOCK], (HEAD_DIM, Q_BLOCK)),
                                     jnp.broadcast_to(lse[:, Q_BLOCK:], (HEAD_DIM, Q_BLOCK))], axis=0)
            return jnp.concatenate([ot[0:HEAD_DIM, 0:Q_BLOCK], ot[HEAD_DIM:, Q_BLOCK:]], axis=0).T, lse_t.T

        def units(gi, carry, pi=pi, nb=nb, dil=dil, rows=rows, unit=unit, softmax=softmax, finish=finish):
            ids = [gi * group + j for j in range(group)]
            scored = [unit(u) for u in ids]
            probs = [softmax(s) for s, _ in scored]
            outs = [finish(vt, p, lse) for (_, vt), (p, lse) in zip(scored, probs)]
            for u, (o, l) in zip(ids, outs):
                cur = u // nb + (u % nb) * (Q_BLOCK * dil)
                os_ref[pi, rows(cur), :] = o
                ls_ref[pi, rows(cur), :] = l
            return carry

        lax.fori_loop(0, dil * nb // group, units, 0)

    chunk = 256

    def combine(ci, carry):
        rs = pl.ds(pl.multiple_of(ci * chunk, chunk), chunk)
        l0, l1, l2 = ls_ref[0, rs, :], ls_ref[1, rs, :], ls_ref[2, rs, :]
        m = jnp.maximum(jnp.maximum(l0, l1), l2)
        e0, e1, e2 = jnp.exp(l0 - m), jnp.exp(l1 - m), jnp.exp(l2 - m)
        tot = e0 + e1 + e2
        o_ref[rs, :] = ((e0 / tot) * os_ref[0, rs, :] + (e1 / tot) * os_ref[1, rs, :]
                        + (e2 / tot) * os_ref[2, rs, :]).astype(o_ref.dtype)
        return carry

    lax.fori_loop(0, seq // chunk, combine, 0)


def _dilated_attention(proj, batch, seq):
    npair = DIL_WIDTH // LANES
    base = 3 * FOX_WIDTH // LANES
    return pl.pallas_call(
        functools.partial(_dil_kernel, seq=seq),
        grid=(batch, npair),
        in_specs=[pl.BlockSpec((seq, LANES), lambda b, p: (b, base + p)),
                  pl.BlockSpec((seq, LANES), lambda b, p: (b, base + npair + p)),
                  pl.BlockSpec((seq, LANES), lambda b, p: (b, base + 2 * npair + p))],
        out_specs=pl.BlockSpec((seq, LANES), lambda b, p: (b, p)),
        out_shape=jax.ShapeDtypeStruct((batch * seq, DIL_WIDTH), BF16),
        scratch_shapes=[pltpu.VMEM((3, seq, LANES), F32), pltpu.VMEM((3, seq, LANES), F32)],
        compiler_params=_params("parallel", "parallel"),
        name="dilated_attention",
    )(proj, proj, proj)


def _outproj_kernel(*refs, n_in, tn):
    h_ref = refs[0]
    a_refs = refs[1:1 + n_in]
    w_ref = refs[1 + n_in]
    o_ref = refs[2 + n_in]
    acts = [a[...].astype(BF16) for a in a_refs]
    for c0 in range(0, D_MODEL, tn):
        acc = h_ref[:, c0:c0 + tn]
        k0 = 0
        for a in acts:
            acc = acc + _dot(a, w_ref[k0:k0 + a.shape[1], c0:c0 + tn])
            k0 += a.shape[1]
        o_ref[:, c0:c0 + tn] = acc


def _outproj(h, acts, w, tm=1024, tn=512):
    t = h.shape[0]
    return pl.pallas_call(
        functools.partial(_outproj_kernel, n_in=len(acts), tn=tn),
        grid=(t // tm,),
        in_specs=([pl.BlockSpec((tm, D_MODEL), lambda i: (i, 0))]
                  + [pl.BlockSpec((tm, a.shape[1]), lambda i: (i, 0)) for a in acts]
                  + [pl.BlockSpec((D_MODEL, D_MODEL), lambda i: (0, 0))]),
        out_specs=pl.BlockSpec((tm, D_MODEL), lambda i: (i, 0)),
        out_shape=jax.ShapeDtypeStruct((t, D_MODEL), F32),
        compiler_params=_params("parallel"),
        name="out_proj",
    )(h, *acts, w)


def _swiglu_tile(x, wg, wu, wd):
    gate = _dot(x, wg)
    up = _dot(x, wu)
    return _dot((gate * jax.nn.sigmoid(gate) * up).astype(BF16), wd)


def _ffn_kernel(h_ref, g_ref, wg_ref, wu_ref, wd_ref, o_ref, xn_ref, acc_ref, *, n_f):
    f = pl.program_id(1)

    @pl.when(f == 0)
    def _():
        xn_ref[...] = _rms(h_ref[...], g_ref[...]).astype(BF16)
        acc_ref[...] = jnp.zeros_like(acc_ref)

    acc_ref[...] += _swiglu_tile(xn_ref[...], wg_ref[...], wu_ref[...], wd_ref[...])

    @pl.when(f == n_f - 1)
    def _():
        o_ref[...] = h_ref[...] + acc_ref[...]


def _ffn(h, gain, wg, wu, wd, tm=512, tf=1792):
    t, dff = h.shape[0], wg.shape[1]
    n_f = dff // tf
    return pl.pallas_call(
        functools.partial(_ffn_kernel, n_f=n_f),
        grid=(t // tm, n_f),
        in_specs=[pl.BlockSpec((tm, D_MODEL), lambda i, f: (i, 0)),
                  pl.BlockSpec((1, D_MODEL), lambda i, f: (0, 0)),
                  pl.BlockSpec((D_MODEL, tf), lambda i, f: (0, f)),
                  pl.BlockSpec((D_MODEL, tf), lambda i, f: (0, f)),
                  pl.BlockSpec((tf, D_MODEL), lambda i, f: (f, 0))],
        out_specs=pl.BlockSpec((tm, D_MODEL), lambda i, f: (i, 0)),
        out_shape=jax.ShapeDtypeStruct((t, D_MODEL), F32),
        scratch_shapes=[pltpu.VMEM((tm, D_MODEL), BF16), pltpu.VMEM((tm, D_MODEL), F32)],
        compiler_params=_params("parallel", "arbitrary"),
        name="dense_swiglu",
    )(h, gain.reshape(1, D_MODEL), wg, wu, wd)


MOE_CHUNK = 1024
MOE_ROWS = 128
MOE_SCATTER = 256


def _moe_kernel(cnt_ref, h_ref, g_ref, cw_ref, m_ref, rcol_ref, rrow_ref, mrow_ref, wg_ref, wu_ref, wd_ref,
                o_ref, xn_ref, xe_ref, ye_ref, *, n_f):
    c = pl.program_id(0)
    e = pl.program_id(1)
    f = pl.program_id(2)
    chunk = MOE_CHUNK
    n = cnt_ref[c * N_EXPERTS + e]
    n_scatter = (n + MOE_SCATTER - 1) // MOE_SCATTER
    n_tiles = (n + MOE_ROWS - 1) // MOE_ROWS

    @pl.when((e == 0) & (f == 0))
    def _():
        h = h_ref[...]
        xn_ref[...] = _rms(h, g_ref[...]).astype(BF16)
        o_ref[...] = h

    @pl.when(f == 0)
    def _():
        rank = rrow_ref[pl.ds(e, 1), :] * mrow_ref[pl.ds(e, 1), :]
        slot = lax.broadcasted_iota(jnp.int32, (MOE_ROWS, chunk), 0) + 1

        def gather(i, carry):
            rows = pl.ds(pl.multiple_of(i * MOE_ROWS, MOE_ROWS), MOE_ROWS)
            onehot = jnp.where(rank == (slot + i * MOE_ROWS).astype(F32), 1.0, 0.0).astype(BF16)
            xe_ref[rows, :] = _dot(onehot, xn_ref[...]).astype(BF16)
            ye_ref[rows, :] = jnp.zeros((MOE_ROWS, D_MODEL), F32)
            return carry

        def clear(i, carry):
            rows = pl.ds(pl.multiple_of(i * MOE_ROWS, MOE_ROWS), MOE_ROWS)
            ye_ref[rows, :] = jnp.zeros((MOE_ROWS, D_MODEL), F32)
            return carry

        lax.fori_loop(0, n_tiles, gather, 0)
        lax.fori_loop(n_tiles, n_scatter * (MOE_SCATTER // MOE_ROWS), clear, 0)

    def tile(i, carry):
        rows = pl.ds(pl.multiple_of(i * MOE_ROWS, MOE_ROWS), MOE_ROWS)
        ye_ref[rows, :] += _swiglu_tile(xe_ref[rows, :], wg_ref[...], wu_ref[...], wd_ref[...])
        return carry

    lax.fori_loop(0, n_tiles, tile, 0)

    @pl.when(f == n_f - 1)
    def _():
        rank = _lane_col(rcol_ref[...] * m_ref[...], e)
        weight = _lane_col(cw_ref[...], e)
        slot = lax.broadcasted_iota(jnp.int32, (chunk, MOE_SCATTER), 1) + 1

        def scatter(i, carry):
            rows = pl.ds(pl.multiple_of(i * MOE_SCATTER, MOE_SCATTER), MOE_SCATTER)
            onehot = jnp.where(rank == (slot + i * MOE_SCATTER).astype(F32), 1.0, 0.0).astype(BF16)
            o_ref[...] += weight * _dot(onehot, ye_ref[rows, :].astype(BF16))
            return carry

        lax.fori_loop(0, n_scatter, scatter, 0)


def _moe_ffn(h, gain, cw, mask, rcol, rrow, mrow, counts, wg, wu, wd, tf=1792):
    t, dff = h.shape[0], wg.shape[2]
    n_f = dff // tf
    chunk = MOE_CHUNK
    tok = lambda width: pl.BlockSpec((chunk, width), lambda c, e, f, cnt: (c, 0))
    lane_major = pl.BlockSpec((None, N_EXPERTS, chunk), lambda c, e, f, cnt: (c, 0, 0))
    grid_spec = pltpu.PrefetchScalarGridSpec(
        num_scalar_prefetch=1,
        grid=(t // chunk, N_EXPERTS, n_f),
        in_specs=[tok(D_MODEL),
                  pl.BlockSpec((1, D_MODEL), lambda c, e, f, cnt: (0, 0)),
                  tok(LANES), tok(LANES), tok(LANES), lane_major, lane_major,
                  pl.BlockSpec((None, D_MODEL, tf), lambda c, e, f, cnt: (e, 0, f)),
                  pl.BlockSpec((None, D_MODEL, tf), lambda c, e, f, cnt: (e, 0, f)),
                  pl.BlockSpec((None, tf, D_MODEL), lambda c, e, f, cnt: (e, f, 0))],
        out_specs=tok(D_MODEL),
        scratch_shapes=[pltpu.VMEM((chunk, D_MODEL), BF16), pltpu.VMEM((chunk, D_MODEL), BF16),
                        pltpu.VMEM((chunk, D_MODEL), F32)],
    )
    return pl.pallas_call(
        functools.partial(_moe_kernel, n_f=n_f),
        grid_spec=grid_spec,
        out_shape=jax.ShapeDtypeStruct((t, D_MODEL), F32),
        compiler_params=_params("parallel", "arbitrary", "arbitrary"),
        name="moe_swiglu",
    )(counts, h, gain.reshape(1, D_MODEL), cw, mask, rcol, rrow, mrow, wg, wu, wd)


def _router_kernel(h_ref, g_ref, w_ref, b_ref, cw_ref, m_ref, rcol_ref, rrow_ref, mrow_ref):
    xn = _rms(h_ref[...], g_ref[...]).astype(BF16)
    logits = _dot(xn, w_ref[...]) + b_ref[...]
    lane = lax.broadcasted_iota(jnp.int32, logits.shape, 1).astype(F32)
    logits = jnp.where(lane < N_EXPERTS, logits, -jnp.inf)
    m1 = jnp.max(logits, axis=1, keepdims=True)
    i1 = jnp.min(jnp.where(logits == m1, lane, float(LANES)), axis=1, keepdims=True)
    rest = jnp.where(lane == i1, -jnp.inf, logits)
    m2 = jnp.max(rest, axis=1, keepdims=True)
    i2 = jnp.min(jnp.where(rest == m2, lane, float(LANES)), axis=1, keepdims=True)
    e2 = jnp.exp(m2 - m1)
    w1 = 1.0 / (1.0 + e2)
    w2 = e2 / (1.0 + e2)
    cw_ref[...] = jnp.where(lane == i1, w1, jnp.where(lane == i2, w2, 0.0))
    mask = jnp.where((lane == i1) | (lane == i2), 1.0, 0.0)
    m_ref[...] = mask
    r = lax.broadcasted_iota(jnp.int32, (CUM_BLOCK, CUM_BLOCK), 0)
    c = lax.broadcasted_iota(jnp.int32, (CUM_BLOCK, CUM_BLOCK), 1)
    tri = jnp.where(r >= c, 1.0, 0.0).astype(BF16)
    carry = jnp.zeros((1, LANES), F32)
    for r0 in range(0, mask.shape[0], CUM_BLOCK):
        blk = mask[r0:r0 + CUM_BLOCK]
        cum = _dot(tri, blk.astype(BF16)) + carry
        rcol_ref[r0:r0 + CUM_BLOCK, :] = cum
        rrow_ref[:, r0:r0 + CUM_BLOCK] = cum.T
        mrow_ref[:, r0:r0 + CUM_BLOCK] = blk.T
        carry = cum[CUM_BLOCK - 1:CUM_BLOCK, :]


def _router(h, gain, w, b):
    t = h.shape[0]
    tm = MOE_CHUNK
    token_major = pl.BlockSpec((tm, LANES), lambda i: (i, 0))
    lane_major = pl.BlockSpec((None, LANES, tm), lambda i: (i, 0, 0))
    return pl.pallas_call(
        _router_kernel,
        grid=(t // tm,),
        in_specs=[pl.BlockSpec((tm, D_MODEL), lambda i: (i, 0)),
                  pl.BlockSpec((1, D_MODEL), lambda i: (0, 0)),
                  pl.BlockSpec((D_MODEL, LANES), lambda i: (0, 0)),
                  pl.BlockSpec((1, LANES), lambda i: (0, 0))],
        out_specs=[token_major, token_major, token_major, lane_major, lane_major],
        out_shape=[jax.ShapeDtypeStruct((t, LANES), F32)] * 3
                  + [jax.ShapeDtypeStruct((t // tm, LANES, tm), F32)] * 2,
        compiler_params=_params("parallel"),
        name="moe_router",
    )(h, gain.reshape(1, D_MODEL), w, b)


def _ple_kernel(h_ref, g_ref, p_ref, wg_ref, wp_ref, fg_ref, o_ref, *, final, tn):
    h = h_ref[...]
    xn = _rms(h, g_ref[...]).astype(BF16)
    pe = p_ref[...].astype(BF16)
    outs = []
    for c0 in range(0, D_MODEL, tn):
        gate = jax.nn.sigmoid(_dot(xn, wg_ref[:, c0:c0 + tn]))
        outs.append(h[:, c0:c0 + tn] + gate * _dot(pe, wp_ref[:, c0:c0 + tn]))
    new = jnp.concatenate(outs, axis=1)
    o_ref[...] = _rms(new, fg_ref[...]) if final else new


def _ple(h, gain, p, wg, wp, final_gain, final, tm=1024, tn=512):
    t, pd = p.shape
    return pl.pallas_call(
        functools.partial(_ple_kernel, final=final, tn=tn),
        grid=(t // tm,),
        in_specs=[pl.BlockSpec((tm, D_MODEL), lambda i: (i, 0)),
                  pl.BlockSpec((1, D_MODEL), lambda i: (0, 0)),
                  pl.BlockSpec((tm, pd), lambda i: (i, 0)),
                  pl.BlockSpec((D_MODEL, D_MODEL), lambda i: (0, 0)),
                  pl.BlockSpec((pd, D_MODEL), lambda i: (0, 0)),
                  pl.BlockSpec((1, D_MODEL), lambda i: (0, 0))],
        out_specs=pl.BlockSpec((tm, D_MODEL), lambda i: (i, 0)),
        out_shape=jax.ShapeDtypeStruct((t, D_MODEL), F32),
        compiler_params=_params("parallel"),
        name="ple",
    )(h, gain.reshape(1, D_MODEL), p, wg, wp, final_gain.reshape(1, D_MODEL))


def _compress_kernel(x01_ref, x23_ref, pos_ref, w1_ref, b1_ref, w2_ref, o_ref, *, blocks):
    stride = NSA_CMP_STRIDE
    hidden = b1_ref.shape[1]
    low = lax.broadcasted_iota(jnp.int32, (blocks, LANES), 1) < HEAD_DIM
    first = [jnp.zeros((blocks, hidden), F32) for _ in range(NSA_GROUPS)]
    second = [jnp.zeros((blocks, hidden), F32) for _ in range(NSA_GROUPS)]
    for j in range(stride):
        for half, x_ref in enumerate((x01_ref, x23_ref)):
            xs = x_ref[pl.ds(j, blocks, stride=stride), :]
            xa = xs + pos_ref[j:j + 1, :]
            xb = xs + pos_ref[stride + j:stride + j + 1, :]
            for sub in range(2):
                g = 2 * half + sub
                keep = low if sub == 0 else jnp.logical_not(low)
                first[g] = first[g] + _dot(jnp.where(keep, xa, 0.0).astype(BF16), w1_ref[j])
                second[g] = second[g] + _dot(jnp.where(keep, xb, 0.0).astype(BF16), w1_ref[stride + j])
    row = lax.broadcasted_iota(jnp.int32, (blocks, NSA_WIDTH), 0)
    for g in range(NSA_GROUPS):
        hid = first[g] + pltpu.roll(second[g], blocks - 1, 0) + b1_ref[...]
        out = _dot(jax.nn.gelu(hid, approximate=True).astype(BF16), w2_ref[...])
        o_ref[g] = jnp.where(row == blocks - 1, 0.0, out)


def _compress(proj, col_block, pos, w1, b1, w2, batch, seq):
    blocks = seq // NSA_CMP_STRIDE
    hidden = w1.shape[1]
    w1_rep = jnp.tile(w1.reshape(NSA_CMP_LEN, HEAD_DIM, hidden), (1, 2, 1))
    pos_rep = jnp.tile(pos, (1, 2))
    halves = NSA_WIDTH // LANES
    return pl.pallas_call(
        functools.partial(_compress_kernel, blocks=blocks),
        grid=(batch,),
        in_specs=[pl.BlockSpec((seq, LANES), lambda b: (b, halves * col_block)),
                  pl.BlockSpec((seq, LANES), lambda b: (b, halves * col_block + 1)),
                  pl.BlockSpec((NSA_CMP_LEN, LANES), lambda b: (0, 0)),
                  pl.BlockSpec((NSA_CMP_LEN, LANES, hidden), lambda b: (0, 0, 0)),
                  pl.BlockSpec((1, hidden), lambda b: (0, 0)),
                  pl.BlockSpec((hidden, NSA_WIDTH), lambda b: (0, 0))],
        out_specs=pl.BlockSpec((NSA_GROUPS, blocks, NSA_WIDTH), lambda b: (b, 0, 0)),
        out_shape=jax.ShapeDtypeStruct((batch * NSA_GROUPS, blocks, NSA_WIDTH), F32),
        compiler_params=_params("parallel"),
        name="nsa_compress",
    )(proj, proj, pos_rep, w1_rep.astype(BF16), b1.reshape(1, hidden), w2)


NSA_TQ = 256
NSA_TK = 512
NSA_WIDTH = NSA_HPG * HEAD_DIM
LOG_HEAD_DIM = HEAD_DIM.bit_length() - 1
LOG_SEL_LEN = NSA_SEL_LEN.bit_length() - 1


def _nsa_kernel(q_ref, kc_ref, vc_ref, ks_ref, vs_ref, kw_ref, vw_ref, gate_ref, o_ref,
                ksb, kwb, vst_all, vwt_all, vct, gt_ref, osel_ref, *, seq, n_cmp):
    g = pl.program_id(1)
    i = pl.program_id(2)
    tq, tk = NSA_TQ, NSA_TK
    cols = NSA_HPG * tq
    vrows = HEAD_DIM + ONES_ROWS
    grow = pl.multiple_of(g * vrows, vrows)

    @pl.when((i == 0) & (g == 0))
    def _():
        def fill(ci, carry):
            rs = pl.ds(pl.multiple_of(ci * NSA_WIDTH, NSA_WIDTH), NSA_WIDTH)
            ksb[rs, :] = ks_ref[rs, :].astype(BF16)
            kwb[rs, :] = kw_ref[rs, :].astype(BF16)
            for src, dst in ((vs_ref, vst_all), (vw_ref, vwt_all)):
                vt = src[rs, :].T.astype(BF16)
                for grp in range(NSA_GROUPS):
                    dst[grp * vrows:grp * vrows + HEAD_DIM, rs] = vt[grp * HEAD_DIM:(grp + 1) * HEAD_DIM]
                    dst[grp * vrows + HEAD_DIM:(grp + 1) * vrows, rs] = jnp.ones((ONES_ROWS, NSA_WIDTH), BF16)
            return carry

        lax.fori_loop(0, seq // NSA_WIDTH, fill, 0)

    @pl.when(i == 0)
    def _():
        vct[...] = vc_ref[0].T[0:HEAD_DIM, :].astype(BF16)

    vst = vst_all.at[pl.ds(grow, vrows)]
    vwt = vwt_all.at[pl.ds(grow, vrows)]

    t0 = i * tq
    lane_grp = lax.broadcasted_iota(jnp.int32, (tq, NSA_WIDTH), 1) >> LOG_HEAD_DIM
    q = q_ref[...] * SCALE2
    rolled = [q] + [pltpu.roll(q, s * HEAD_DIM, 1) for s in range(1, NSA_HPG)]
    parts = []
    for j in range(NSA_HPG):
        shift = (g - j) & (NSA_HPG - 1)
        moved = jnp.where(shift == 0, rolled[0],
                          jnp.where(shift == 1, rolled[1], jnp.where(shift == 2, rolled[2], rolled[3])))
        parts.append(jnp.where(lane_grp == g, moved, 0.0))
    qs = jnp.concatenate(parts, axis=0).astype(BF16)

    def heads_sum(x):
        out = x[:, 0:tq]
        for j in range(1, NSA_HPG):
            out = out + x[:, j * tq:(j + 1) * tq]
        return out

    def lanes4(x):
        return jnp.concatenate([x] * NSA_HPG, axis=1)

    wk = NSA_WINDOW + tq
    ws = pl.multiple_of(jnp.maximum(t0 - NSA_WINDOW, 0), tq)
    kpos = ws + lax.broadcasted_iota(jnp.int32, (wk, tq), 0)
    qw = t0 + lax.broadcasted_iota(jnp.int32, (wk, tq), 1)
    bias_w = jnp.where((kpos <= qw) & (kpos > qw - NSA_WINDOW), 0.0, NEG_INF)
    sw = _dot_nt(kwb[pl.ds(ws, wk), :], qs) + lanes4(bias_w)

    nrow = lax.broadcasted_iota(jnp.int32, (LANES, cols), 0)
    tcol = t0 + (lax.broadcasted_iota(jnp.int32, (LANES, cols), 1) & (tq - 1))
    valid_c = (nrow * NSA_CMP_STRIDE + NSA_CMP_LEN - 1 <= tcol) & (nrow < n_cmp)
    sc = jnp.where(valid_c, _dot_nt(kc_ref[0].astype(BF16), qs), NEG_INF)
    mc = jnp.max(sc, axis=0, keepdims=True)
    ec = jnp.where(valid_c, jnp.exp2(sc - mc), 0.0)
    dc = jnp.sum(ec, axis=0, keepdims=True)
    pc = ec / jnp.where(dc > 0.0, dc, 1.0)
    o_cmp = _dot(vct[...], pc.astype(BF16))
    pc_sum = heads_sum(pc)

    n_sel_blocks = seq // NSA_SEL_LEN
    jrow = lax.broadcasted_iota(jnp.int32, (LANES, LANES), 0)
    ncol = lax.broadcasted_iota(jnp.int32, (LANES, LANES), 1)
    overlap = ((ncol * NSA_CMP_STRIDE < (jrow + 1) * NSA_SEL_LEN)
               & (ncol * NSA_CMP_STRIDE + NSA_CMP_LEN > jrow * NSA_SEL_LEN)
               & (ncol < n_cmp) & (jrow < n_sel_blocks))
    overlap = jnp.where(overlap, 1.0, 0.0).astype(BF16)
    hi, mid, lo = _split3(pc_sum)
    imp = (_dot(overlap, hi) + _dot(overlap, mid) + _dot(overlap, lo))[0:n_sel_blocks]
    pw = jnp.exp2(sw - jnp.max(sw, axis=0, keepdims=True))
    o_win = _dot(vwt[:, pl.ds(ws, wk)], pw.astype(BF16))
    o_win = o_win[0:HEAD_DIM] / o_win[HEAD_DIM:HEAD_DIM + 1]
    blk = lax.broadcasted_iota(jnp.int32, (n_sel_blocks, tq), 0)
    cur = (t0 + lax.broadcasted_iota(jnp.int32, (n_sel_blocks, tq), 1)) >> LOG_SEL_LEN
    forced = (blk == 0) | (blk == cur) | (blk == cur - 1)
    imp = jnp.where(blk > cur, -1.0, jnp.where(forced, 1e6, imp))
    beaten = jnp.zeros((n_sel_blocks, tq), jnp.int32)
    for c in range(n_sel_blocks):
        row = imp[c:c + 1, :]
        wins = (row > imp) | ((row == imp) & (blk > c))
        beaten = beaten + jnp.where(wins, 1, 0)
    sel_bias = jnp.where(beaten < NSA_TOP_N, 0.0, NEG_INF)
    sel_bias = jnp.concatenate([sel_bias, jnp.zeros((LANES - n_sel_blocks, tq), F32)], axis=0).astype(BF16)

    n_tiles = (t0 + tq - 1) // tk + 1

    def sel_branch(n):
        nk = n * tk
        erow = lax.broadcasted_iota(jnp.int32, (nk, LANES), 0)
        ecol = lax.broadcasted_iota(jnp.int32, (nk, LANES), 1)
        expand = jnp.where((erow >> LOG_SEL_LEN) == ecol, 1.0, 0.0).astype(BF16)
        krow = lax.broadcasted_iota(jnp.int32, (nk, tq), 0)
        qlane = t0 + lax.broadcasted_iota(jnp.int32, (nk, tq), 1)
        bias = jnp.where(krow <= qlane, _dot(expand, sel_bias), NEG_INF)
        s = _dot_nt(ksb[0:nk, :], qs) + lanes4(bias)
        p = jnp.exp2(s - jnp.max(s, axis=0, keepdims=True))
        acc = _dot(vst[:, 0:nk], p.astype(BF16))
        osel_ref[...] = acc[0:HEAD_DIM] / acc[HEAD_DIM:HEAD_DIM + 1]

    for n in range(1, seq // tk + 1):
        pl.when(n_tiles == n)(functools.partial(sel_branch, n))
    o_sel = osel_ref[...]

    gt_ref[...] = gate_ref[...].T
    out = jnp.zeros((HEAD_DIM, cols), F32)
    for c, branch in enumerate((o_cmp, o_sel, o_win)):
        gate = jnp.concatenate([gt_ref[pl.ds(c * NSA_HEADS + g * NSA_HPG + j, 1), :] for j in range(NSA_HPG)],
                               axis=1)
        out = out + gate * branch
    o_ref[...] = jnp.concatenate([out[:, j * tq:(j + 1) * tq] for j in range(NSA_HPG)], axis=0).T.astype(o_ref.dtype)


def _nsa_attention(proj, gates, kc_cmp, vc_cmp, batch, seq, n_cmp):
    nq = seq // NSA_TQ
    ncb = kc_cmp.shape[1]
    qcols = NSA_HEADS * HEAD_DIM // NSA_WIDTH
    kv = lambda c: pl.BlockSpec((seq, NSA_WIDTH), lambda b, g, i, c=c: (b, qcols + c))
    cmp_spec = pl.BlockSpec((1, ncb, NSA_WIDTH), lambda b, g, i: (b * NSA_GROUPS + g, 0, 0))
    return pl.pallas_call(
        functools.partial(_nsa_kernel, seq=seq, n_cmp=n_cmp),
        grid=(batch, NSA_GROUPS, nq),
        in_specs=[pl.BlockSpec((NSA_TQ, NSA_WIDTH), lambda b, g, i: (b * nq + i, g)),
                  cmp_spec, cmp_spec, kv(2), kv(3), kv(4), kv(5),
                  pl.BlockSpec((NSA_TQ, LANES), lambda b, g, i: (b * nq + i, 0))],
        out_specs=pl.BlockSpec((NSA_TQ, NSA_WIDTH), lambda b, g, i: (b * nq + i, g)),
        out_shape=jax.ShapeDtypeStruct((batch * seq, NSA_HEADS * HEAD_DIM), BF16),
        scratch_shapes=[pltpu.VMEM((seq, NSA_WIDTH), BF16), pltpu.VMEM((seq, NSA_WIDTH), BF16),
                        pltpu.VMEM((NSA_GROUPS * (HEAD_DIM + ONES_ROWS), seq), BF16),
                        pltpu.VMEM((NSA_GROUPS * (HEAD_DIM + ONES_ROWS), seq), BF16),
                        pltpu.VMEM((HEAD_DIM, ncb), BF16), pltpu.VMEM((LANES, NSA_TQ), F32),
                        pltpu.VMEM((HEAD_DIM, NSA_HPG * NSA_TQ), F32)],
        compiler_params=_params("parallel", "arbitrary", "arbitrary"),
        name="nsa_attention",
    )(proj, kc_cmp, vc_cmp, proj, proj, proj, proj, gates)


def _pad_cols(w, width=LANES):
    return jnp.pad(w, ((0, 0), (0, width - w.shape[1])))


def kernel(x, p, positions, mix_norm, ffn_norm, ple_norm, ple_gate_w, ple_proj_w, fd_w_in, fd_forget_b, fd_w_out, dense_w_gate, dense_w_up, dense_w_down, nsa_w_in, nsa_pos_k, nsa_w1_k, nsa_b1_k, nsa_w2_k, nsa_pos_v, nsa_w1_v, nsa_b1_v, nsa_w2_v, nsa_w_out, moe_w_router, moe_b_router, moe_w_gate, moe_w_up, moe_w_down, final_norm):
    batch, seq, _ = x.shape
    t = batch * seq
    h = x.reshape(t, D_MODEL)
    cos, sin = _rope_tables(positions)

    n_main = 3 * FOX_WIDTH + 3 * DIL_WIDTH
    w_in = fd_w_in[0]
    tiles0 = tuple((c, 512, c in (3 * FOX_WIDTH, 3 * FOX_WIDTH + DIL_WIDTH)) for c in range(0, n_main, 512))
    proj0, log_f = _proj(h, mix_norm[0], w_in[:, :n_main].astype(BF16),
                         _pad_cols(w_in[:, n_main:]).astype(BF16),
                         _pad_cols(fd_forget_b[0].reshape(1, FOX_HEADS)).astype(F32),
                         cos, sin, tiles0, "log_sigmoid")
    ccol, crow = _cumsum(log_f, batch, seq)
    o_fox = _fox_attention(proj0, ccol, crow, batch, seq)
    o_dil = _dilated_attention(proj0, batch, seq)
    h = _outproj(h, [o_fox, o_dil], fd_w_out[0].astype(BF16))
    h = _ffn(h, ffn_norm[0], dense_w_gate[0].astype(BF16), dense_w_up[0].astype(BF16),
             dense_w_down[0].astype(BF16))
    h = _ple(h, ple_norm[0], p[0].reshape(t, -1), ple_gate_w[0].astype(BF16), ple_proj_w[0].astype(BF16),
             final_norm, final=False)

    qw = NSA_HEADS * HEAD_DIM
    n_main1 = qw + 6 * NSA_KV
    w_in1 = nsa_w_in[0]
    rope_cols = set(range(0, qw, 256)) | {qw, qw + 2 * NSA_KV, qw + 4 * NSA_KV}
    tiles1 = tuple((c, 256, c in rope_cols) for c in range(0, n_main1, 256))
    w_gate = w_in1[:, n_main1:].reshape(D_MODEL, NSA_HEADS, 3).transpose(0, 2, 1).reshape(D_MODEL, 3 * NSA_HEADS)
    proj1, gates = _proj(h, mix_norm[1], w_in1[:, :n_main1].astype(BF16), _pad_cols(w_gate).astype(BF16),
                         jnp.zeros((1, LANES), F32), cos, sin, tiles1, "sigmoid")
    n_cmp = (seq - NSA_CMP_LEN) // NSA_CMP_STRIDE + 1
    kc_cmp = _compress(proj1, qw // NSA_WIDTH, nsa_pos_k[0], nsa_w1_k[0], nsa_b1_k[0],
                       jnp.tile(nsa_w2_k[0], (1, NSA_HPG)).astype(BF16), batch, seq)
    vc_cmp = _compress(proj1, qw // NSA_WIDTH + 1, nsa_pos_v[0], nsa_w1_v[0], nsa_b1_v[0],
                       jnp.tile(nsa_w2_v[0], (1, NSA_HPG)).astype(BF16), batch, seq)
    o_nsa = _nsa_attention(proj1, gates, kc_cmp, vc_cmp, batch, seq, n_cmp)
    h = _outproj(h, [o_nsa], nsa_w_out[0].astype(BF16))
    cw, routed, rank_col, rank_row, routed_row = _router(
        h, ffn_norm[1], _pad_cols(moe_w_router[0]).astype(BF16),
        _pad_cols(moe_b_router[0].reshape(1, N_EXPERTS)).astype(F32))
    n_chunks = t // MOE_CHUNK
    counts = rank_col.reshape(n_chunks, MOE_CHUNK, LANES)[:, -1, :N_EXPERTS].astype(jnp.int32).reshape(-1)
    h = _moe_ffn(h, ffn_norm[1], cw, routed, rank_col, rank_row, routed_row, counts,
                 moe_w_gate[0].astype(BF16), moe_w_up[0].astype(BF16), moe_w_down[0].astype(BF16))
    h = _ple(h, ple_norm[1], p[1].reshape(t, -1), ple_gate_w[1].astype(BF16), ple_proj_w[1].astype(BF16),
             final_norm, final=True)
    return h.reshape(batch, seq, D_MODEL)
```

```python
import functools

import jax
import jax.numpy as jnp
from jax import lax
from jax.experimental import pallas as pl
from jax.experimental.pallas import tpu as pltpu

F32 = jnp.float32
BF16 = jnp.bfloat16

D_MODEL = 1024
HEAD_DIM = 64
LANES = 128
FOX_HEADS = 8
DIL_HEADS = 8
FOX_WIDTH = FOX_HEADS * HEAD_DIM
DIL_WIDTH = DIL_HEADS * HEAD_DIM
DIL_PATTERNS = ((128, 1), (512, 4), (2048, 16))
Q_BLOCK = 128
NSA_HEADS = 16
NSA_GROUPS = 4
NSA_HPG = NSA_HEADS // NSA_GROUPS
NSA_KV = NSA_GROUPS * HEAD_DIM
NSA_CMP_LEN = 32
NSA_CMP_STRIDE = 16
NSA_SEL_LEN = 64
NSA_TOP_N = 8
NSA_WINDOW = 512
N_EXPERTS = 8
ROPE_THETA = 10000.0
RMS_EPS = 1e-6
NEG_INF = -1e30
SCALE = HEAD_DIM ** -0.5
LOG2E = 1.4426950408889634
LN2 = 0.6931471805599453
SCALE2 = SCALE * LOG2E
ONES_ROWS = 16

VMEM_LIMIT_BYTES = 52 * 1024 * 1024


def _params(*sem):
    return pltpu.CompilerParams(dimension_semantics=sem, vmem_limit_bytes=VMEM_LIMIT_BYTES)


def _rms(x, g):
    return x * lax.rsqrt(jnp.mean(x * x, axis=-1, keepdims=True) + RMS_EPS) * g


def _dot(a, b):
    return jnp.dot(a, b, preferred_element_type=F32)


def _dot_nt(a, b):
    return lax.dot_general(a, b, (((1,), (1,)), ((), ())), preferred_element_type=F32)


def _split3(x):
    hi = x.astype(BF16)
    r = x - hi.astype(F32)
    mid = r.astype(BF16)
    lo = (r - mid.astype(F32)).astype(BF16)
    return hi, mid, lo


def _lane_col(x, idx):
    lane = lax.broadcasted_iota(jnp.int32, x.shape, 1)
    return jnp.sum(jnp.where(lane == idx, x, 0.0), axis=1, keepdims=True)


def _rope_table_kernel(pos_ref, inv_ref, cos_ref, sin_ref):
    ang = pos_ref[...] * inv_ref[...]
    lane = lax.broadcasted_iota(jnp.int32, ang.shape, 1)
    sign = jnp.where((lane & (HEAD_DIM - 1)) < HEAD_DIM // 2, -1.0, 1.0)
    cos_ref[...] = jnp.cos(ang)
    sin_ref[...] = jnp.sin(ang) * sign


def _rope_tables(positions):
    t = positions.size
    half = HEAD_DIM // 2
    inv_freq = ROPE_THETA ** (-jnp.arange(half, dtype=F32) / half)
    inv = jnp.tile(inv_freq, LANES // half).reshape(1, LANES)
    pos = jnp.broadcast_to(positions.astype(F32).reshape(t, 1), (t, LANES))
    tm = 1024
    return pl.pallas_call(
        _rope_table_kernel,
        grid=(t // tm,),
        in_specs=[pl.BlockSpec((tm, LANES), lambda i: (i, 0)),
                  pl.BlockSpec((1, LANES), lambda i: (0, 0))],
        out_specs=[pl.BlockSpec((tm, LANES), lambda i: (i, 0))] * 2,
        out_shape=[jax.ShapeDtypeStruct((t, LANES), F32)] * 2,
        compiler_params=_params("parallel"),
        name="rope_tables",
    )(pos, inv)


def _proj_kernel(h_ref, g_ref, w_ref, wa_ref, ba_ref, cos_ref, sin_ref, o_ref, oa_ref, *, tiles, aux_act):
    xn = _rms(h_ref[...], g_ref[...]).astype(BF16)
    for c0, width, rope in tiles:
        acc = _dot(xn, w_ref[:, c0:c0 + width])
        if rope:
            reps = width // LANES
            cos = jnp.tile(cos_ref[...], (1, reps))
            sin = jnp.tile(sin_ref[...], (1, reps))
            lane = lax.broadcasted_iota(jnp.int32, acc.shape, 1)
            first_half = (lane & (HEAD_DIM - 1)) < HEAD_DIM // 2
            partner = jnp.where(first_half,
                                pltpu.roll(acc, width - HEAD_DIM // 2, 1),
                                pltpu.roll(acc, HEAD_DIM // 2, 1))
            acc = acc * cos + partner * sin
        o_ref[:, c0:c0 + width] = acc
    aux = _dot(xn, wa_ref[...]) + ba_ref[...]
    if aux_act == "log_sigmoid":
        oa_ref[...] = jnp.minimum(aux, 0.0) - jnp.log1p(jnp.exp(-jnp.abs(aux)))
    else:
        oa_ref[...] = jax.nn.sigmoid(aux)


def _proj(h, gain, w, w_aux, b_aux, cos, sin, tiles, aux_act, tm=512):
    t, n = h.shape[0], w.shape[1]
    return pl.pallas_call(
        functools.partial(_proj_kernel, tiles=tiles, aux_act=aux_act),
        grid=(t // tm,),
        in_specs=[pl.BlockSpec((tm, D_MODEL), lambda i: (i, 0)),
                  pl.BlockSpec((1, D_MODEL), lambda i: (0, 0)),
                  pl.BlockSpec((D_MODEL, n), lambda i: (0, 0)),
                  pl.BlockSpec((D_MODEL, LANES), lambda i: (0, 0)),
                  pl.BlockSpec((1, LANES), lambda i: (0, 0)),
                  pl.BlockSpec((tm, LANES), lambda i: (i, 0)),
                  pl.BlockSpec((tm, LANES), lambda i: (i, 0))],
        out_specs=[pl.BlockSpec((tm, n), lambda i: (i, 0)),
                   pl.BlockSpec((tm, LANES), lambda i: (i, 0))],
        out_shape=[jax.ShapeDtypeStruct((t, n), F32), jax.ShapeDtypeStruct((t, LANES), F32)],
        compiler_params=_params("parallel"),
        name="in_proj",
    )(h, gain.reshape(1, D_MODEL), w, w_aux, b_aux, cos, sin)


CUM_BLOCK = 512


def _cumsum_kernel(x_ref, ccol_ref, crow_ref, carry_ref):
    j = pl.program_id(1)

    @pl.when(j == 0)
    def _():
        carry_ref[...] = jnp.zeros_like(carry_ref)

    x = x_ref[...]
    r = lax.broadcasted_iota(jnp.int32, (CUM_BLOCK, CUM_BLOCK), 0)
    c = lax.broadcasted_iota(jnp.int32, (CUM_BLOCK, CUM_BLOCK), 1)
    tri = jnp.where(r >= c, 1.0, 0.0).astype(BF16)
    hi, mid, lo = _split3(x)
    cum = _dot(tri, hi) + _dot(tri, mid) + _dot(tri, lo) + carry_ref[0:1, :]
    ccol_ref[...] = cum
    crow_ref[0] = cum.T
    carry_ref[...] = jnp.broadcast_to(cum[CUM_BLOCK - 1:CUM_BLOCK, :], carry_ref.shape)


def _cumsum(x, batch, seq):
    nb = seq // CUM_BLOCK
    return pl.pallas_call(
        _cumsum_kernel,
        grid=(batch, nb),
        in_specs=[pl.BlockSpec((CUM_BLOCK, LANES), lambda b, j: (b * nb + j, 0))],
        out_specs=[pl.BlockSpec((CUM_BLOCK, LANES), lambda b, j: (b * nb + j, 0)),
                   pl.BlockSpec((1, LANES, CUM_BLOCK), lambda b, j: (b, 0, j))],
        out_shape=[jax.ShapeDtypeStruct((batch * seq, LANES), F32),
                   jax.ShapeDtypeStruct((batch, LANES, seq), F32)],
        scratch_shapes=[pltpu.VMEM((8, LANES), F32)],
        compiler_params=_params("parallel", "arbitrary"),
        name="token_cumsum",
    )(x)


def _fox_kernel(q_ref, k_ref, v_ref, ccol_ref, crow_ref, o_ref, kb_ref, vt_ref, ck_ref, *, tq, seq):
    pair = pl.program_id(1)
    i = pl.program_id(2)
    tk = tq
    h0 = 2 * pair

    @pl.when(i == 0)
    def _():
        def fill(ci, carry):
            rs = pl.ds(pl.multiple_of(ci * tk, tk), tk)
            kb_ref[rs, :] = k_ref[rs, :].astype(BF16)
            vt_ref[0:LANES, rs] = v_ref[rs, :].T.astype(BF16)
            vt_ref[LANES:, rs] = jnp.ones((ONES_ROWS, tk), BF16)
            cc = ccol_ref[rs, :] * LOG2E
            ck_ref[0, rs, :] = jnp.broadcast_to(_lane_col(cc, h0), (tk, LANES))
            ck_ref[1, rs, :] = jnp.broadcast_to(_lane_col(cc, h0 + 1), (tk, LANES))
            return carry

        lax.fori_loop(0, seq // tk, fill, 0)

    t0 = pl.multiple_of(i * tq, tq)
    low = lax.broadcasted_iota(jnp.int32, (tq, LANES), 1) < HEAD_DIM
    q = q_ref[...] * SCALE2
    qs = jnp.concatenate([jnp.where(low, q, 0.0), jnp.where(low, 0.0, q)], axis=0).astype(BF16)
    cq = jnp.concatenate([crow_ref[0, pl.ds(h0, 1), pl.ds(t0, tq)],
                          crow_ref[0, pl.ds(h0 + 1, 1), pl.ds(t0, tq)]], axis=1) * LOG2E
    reps = tq // LANES

    def scores(k0, nk):
        ck = jnp.concatenate([ck_ref[0, k0:k0 + nk, :]] * reps + [ck_ref[1, k0:k0 + nk, :]] * reps, axis=1)
        return _dot_nt(kb_ref[k0:k0 + nk, :], qs) + cq - ck

    krow = lax.broadcasted_iota(jnp.int32, (tk, tq), 0)
    qlane = lax.broadcasted_iota(jnp.int32, (tk, tq), 1)
    causal = jnp.where(krow <= qlane, 0.0, NEG_INF)

    def branch(n):
        below = (n - 1) * tk
        s_diag = scores(below, tk) + jnp.concatenate([causal, causal], axis=1)
        m = jnp.max(s_diag, axis=0, keepdims=True)
        if below:
            s_below = scores(0, below)
            m = jnp.maximum(m, jnp.max(s_below, axis=0, keepdims=True))
            acc = _dot(vt_ref[:, 0:below], jnp.exp2(s_below - m).astype(BF16))
        else:
            acc = 0.0
        acc = acc + _dot(vt_ref[:, below:below + tk], jnp.exp2(s_diag - m).astype(BF16))
        out = acc[0:LANES] / acc[LANES:LANES + 1]
        o_ref[...] = jnp.concatenate([out[0:HEAD_DIM, 0:tq], out[HEAD_DIM:, tq:]], axis=0).T.astype(o_ref.dtype)

    for n in range(1, seq // tq + 1):
        pl.when(i == n - 1)(functools.partial(branch, n))


def _fox_attention(proj, ccol, crow, batch, seq, tq=512):
    nq = seq // tq
    npair = FOX_WIDTH // LANES
    return pl.pallas_call(
        functools.partial(_fox_kernel, tq=tq, seq=seq),
        grid=(batch, npair, nq),
        in_specs=[pl.BlockSpec((tq, LANES), lambda b, p, i: (b * nq + i, p)),
                  pl.BlockSpec((seq, LANES), lambda b, p, i: (b, npair + p)),
                  pl.BlockSpec((seq, LANES), lambda b, p, i: (b, 2 * npair + p)),
                  pl.BlockSpec((seq, LANES), lambda b, p, i: (b, 0)),
                  pl.BlockSpec((1, 8, seq), lambda b, p, i: (b, 0, 0))],
        out_specs=pl.BlockSpec((tq, LANES), lambda b, p, i: (b * nq + i, p)),
        out_shape=jax.ShapeDtypeStruct((batch * seq, FOX_WIDTH), BF16),
        scratch_shapes=[pltpu.VMEM((seq, LANES), BF16), pltpu.VMEM((LANES + ONES_ROWS, seq), BF16),
                        pltpu.VMEM((2, seq, LANES), F32)],
        compiler_params=_params("parallel", "parallel", "arbitrary"),
        name="fox_attention",
    )(proj, proj, proj, ccol, crow)


def _dil_kernel(q_ref, k_ref, v_ref, o_ref, os_ref, ls_ref, *, seq):
    low = lax.broadcasted_iota(jnp.int32, (Q_BLOCK, LANES), 1) < HEAD_DIM
    kr = lax.broadcasted_iota(jnp.int32, (2 * Q_BLOCK, Q_BLOCK), 0)
    qc = lax.broadcasted_iota(jnp.int32, (2 * Q_BLOCK, Q_BLOCK), 1)
    dist = qc + Q_BLOCK - kr
    group = 8

    for pi, (window, dil) in enumerate(DIL_PATTERNS):
        span = window // dil
        nb = (seq // dil) // Q_BLOCK
        band_bias = jnp.where((dist >= 0) & (dist <= span), 0.0, NEG_INF)
        first_bias = jnp.where(kr >= Q_BLOCK, band_bias, NEG_INF)

        def rows(start, dil=dil):
            return pl.ds(start, Q_BLOCK, stride=dil) if dil > 1 else pl.ds(start, Q_BLOCK)

        def unit(u, pi=pi, dil=dil, nb=nb, band_bias=band_bias, first_bias=first_bias, rows=rows):
            r = u // nb
            blk = u % nb
            cur = r + blk * (Q_BLOCK * dil)
            q = q_ref[rows(cur), :] * SCALE2
            qs = jnp.concatenate([jnp.where(low, q, 0.0), jnp.where(low, 0.0, q)], axis=0).astype(BF16)
            if nb == 1:
                kk = k_ref[rows(cur), :].astype(BF16)
                vt = v_ref[rows(cur), :].T.astype(BF16)
                bias = band_bias[Q_BLOCK:]
            else:
                prev = jnp.maximum(cur - Q_BLOCK * dil, r)
                kk = jnp.concatenate([k_ref[rows(prev), :], k_ref[rows(cur), :]], axis=0).astype(BF16)
                vt = jnp.concatenate([v_ref[rows(prev), :].T, v_ref[rows(cur), :].T], axis=1).astype(BF16)
                bias = jnp.where(blk > 0, band_bias, first_bias)
            vt = jnp.concatenate([vt, jnp.ones((ONES_ROWS, vt.shape[1]), BF16)], axis=0)
            s = _dot_nt(kk, qs) + jnp.concatenate([bias, bias], axis=1)
            return s, vt

        def softmax(s):
            m = jnp.max(s, axis=0, keepdims=True)
            return jnp.exp2(s - m).astype(BF16), m

        def finish(vt, p, m):
            ot = _dot(vt, p)
            den = ot[LANES:LANES + 1]
            ot = ot[0:LANES] * (1.0 / den)
            lse = m * LN2 + jnp.log(den)
            lse_t = jnp.concatenate([jnp.broadcast_to(lse[:, 0:Q_BLOCK], (HEAD_DIM, Q_BLOCK)),
                                     jnp.broadcast_to(lse[:, Q_BLOCK:], (HEAD_DIM, Q_BLOCK))], axis=0)
            return jnp.concatenate([ot[0:HEAD_DIM, 0:Q_BLOCK], ot[HEAD_DIM:, Q_BLOCK:]], axis=0).T, lse_t.T

        def units(gi, carry, pi=pi, nb=nb, dil=dil, rows=rows, unit=unit, softmax=softmax, finish=finish):
            ids = [gi * group + j for j in range(group)]
            scored = [unit(u) for u in ids]
            probs = [softmax(s) for s, _ in scored]
            outs = [finish(vt, p, lse) for (_, vt), (p, lse) in zip(scored, probs)]
            for u, (o, l) in zip(ids, outs):
                cur = u // nb + (u % nb) * (Q_BLOCK * dil)
                os_ref[pi, rows(cur), :] = o
                ls_ref[pi, rows(cur), :] = l
            return carry

        lax.fori_loop(0, dil * nb // group, units, 0)

    chunk = 256

    def combine(ci, carry):
        rs = pl.ds(pl.multiple_of(ci * chunk, chunk), chunk)
        l0, l1, l2 = ls_ref[0, rs, :], ls_ref[1, rs, :], ls_ref[2, rs, :]
        m = jnp.maximum(jnp.maximum(l0, l1), l2)
        e0, e1, e2 = jnp.exp(l0 - m), jnp.exp(l1 - m), jnp.exp(l2 - m)
        tot = e0 + e1 + e2
        o_ref[rs, :] = ((e0 / tot) * os_ref[0, rs, :] + (e1 / tot) * os_ref[1, rs, :]
                        + (e2 / tot) * os_ref[2, rs, :]).astype(o_ref.dtype)
        return carry

    lax.fori_loop(0, seq // chunk, combine, 0)


def _dilated_attention(proj, batch, seq):
    npair = DIL_WIDTH // LANES
    base = 3 * FOX_WIDTH // LANES
    return pl.pallas_call(
        functools.partial(_dil_kernel, seq=seq),
        grid=(batch, npair),
        in_specs=[pl.BlockSpec((seq, LANES), lambda b, p: (b, base + p)),
                  pl.BlockSpec((seq, LANES), lambda b, p: (b, base + npair + p)),
                  pl.BlockSpec((seq, LANES), lambda b, p: (b, base + 2 * npair + p))],
        out_specs=pl.BlockSpec((seq, LANES), lambda b, p: (b, p)),
        out_shape=jax.ShapeDtypeStruct((batch * seq, DIL_WIDTH), BF16),
        scratch_shapes=[pltpu.VMEM((3, seq, LANES), F32), pltpu.VMEM((3, seq, LANES), F32)],
        compiler_params=_params("parallel", "parallel"),
        name="dilated_attention",
    )(proj, proj, proj)


def _outproj_kernel(*refs, n_in, tn):
    h_ref = refs[0]
    a_refs = refs[1:1 + n_in]
    w_ref = refs[1 + n_in]
    o_ref = refs[2 + n_in]
    acts = [a[...].astype(BF16) for a in a_refs]
    for c0 in range(0, D_MODEL, tn):
        acc = h_ref[:, c0:c0 + tn]
        k0 = 0
        for a in acts:
            acc = acc + _dot(a, w_ref[k0:k0 + a.shape[1], c0:c0 + tn])
            k0 += a.shape[1]
        o_ref[:, c0:c0 + tn] = acc


def _outproj(h, acts, w, tm=1024, tn=512):
    t = h.shape[0]
    return pl.pallas_call(
        functools.partial(_outproj_kernel, n_in=len(acts), tn=tn),
        grid=(t // tm,),
        in_specs=([pl.BlockSpec((tm, D_MODEL), lambda i: (i, 0))]
                  + [pl.BlockSpec((tm, a.shape[1]), lambda i: (i, 0)) for a in acts]
                  + [pl.BlockSpec((D_MODEL, D_MODEL), lambda i: (0, 0))]),
        out_specs=pl.BlockSpec((tm, D_MODEL), lambda i: (i, 0)),
        out_shape=jax.ShapeDtypeStruct((t, D_MODEL), F32),
        compiler_params=_params("parallel"),
        name="out_proj",
    )(h, *acts, w)


def _swiglu_tile(x, wg, wu, wd):
    gate = _dot(x, wg)
    up = _dot(x, wu)
    return _dot((gate * jax.nn.sigmoid(gate) * up).astype(BF16), wd)


def _ffn_kernel(h_ref, g_ref, wg_ref, wu_ref, wd_ref, o_ref, xn_ref, acc_ref, *, n_f):
    f = pl.program_id(1)

    @pl.when(f == 0)
    def _():
        xn_ref[...] = _rms(h_ref[...], g_ref[...]).astype(BF16)
        acc_ref[...] = jnp.zeros_like(acc_ref)

    acc_ref[...] += _swiglu_tile(xn_ref[...], wg_ref[...], wu_ref[...], wd_ref[...])

    @pl.when(f == n_f - 1)
    def _():
        o_ref[...] = h_ref[...] + acc_ref[...]


def _ffn(h, gain, wg, wu, wd, tm=512, tf=1792):
    t, dff = h.shape[0], wg.shape[1]
    n_f = dff // tf
    return pl.pallas_call(
        functools.partial(_ffn_kernel, n_f=n_f),
        grid=(t // tm, n_f),
        in_specs=[pl.BlockSpec((tm, D_MODEL), lambda i, f: (i, 0)),
                  pl.BlockSpec((1, D_MODEL), lambda i, f: (0, 0)),
                  pl.BlockSpec((D_MODEL, tf), lambda i, f: (0, f)),
                  pl.BlockSpec((D_MODEL, tf), lambda i, f: (0, f)),
                  pl.BlockSpec((tf, D_MODEL), lambda i, f: (f, 0))],
        out_specs=pl.BlockSpec((tm, D_MODEL), lambda i, f: (i, 0)),
        out_shape=jax.ShapeDtypeStruct((t, D_MODEL), F32),
        scratch_shapes=[pltpu.VMEM((tm, D_MODEL), BF16), pltpu.VMEM((tm, D_MODEL), F32)],
        compiler_params=_params("parallel", "arbitrary"),
        name="dense_swiglu",
    )(h, gain.reshape(1, D_MODEL), wg, wu, wd)


MOE_CHUNK = 1024
MOE_ROWS = 128
MOE_SCATTER = 256


def _moe_kernel(cnt_ref, h_ref, g_ref, cw_ref, m_ref, rcol_ref, rrow_ref, mrow_ref, wg_ref, wu_ref, wd_ref,
                o_ref, xn_ref, xe_ref, ye_ref, *, n_f):
    c = pl.program_id(0)
    e = pl.program_id(1)
    f = pl.program_id(2)
    chunk = MOE_CHUNK
    n = cnt_ref[c * N_EXPERTS + e]
    n_scatter = (n + MOE_SCATTER - 1) // MOE_SCATTER
    n_tiles = (n + MOE_ROWS - 1) // MOE_ROWS

    @pl.when((e == 0) & (f == 0))
    def _():
        h = h_ref[...]
        xn_ref[...] = _rms(h, g_ref[...]).astype(BF16)
        o_ref[...] = h

    @pl.when(f == 0)
    def _():
        rank = rrow_ref[pl.ds(e, 1), :] * mrow_ref[pl.ds(e, 1), :]
        slot = lax.broadcasted_iota(jnp.int32, (MOE_ROWS, chunk), 0) + 1

        def gather(i, carry):
            rows = pl.ds(pl.multiple_of(i * MOE_ROWS, MOE_ROWS), MOE_ROWS)
            onehot = jnp.where(rank == (slot + i * MOE_ROWS).astype(F32), 1.0, 0.0).astype(BF16)
            xe_ref[rows, :] = _dot(onehot, xn_ref[...]).astype(BF16)
            ye_ref[rows, :] = jnp.zeros((MOE_ROWS, D_MODEL), F32)
            return carry

        def clear(i, carry):
            rows = pl.ds(pl.multiple_of(i * MOE_ROWS, MOE_ROWS), MOE_ROWS)
            ye_ref[rows, :] = jnp.zeros((MOE_ROWS, D_MODEL), F32)
            return carry

        lax.fori_loop(0, n_tiles, gather, 0)
        lax.fori_loop(n_tiles, n_scatter * (MOE_SCATTER // MOE_ROWS), clear, 0)

    def tile(i, carry):
        rows = pl.ds(pl.multiple_of(i * MOE_ROWS, MOE_ROWS), MOE_ROWS)
        ye_ref[rows, :] += _swiglu_tile(xe_ref[rows, :], wg_ref[...], wu_ref[...], wd_ref[...])
        return carry

    lax.fori_loop(0, n_tiles, tile, 0)

    @pl.when(f == n_f - 1)
    def _():
        rank = _lane_col(rcol_ref[...] * m_ref[...], e)
        weight = _lane_col(cw_ref[...], e)
        slot = lax.broadcasted_iota(jnp.int32, (chunk, MOE_SCATTER), 1) + 1

        def scatter(i, carry):
            rows = pl.ds(pl.multiple_of(i * MOE_SCATTER, MOE_SCATTER), MOE_SCATTER)
            onehot = jnp.where(rank == (slot + i * MOE_SCATTER).astype(F32), 1.0, 0.0).astype(BF16)
            o_ref[...] += weight * _dot(onehot, ye_ref[rows, :].astype(BF16))
            return carry

        lax.fori_loop(0, n_scatter, scatter, 0)


def _moe_ffn(h, gain, cw, mask, rcol, rrow, mrow, counts, wg, wu, wd, tf=1792):
    t, dff = h.shape[0], wg.shape[2]
    n_f = dff // tf
    chunk = MOE_CHUNK
    tok = lambda width: pl.BlockSpec((chunk, width), lambda c, e, f, cnt: (c, 0))
    lane_major = pl.BlockSpec((None, N_EXPERTS, chunk), lambda c, e, f, cnt: (c, 0, 0))
    grid_spec = pltpu.PrefetchScalarGridSpec(
        num_scalar_prefetch=1,
        grid=(t // chunk, N_EXPERTS, n_f),
        in_specs=[tok(D_MODEL),
                  pl.BlockSpec((1, D_MODEL), lambda c, e, f, cnt: (0, 0)),
                  tok(LANES), tok(LANES), tok(LANES), lane_major, lane_major,
                  pl.BlockSpec((None, D_MODEL, tf), lambda c, e, f, cnt: (e, 0, f)),
                  pl.BlockSpec((None, D_MODEL, tf), lambda c, e, f, cnt: (e, 0, f)),
                  pl.BlockSpec((None, tf, D_MODEL), lambda c, e, f, cnt: (e, f, 0))],
        out_specs=tok(D_MODEL),
        scratch_shapes=[pltpu.VMEM((chunk, D_MODEL), BF16), pltpu.VMEM((chunk, D_MODEL), BF16),
                        pltpu.VMEM((chunk, D_MODEL), F32)],
    )
    return pl.pallas_call(
        functools.partial(_moe_kernel, n_f=n_f),
        grid_spec=grid_spec,
        out_shape=jax.ShapeDtypeStruct((t, D_MODEL), F32),
        compiler_params=_params("parallel", "arbitrary", "arbitrary"),
        name="moe_swiglu",
    )(counts, h, gain.reshape(1, D_MODEL), cw, mask, rcol, rrow, mrow, wg, wu, wd)


def _router_kernel(h_ref, g_ref, w_ref, b_ref, cw_ref, m_ref, rcol_ref, rrow_ref, mrow_ref):
    xn = _rms(h_ref[...], g_ref[...]).astype(BF16)
    logits = _dot(xn, w_ref[...]) + b_ref[...]
    lane = lax.broadcasted_iota(jnp.int32, logits.shape, 1).astype(F32)
    logits = jnp.where(lane < N_EXPERTS, logits, -jnp.inf)
    m1 = jnp.max(logits, axis=1, keepdims=True)
    i1 = jnp.min(jnp.where(logits == m1, lane, float(LANES)), axis=1, keepdims=True)
    rest = jnp.where(lane == i1, -jnp.inf, logits)
    m2 = jnp.max(rest, axis=1, keepdims=True)
    i2 = jnp.min(jnp.where(rest == m2, lane, float(LANES)), axis=1, keepdims=True)
    e2 = jnp.exp(m2 - m1)
    w1 = 1.0 / (1.0 + e2)
    w2 = e2 / (1.0 + e2)
    cw_ref[...] = jnp.where(lane == i1, w1, jnp.where(lane == i2, w2, 0.0))
    mask = jnp.where((lane == i1) | (lane == i2), 1.0, 0.0)
    m_ref[...] = mask
    r = lax.broadcasted_iota(jnp.int32, (CUM_BLOCK, CUM_BLOCK), 0)
    c = lax.broadcasted_iota(jnp.int32, (CUM_BLOCK, CUM_BLOCK), 1)
    tri = jnp.where(r >= c, 1.0, 0.0).astype(BF16)
    carry = jnp.zeros((1, LANES), F32)
    for r0 in range(0, mask.shape[0], CUM_BLOCK):
        blk = mask[r0:r0 + CUM_BLOCK]
        cum = _dot(tri, blk.astype(BF16)) + carry
        rcol_ref[r0:r0 + CUM_BLOCK, :] = cum
        rrow_ref[:, r0:r0 + CUM_BLOCK] = cum.T
        mrow_ref[:, r0:r0 + CUM_BLOCK] = blk.T
        carry = cum[CUM_BLOCK - 1:CUM_BLOCK, :]


def _router(h, gain, w, b):
    t = h.shape[0]
    tm = MOE_CHUNK
    token_major = pl.BlockSpec((tm, LANES), lambda i: (i, 0))
    lane_major = pl.BlockSpec((None, LANES, tm), lambda i: (i, 0, 0))
    return pl.pallas_call(
        _router_kernel,
        grid=(t // tm,),
        in_specs=[pl.BlockSpec((tm, D_MODEL), lambda i: (i, 0)),
                  pl.BlockSpec((1, D_MODEL), lambda i: (0, 0)),
                  pl.BlockSpec((D_MODEL, LANES), lambda i: (0, 0)),
                  pl.BlockSpec((1, LANES), lambda i: (0, 0))],
        out_specs=[token_major, token_major, token_major, lane_major, lane_major],
        out_shape=[jax.ShapeDtypeStruct((t, LANES), F32)] * 3
                  + [jax.ShapeDtypeStruct((t // tm, LANES, tm), F32)] * 2,
        compiler_params=_params("parallel"),
        name="moe_router",
    )(h, gain.reshape(1, D_MODEL), w, b)


def _ple_kernel(h_ref, g_ref, p_ref, wg_ref, wp_ref, fg_ref, o_ref, *, final, tn):
    h = h_ref[...]
    xn = _rms(h, g_ref[...]).astype(BF16)
    pe = p_ref[...].astype(BF16)
    outs = []
    for c0 in range(0, D_MODEL, tn):
        gate = jax.nn.sigmoid(_dot(xn, wg_ref[:, c0:c0 + tn]))
        outs.append(h[:, c0:c0 + tn] + gate * _dot(pe, wp_ref[:, c0:c0 + tn]))
    new = jnp.concatenate(outs, axis=1)
    o_ref[...] = _rms(new, fg_ref[...]) if final else new


def _ple(h, gain, p, wg, wp, final_gain, final, tm=1024, tn=512):
    t, pd = p.shape
    return pl.pallas_call(
        functools.partial(_ple_kernel, final=final, tn=tn),
        grid=(t // tm,),
        in_specs=[pl.BlockSpec((tm, D_MODEL), lambda i: (i, 0)),
                  pl.BlockSpec((1, D_MODEL), lambda i: (0, 0)),
                  pl.BlockSpec((tm, pd), lambda i: (i, 0)),
                  pl.BlockSpec((D_MODEL, D_MODEL), lambda i: (0, 0)),
                  pl.BlockSpec((pd, D_MODEL), lambda i: (0, 0)),
                  pl.BlockSpec((1, D_MODEL), lambda i: (0, 0))],
        out_specs=pl.BlockSpec((tm, D_MODEL), lambda i: (i, 0)),
        out_shape=jax.ShapeDtypeStruct((t, D_MODEL), F32),
        compiler_params=_params("parallel"),
        name="ple",
    )(h, gain.reshape(1, D_MODEL), p, wg, wp, final_gain.reshape(1, D_MODEL))


def _compress_kernel(x01_ref, x23_ref, pos_ref, w1_ref, b1_ref, w2_ref, o_ref, *, blocks):
    stride = NSA_CMP_STRIDE
    hidden = b1_ref.shape[1]
    low = lax.broadcasted_iota(jnp.int32, (blocks, LANES), 1) < HEAD_DIM
    first = [jnp.zeros((blocks, hidden), F32) for _ in range(NSA_GROUPS)]
    second = [jnp.zeros((blocks, hidden), F32) for _ in range(NSA_GROUPS)]
    for j in range(stride):
        for half, x_ref in enumerate((x01_ref, x23_ref)):
            xs = x_ref[pl.ds(j, blocks, stride=stride), :]
            xa = xs + pos_ref[j:j + 1, :]
            xb = xs + pos_ref[stride + j:stride + j + 1, :]
            for sub in range(2):
                g = 2 * half + sub
                keep = low if sub == 0 else jnp.logical_not(low)
                first[g] = first[g] + _dot(jnp.where(keep, xa, 0.0).astype(BF16), w1_ref[j])
                second[g] = second[g] + _dot(jnp.where(keep, xb, 0.0).astype(BF16), w1_ref[stride + j])
    row = lax.broadcasted_iota(jnp.int32, (blocks, NSA_WIDTH), 0)
    for g in range(NSA_GROUPS):
        hid = first[g] + pltpu.roll(second[g], blocks - 1, 0) + b1_ref[...]
        out = _dot(jax.nn.gelu(hid, approximate=True).astype(BF16), w2_ref[...])
        o_ref[g] = jnp.where(row == blocks - 1, 0.0, out)


def _compress(proj, col_block, pos, w1, b1, w2, batch, seq):
    blocks = seq // NSA_CMP_STRIDE
    hidden = w1.shape[1]
    w1_rep = jnp.tile(w1.reshape(NSA_CMP_LEN, HEAD_DIM, hidden), (1, 2, 1))
    pos_rep = jnp.tile(pos, (1, 2))
    halves = NSA_WIDTH // LANES
    return pl.pallas_call(
        functools.partial(_compress_kernel, blocks=blocks),
        grid=(batch,),
        in_specs=[pl.BlockSpec((seq, LANES), lambda b: (b, halves * col_block)),
                  pl.BlockSpec((seq, LANES), lambda b: (b, halves * col_block + 1)),
                  pl.BlockSpec((NSA_CMP_LEN, LANES), lambda b: (0, 0)),
                  pl.BlockSpec((NSA_CMP_LEN, LANES, hidden), lambda b: (0, 0, 0)),
                  pl.BlockSpec((1, hidden), lambda b: (0, 0)),
                  pl.BlockSpec((hidden, NSA_WIDTH), lambda b: (0, 0))],
        out_specs=pl.BlockSpec((NSA_GROUPS, blocks, NSA_WIDTH), lambda b: (b, 0, 0)),
        out_shape=jax.ShapeDtypeStruct((batch * NSA_GROUPS, blocks, NSA_WIDTH), F32),
        compiler_params=_params("parallel"),
        name="nsa_compress",
    )(proj, proj, pos_rep, w1_rep.astype(BF16), b1.reshape(1, hidden), w2)


NSA_TQ = 256
NSA_TK = 256
NSA_WIDTH = NSA_HPG * HEAD_DIM
LOG_HEAD_DIM = HEAD_DIM.bit_length() - 1
LOG_SEL_LEN = NSA_SEL_LEN.bit_length() - 1


def _nsa_kernel(q_ref, kc_ref, vc_ref, ks_ref, vs_ref, kw_ref, vw_ref, gate_ref, o_ref,
                ksb, kwb, vst_all, vwt_all, vct, gt_ref, *, seq, n_cmp):
    g = pl.program_id(1)
    i = pl.program_id(2)
    tq, tk = NSA_TQ, NSA_TK
    cols = NSA_HPG * tq
    vrows = HEAD_DIM + ONES_ROWS
    grow = pl.multiple_of(g * vrows, vrows)

    @pl.when((i == 0) & (g == 0))
    def _():
        def fill(ci, carry):
            rs = pl.ds(pl.multiple_of(ci * NSA_WIDTH, NSA_WIDTH), NSA_WIDTH)
            ksb[rs, :] = ks_ref[rs, :].astype(BF16)
            kwb[rs, :] = kw_ref[rs, :].astype(BF16)
            for src, dst in ((vs_ref, vst_all), (vw_ref, vwt_all)):
                vt = src[rs, :].T.astype(BF16)
                for grp in range(NSA_GROUPS):
                    dst[grp * vrows:grp * vrows + HEAD_DIM, rs] = vt[grp * HEAD_DIM:(grp + 1) * HEAD_DIM]
                    dst[grp * vrows + HEAD_DIM:(grp + 1) * vrows, rs] = jnp.ones((ONES_ROWS, NSA_WIDTH), BF16)
            return carry

        lax.fori_loop(0, seq // NSA_WIDTH, fill, 0)

    @pl.when(i == 0)
    def _():
        vct[...] = vc_ref[0].T[0:HEAD_DIM, :].astype(BF16)

    vst = vst_all.at[pl.ds(grow, vrows)]
    vwt = vwt_all.at[pl.ds(grow, vrows)]

    t0 = i * tq
    lane_grp = lax.broadcasted_iota(jnp.int32, (tq, NSA_WIDTH), 1) >> LOG_HEAD_DIM
    q = q_ref[...] * SCALE2
    rolled = [q] + [pltpu.roll(q, s * HEAD_DIM, 1) for s in range(1, NSA_HPG)]
    parts = []
    for j in range(NSA_HPG):
        shift = (g - j) & (NSA_HPG - 1)
        moved = jnp.where(shift == 0, rolled[0],
                          jnp.where(shift == 1, rolled[1], jnp.where(shift == 2, rolled[2], rolled[3])))
        parts.append(jnp.where(lane_grp == g, moved, 0.0))
    qs = jnp.concatenate(parts, axis=0).astype(BF16)

    def heads_sum(x):
        out = x[:, 0:tq]
        for j in range(1, NSA_HPG):
            out = out + x[:, j * tq:(j + 1) * tq]
        return out

    def lanes4(x):
        return jnp.concatenate([x] * NSA_HPG, axis=1)

    q_heads = [qs[j * tq:(j + 1) * tq] for j in range(NSA_HPG)]
    krow = lax.broadcasted_iota(jnp.int32, (tk, tq), 0)
    qlane = t0 + lax.broadcasted_iota(jnp.int32, (tk, tq), 1)

    def unit_scores(k_ref, k0, bias):
        kb = k_ref[pl.ds(k0, tk), :]
        return [_dot_nt(kb, qj) + bias for qj in q_heads]

    def unit_softmax(s):
        m = jnp.max(s, axis=0, keepdims=True)
        return jnp.exp2(s - m).astype(BF16), m

    def merge(m_acc, acc, k0s, scored, vt_ref):
        probs = [[unit_softmax(s) for s in tile] for tile in scored]
        vals = [[_dot(vt_ref[:, pl.ds(k0, tk)], p) for p, _ in tile] for k0, tile in zip(k0s, probs)]
        m_out, acc_out = [], []
        for j in range(NSA_HPG):
            lanes = slice(j * tq, (j + 1) * tq)
            m_new = functools.reduce(jnp.maximum, [m_acc[:, lanes]] + [tile[j][1] for tile in probs])
            a = acc[:, lanes] * jnp.exp2(m_acc[:, lanes] - m_new)
            for tile_p, tile_v in zip(probs, vals):
                a = a + tile_v[j] * jnp.exp2(tile_p[j][1] - m_new)
            m_out.append(m_new)
            acc_out.append(a)
        return jnp.concatenate(m_out, axis=1), jnp.concatenate(acc_out, axis=1)

    fresh = (jnp.full((1, cols), NEG_INF, F32), jnp.zeros((vrows, cols), F32))

    ws = pl.multiple_of(jnp.maximum(t0 - NSA_WINDOW, 0), tq)
    win_k0s = [pl.multiple_of(ws + k0, tk) for k0 in range(0, NSA_WINDOW + tq, tk)]
    win_scored = []
    for k0 in win_k0s:
        kpos = k0 + krow
        win_scored.append(unit_scores(kwb, k0, jnp.where((kpos <= qlane) & (kpos > qlane - NSA_WINDOW), 0.0, NEG_INF)))

    nrow = lax.broadcasted_iota(jnp.int32, (LANES, cols), 0)
    tcol = t0 + (lax.broadcasted_iota(jnp.int32, (LANES, cols), 1) & (tq - 1))
    valid_c = (nrow * NSA_CMP_STRIDE + NSA_CMP_LEN - 1 <= tcol) & (nrow < n_cmp)
    sc = jnp.where(valid_c, _dot_nt(kc_ref[0].astype(BF16), qs), NEG_INF)
    mc = jnp.max(sc, axis=0, keepdims=True)
    ec = jnp.where(valid_c, jnp.exp2(sc - mc), 0.0)
    dc = jnp.sum(ec, axis=0, keepdims=True)
    pc = ec / jnp.where(dc > 0.0, dc, 1.0)
    o_cmp = _dot(vct[...], pc.astype(BF16))
    pc_sum = heads_sum(pc)

    n_sel_blocks = seq // NSA_SEL_LEN
    jrow = lax.broadcasted_iota(jnp.int32, (LANES, LANES), 0)
    ncol = lax.broadcasted_iota(jnp.int32, (LANES, LANES), 1)
    overlap = ((ncol * NSA_CMP_STRIDE < (jrow + 1) * NSA_SEL_LEN)
               & (ncol * NSA_CMP_STRIDE + NSA_CMP_LEN > jrow * NSA_SEL_LEN)
               & (ncol < n_cmp) & (jrow < n_sel_blocks))
    overlap = jnp.where(overlap, 1.0, 0.0).astype(BF16)
    hi, mid, lo = _split3(pc_sum)
    imp = (_dot(overlap, hi) + _dot(overlap, mid) + _dot(overlap, lo))[0:n_sel_blocks]
    _, acc_w = merge(*fresh, win_k0s, win_scored, vwt)
    o_win = acc_w[0:HEAD_DIM] / acc_w[HEAD_DIM:HEAD_DIM + 1]
    blk = lax.broadcasted_iota(jnp.int32, (n_sel_blocks, tq), 0)
    cur = (t0 + lax.broadcasted_iota(jnp.int32, (n_sel_blocks, tq), 1)) >> LOG_SEL_LEN
    forced = (blk == 0) | (blk == cur) | (blk == cur - 1)
    imp = jnp.where(blk > cur, -1.0, jnp.where(forced, 1e6, imp))
    beaten = jnp.zeros((n_sel_blocks, tq), jnp.int32)
    for c in range(n_sel_blocks):
        row = imp[c:c + 1, :]
        wins = (row > imp) | ((row == imp) & (blk > c))
        beaten = beaten + jnp.where(wins, 1, 0)
    sel_bias = jnp.where(beaten < NSA_TOP_N, 0.0, NEG_INF)
    sel_bias = jnp.concatenate([sel_bias, jnp.zeros((LANES - n_sel_blocks, tq), F32)], axis=0).astype(BF16)

    n_tiles = (t0 + tq - 1) // tk + 1
    erow = lax.broadcasted_iota(jnp.int32, (tk, LANES), 0)
    ecol = lax.broadcasted_iota(jnp.int32, (tk, LANES), 1)

    def sel_body(it, carry):
        k0s = [pl.multiple_of((2 * it + d) * tk, tk) for d in range(2)]
        scored = []
        for k0 in k0s:
            expand = jnp.where(((k0 + erow) >> LOG_SEL_LEN) == ecol, 1.0, 0.0).astype(BF16)
            scored.append(unit_scores(ksb, k0, jnp.where(k0 + krow <= qlane, _dot(expand, sel_bias), NEG_INF)))
        return merge(*carry, k0s, scored, vst)

    _, acc_s = lax.fori_loop(0, (n_tiles + 1) // 2, sel_body, fresh)
    o_sel = acc_s[0:HEAD_DIM] / acc_s[HEAD_DIM:HEAD_DIM + 1]

    gt_ref[...] = gate_ref[...].T
    out = jnp.zeros((HEAD_DIM, cols), F32)
    for c, branch in enumerate((o_cmp, o_sel, o_win)):
        gate = jnp.concatenate([gt_ref[pl.ds(c * NSA_HEADS + g * NSA_HPG + j, 1), :] for j in range(NSA_HPG)],
                               axis=1)
        out = out + gate * branch
    o_ref[...] = jnp.concatenate([out[:, j * tq:(j + 1) * tq] for j in range(NSA_HPG)], axis=0).T.astype(o_ref.dtype)


def _nsa_attention(proj, gates, kc_cmp, vc_cmp, batch, seq, n_cmp):
    nq = seq // NSA_TQ
    ncb = kc_cmp.shape[1]
    qcols = NSA_HEADS * HEAD_DIM // NSA_WIDTH
    kv = lambda c: pl.BlockSpec((seq, NSA_WIDTH), lambda b, g, i, c=c: (b, qcols + c))
    cmp_spec = pl.BlockSpec((1, ncb, NSA_WIDTH), lambda b, g, i: (b * NSA_GROUPS + g, 0, 0))
    return pl.pallas_call(
        functools.partial(_nsa_kernel, seq=seq, n_cmp=n_cmp),
        grid=(batch, NSA_GROUPS, nq),
        in_specs=[pl.BlockSpec((NSA_TQ, NSA_WIDTH), lambda b, g, i: (b * nq + i, g)),
                  cmp_spec, cmp_spec, kv(2), kv(3), kv(4), kv(5),
                  pl.BlockSpec((NSA_TQ, LANES), lambda b, g, i: (b * nq + i, 0))],
        out_specs=pl.BlockSpec((NSA_TQ, NSA_WIDTH), lambda b, g, i: (b * nq + i, g)),
        out_shape=jax.ShapeDtypeStruct((batch * seq, NSA_HEADS * HEAD_DIM), BF16),
        scratch_shapes=[pltpu.VMEM((seq, NSA_WIDTH), BF16), pltpu.VMEM((seq, NSA_WIDTH), BF16),
                        pltpu.VMEM((NSA_GROUPS * (HEAD_DIM + ONES_ROWS), seq), BF16),
                        pltpu.VMEM((NSA_GROUPS * (HEAD_DIM + ONES_ROWS), seq), BF16),
                        pltpu.VMEM((HEAD_DIM, ncb), BF16), pltpu.VMEM((LANES, NSA_TQ), F32)],
        compiler_params=_params("parallel", "arbitrary", "arbitrary"),
        name="nsa_attention",
    )(proj, kc_cmp, vc_cmp, proj, proj, proj, proj, gates)


def _pad_cols(w, width=LANES):
    return jnp.pad(w, ((0, 0), (0, width - w.shape[1])))


def kernel(x, p, positions, mix_norm, ffn_norm, ple_norm, ple_gate_w, ple_proj_w, fd_w_in, fd_forget_b, fd_w_out, dense_w_gate, dense_w_up, dense_w_down, nsa_w_in, nsa_pos_k, nsa_w1_k, nsa_b1_k, nsa_w2_k, nsa_pos_v, nsa_w1_v, nsa_b1_v, nsa_w2_v, nsa_w_out, moe_w_router, moe_b_router, moe_w_gate, moe_w_up, moe_w_down, final_norm):
    batch, seq, _ = x.shape
    t = batch * seq
    h = x.reshape(t, D_MODEL)
    cos, sin = _rope_tables(positions)

    n_main = 3 * FOX_WIDTH + 3 * DIL_WIDTH
    w_in = fd_w_in[0]
    tiles0 = tuple((c, 512, c in (3 * FOX_WIDTH, 3 * FOX_WIDTH + DIL_WIDTH)) for c in range(0, n_main, 512))
    proj0, log_f = _proj(h, mix_norm[0], w_in[:, :n_main].astype(BF16),
                         _pad_cols(w_in[:, n_main:]).astype(BF16),
                         _pad_cols(fd_forget_b[0].reshape(1, FOX_HEADS)).astype(F32),
                         cos, sin, tiles0, "log_sigmoid")
    ccol, crow = _cumsum(log_f, batch, seq)
    o_fox = _fox_attention(proj0, ccol, crow, batch, seq)
    o_dil = _dilated_attention(proj0, batch, seq)
    h = _outproj(h, [o_fox, o_dil], fd_w_out[0].astype(BF16))
    h = _ffn(h, ffn_norm[0], dense_w_gate[0].astype(BF16), dense_w_up[0].astype(BF16),
             dense_w_down[0].astype(BF16))
    h = _ple(h, ple_norm[0], p[0].reshape(t, -1), ple_gate_w[0].astype(BF16), ple_proj_w[0].astype(BF16),
             final_norm, final=False)

    qw = NSA_HEADS * HEAD_DIM
    n_main1 = qw + 6 * NSA_KV
    w_in1 = nsa_w_in[0]
    rope_cols = set(range(0, qw, 256)) | {qw, qw + 2 * NSA_KV, qw + 4 * NSA_KV}
    tiles1 = tuple((c, 256, c in rope_cols) for c in range(0, n_main1, 256))
    w_gate = w_in1[:, n_main1:].reshape(D_MODEL, NSA_HEADS, 3).transpose(0, 2, 1).reshape(D_MODEL, 3 * NSA_HEADS)
    proj1, gates = _proj(h, mix_norm[1], w_in1[:, :n_main1].astype(BF16), _pad_cols(w_gate).astype(BF16),
                         jnp.zeros((1, LANES), F32), cos, sin, tiles1, "sigmoid")
    n_cmp = (seq - NSA_CMP_LEN) // NSA_CMP_STRIDE + 1
    kc_cmp = _compress(proj1, qw // NSA_WIDTH, nsa_pos_k[0], nsa_w1_k[0], nsa_b1_k[0],
                       jnp.tile(nsa_w2_k[0], (1, NSA_HPG)).astype(BF16), batch, seq)
    vc_cmp = _compress(proj1, qw // NSA_WIDTH + 1, nsa_pos_v[0], nsa_w1_v[0], nsa_b1_v[0],
                       jnp.tile(nsa_w2_v[0], (1, NSA_HPG)).astype(BF16), batch, seq)
    o_nsa = _nsa_attention(proj1, gates, kc_cmp, vc_cmp, batch, seq, n_cmp)
    h = _outproj(h, [o_nsa], nsa_w_out[0].astype(BF16))
    cw, routed, rank_col, rank_row, routed_row = _router(
        h, ffn_norm[1], _pad_cols(moe_w_router[0]).astype(BF16),
        _pad_cols(moe_b_router[0].reshape(1, N_EXPERTS)).astype(F32))
    n_chunks = t // MOE_CHUNK
    counts = rank_col.reshape(n_chunks, MOE_CHUNK, LANES)[:, -1, :N_EXPERTS].astype(jnp.int32).reshape(-1)
    h = _moe_ffn(h, ffn_norm[1], cw, routed, rank_col, rank_row, routed_row, counts,
                 moe_w_gate[0].astype(BF16), moe_w_up[0].astype(BF16), moe_w_down[0].astype(BF16))
    h = _ple(h, ple_norm[1], p[1].reshape(t, -1), ple_gate_w[1].astype(BF16), ple_proj_w[1].astype(BF16),
             final_norm, final=True)
    return h.reshape(batch, seq, D_MODEL)
```

```python
import functools

import jax
import jax.numpy as jnp
from jax import lax
from jax.experimental import pallas as pl
from jax.experimental.pallas import tpu as pltpu

F32 = jnp.float32
BF16 = jnp.bfloat16

D_MODEL = 1024
HEAD_DIM = 64
LANES = 128
FOX_HEADS = 8
DIL_HEADS = 8
FOX_WIDTH = FOX_HEADS * HEAD_DIM
DIL_WIDTH = DIL_HEADS * HEAD_DIM
DIL_PATTERNS = ((128, 1), (512, 4), (2048, 16))
Q_BLOCK = 128
NSA_HEADS = 16
NSA_GROUPS = 4
NSA_HPG = NSA_HEADS // NSA_GROUPS
NSA_KV = NSA_GROUPS * HEAD_DIM
NSA_CMP_LEN = 32
NSA_CMP_STRIDE = 16
NSA_SEL_LEN = 64
NSA_TOP_N = 8
NSA_WINDOW = 512
N_EXPERTS = 8
ROPE_THETA = 10000.0
RMS_EPS = 1e-6
NEG_INF = -1e30
SCALE = HEAD_DIM ** -0.5
LOG2E = 1.4426950408889634
LN2 = 0.6931471805599453
SCALE2 = SCALE * LOG2E
ONES_ROWS = 16

VMEM_LIMIT_BYTES = 52 * 1024 * 1024


def _params(*sem):
    return pltpu.CompilerParams(dimension_semantics=sem, vmem_limit_bytes=VMEM_LIMIT_BYTES)


def _rms(x, g):
    return x * lax.rsqrt(jnp.mean(x * x, axis=-1, keepdims=True) + RMS_EPS) * g


def _dot(a, b):
    return jnp.dot(a, b, preferred_element_type=F32)


def _dot_nt(a, b):
    return lax.dot_general(a, b, (((1,), (1,)), ((), ())), preferred_element_type=F32)


def _split3(x):
    hi = x.astype(BF16)
    r = x - hi.astype(F32)
    mid = r.astype(BF16)
    lo = (r - mid.astype(F32)).astype(BF16)
    return hi, mid, lo


def _lane_col(x, idx):
    lane = lax.broadcasted_iota(jnp.int32, x.shape, 1)
    return jnp.sum(jnp.where(lane == idx, x, 0.0), axis=1, keepdims=True)


def _rope_table_kernel(pos_ref, inv_ref, cos_ref, sin_ref):
    ang = pos_ref[...] * inv_ref[...]
    lane = lax.broadcasted_iota(jnp.int32, ang.shape, 1)
    sign = jnp.where((lane & (HEAD_DIM - 1)) < HEAD_DIM // 2, -1.0, 1.0)
    cos_ref[...] = jnp.cos(ang)
    sin_ref[...] = jnp.sin(ang) * sign


def _rope_tables(positions):
    t = positions.size
    half = HEAD_DIM // 2
    inv_freq = ROPE_THETA ** (-jnp.arange(half, dtype=F32) / half)
    inv = jnp.tile(inv_freq, LANES // half).reshape(1, LANES)
    pos = jnp.broadcast_to(positions.astype(F32).reshape(t, 1), (t, LANES))
    tm = 1024
    return pl.pallas_call(
        _rope_table_kernel,
        grid=(t // tm,),
        in_specs=[pl.BlockSpec((tm, LANES), lambda i: (i, 0)),
                  pl.BlockSpec((1, LANES), lambda i: (0, 0))],
        out_specs=[pl.BlockSpec((tm, LANES), lambda i: (i, 0))] * 2,
        out_shape=[jax.ShapeDtypeStruct((t, LANES), F32)] * 2,
        compiler_params=_params("parallel"),
        name="rope_tables",
    )(pos, inv)


def _proj_kernel(h_ref, g_ref, w_ref, wa_ref, ba_ref, cos_ref, sin_ref, o_ref, oa_ref, *, tiles, aux_act):
    xn = _rms(h_ref[...], g_ref[...]).astype(BF16)
    for c0, width, rope in tiles:
        acc = _dot(xn, w_ref[:, c0:c0 + width])
        if rope:
            reps = width // LANES
            cos = jnp.tile(cos_ref[...], (1, reps))
            sin = jnp.tile(sin_ref[...], (1, reps))
            lane = lax.broadcasted_iota(jnp.int32, acc.shape, 1)
            first_half = (lane & (HEAD_DIM - 1)) < HEAD_DIM // 2
            partner = jnp.where(first_half,
                                pltpu.roll(acc, width - HEAD_DIM // 2, 1),
                                pltpu.roll(acc, HEAD_DIM // 2, 1))
            acc = acc * cos + partner * sin
        o_ref[:, c0:c0 + width] = acc
    aux = _dot(xn, wa_ref[...]) + ba_ref[...]
    if aux_act == "log_sigmoid":
        oa_ref[...] = jnp.minimum(aux, 0.0) - jnp.log1p(jnp.exp(-jnp.abs(aux)))
    else:
        oa_ref[...] = jax.nn.sigmoid(aux)


def _proj(h, gain, w, w_aux, b_aux, cos, sin, tiles, aux_act, tm=512):
    t, n = h.shape[0], w.shape[1]
    return pl.pallas_call(
        functools.partial(_proj_kernel, tiles=tiles, aux_act=aux_act),
        grid=(t // tm,),
        in_specs=[pl.BlockSpec((tm, D_MODEL), lambda i: (i, 0)),
                  pl.BlockSpec((1, D_MODEL), lambda i: (0, 0)),
                  pl.BlockSpec((D_MODEL, n), lambda i: (0, 0)),
                  pl.BlockSpec((D_MODEL, LANES), lambda i: (0, 0)),
                  pl.BlockSpec((1, LANES), lambda i: (0, 0)),
                  pl.BlockSpec((tm, LANES), lambda i: (i, 0)),
                  pl.BlockSpec((tm, LANES), lambda i: (i, 0))],
        out_specs=[pl.BlockSpec((tm, n), lambda i: (i, 0)),
                   pl.BlockSpec((tm, LANES), lambda i: (i, 0))],
        out_shape=[jax.ShapeDtypeStruct((t, n), F32), jax.ShapeDtypeStruct((t, LANES), F32)],
        compiler_params=_params("parallel"),
        name="in_proj",
    )(h, gain.reshape(1, D_MODEL), w, w_aux, b_aux, cos, sin)


CUM_BLOCK = 512


def _cumsum_kernel(x_ref, ccol_ref, crow_ref, carry_ref):
    j = pl.program_id(1)

    @pl.when(j == 0)
    def _():
        carry_ref[...] = jnp.zeros_like(carry_ref)

    x = x_ref[...]
    r = lax.broadcasted_iota(jnp.int32, (CUM_BLOCK, CUM_BLOCK), 0)
    c = lax.broadcasted_iota(jnp.int32, (CUM_BLOCK, CUM_BLOCK), 1)
    tri = jnp.where(r >= c, 1.0, 0.0).astype(BF16)
    hi, mid, lo = _split3(x)
    cum = _dot(tri, hi) + _dot(tri, mid) + _dot(tri, lo) + carry_ref[0:1, :]
    ccol_ref[...] = cum
    crow_ref[0] = cum.T
    carry_ref[...] = jnp.broadcast_to(cum[CUM_BLOCK - 1:CUM_BLOCK, :], carry_ref.shape)


def _cumsum(x, batch, seq):
    nb = seq // CUM_BLOCK
    return pl.pallas_call(
        _cumsum_kernel,
        grid=(batch, nb),
        in_specs=[pl.BlockSpec((CUM_BLOCK, LANES), lambda b, j: (b * nb + j, 0))],
        out_specs=[pl.BlockSpec((CUM_BLOCK, LANES), lambda b, j: (b * nb + j, 0)),
                   pl.BlockSpec((1, LANES, CUM_BLOCK), lambda b, j: (b, 0, j))],
        out_shape=[jax.ShapeDtypeStruct((batch * seq, LANES), F32),
                   jax.ShapeDtypeStruct((batch, LANES, seq), F32)],
        scratch_shapes=[pltpu.VMEM((8, LANES), F32)],
        compiler_params=_params("parallel", "arbitrary"),
        name="token_cumsum",
    )(x)


def _fox_kernel(q_ref, k_ref, v_ref, ccol_ref, crow_ref, o_ref, kb_ref, vt_ref, ck_ref, *, tq, seq):
    pair = pl.program_id(1)
    i = pl.program_id(2)
    tk = tq
    h0 = 2 * pair

    @pl.when(i == 0)
    def _():
        def fill(ci, carry):
            rs = pl.ds(pl.multiple_of(ci * tk, tk), tk)
            kb_ref[rs, :] = k_ref[rs, :].astype(BF16)
            vt_ref[0:LANES, rs] = v_ref[rs, :].T.astype(BF16)
            vt_ref[LANES:, rs] = jnp.ones((ONES_ROWS, tk), BF16)
            cc = ccol_ref[rs, :] * LOG2E
            ck_ref[0, rs, :] = jnp.broadcast_to(_lane_col(cc, h0), (tk, LANES))
            ck_ref[1, rs, :] = jnp.broadcast_to(_lane_col(cc, h0 + 1), (tk, LANES))
            return carry

        lax.fori_loop(0, seq // tk, fill, 0)

    t0 = pl.multiple_of(i * tq, tq)
    low = lax.broadcasted_iota(jnp.int32, (tq, LANES), 1) < HEAD_DIM
    q = q_ref[...] * SCALE2
    qs = jnp.concatenate([jnp.where(low, q, 0.0), jnp.where(low, 0.0, q)], axis=0).astype(BF16)
    cq = jnp.concatenate([crow_ref[0, pl.ds(h0, 1), pl.ds(t0, tq)],
                          crow_ref[0, pl.ds(h0 + 1, 1), pl.ds(t0, tq)]], axis=1) * LOG2E
    reps = tq // LANES

    def scores(k0, nk):
        ck = jnp.concatenate([ck_ref[0, k0:k0 + nk, :]] * reps + [ck_ref[1, k0:k0 + nk, :]] * reps, axis=1)
        return _dot_nt(kb_ref[k0:k0 + nk, :], qs) + cq - ck

    krow = lax.broadcasted_iota(jnp.int32, (tk, tq), 0)
    qlane = lax.broadcasted_iota(jnp.int32, (tk, tq), 1)
    causal = jnp.where(krow <= qlane, 0.0, NEG_INF)

    def branch(n):
        below = (n - 1) * tk
        s_diag = scores(below, tk) + jnp.concatenate([causal, causal], axis=1)
        m = jnp.max(s_diag, axis=0, keepdims=True)
        if below:
            s_below = scores(0, below)
            m = jnp.maximum(m, jnp.max(s_below, axis=0, keepdims=True))
            acc = _dot(vt_ref[:, 0:below], jnp.exp2(s_below - m).astype(BF16))
        else:
            acc = 0.0
        acc = acc + _dot(vt_ref[:, below:below + tk], jnp.exp2(s_diag - m).astype(BF16))
        out = acc[0:LANES] / acc[LANES:LANES + 1]
        o_ref[...] = jnp.concatenate([out[0:HEAD_DIM, 0:tq], out[HEAD_DIM:, tq:]], axis=0).T.astype(o_ref.dtype)

    for n in range(1, seq // tq + 1):
        pl.when(i == n - 1)(functools.partial(branch, n))


def _fox_attention(proj, ccol, crow, batch, seq, tq=512):
    nq = seq // tq
    npair = FOX_WIDTH // LANES
    return pl.pallas_call(
        functools.partial(_fox_kernel, tq=tq, seq=seq),
        grid=(batch, npair, nq),
        in_specs=[pl.BlockSpec((tq, LANES), lambda b, p, i: (b * nq + i, p)),
                  pl.BlockSpec((seq, LANES), lambda b, p, i: (b, npair + p)),
                  pl.BlockSpec((seq, LANES), lambda b, p, i: (b, 2 * npair + p)),
                  pl.BlockSpec((seq, LANES), lambda b, p, i: (b, 0)),
                  pl.BlockSpec((1, 8, seq), lambda b, p, i: (b, 0, 0))],
        out_specs=pl.BlockSpec((tq, LANES), lambda b, p, i: (b * nq + i, p)),
        out_shape=jax.ShapeDtypeStruct((batch * seq, FOX_WIDTH), BF16),
        scratch_shapes=[pltpu.VMEM((seq, LANES), BF16), pltpu.VMEM((LANES + ONES_ROWS, seq), BF16),
                        pltpu.VMEM((2, seq, LANES), F32)],
        compiler_params=_params("parallel", "parallel", "arbitrary"),
        name="fox_attention",
    )(proj, proj, proj, ccol, crow)


def _dil_kernel(q_ref, k_ref, v_ref, o_ref, os_ref, ls_ref, *, seq):
    low = lax.broadcasted_iota(jnp.int32, (Q_BLOCK, LANES), 1) < HEAD_DIM
    kr = lax.broadcasted_iota(jnp.int32, (2 * Q_BLOCK, Q_BLOCK), 0)
    qc = lax.broadcasted_iota(jnp.int32, (2 * Q_BLOCK, Q_BLOCK), 1)
    dist = qc + Q_BLOCK - kr
    group = 16

    for pi, (window, dil) in enumerate(DIL_PATTERNS):
        span = window // dil
        nb = (seq // dil) // Q_BLOCK
        band_bias = jnp.where((dist >= 0) & (dist <= span), 0.0, NEG_INF)
        first_bias = jnp.where(kr >= Q_BLOCK, band_bias, NEG_INF)

        def rows(start, dil=dil):
            return pl.ds(start, Q_BLOCK, stride=dil) if dil > 1 else pl.ds(start, Q_BLOCK)

        def unit(u, pi=pi, dil=dil, nb=nb, band_bias=band_bias, first_bias=first_bias, rows=rows):
            r = u // nb
            blk = u % nb
            cur = r + blk * (Q_BLOCK * dil)
            q = q_ref[rows(cur), :] * SCALE2
            qs = jnp.concatenate([jnp.where(low, q, 0.0), jnp.where(low, 0.0, q)], axis=0).astype(BF16)
            if nb == 1:
                kk = k_ref[rows(cur), :].astype(BF16)
                vt = v_ref[rows(cur), :].T.astype(BF16)
                bias = band_bias[Q_BLOCK:]
            else:
                prev = jnp.maximum(cur - Q_BLOCK * dil, r)
                kk = jnp.concatenate([k_ref[rows(prev), :], k_ref[rows(cur), :]], axis=0).astype(BF16)
                vt = jnp.concatenate([v_ref[rows(prev), :].T, v_ref[rows(cur), :].T], axis=1).astype(BF16)
                bias = jnp.where(blk > 0, band_bias, first_bias)
            vt = jnp.concatenate([vt, jnp.ones((ONES_ROWS, vt.shape[1]), BF16)], axis=0)
            s = _dot_nt(kk, qs) + jnp.concatenate([bias, bias], axis=1)
            return s, vt

        def softmax(s):
            m = jnp.max(s, axis=0, keepdims=True)
            return jnp.exp2(s - m).astype(BF16), m

        def finish(vt, p, m):
            ot = _dot(vt, p)
            den = ot[LANES:LANES + 1]
            ot = ot[0:LANES] * (1.0 / den)
            lse = m * LN2 + jnp.log(den)
            lse_t = jnp.concatenate([jnp.broadcast_to(lse[:, 0:Q_BLOCK], (HEAD_DIM, Q_BLOCK)),
                                     jnp.broadcast_to(lse[:, Q_BLOCK:], (HEAD_DIM, Q_BLOCK))], axis=0)
            return jnp.concatenate([ot[0:HEAD_DIM, 0:Q_BLOCK], ot[HEAD_DIM:, Q_BLOCK:]], axis=0).T, lse_t.T

        def units(gi, carry, pi=pi, nb=nb, dil=dil, rows=rows, unit=unit, softmax=softmax, finish=finish):
            ids = [gi * group + j for j in range(group)]
            scored = [unit(u) for u in ids]
            probs = [softmax(s) for s, _ in scored]
            outs = [finish(vt, p, lse) for (_, vt), (p, lse) in zip(scored, probs)]
            for u, (o, l) in zip(ids, outs):
                cur = u // nb + (u % nb) * (Q_BLOCK * dil)
                os_ref[pi, rows(cur), :] = o
                ls_ref[pi, rows(cur), :] = l
            return carry

        lax.fori_loop(0, dil * nb // group, units, 0)

    chunk = 256

    def combine(ci, carry):
        rs = pl.ds(pl.multiple_of(ci * chunk, chunk), chunk)
        l0, l1, l2 = ls_ref[0, rs, :], ls_ref[1, rs, :], ls_ref[2, rs, :]
        m = jnp.maximum(jnp.maximum(l0, l1), l2)
        e0, e1, e2 = jnp.exp(l0 - m), jnp.exp(l1 - m), jnp.exp(l2 - m)
        tot = e0 + e1 + e2
        o_ref[rs, :] = ((e0 / tot) * os_ref[0, rs, :] + (e1 / tot) * os_ref[1, rs, :]
                        + (e2 / tot) * os_ref[2, rs, :]).astype(o_ref.dtype)
        return carry

    lax.fori_loop(0, seq // chunk, combine, 0)


def _dilated_attention(proj, batch, seq):
    npair = DIL_WIDTH // LANES
    base = 3 * FOX_WIDTH // LANES
    return pl.pallas_call(
        functools.partial(_dil_kernel, seq=seq),
        grid=(batch, npair),
        in_specs=[pl.BlockSpec((seq, LANES), lambda b, p: (b, base + p)),
                  pl.BlockSpec((seq, LANES), lambda b, p: (b, base + npair + p)),
                  pl.BlockSpec((seq, LANES), lambda b, p: (b, base + 2 * npair + p))],
        out_specs=pl.BlockSpec((seq, LANES), lambda b, p: (b, p)),
        out_shape=jax.ShapeDtypeStruct((batch * seq, DIL_WIDTH), BF16),
        scratch_shapes=[pltpu.VMEM((3, seq, LANES), F32), pltpu.VMEM((3, seq, LANES), F32)],
        compiler_params=_params("parallel", "parallel"),
        name="dilated_attention",
    )(proj, proj, proj)


def _outproj_kernel(*refs, n_in, tn):
    h_ref = refs[0]
    a_refs = refs[1:1 + n_in]
    w_ref = refs[1 + n_in]
    o_ref = refs[2 + n_in]
    acts = [a[...].astype(BF16) for a in a_refs]
    for c0 in range(0, D_MODEL, tn):
        acc = h_ref[:, c0:c0 + tn]
        k0 = 0
        for a in acts:
            acc = acc + _dot(a, w_ref[k0:k0 + a.shape[1], c0:c0 + tn])
            k0 += a.shape[1]
        o_ref[:, c0:c0 + tn] = acc


def _outproj(h, acts, w, tm=1024, tn=512):
    t = h.shape[0]
    return pl.pallas_call(
        functools.partial(_outproj_kernel, n_in=len(acts), tn=tn),
        grid=(t // tm,),
        in_specs=([pl.BlockSpec((tm, D_MODEL), lambda i: (i, 0))]
                  + [pl.BlockSpec((tm, a.shape[1]), lambda i: (i, 0)) for a in acts]
                  + [pl.BlockSpec((D_MODEL, D_MODEL), lambda i: (0, 0))]),
        out_specs=pl.BlockSpec((tm, D_MODEL), lambda i: (i, 0)),
        out_shape=jax.ShapeDtypeStruct((t, D_MODEL), F32),
        compiler_params=_params("parallel"),
        name="out_proj",
    )(h, *acts, w)


def _swiglu_tile(x, wg, wu, wd):
    gate = _dot(x, wg)
    up = _dot(x, wu)
    return _dot((gate * jax.nn.sigmoid(gate) * up).astype(BF16), wd)


def _ffn_kernel(h_ref, g_ref, wg_ref, wu_ref, wd_ref, o_ref, xn_ref, acc_ref, *, n_f):
    f = pl.program_id(1)

    @pl.when(f == 0)
    def _():
        xn_ref[...] = _rms(h_ref[...], g_ref[...]).astype(BF16)
        acc_ref[...] = jnp.zeros_like(acc_ref)

    acc_ref[...] += _swiglu_tile(xn_ref[...], wg_ref[...], wu_ref[...], wd_ref[...])

    @pl.when(f == n_f - 1)
    def _():
        o_ref[...] = h_ref[...] + acc_ref[...]


def _ffn(h, gain, wg, wu, wd, tm=512, tf=1792):
    t, dff = h.shape[0], wg.shape[1]
    n_f = dff // tf
    return pl.pallas_call(
        functools.partial(_ffn_kernel, n_f=n_f),
        grid=(t // tm, n_f),
        in_specs=[pl.BlockSpec((tm, D_MODEL), lambda i, f: (i, 0)),
                  pl.BlockSpec((1, D_MODEL), lambda i, f: (0, 0)),
                  pl.BlockSpec((D_MODEL, tf), lambda i, f: (0, f)),
                  pl.BlockSpec((D_MODEL, tf), lambda i, f: (0, f)),
                  pl.BlockSpec((tf, D_MODEL), lambda i, f: (f, 0))],
        out_specs=pl.BlockSpec((tm, D_MODEL), lambda i, f: (i, 0)),
        out_shape=jax.ShapeDtypeStruct((t, D_MODEL), F32),
        scratch_shapes=[pltpu.VMEM((tm, D_MODEL), BF16), pltpu.VMEM((tm, D_MODEL), F32)],
        compiler_params=_params("parallel", "arbitrary"),
        name="dense_swiglu",
    )(h, gain.reshape(1, D_MODEL), wg, wu, wd)


MOE_CHUNK = 1024
MOE_ROWS = 128
MOE_SCATTER = 256


def _moe_kernel(cnt_ref, h_ref, g_ref, cw_ref, m_ref, rcol_ref, rrow_ref, mrow_ref, wg_ref, wu_ref, wd_ref,
                o_ref, xn_ref, xe_ref, ye_ref, *, n_f):
    c = pl.program_id(0)
    e = pl.program_id(1)
    f = pl.program_id(2)
    chunk = MOE_CHUNK
    n = cnt_ref[c * N_EXPERTS + e]
    n_scatter = (n + MOE_SCATTER - 1) // MOE_SCATTER
    n_tiles = (n + MOE_ROWS - 1) // MOE_ROWS

    @pl.when((e == 0) & (f == 0))
    def _():
        h = h_ref[...]
        xn_ref[...] = _rms(h, g_ref[...]).astype(BF16)
        o_ref[...] = h

    @pl.when(f == 0)
    def _():
        rank = rrow_ref[pl.ds(e, 1), :] * mrow_ref[pl.ds(e, 1), :]
        slot = lax.broadcasted_iota(jnp.int32, (MOE_ROWS, chunk), 0) + 1

        def gather(i, carry):
            rows = pl.ds(pl.multiple_of(i * MOE_ROWS, MOE_ROWS), MOE_ROWS)
            onehot = jnp.where(rank == (slot + i * MOE_ROWS).astype(F32), 1.0, 0.0).astype(BF16)
            xe_ref[rows, :] = _dot(onehot, xn_ref[...]).astype(BF16)
            ye_ref[rows, :] = jnp.zeros((MOE_ROWS, D_MODEL), F32)
            return carry

        def clear(i, carry):
            rows = pl.ds(pl.multiple_of(i * MOE_ROWS, MOE_ROWS), MOE_ROWS)
            ye_ref[rows, :] = jnp.zeros((MOE_ROWS, D_MODEL), F32)
            return carry

        lax.fori_loop(0, n_tiles, gather, 0)
        lax.fori_loop(n_tiles, n_scatter * (MOE_SCATTER // MOE_ROWS), clear, 0)

    def tile(i, carry):
        rows = pl.ds(pl.multiple_of(i * MOE_ROWS, MOE_ROWS), MOE_ROWS)
        ye_ref[rows, :] += _swiglu_tile(xe_ref[rows, :], wg_ref[...], wu_ref[...], wd_ref[...])
        return carry

    lax.fori_loop(0, n_tiles, tile, 0)

    @pl.when(f == n_f - 1)
    def _():
        rank = _lane_col(rcol_ref[...] * m_ref[...], e)
        weight = _lane_col(cw_ref[...], e)
        slot = lax.broadcasted_iota(jnp.int32, (chunk, MOE_SCATTER), 1) + 1

        def scatter(i, carry):
            rows = pl.ds(pl.multiple_of(i * MOE_SCATTER, MOE_SCATTER), MOE_SCATTER)
            onehot = jnp.where(rank == (slot + i * MOE_SCATTER).astype(F32), 1.0, 0.0).astype(BF16)
            o_ref[...] += weight * _dot(onehot, ye_ref[rows, :].astype(BF16))
            return carry

        lax.fori_loop(0, n_scatter, scatter, 0)


def _moe_ffn(h, gain, cw, mask, rcol, rrow, mrow, counts, wg, wu, wd, tf=1792):
    t, dff = h.shape[0], wg.shape[2]
    n_f = dff // tf
    chunk = MOE_CHUNK
    tok = lambda width: pl.BlockSpec((chunk, width), lambda c, e, f, cnt: (c, 0))
    lane_major = pl.BlockSpec((None, N_EXPERTS, chunk), lambda c, e, f, cnt: (c, 0, 0))
    grid_spec = pltpu.PrefetchScalarGridSpec(
        num_scalar_prefetch=1,
        grid=(t // chunk, N_EXPERTS, n_f),
        in_specs=[tok(D_MODEL),
                  pl.BlockSpec((1, D_MODEL), lambda c, e, f, cnt: (0, 0)),
                  tok(LANES), tok(LANES), tok(LANES), lane_major, lane_major,
                  pl.BlockSpec((None, D_MODEL, tf), lambda c, e, f, cnt: (e, 0, f)),
                  pl.BlockSpec((None, D_MODEL, tf), lambda c, e, f, cnt: (e, 0, f)),
                  pl.BlockSpec((None, tf, D_MODEL), lambda c, e, f, cnt: (e, f, 0))],
        out_specs=tok(D_MODEL),
        scratch_shapes=[pltpu.VMEM((chunk, D_MODEL), BF16), pltpu.VMEM((chunk, D_MODEL), BF16),
                        pltpu.VMEM((chunk, D_MODEL), F32)],
    )
    return pl.pallas_call(
        functools.partial(_moe_kernel, n_f=n_f),
        grid_spec=grid_spec,
        out_shape=jax.ShapeDtypeStruct((t, D_MODEL), F32),
        compiler_params=_params("parallel", "arbitrary", "arbitrary"),
        name="moe_swiglu",
    )(counts, h, gain.reshape(1, D_MODEL), cw, mask, rcol, rrow, mrow, wg, wu, wd)


def _router_kernel(h_ref, g_ref, w_ref, b_ref, cw_ref, m_ref, rcol_ref, rrow_ref, mrow_ref):
    xn = _rms(h_ref[...], g_ref[...]).astype(BF16)
    logits = _dot(xn, w_ref[...]) + b_ref[...]
    lane = lax.broadcasted_iota(jnp.int32, logits.shape, 1).astype(F32)
    logits = jnp.where(lane < N_EXPERTS, logits, -jnp.inf)
    m1 = jnp.max(logits, axis=1, keepdims=True)
    i1 = jnp.min(jnp.where(logits == m1, lane, float(LANES)), axis=1, keepdims=True)
    rest = jnp.where(lane == i1, -jnp.inf, logits)
    m2 = jnp.max(rest, axis=1, keepdims=True)
    i2 = jnp.min(jnp.where(rest == m2, lane, float(LANES)), axis=1, keepdims=True)
    e2 = jnp.exp(m2 - m1)
    w1 = 1.0 / (1.0 + e2)
    w2 = e2 / (1.0 + e2)
    cw_ref[...] = jnp.where(lane == i1, w1, jnp.where(lane == i2, w2, 0.0))
    mask = jnp.where((lane == i1) | (lane == i2), 1.0, 0.0)
    m_ref[...] = mask
    r = lax.broadcasted_iota(jnp.int32, (CUM_BLOCK, CUM_BLOCK), 0)
    c = lax.broadcasted_iota(jnp.int32, (CUM_BLOCK, CUM_BLOCK), 1)
    tri = jnp.where(r >= c, 1.0, 0.0).astype(BF16)
    carry = jnp.zeros((1, LANES), F32)
    for r0 in range(0, mask.shape[0], CUM_BLOCK):
        blk = mask[r0:r0 + CUM_BLOCK]
        cum = _dot(tri, blk.astype(BF16)) + carry
        rcol_ref[r0:r0 + CUM_BLOCK, :] = cum
        rrow_ref[:, r0:r0 + CUM_BLOCK] = cum.T
        mrow_ref[:, r0:r0 + CUM_BLOCK] = blk.T
        carry = cum[CUM_BLOCK - 1:CUM_BLOCK, :]


def _router(h, gain, w, b):
    t = h.shape[0]
    tm = MOE_CHUNK
    token_major = pl.BlockSpec((tm, LANES), lambda i: (i, 0))
    lane_major = pl.BlockSpec((None, LANES, tm), lambda i: (i, 0, 0))
    return pl.pallas_call(
        _router_kernel,
        grid=(t // tm,),
        in_specs=[pl.BlockSpec((tm, D_MODEL), lambda i: (i, 0)),
                  pl.BlockSpec((1, D_MODEL), lambda i: (0, 0)),
                  pl.BlockSpec((D_MODEL, LANES), lambda i: (0, 0)),
                  pl.BlockSpec((1, LANES), lambda i: (0, 0))],
        out_specs=[token_major, token_major, token_major, lane_major, lane_major],
        out_shape=[jax.ShapeDtypeStruct((t, LANES), F32)] * 3
                  + [jax.ShapeDtypeStruct((t // tm, LANES, tm), F32)] * 2,
        compiler_params=_params("parallel"),
        name="moe_router",
    )(h, gain.reshape(1, D_MODEL), w, b)


def _ple_kernel(h_ref, g_ref, p_ref, wg_ref, wp_ref, fg_ref, o_ref, *, final, tn):
    h = h_ref[...]
    xn = _rms(h, g_ref[...]).astype(BF16)
    pe = p_ref[...].astype(BF16)
    outs = []
    for c0 in range(0, D_MODEL, tn):
        gate = jax.nn.sigmoid(_dot(xn, wg_ref[:, c0:c0 + tn]))
        outs.append(h[:, c0:c0 + tn] + gate * _dot(pe, wp_ref[:, c0:c0 + tn]))
    new = jnp.concatenate(outs, axis=1)
    o_ref[...] = _rms(new, fg_ref[...]) if final else new


def _ple(h, gain, p, wg, wp, final_gain, final, tm=1024, tn=512):
    t, pd = p.shape
    return pl.pallas_call(
        functools.partial(_ple_kernel, final=final, tn=tn),
        grid=(t // tm,),
        in_specs=[pl.BlockSpec((tm, D_MODEL), lambda i: (i, 0)),
                  pl.BlockSpec((1, D_MODEL), lambda i: (0, 0)),
                  pl.BlockSpec((tm, pd), lambda i: (i, 0)),
                  pl.BlockSpec((D_MODEL, D_MODEL), lambda i: (0, 0)),
                  pl.BlockSpec((pd, D_MODEL), lambda i: (0, 0)),
                  pl.BlockSpec((1, D_MODEL), lambda i: (0, 0))],
        out_specs=pl.BlockSpec((tm, D_MODEL), lambda i: (i, 0)),
        out_shape=jax.ShapeDtypeStruct((t, D_MODEL), F32),
        compiler_params=_params("parallel"),
        name="ple",
    )(h, gain.reshape(1, D_MODEL), p, wg, wp, final_gain.reshape(1, D_MODEL))


def _compress_kernel(x01_ref, x23_ref, pos_ref, w1_ref, b1_ref, w2_ref, o_ref, *, blocks):
    stride = NSA_CMP_STRIDE
    hidden = b1_ref.shape[1]
    low = lax.broadcasted_iota(jnp.int32, (blocks, LANES), 1) < HEAD_DIM
    first = [jnp.zeros((blocks, hidden), F32) for _ in range(NSA_GROUPS)]
    second = [jnp.zeros((blocks, hidden), F32) for _ in range(NSA_GROUPS)]
    for j in range(stride):
        for half, x_ref in enumerate((x01_ref, x23_ref)):
            xs = x_ref[pl.ds(j, blocks, stride=stride), :]
            xa = xs + pos_ref[j:j + 1, :]
            xb = xs + pos_ref[stride + j:stride + j + 1, :]
            for sub in range(2):
                g = 2 * half + sub
                keep = low if sub == 0 else jnp.logical_not(low)
                first[g] = first[g] + _dot(jnp.where(keep, xa, 0.0).astype(BF16), w1_ref[j])
                second[g] = second[g] + _dot(jnp.where(keep, xb, 0.0).astype(BF16), w1_ref[stride + j])
    row = lax.broadcasted_iota(jnp.int32, (blocks, NSA_WIDTH), 0)
    for g in range(NSA_GROUPS):
        hid = first[g] + pltpu.roll(second[g], blocks - 1, 0) + b1_ref[...]
        out = _dot(jax.nn.gelu(hid, approximate=True).astype(BF16), w2_ref[...])
        o_ref[g] = jnp.where(row == blocks - 1, 0.0, out)


def _compress(proj, col_block, pos, w1, b1, w2, batch, seq):
    blocks = seq // NSA_CMP_STRIDE
    hidden = w1.shape[1]
    w1_rep = jnp.tile(w1.reshape(NSA_CMP_LEN, HEAD_DIM, hidden), (1, 2, 1))
    pos_rep = jnp.tile(pos, (1, 2))
    halves = NSA_WIDTH // LANES
    return pl.pallas_call(
        functools.partial(_compress_kernel, blocks=blocks),
        grid=(batch,),
        in_specs=[pl.BlockSpec((seq, LANES), lambda b: (b, halves * col_block)),
                  pl.BlockSpec((seq, LANES), lambda b: (b, halves * col_block + 1)),
                  pl.BlockSpec((NSA_CMP_LEN, LANES), lambda b: (0, 0)),
                  pl.BlockSpec((NSA_CMP_LEN, LANES, hidden), lambda b: (0, 0, 0)),
                  pl.BlockSpec((1, hidden), lambda b: (0, 0)),
                  pl.BlockSpec((hidden, NSA_WIDTH), lambda b: (0, 0))],
        out_specs=pl.BlockSpec((NSA_GROUPS, blocks, NSA_WIDTH), lambda b: (b, 0, 0)),
        out_shape=jax.ShapeDtypeStruct((batch * NSA_GROUPS, blocks, NSA_WIDTH), F32),
        compiler_params=_params("parallel"),
        name="nsa_compress",
    )(proj, proj, pos_rep, w1_rep.astype(BF16), b1.reshape(1, hidden), w2)


NSA_TQ = 256
NSA_TK = 512
NSA_WIDTH = NSA_HPG * HEAD_DIM
LOG_HEAD_DIM = HEAD_DIM.bit_length() - 1
LOG_SEL_LEN = NSA_SEL_LEN.bit_length() - 1


def _nsa_kernel(q_ref, kc_ref, vc_ref, ks_ref, vs_ref, kw_ref, vw_ref, gate_ref, o_ref,
                ksb, kwb, vst_all, vwt_all, vct, gt_ref, osel_ref, *, seq, n_cmp):
    g = pl.program_id(1)
    i = pl.program_id(2)
    tq, tk = NSA_TQ, NSA_TK
    cols = NSA_HPG * tq
    vrows = HEAD_DIM + ONES_ROWS
    grow = pl.multiple_of(g * vrows, vrows)

    @pl.when((i == 0) & (g == 0))
    def _():
        def fill(ci, carry):
            rs = pl.ds(pl.multiple_of(ci * NSA_WIDTH, NSA_WIDTH), NSA_WIDTH)
            ksb[rs, :] = ks_ref[rs, :].astype(BF16)
            kwb[rs, :] = kw_ref[rs, :].astype(BF16)
            for src, dst in ((vs_ref, vst_all), (vw_ref, vwt_all)):
                vt = src[rs, :].T.astype(BF16)
                for grp in range(NSA_GROUPS):
                    dst[grp * vrows:grp * vrows + HEAD_DIM, rs] = vt[grp * HEAD_DIM:(grp + 1) * HEAD_DIM]
                    dst[grp * vrows + HEAD_DIM:(grp + 1) * vrows, rs] = jnp.ones((ONES_ROWS, NSA_WIDTH), BF16)
            return carry

        lax.fori_loop(0, seq // NSA_WIDTH, fill, 0)

    @pl.when(i == 0)
    def _():
        vct[...] = vc_ref[0].T[0:HEAD_DIM, :].astype(BF16)

    vst = vst_all.at[pl.ds(grow, vrows)]
    vwt = vwt_all.at[pl.ds(grow, vrows)]

    t0 = i * tq
    lane_grp = lax.broadcasted_iota(jnp.int32, (tq, NSA_WIDTH), 1) >> LOG_HEAD_DIM
    q = q_ref[...] * SCALE2
    rolled = [q] + [pltpu.roll(q, s * HEAD_DIM, 1) for s in range(1, NSA_HPG)]
    parts = []
    for j in range(NSA_HPG):
        shift = (g - j) & (NSA_HPG - 1)
        moved = jnp.where(shift == 0, rolled[0],
                          jnp.where(shift == 1, rolled[1], jnp.where(shift == 2, rolled[2], rolled[3])))
        parts.append(jnp.where(lane_grp == g, moved, 0.0))
    qs = jnp.concatenate(parts, axis=0).astype(BF16)

    def heads_sum(x):
        out = x[:, 0:tq]
        for j in range(1, NSA_HPG):
            out = out + x[:, j * tq:(j + 1) * tq]
        return out

    def lanes4(x):
        return jnp.concatenate([x] * NSA_HPG, axis=1)

    wk = NSA_WINDOW + tq
    ws = pl.multiple_of(jnp.maximum(t0 - NSA_WINDOW, 0), tq)
    kpos = ws + lax.broadcasted_iota(jnp.int32, (wk, tq), 0)
    qw = t0 + lax.broadcasted_iota(jnp.int32, (wk, tq), 1)
    bias_w = jnp.where((kpos <= qw) & (kpos > qw - NSA_WINDOW), 0.0, NEG_INF)
    sw = _dot_nt(kwb[pl.ds(ws, wk), :], qs) + lanes4(bias_w)

    nrow = lax.broadcasted_iota(jnp.int32, (LANES, cols), 0)
    tcol = t0 + (lax.broadcasted_iota(jnp.int32, (LANES, cols), 1) & (tq - 1))
    valid_c = (nrow * NSA_CMP_STRIDE + NSA_CMP_LEN - 1 <= tcol) & (nrow < n_cmp)
    sc = jnp.where(valid_c, _dot_nt(kc_ref[0].astype(BF16), qs), NEG_INF)
    mc = jnp.max(sc, axis=0, keepdims=True)
    ec = jnp.where(valid_c, jnp.exp2(sc - mc), 0.0)
    dc = jnp.sum(ec, axis=0, keepdims=True)
    pc = ec / jnp.where(dc > 0.0, dc, 1.0)
    o_cmp = _dot(vct[...], pc.astype(BF16))
    pc_sum = heads_sum(pc)

    n_sel_blocks = seq // NSA_SEL_LEN
    jrow = lax.broadcasted_iota(jnp.int32, (LANES, LANES), 0)
    ncol = lax.broadcasted_iota(jnp.int32, (LANES, LANES), 1)
    overlap = ((ncol * NSA_CMP_STRIDE < (jrow + 1) * NSA_SEL_LEN)
               & (ncol * NSA_CMP_STRIDE + NSA_CMP_LEN > jrow * NSA_SEL_LEN)
               & (ncol < n_cmp) & (jrow < n_sel_blocks))
    overlap = jnp.where(overlap, 1.0, 0.0).astype(BF16)
    hi, mid, lo = _split3(pc_sum)
    imp = (_dot(overlap, hi) + _dot(overlap, mid) + _dot(overlap, lo))[0:n_sel_blocks]
    pw = jnp.exp2(sw - jnp.max(sw, axis=0, keepdims=True))
    o_win = _dot(vwt[:, pl.ds(ws, wk)], pw.astype(BF16))
    o_win = o_win[0:HEAD_DIM] / o_win[HEAD_DIM:HEAD_DIM + 1]
    blk = lax.broadcasted_iota(jnp.int32, (n_sel_blocks, tq), 0)
    cur = (t0 + lax.broadcasted_iota(jnp.int32, (n_sel_blocks, tq), 1)) >> LOG_SEL_LEN
    forced = (blk == 0) | (blk == cur) | (blk == cur - 1)
    imp = jnp.where(blk > cur, -1.0, jnp.where(forced, 1e6, imp))
    beaten = jnp.zeros((n_sel_blocks, tq), jnp.int32)
    for c in range(n_sel_blocks):
        row = imp[c:c + 1, :]
        wins = (row > imp) | ((row == imp) & (blk > c))
        beaten = beaten + jnp.where(wins, 1, 0)
    sel_bias = jnp.where(beaten < NSA_TOP_N, 0.0, NEG_INF)
    sel_bias = jnp.concatenate([sel_bias, jnp.zeros((LANES - n_sel_blocks, tq), F32)], axis=0).astype(BF16)

    n_tiles = (t0 + tq - 1) // tk + 1

    def sel_branch(n):
        nk = n * tk
        erow = lax.broadcasted_iota(jnp.int32, (nk, LANES), 0)
        ecol = lax.broadcasted_iota(jnp.int32, (nk, LANES), 1)
        expand = jnp.where((erow >> LOG_SEL_LEN) == ecol, 1.0, 0.0).astype(BF16)
        krow = lax.broadcasted_iota(jnp.int32, (nk, tq), 0)
        qlane = t0 + lax.broadcasted_iota(jnp.int32, (nk, tq), 1)
        bias = jnp.where(krow <= qlane, _dot(expand, sel_bias), NEG_INF)
        s = _dot_nt(ksb[0:nk, :], qs) + lanes4(bias)
        p = jnp.exp2(s - jnp.max(s, axis=0, keepdims=True))
        acc = _dot(vst[:, 0:nk], p.astype(BF16))
        osel_ref[...] = acc[0:HEAD_DIM] / acc[HEAD_DIM:HEAD_DIM + 1]

    for n in range(1, seq // tk + 1):
        pl.when(n_tiles == n)(functools.partial(sel_branch, n))
    o_sel = osel_ref[...]

    gt_ref[...] = gate_ref[...].T
    out = jnp.zeros((HEAD_DIM, cols), F32)
    for c, branch in enumerate((o_cmp, o_sel, o_win)):
        gate = jnp.concatenate([gt_ref[pl.ds(c * NSA_HEADS + g * NSA_HPG + j, 1), :] for j in range(NSA_HPG)],
                               axis=1)
        out = out + gate * branch
    o_ref[...] = jnp.concatenate([out[:, j * tq:(j + 1) * tq] for j in range(NSA_HPG)], axis=0).T.astype(o_ref.dtype)


def _nsa_attention(proj, gates, kc_cmp, vc_cmp, batch, seq, n_cmp):
    nq = seq // NSA_TQ
    ncb = kc_cmp.shape[1]
    qcols = NSA_HEADS * HEAD_DIM // NSA_WIDTH
    kv = lambda c: pl.BlockSpec((seq, NSA_WIDTH), lambda b, g, i, c=c: (b, qcols + c))
    cmp_spec = pl.BlockSpec((1, ncb, NSA_WIDTH), lambda b, g, i: (b * NSA_GROUPS + g, 0, 0))
    return pl.pallas_call(
        functools.partial(_nsa_kernel, seq=seq, n_cmp=n_cmp),
        grid=(batch, NSA_GROUPS, nq),
        in_specs=[pl.BlockSpec((NSA_TQ, NSA_WIDTH), lambda b, g, i: (b * nq + i, g)),
                  cmp_spec, cmp_spec, kv(2), kv(3), kv(4), kv(5),
                  pl.BlockSpec((NSA_TQ, LANES), lambda b, g, i: (b * nq + i, 0))],
        out_specs=pl.BlockSpec((NSA_TQ, NSA_WIDTH), lambda b, g, i: (b * nq + i, g)),
        out_shape=jax.ShapeDtypeStruct((batch * seq, NSA_HEADS * HEAD_DIM), BF16),
        scratch_shapes=[pltpu.VMEM((seq, NSA_WIDTH), BF16), pltpu.VMEM((seq, NSA_WIDTH), BF16),
                        pltpu.VMEM((NSA_GROUPS * (HEAD_DIM + ONES_ROWS), seq), BF16),
                        pltpu.VMEM((NSA_GROUPS * (HEAD_DIM + ONES_ROWS), seq), BF16),
                        pltpu.VMEM((HEAD_DIM, ncb), BF16), pltpu.VMEM((LANES, NSA_TQ), F32),
                        pltpu.VMEM((HEAD_DIM, NSA_HPG * NSA_TQ), F32)],
        compiler_params=_params("parallel", "arbitrary", "arbitrary"),
        name="nsa_attention",
    )(proj, kc_cmp, vc_cmp, proj, proj, proj, proj, gates)


def _pad_cols(w, width=LANES):
    return jnp.pad(w, ((0, 0), (0, width - w.shape[1])))


def kernel(x, p, positions, mix_norm, ffn_norm, ple_norm, ple_gate_w, ple_proj_w, fd_w_in, fd_forget_b, fd_w_out, dense_w_gate, dense_w_up, dense_w_down, nsa_w_in, nsa_pos_k, nsa_w1_k, nsa_b1_k, nsa_w2_k, nsa_pos_v, nsa_w1_v, nsa_b1_v, nsa_w2_v, nsa_w_out, moe_w_router, moe_b_router, moe_w_gate, moe_w_up, moe_w_down, final_norm):
    batch, seq, _ = x.shape
    t = batch * seq
    h = x.reshape(t, D_MODEL)
    cos, sin = _rope_tables(positions)

    n_main = 3 * FOX_WIDTH + 3 * DIL_WIDTH
    w_in = fd_w_in[0]
    tiles0 = tuple((c, 512, c in (3 * FOX_WIDTH, 3 * FOX_WIDTH + DIL_WIDTH)) for c in range(0, n_main, 512))
    proj0, log_f = _proj(h, mix_norm[0], w_in[:, :n_main].astype(BF16),
                         _pad_cols(w_in[:, n_main:]).astype(BF16),
                         _pad_cols(fd_forget_b[0].reshape(1, FOX_HEADS)).astype(F32),
                         cos, sin, tiles0, "log_sigmoid")
    ccol, crow = _cumsum(log_f, batch, seq)
    o_fox = _fox_attention(proj0, ccol, crow, batch, seq)
    o_dil = _dilated_attention(proj0, batch, seq)
    h = _outproj(h, [o_fox, o_dil], fd_w_out[0].astype(BF16))
    h = _ffn(h, ffn_norm[0], dense_w_gate[0].astype(BF16), dense_w_up[0].astype(BF16),
             dense_w_down[0].astype(BF16))
    h = _ple(h, ple_norm[0], p[0].reshape(t, -1), ple_gate_w[0].astype(BF16), ple_proj_w[0].astype(BF16),
             final_norm, final=False)

    qw = NSA_HEADS * HEAD_DIM
    n_main1 = qw + 6 * NSA_KV
    w_in1 = nsa_w_in[0]
    rope_cols = set(range(0, qw, 256)) | {qw, qw + 2 * NSA_KV, qw + 4 * NSA_KV}
    tiles1 = tuple((c, 256, c in rope_cols) for c in range(0, n_main1, 256))
    w_gate = w_in1[:, n_main1:].reshape(D_MODEL, NSA_HEADS, 3).transpose(0, 2, 1).reshape(D_MODEL, 3 * NSA_HEADS)
    proj1, gates = _proj(h, mix_norm[1], w_in1[:, :n_main1].astype(BF16), _pad_cols(w_gate).astype(BF16),
                         jnp.zeros((1, LANES), F32), cos, sin, tiles1, "sigmoid")
    n_cmp = (seq - NSA_CMP_LEN) // NSA_CMP_STRIDE + 1
    kc_cmp = _compress(proj1, qw // NSA_WIDTH, nsa_pos_k[0], nsa_w1_k[0], nsa_b1_k[0],
                       jnp.tile(nsa_w2_k[0], (1, NSA_HPG)).astype(BF16), batch, seq)
    vc_cmp = _compress(proj1, qw // NSA_WIDTH + 1, nsa_pos_v[0], nsa_w1_v[0], nsa_b1_v[0],
                       jnp.tile(nsa_w2_v[0], (1, NSA_HPG)).astype(BF16), batch, seq)
    o_nsa = _nsa_attention(proj1, gates, kc_cmp, vc_cmp, batch, seq, n_cmp)
    h = _outproj(h, [o_nsa], nsa_w_out[0].astype(BF16))
    cw, routed, rank_col, rank_row, routed_row = _router(
        h, ffn_norm[1], _pad_cols(moe_w_router[0]).astype(BF16),
        _pad_cols(moe_b_router[0].reshape(1, N_EXPERTS)).astype(F32))
    n_chunks = t // MOE_CHUNK
    counts = rank_col.reshape(n_chunks, MOE_CHUNK, LANES)[:, -1, :N_EXPERTS].astype(jnp.int32).reshape(-1)
    h = _moe_ffn(h, ffn_norm[1], cw, routed, rank_col, rank_row, routed_row, counts,
                 moe_w_gate[0].astype(BF16), moe_w_up[0].astype(BF16), moe_w_down[0].astype(BF16))
    h = _ple(h, ple_norm[1], p[1].reshape(t, -1), ple_gate_w[1].astype(BF16), ple_proj_w[1].astype(BF16),
             final_norm, final=True)
    return h.reshape(batch, seq, D_MODEL)
```

```python
import functools

import jax
import jax.numpy as jnp
from jax import lax
from jax.experimental import pallas as pl
from jax.experimental.pallas import tpu as pltpu

F32 = jnp.float32
BF16 = jnp.bfloat16

D_MODEL = 1024
HEAD_DIM = 64
LANES = 128
FOX_HEADS = 8
DIL_HEADS = 8
FOX_WIDTH = FOX_HEADS * HEAD_DIM
DIL_WIDTH = DIL_HEADS * HEAD_DIM
DIL_PATTERNS = ((128, 1), (512, 4), (2048, 16))
Q_BLOCK = 128
NSA_HEADS = 16
NSA_GROUPS = 4
NSA_HPG = NSA_HEADS // NSA_GROUPS
NSA_KV = NSA_GROUPS * HEAD_DIM
NSA_CMP_LEN = 32
NSA_CMP_STRIDE = 16
NSA_SEL_LEN = 64
NSA_TOP_N = 8
NSA_WINDOW = 512
N_EXPERTS = 8
ROPE_THETA = 10000.0
RMS_EPS = 1e-6
NEG_INF = -1e30
SCALE = HEAD_DIM ** -0.5
LOG2E = 1.4426950408889634
LN2 = 0.6931471805599453
SCALE2 = SCALE * LOG2E
ONES_ROWS = 16

VMEM_LIMIT_BYTES = 52 * 1024 * 1024


def _params(*sem):
    return pltpu.CompilerParams(dimension_semantics=sem, vmem_limit_bytes=VMEM_LIMIT_BYTES)


def _rms(x, g):
    return x * lax.rsqrt(jnp.mean(x * x, axis=-1, keepdims=True) + RMS_EPS) * g


def _dot(a, b):
    return jnp.dot(a, b, preferred_element_type=F32)


def _dot_nt(a, b):
    return lax.dot_general(a, b, (((1,), (1,)), ((), ())), preferred_element_type=F32)


def _split3(x):
    hi = x.astype(BF16)
    r = x - hi.astype(F32)
    mid = r.astype(BF16)
    lo = (r - mid.astype(F32)).astype(BF16)
    return hi, mid, lo


def _lane_col(x, idx):
    lane = lax.broadcasted_iota(jnp.int32, x.shape, 1)
    return jnp.sum(jnp.where(lane == idx, x, 0.0), axis=1, keepdims=True)


def _rope_table_kernel(pos_ref, inv_ref, cos_ref, sin_ref):
    ang = pos_ref[...] * inv_ref[...]
    lane = lax.broadcasted_iota(jnp.int32, ang.shape, 1)
    sign = jnp.where((lane & (HEAD_DIM - 1)) < HEAD_DIM // 2, -1.0, 1.0)
    cos_ref[...] = jnp.cos(ang)
    sin_ref[...] = jnp.sin(ang) * sign


def _rope_tables(positions):
    t = positions.size
    half = HEAD_DIM // 2
    inv_freq = ROPE_THETA ** (-jnp.arange(half, dtype=F32) / half)
    inv = jnp.tile(inv_freq, LANES // half).reshape(1, LANES)
    pos = jnp.broadcast_to(positions.astype(F32).reshape(t, 1), (t, LANES))
    tm = 1024
    return pl.pallas_call(
        _rope_table_kernel,
        grid=(t // tm,),
        in_specs=[pl.BlockSpec((tm, LANES), lambda i: (i, 0)),
                  pl.BlockSpec((1, LANES), lambda i: (0, 0))],
        out_specs=[pl.BlockSpec((tm, LANES), lambda i: (i, 0))] * 2,
        out_shape=[jax.ShapeDtypeStruct((t, LANES), F32)] * 2,
        compiler_params=_params("parallel"),
        name="rope_tables",
    )(pos, inv)


def _proj_kernel(h_ref, g_ref, w_ref, wa_ref, ba_ref, cos_ref, sin_ref, o_ref, oa_ref, *, tiles, aux_act):
    xn = _rms(h_ref[...], g_ref[...]).astype(BF16)
    for c0, width, rope in tiles:
        acc = _dot(xn, w_ref[:, c0:c0 + width])
        if rope:
            reps = width // LANES
            cos = jnp.tile(cos_ref[...], (1, reps))
            sin = jnp.tile(sin_ref[...], (1, reps))
            lane = lax.broadcasted_iota(jnp.int32, acc.shape, 1)
            first_half = (lane & (HEAD_DIM - 1)) < HEAD_DIM // 2
            partner = jnp.where(first_half,
                                pltpu.roll(acc, width - HEAD_DIM // 2, 1),
                                pltpu.roll(acc, HEAD_DIM // 2, 1))
            acc = acc * cos + partner * sin
        o_ref[:, c0:c0 + width] = acc
    aux = _dot(xn, wa_ref[...]) + ba_ref[...]
    if aux_act == "log_sigmoid":
        oa_ref[...] = jnp.minimum(aux, 0.0) - jnp.log1p(jnp.exp(-jnp.abs(aux)))
    else:
        oa_ref[...] = jax.nn.sigmoid(aux)


def _proj(h, gain, w, w_aux, b_aux, cos, sin, tiles, aux_act, tm=512):
    t, n = h.shape[0], w.shape[1]
    return pl.pallas_call(
        functools.partial(_proj_kernel, tiles=tiles, aux_act=aux_act),
        grid=(t // tm,),
        in_specs=[pl.BlockSpec((tm, D_MODEL), lambda i: (i, 0)),
                  pl.BlockSpec((1, D_MODEL), lambda i: (0, 0)),
                  pl.BlockSpec((D_MODEL, n), lambda i: (0, 0)),
                  pl.BlockSpec((D_MODEL, LANES), lambda i: (0, 0)),
                  pl.BlockSpec((1, LANES), lambda i: (0, 0)),
                  pl.BlockSpec((tm, LANES), lambda i: (i, 0)),
                  pl.BlockSpec((tm, LANES), lambda i: (i, 0))],
        out_specs=[pl.BlockSpec((tm, n), lambda i: (i, 0)),
                   pl.BlockSpec((tm, LANES), lambda i: (i, 0))],
        out_shape=[jax.ShapeDtypeStruct((t, n), F32), jax.ShapeDtypeStruct((t, LANES), F32)],
        compiler_params=_params("parallel"),
        name="in_proj",
    )(h, gain.reshape(1, D_MODEL), w, w_aux, b_aux, cos, sin)


CUM_BLOCK = 512


def _cumsum_kernel(x_ref, ccol_ref, crow_ref, carry_ref):
    j = pl.program_id(1)

    @pl.when(j == 0)
    def _():
        carry_ref[...] = jnp.zeros_like(carry_ref)

    x = x_ref[...]
    r = lax.broadcasted_iota(jnp.int32, (CUM_BLOCK, CUM_BLOCK), 0)
    c = lax.broadcasted_iota(jnp.int32, (CUM_BLOCK, CUM_BLOCK), 1)
    tri = jnp.where(r >= c, 1.0, 0.0).astype(BF16)
    hi, mid, lo = _split3(x)
    cum = _dot(tri, hi) + _dot(tri, mid) + _dot(tri, lo) + carry_ref[0:1, :]
    ccol_ref[...] = cum
    crow_ref[0] = cum.T
    carry_ref[...] = jnp.broadcast_to(cum[CUM_BLOCK - 1:CUM_BLOCK, :], carry_ref.shape)


def _cumsum(x, batch, seq):
    nb = seq // CUM_BLOCK
    return pl.pallas_call(
        _cumsum_kernel,
        grid=(batch, nb),
        in_specs=[pl.BlockSpec((CUM_BLOCK, LANES), lambda b, j: (b * nb + j, 0))],
        out_specs=[pl.BlockSpec((CUM_BLOCK, LANES), lambda b, j: (b * nb + j, 0)),
                   pl.BlockSpec((1, LANES, CUM_BLOCK), lambda b, j: (b, 0, j))],
        out_shape=[jax.ShapeDtypeStruct((batch * seq, LANES), F32),
                   jax.ShapeDtypeStruct((batch, LANES, seq), F32)],
        scratch_shapes=[pltpu.VMEM((8, LANES), F32)],
        compiler_params=_params("parallel", "arbitrary"),
        name="token_cumsum",
    )(x)


def _fox_kernel(q_ref, k_ref, v_ref, ccol_ref, crow_ref, o_ref, kb_ref, vt_ref, ck_ref, *, tq, seq):
    pair = pl.program_id(1)
    i = pl.program_id(2)
    tk = tq
    h0 = 2 * pair

    @pl.when(i == 0)
    def _():
        def fill(ci, carry):
            rs = pl.ds(pl.multiple_of(ci * tk, tk), tk)
            kb_ref[rs, :] = k_ref[rs, :].astype(BF16)
            vt_ref[0:LANES, rs] = v_ref[rs, :].T.astype(BF16)
            vt_ref[LANES:, rs] = jnp.ones((ONES_ROWS, tk), BF16)
            cc = ccol_ref[rs, :] * LOG2E
            ck_ref[0, rs, :] = jnp.broadcast_to(_lane_col(cc, h0), (tk, LANES))
            ck_ref[1, rs, :] = jnp.broadcast_to(_lane_col(cc, h0 + 1), (tk, LANES))
            return carry

        lax.fori_loop(0, seq // tk, fill, 0)

    t0 = pl.multiple_of(i * tq, tq)
    low = lax.broadcasted_iota(jnp.int32, (tq, LANES), 1) < HEAD_DIM
    q = q_ref[...] * SCALE2
    qs = jnp.concatenate([jnp.where(low, q, 0.0), jnp.where(low, 0.0, q)], axis=0).astype(BF16)
    cq = jnp.concatenate([crow_ref[0, pl.ds(h0, 1), pl.ds(t0, tq)],
                          crow_ref[0, pl.ds(h0 + 1, 1), pl.ds(t0, tq)]], axis=1) * LOG2E
    reps = tq // LANES

    def scores(k0, nk):
        ck = jnp.concatenate([ck_ref[0, k0:k0 + nk, :]] * reps + [ck_ref[1, k0:k0 + nk, :]] * reps, axis=1)
        return _dot_nt(kb_ref[k0:k0 + nk, :], qs) + cq - ck

    krow = lax.broadcasted_iota(jnp.int32, (tk, tq), 0)
    qlane = lax.broadcasted_iota(jnp.int32, (tk, tq), 1)
    causal = jnp.where(krow <= qlane, 0.0, NEG_INF)

    def branch(n):
        below = (n - 1) * tk
        s_diag = scores(below, tk) + jnp.concatenate([causal, causal], axis=1)
        m = jnp.max(s_diag, axis=0, keepdims=True)
        if below:
            s_below = scores(0, below)
            m = jnp.maximum(m, jnp.max(s_below, axis=0, keepdims=True))
            acc = _dot(vt_ref[:, 0:below], jnp.exp2(s_below - m).astype(BF16))
        else:
            acc = 0.0
        acc = acc + _dot(vt_ref[:, below:below + tk], jnp.exp2(s_diag - m).astype(BF16))
        out = acc[0:LANES] / acc[LANES:LANES + 1]
        o_ref[...] = jnp.concatenate([out[0:HEAD_DIM, 0:tq], out[HEAD_DIM:, tq:]], axis=0).T.astype(o_ref.dtype)

    for n in range(1, seq // tq + 1):
        pl.when(i == n - 1)(functools.partial(branch, n))


def _fox_attention(proj, ccol, crow, batch, seq, tq=512):
    nq = seq // tq
    npair = FOX_WIDTH // LANES
    return pl.pallas_call(
        functools.partial(_fox_kernel, tq=tq, seq=seq),
        grid=(batch, npair, nq),
        in_specs=[pl.BlockSpec((tq, LANES), lambda b, p, i: (b * nq + i, p)),
                  pl.BlockSpec((seq, LANES), lambda b, p, i: (b, npair + p)),
                  pl.BlockSpec((seq, LANES), lambda b, p, i: (b, 2 * npair + p)),
                  pl.BlockSpec((seq, LANES), lambda b, p, i: (b, 0)),
                  pl.BlockSpec((1, 8, seq), lambda b, p, i: (b, 0, 0))],
        out_specs=pl.BlockSpec((tq, LANES), lambda b, p, i: (b * nq + i, p)),
        out_shape=jax.ShapeDtypeStruct((batch * seq, FOX_WIDTH), BF16),
        scratch_shapes=[pltpu.VMEM((seq, LANES), BF16), pltpu.VMEM((LANES + ONES_ROWS, seq), BF16),
                        pltpu.VMEM((2, seq, LANES), F32)],
        compiler_params=_params("parallel", "parallel", "arbitrary"),
        name="fox_attention",
    )(proj, proj, proj, ccol, crow)


def _dil_kernel(q_ref, k_ref, v_ref, o_ref, os_ref, ls_ref, *, seq):
    low = lax.broadcasted_iota(jnp.int32, (Q_BLOCK, LANES), 1) < HEAD_DIM
    kr = lax.broadcasted_iota(jnp.int32, (2 * Q_BLOCK, Q_BLOCK), 0)
    qc = lax.broadcasted_iota(jnp.int32, (2 * Q_BLOCK, Q_BLOCK), 1)
    dist = qc + Q_BLOCK - kr
    scored, dests = [], []

    for pi, (window, dil) in enumerate(DIL_PATTERNS):
        span = window // dil
        nb = (seq // dil) // Q_BLOCK
        band_bias = jnp.where((dist >= 0) & (dist <= span), 0.0, NEG_INF)

        def rows(start, dil=dil):
            return pl.ds(start, Q_BLOCK, stride=dil) if dil > 1 else pl.ds(start, Q_BLOCK)

        for u in range(dil * nb):
            r, blk = divmod(u, nb)
            cur = r + blk * (Q_BLOCK * dil)
            q = q_ref[rows(cur), :] * SCALE2
            qs = jnp.concatenate([jnp.where(low, q, 0.0), jnp.where(low, 0.0, q)], axis=0).astype(BF16)
            if blk == 0:
                kk = k_ref[rows(cur), :].astype(BF16)
                vt = v_ref[rows(cur), :].T.astype(BF16)
                bias = band_bias[Q_BLOCK:]
            else:
                prev = cur - Q_BLOCK * dil
                kk = jnp.concatenate([k_ref[rows(prev), :], k_ref[rows(cur), :]], axis=0).astype(BF16)
                vt = jnp.concatenate([v_ref[rows(prev), :].T, v_ref[rows(cur), :].T], axis=1).astype(BF16)
                bias = band_bias
            vt = jnp.concatenate([vt, jnp.ones((ONES_ROWS, vt.shape[1]), BF16)], axis=0)
            s = _dot_nt(kk, qs) + jnp.concatenate([bias, bias], axis=1)
            scored.append((s, vt))
            dests.append((pi, rows(cur)))

    def softmax(s):
        m = jnp.max(s, axis=0, keepdims=True)
        return jnp.exp2(s - m).astype(BF16), m

    def finish(vt, p, m):
        ot = _dot(vt, p)
        den = ot[LANES:LANES + 1]
        ot = ot[0:LANES] * (1.0 / den)
        lse = m * LN2 + jnp.log(den)
        lse_t = jnp.concatenate([jnp.broadcast_to(lse[:, 0:Q_BLOCK], (HEAD_DIM, Q_BLOCK)),
                                 jnp.broadcast_to(lse[:, Q_BLOCK:], (HEAD_DIM, Q_BLOCK))], axis=0)
        return jnp.concatenate([ot[0:HEAD_DIM, 0:Q_BLOCK], ot[HEAD_DIM:, Q_BLOCK:]], axis=0).T, lse_t.T

    probs = [softmax(s) for s, _ in scored]
    outs = [finish(vt, p, m) for (_, vt), (p, m) in zip(scored, probs)]
    for (pi, dest), (o, l) in zip(dests, outs):
        os_ref[pi, dest, :] = o
        ls_ref[pi, dest, :] = l

    chunk = 256

    def combine(ci, carry):
        rs = pl.ds(pl.multiple_of(ci * chunk, chunk), chunk)
        l0, l1, l2 = ls_ref[0, rs, :], ls_ref[1, rs, :], ls_ref[2, rs, :]
        m = jnp.maximum(jnp.maximum(l0, l1), l2)
        e0, e1, e2 = jnp.exp(l0 - m), jnp.exp(l1 - m), jnp.exp(l2 - m)
        tot = e0 + e1 + e2
        o_ref[rs, :] = ((e0 / tot) * os_ref[0, rs, :] + (e1 / tot) * os_ref[1, rs, :]
                        + (e2 / tot) * os_ref[2, rs, :]).astype(o_ref.dtype)
        return carry

    lax.fori_loop(0, seq // chunk, combine, 0)


def _dilated_attention(proj, batch, seq):
    npair = DIL_WIDTH // LANES
    base = 3 * FOX_WIDTH // LANES
    return pl.pallas_call(
        functools.partial(_dil_kernel, seq=seq),
        grid=(batch, npair),
        in_specs=[pl.BlockSpec((seq, LANES), lambda b, p: (b, base + p)),
                  pl.BlockSpec((seq, LANES), lambda b, p: (b, base + npair + p)),
                  pl.BlockSpec((seq, LANES), lambda b, p: (b, base + 2 * npair + p))],
        out_specs=pl.BlockSpec((seq, LANES), lambda b, p: (b, p)),
        out_shape=jax.ShapeDtypeStruct((batch * seq, DIL_WIDTH), BF16),
        scratch_shapes=[pltpu.VMEM((3, seq, LANES), F32), pltpu.VMEM((3, seq, LANES), F32)],
        compiler_params=_params("parallel", "parallel"),
        name="dilated_attention",
    )(proj, proj, proj)


def _outproj_kernel(*refs, n_in, tn):
    h_ref = refs[0]
    a_refs = refs[1:1 + n_in]
    w_ref = refs[1 + n_in]
    o_ref = refs[2 + n_in]
    acts = [a[...].astype(BF16) for a in a_refs]
    for c0 in range(0, D_MODEL, tn):
        acc = h_ref[:, c0:c0 + tn]
        k0 = 0
        for a in acts:
            acc = acc + _dot(a, w_ref[k0:k0 + a.shape[1], c0:c0 + tn])
            k0 += a.shape[1]
        o_ref[:, c0:c0 + tn] = acc


def _outproj(h, acts, w, tm=1024, tn=512):
    t = h.shape[0]
    return pl.pallas_call(
        functools.partial(_outproj_kernel, n_in=len(acts), tn=tn),
        grid=(t // tm,),
        in_specs=([pl.BlockSpec((tm, D_MODEL), lambda i: (i, 0))]
                  + [pl.BlockSpec((tm, a.shape[1]), lambda i: (i, 0)) for a in acts]
                  + [pl.BlockSpec((D_MODEL, D_MODEL), lambda i: (0, 0))]),
        out_specs=pl.BlockSpec((tm, D_MODEL), lambda i: (i, 0)),
        out_shape=jax.ShapeDtypeStruct((t, D_MODEL), F32),
        compiler_params=_params("parallel"),
        name="out_proj",
    )(h, *acts, w)


def _swiglu_tile(x, wg, wu, wd):
    gate = _dot(x, wg)
    up = _dot(x, wu)
    return _dot((gate * jax.nn.sigmoid(gate) * up).astype(BF16), wd)


def _ffn_kernel(h_ref, g_ref, wg_ref, wu_ref, wd_ref, o_ref, xn_ref, acc_ref, *, n_f):
    f = pl.program_id(1)

    @pl.when(f == 0)
    def _():
        xn_ref[...] = _rms(h_ref[...], g_ref[...]).astype(BF16)
        acc_ref[...] = jnp.zeros_like(acc_ref)

    acc_ref[...] += _swiglu_tile(xn_ref[...], wg_ref[...], wu_ref[...], wd_ref[...])

    @pl.when(f == n_f - 1)
    def _():
        o_ref[...] = h_ref[...] + acc_ref[...]


def _ffn(h, gain, wg, wu, wd, tm=512, tf=1792):
    t, dff = h.shape[0], wg.shape[1]
    n_f = dff // tf
    return pl.pallas_call(
        functools.partial(_ffn_kernel, n_f=n_f),
        grid=(t // tm, n_f),
        in_specs=[pl.BlockSpec((tm, D_MODEL), lambda i, f: (i, 0)),
                  pl.BlockSpec((1, D_MODEL), lambda i, f: (0, 0)),
                  pl.BlockSpec((D_MODEL, tf), lambda i, f: (0, f)),
                  pl.BlockSpec((D_MODEL, tf), lambda i, f: (0, f)),
                  pl.BlockSpec((tf, D_MODEL), lambda i, f: (f, 0))],
        out_specs=pl.BlockSpec((tm, D_MODEL), lambda i, f: (i, 0)),
        out_shape=jax.ShapeDtypeStruct((t, D_MODEL), F32),
        scratch_shapes=[pltpu.VMEM((tm, D_MODEL), BF16), pltpu.VMEM((tm, D_MODEL), F32)],
        compiler_params=_params("parallel", "arbitrary"),
        name="dense_swiglu",
    )(h, gain.reshape(1, D_MODEL), wg, wu, wd)


MOE_CHUNK = 1024
MOE_ROWS = 128
MOE_SCATTER = 256


def _moe_kernel(cnt_ref, h_ref, g_ref, cw_ref, m_ref, rcol_ref, rrow_ref, mrow_ref, wg_ref, wu_ref, wd_ref,
                o_ref, xn_ref, xe_ref, ye_ref, *, n_f):
    c = pl.program_id(0)
    e = pl.program_id(1)
    f = pl.program_id(2)
    chunk = MOE_CHUNK
    n = cnt_ref[c * N_EXPERTS + e]
    n_scatter = (n + MOE_SCATTER - 1) // MOE_SCATTER
    n_tiles = (n + MOE_ROWS - 1) // MOE_ROWS

    @pl.when((e == 0) & (f == 0))
    def _():
        h = h_ref[...]
        xn_ref[...] = _rms(h, g_ref[...]).astype(BF16)
        o_ref[...] = h

    @pl.when(f == 0)
    def _():
        rank = rrow_ref[pl.ds(e, 1), :] * mrow_ref[pl.ds(e, 1), :]
        slot = lax.broadcasted_iota(jnp.int32, (MOE_ROWS, chunk), 0) + 1

        def gather(i, carry):
            rows = pl.ds(pl.multiple_of(i * MOE_ROWS, MOE_ROWS), MOE_ROWS)
            onehot = jnp.where(rank == (slot + i * MOE_ROWS).astype(F32), 1.0, 0.0).astype(BF16)
            xe_ref[rows, :] = _dot(onehot, xn_ref[...]).astype(BF16)
            ye_ref[rows, :] = jnp.zeros((MOE_ROWS, D_MODEL), F32)
            return carry

        def clear(i, carry):
            rows = pl.ds(pl.multiple_of(i * MOE_ROWS, MOE_ROWS), MOE_ROWS)
            ye_ref[rows, :] = jnp.zeros((MOE_ROWS, D_MODEL), F32)
            return carry

        lax.fori_loop(0, n_tiles, gather, 0)
        lax.fori_loop(n_tiles, n_scatter * (MOE_SCATTER // MOE_ROWS), clear, 0)

    def tile(i, carry):
        rows = pl.ds(pl.multiple_of(i * MOE_ROWS, MOE_ROWS), MOE_ROWS)
        ye_ref[rows, :] += _swiglu_tile(xe_ref[rows, :], wg_ref[...], wu_ref[...], wd_ref[...])
        return carry

    lax.fori_loop(0, n_tiles, tile, 0)

    @pl.when(f == n_f - 1)
    def _():
        rank = _lane_col(rcol_ref[...] * m_ref[...], e)
        weight = _lane_col(cw_ref[...], e)
        slot = lax.broadcasted_iota(jnp.int32, (chunk, MOE_SCATTER), 1) + 1

        def scatter(i, carry):
            rows = pl.ds(pl.multiple_of(i * MOE_SCATTER, MOE_SCATTER), MOE_SCATTER)
            onehot = jnp.where(rank == (slot + i * MOE_SCATTER).astype(F32), 1.0, 0.0).astype(BF16)
            o_ref[...] += weight * _dot(onehot, ye_ref[rows, :].astype(BF16))
            return carry

        lax.fori_loop(0, n_scatter, scatter, 0)


def _moe_ffn(h, gain, cw, mask, rcol, rrow, mrow, counts, wg, wu, wd, tf=1792):
    t, dff = h.shape[0], wg.shape[2]
    n_f = dff // tf
    chunk = MOE_CHUNK
    tok = lambda width: pl.BlockSpec((chunk, width), lambda c, e, f, cnt: (c, 0))
    lane_major = pl.BlockSpec((None, N_EXPERTS, chunk), lambda c, e, f, cnt: (c, 0, 0))
    grid_spec = pltpu.PrefetchScalarGridSpec(
        num_scalar_prefetch=1,
        grid=(t // chunk, N_EXPERTS, n_f),
        in_specs=[tok(D_MODEL),
                  pl.BlockSpec((1, D_MODEL), lambda c, e, f, cnt: (0, 0)),
                  tok(LANES), tok(LANES), tok(LANES), lane_major, lane_major,
                  pl.BlockSpec((None, D_MODEL, tf), lambda c, e, f, cnt: (e, 0, f)),
                  pl.BlockSpec((None, D_MODEL, tf), lambda c, e, f, cnt: (e, 0, f)),
                  pl.BlockSpec((None, tf, D_MODEL), lambda c, e, f, cnt: (e, f, 0))],
        out_specs=tok(D_MODEL),
        scratch_shapes=[pltpu.VMEM((chunk, D_MODEL), BF16), pltpu.VMEM((chunk, D_MODEL), BF16),
                        pltpu.VMEM((chunk, D_MODEL), F32)],
    )
    return pl.pallas_call(
        functools.partial(_moe_kernel, n_f=n_f),
        grid_spec=grid_spec,
        out_shape=jax.ShapeDtypeStruct((t, D_MODEL), F32),
        compiler_params=_params("parallel", "arbitrary", "arbitrary"),
        name="moe_swiglu",
    )(counts, h, gain.reshape(1, D_MODEL), cw, mask, rcol, rrow, mrow, wg, wu, wd)


def _router_kernel(h_ref, g_ref, w_ref, b_ref, cw_ref, m_ref, rcol_ref, rrow_ref, mrow_ref):
    xn = _rms(h_ref[...], g_ref[...]).astype(BF16)
    logits = _dot(xn, w_ref[...]) + b_ref[...]
    lane = lax.broadcasted_iota(jnp.int32, logits.shape, 1).astype(F32)
    logits = jnp.where(lane < N_EXPERTS, logits, -jnp.inf)
    m1 = jnp.max(logits, axis=1, keepdims=True)
    i1 = jnp.min(jnp.where(logits == m1, lane, float(LANES)), axis=1, keepdims=True)
    rest = jnp.where(lane == i1, -jnp.inf, logits)
    m2 = jnp.max(rest, axis=1, keepdims=True)
    i2 = jnp.min(jnp.where(rest == m2, lane, float(LANES)), axis=1, keepdims=True)
    e2 = jnp.exp(m2 - m1)
    w1 = 1.0 / (1.0 + e2)
    w2 = e2 / (1.0 + e2)
    cw_ref[...] = jnp.where(lane == i1, w1, jnp.where(lane == i2, w2, 0.0))
    mask = jnp.where((lane == i1) | (lane == i2), 1.0, 0.0)
    m_ref[...] = mask
    r = lax.broadcasted_iota(jnp.int32, (CUM_BLOCK, CUM_BLOCK), 0)
    c = lax.broadcasted_iota(jnp.int32, (CUM_BLOCK, CUM_BLOCK), 1)
    tri = jnp.where(r >= c, 1.0, 0.0).astype(BF16)
    carry = jnp.zeros((1, LANES), F32)
    for r0 in range(0, mask.shape[0], CUM_BLOCK):
        blk = mask[r0:r0 + CUM_BLOCK]
        cum = _dot(tri, blk.astype(BF16)) + carry
        rcol_ref[r0:r0 + CUM_BLOCK, :] = cum
        rrow_ref[:, r0:r0 + CUM_BLOCK] = cum.T
        mrow_ref[:, r0:r0 + CUM_BLOCK] = blk.T
        carry = cum[CUM_BLOCK - 1:CUM_BLOCK, :]


def _router(h, gain, w, b):
    t = h.shape[0]
    tm = MOE_CHUNK
    token_major = pl.BlockSpec((tm, LANES), lambda i: (i, 0))
    lane_major = pl.BlockSpec((None, LANES, tm), lambda i: (i, 0, 0))
    return pl.pallas_call(
        _router_kernel,
        grid=(t // tm,),
        in_specs=[pl.BlockSpec((tm, D_MODEL), lambda i: (i, 0)),
                  pl.BlockSpec((1, D_MODEL), lambda i: (0, 0)),
                  pl.BlockSpec((D_MODEL, LANES), lambda i: (0, 0)),
                  pl.BlockSpec((1, LANES), lambda i: (0, 0))],
        out_specs=[token_major, token_major, token_major, lane_major, lane_major],
        out_shape=[jax.ShapeDtypeStruct((t, LANES), F32)] * 3
                  + [jax.ShapeDtypeStruct((t // tm, LANES, tm), F32)] * 2,
        compiler_params=_params("parallel"),
        name="moe_router",
    )(h, gain.reshape(1, D_MODEL), w, b)


def _ple_kernel(h_ref, g_ref, p_ref, wg_ref, wp_ref, fg_ref, o_ref, *, final, tn):
    h = h_ref[...]
    xn = _rms(h, g_ref[...]).astype(BF16)
    pe = p_ref[...].astype(BF16)
    outs = []
    for c0 in range(0, D_MODEL, tn):
        gate = jax.nn.sigmoid(_dot(xn, wg_ref[:, c0:c0 + tn]))
        outs.append(h[:, c0:c0 + tn] + gate * _dot(pe, wp_ref[:, c0:c0 + tn]))
    new = jnp.concatenate(outs, axis=1)
    o_ref[...] = _rms(new, fg_ref[...]) if final else new


def _ple(h, gain, p, wg, wp, final_gain, final, tm=1024, tn=512):
    t, pd = p.shape
    return pl.pallas_call(
        functools.partial(_ple_kernel, final=final, tn=tn),
        grid=(t // tm,),
        in_specs=[pl.BlockSpec((tm, D_MODEL), lambda i: (i, 0)),
                  pl.BlockSpec((1, D_MODEL), lambda i: (0, 0)),
                  pl.BlockSpec((tm, pd), lambda i: (i, 0)),
                  pl.BlockSpec((D_MODEL, D_MODEL), lambda i: (0, 0)),
                  pl.BlockSpec((pd, D_MODEL), lambda i: (0, 0)),
                  pl.BlockSpec((1, D_MODEL), lambda i: (0, 0))],
        out_specs=pl.BlockSpec((tm, D_MODEL), lambda i: (i, 0)),
        out_shape=jax.ShapeDtypeStruct((t, D_MODEL), F32),
        compiler_params=_params("parallel"),
        name="ple",
    )(h, gain.reshape(1, D_MODEL), p, wg, wp, final_gain.reshape(1, D_MODEL))


def _compress_kernel(x01_ref, x23_ref, pos_ref, w1_ref, b1_ref, w2_ref, o_ref, *, blocks):
    stride = NSA_CMP_STRIDE
    hidden = b1_ref.shape[1]
    low = lax.broadcasted_iota(jnp.int32, (blocks, LANES), 1) < HEAD_DIM
    first = [jnp.zeros((blocks, hidden), F32) for _ in range(NSA_GROUPS)]
    second = [jnp.zeros((blocks, hidden), F32) for _ in range(NSA_GROUPS)]
    for j in range(stride):
        for half, x_ref in enumerate((x01_ref, x23_ref)):
            xs = x_ref[pl.ds(j, blocks, stride=stride), :]
            xa = xs + pos_ref[j:j + 1, :]
            xb = xs + pos_ref[stride + j:stride + j + 1, :]
            for sub in range(2):
                g = 2 * half + sub
                keep = low if sub == 0 else jnp.logical_not(low)
                first[g] = first[g] + _dot(jnp.where(keep, xa, 0.0).astype(BF16), w1_ref[j])
                second[g] = second[g] + _dot(jnp.where(keep, xb, 0.0).astype(BF16), w1_ref[stride + j])
    row = lax.broadcasted_iota(jnp.int32, (blocks, NSA_WIDTH), 0)
    for g in range(NSA_GROUPS):
        hid = first[g] + pltpu.roll(second[g], blocks - 1, 0) + b1_ref[...]
        out = _dot(jax.nn.gelu(hid, approximate=True).astype(BF16), w2_ref[...])
        o_ref[g] = jnp.where(row == blocks - 1, 0.0, out)


def _compress(proj, col_block, pos, w1, b1, w2, batch, seq):
    blocks = seq // NSA_CMP_STRIDE
    hidden = w1.shape[1]
    w1_rep = jnp.tile(w1.reshape(NSA_CMP_LEN, HEAD_DIM, hidden), (1, 2, 1))
    pos_rep = jnp.tile(pos, (1, 2))
    halves = NSA_WIDTH // LANES
    return pl.pallas_call(
        functools.partial(_compress_kernel, blocks=blocks),
        grid=(batch,),
        in_specs=[pl.BlockSpec((seq, LANES), lambda b: (b, halves * col_block)),
                  pl.BlockSpec((seq, LANES), lambda b: (b, halves * col_block + 1)),
                  pl.BlockSpec((NSA_CMP_LEN, LANES), lambda b: (0, 0)),
                  pl.BlockSpec((NSA_CMP_LEN, LANES, hidden), lambda b: (0, 0, 0)),
                  pl.BlockSpec((1, hidden), lambda b: (0, 0)),
                  pl.BlockSpec((hidden, NSA_WIDTH), lambda b: (0, 0))],
        out_specs=pl.BlockSpec((NSA_GROUPS, blocks, NSA_WIDTH), lambda b: (b, 0, 0)),
        out_shape=jax.ShapeDtypeStruct((batch * NSA_GROUPS, blocks, NSA_WIDTH), F32),
        compiler_params=_params("parallel"),
        name="nsa_compress",
    )(proj, proj, pos_rep, w1_rep.astype(BF16), b1.reshape(1, hidden), w2)


NSA_TQ = 256
NSA_TK = 512
NSA_WIDTH = NSA_HPG * HEAD_DIM
LOG_HEAD_DIM = HEAD_DIM.bit_length() - 1
LOG_SEL_LEN = NSA_SEL_LEN.bit_length() - 1


def _nsa_kernel(q_ref, kc_ref, vc_ref, ks_ref, vs_ref, kw_ref, vw_ref, gate_ref, o_ref,
                ksb, kwb, vst_all, vwt_all, vct, gt_ref, osel_ref, *, seq, n_cmp):
    g = pl.program_id(1)
    i = pl.program_id(2)
    tq, tk = NSA_TQ, NSA_TK
    cols = NSA_HPG * tq
    vrows = HEAD_DIM + ONES_ROWS
    grow = pl.multiple_of(g * vrows, vrows)

    @pl.when((i == 0) & (g == 0))
    def _():
        def fill(ci, carry):
            rs = pl.ds(pl.multiple_of(ci * NSA_WIDTH, NSA_WIDTH), NSA_WIDTH)
            ksb[rs, :] = ks_ref[rs, :].astype(BF16)
            kwb[rs, :] = kw_ref[rs, :].astype(BF16)
            for src, dst in ((vs_ref, vst_all), (vw_ref, vwt_all)):
                vt = src[rs, :].T.astype(BF16)
                for grp in range(NSA_GROUPS):
                    dst[grp * vrows:grp * vrows + HEAD_DIM, rs] = vt[grp * HEAD_DIM:(grp + 1) * HEAD_DIM]
                    dst[grp * vrows + HEAD_DIM:(grp + 1) * vrows, rs] = jnp.ones((ONES_ROWS, NSA_WIDTH), BF16)
            return carry

        lax.fori_loop(0, seq // NSA_WIDTH, fill, 0)

    @pl.when(i == 0)
    def _():
        vct[...] = vc_ref[0].T[0:HEAD_DIM, :].astype(BF16)

    vst = vst_all.at[pl.ds(grow, vrows)]
    vwt = vwt_all.at[pl.ds(grow, vrows)]

    t0 = i * tq
    lane_grp = lax.broadcasted_iota(jnp.int32, (tq, NSA_WIDTH), 1) >> LOG_HEAD_DIM
    q = q_ref[...] * SCALE2
    rolled = [q] + [pltpu.roll(q, s * HEAD_DIM, 1) for s in range(1, NSA_HPG)]
    parts = []
    for j in range(NSA_HPG):
        shift = (g - j) & (NSA_HPG - 1)
        moved = jnp.where(shift == 0, rolled[0],
                          jnp.where(shift == 1, rolled[1], jnp.where(shift == 2, rolled[2], rolled[3])))
        parts.append(jnp.where(lane_grp == g, moved, 0.0))
    qs = jnp.concatenate(parts, axis=0).astype(BF16)

    def heads_sum(x):
        out = x[:, 0:tq]
        for j in range(1, NSA_HPG):
            out = out + x[:, j * tq:(j + 1) * tq]
        return out

    def lanes4(x):
        return jnp.concatenate([x] * NSA_HPG, axis=1)

    wk = NSA_WINDOW + tq
    ws = pl.multiple_of(jnp.maximum(t0 - NSA_WINDOW, 0), tq)
    kpos = ws + lax.broadcasted_iota(jnp.int32, (wk, tq), 0)
    qw = t0 + lax.broadcasted_iota(jnp.int32, (wk, tq), 1)
    bias_w = jnp.where((kpos <= qw) & (kpos > qw - NSA_WINDOW), 0.0, NEG_INF)
    sw = _dot_nt(kwb[pl.ds(ws, wk), :], qs) + lanes4(bias_w)

    nrow = lax.broadcasted_iota(jnp.int32, (LANES, cols), 0)
    tcol = t0 + (lax.broadcasted_iota(jnp.int32, (LANES, cols), 1) & (tq - 1))
    valid_c = (nrow * NSA_CMP_STRIDE + NSA_CMP_LEN - 1 <= tcol) & (nrow < n_cmp)
    sc = jnp.where(valid_c, _dot_nt(kc_ref[0].astype(BF16), qs), NEG_INF)
    mc = jnp.max(sc, axis=0, keepdims=True)
    ec = jnp.where(valid_c, jnp.exp2(sc - mc), 0.0)
    dc = jnp.sum(ec, axis=0, keepdims=True)
    pc = ec / jnp.where(dc > 0.0, dc, 1.0)
    o_cmp = _dot(vct[...], pc.astype(BF16))
    pc_sum = heads_sum(pc)

    n_sel_blocks = seq // NSA_SEL_LEN
    jrow = lax.broadcasted_iota(jnp.int32, (LANES, LANES), 0)
    ncol = lax.broadcasted_iota(jnp.int32, (LANES, LANES), 1)
    overlap = ((ncol * NSA_CMP_STRIDE < (jrow + 1) * NSA_SEL_LEN)
               & (ncol * NSA_CMP_STRIDE + NSA_CMP_LEN > jrow * NSA_SEL_LEN)
               & (ncol < n_cmp) & (jrow < n_sel_blocks))
    overlap = jnp.where(overlap, 1.0, 0.0).astype(BF16)
    hi, mid, lo = _split3(pc_sum)
    imp = (_dot(overlap, hi) + _dot(overlap, mid) + _dot(overlap, lo))[0:n_sel_blocks]
    pw = jnp.exp2(sw - jnp.max(sw, axis=0, keepdims=True))
    o_win = _dot(vwt[:, pl.ds(ws, wk)], pw.astype(BF16))
    o_win = o_win[0:HEAD_DIM] / o_win[HEAD_DIM:HEAD_DIM + 1]
    blk = lax.broadcasted_iota(jnp.int32, (n_sel_blocks, tq), 0)
    cur = (t0 + lax.broadcasted_iota(jnp.int32, (n_sel_blocks, tq), 1)) >> LOG_SEL_LEN
    forced = (blk == 0) | (blk == cur) | (blk == cur - 1)
    imp = jnp.where(blk > cur, -1.0, jnp.where(forced, 1e6, imp))
    beaten = jnp.zeros((n_sel_blocks, tq), jnp.int32)
    for c in range(n_sel_blocks):
        row = imp[c:c + 1, :]
        wins = (row > imp) | ((row == imp) & (blk > c))
        beaten = beaten + jnp.where(wins, 1, 0)
    sel_bias = jnp.where(beaten < NSA_TOP_N, 0.0, NEG_INF)
    sel_bias = jnp.concatenate([sel_bias, jnp.zeros((LANES - n_sel_blocks, tq), F32)], axis=0).astype(BF16)

    n_tiles = (t0 + tq - 1) // tk + 1

    def sel_branch(n):
        nk = n * tk
        erow = lax.broadcasted_iota(jnp.int32, (nk, LANES), 0)
        ecol = lax.broadcasted_iota(jnp.int32, (nk, LANES), 1)
        expand = jnp.where((erow >> LOG_SEL_LEN) == ecol, 1.0, 0.0).astype(BF16)
        krow = lax.broadcasted_iota(jnp.int32, (nk, tq), 0)
        qlane = t0 + lax.broadcasted_iota(jnp.int32, (nk, tq), 1)
        bias = jnp.where(krow <= qlane, _dot(expand, sel_bias), NEG_INF)
        s = _dot_nt(ksb[0:nk, :], qs) + lanes4(bias)
        p = jnp.exp2(s - jnp.max(s, axis=0, keepdims=True))
        acc = _dot(vst[:, 0:nk], p.astype(BF16))
        osel_ref[...] = acc[0:HEAD_DIM] / acc[HEAD_DIM:HEAD_DIM + 1]

    for n in range(1, seq // tk + 1):
        pl.when(n_tiles == n)(functools.partial(sel_branch, n))
    o_sel = osel_ref[...]

    gt_ref[...] = gate_ref[...].T
    out = jnp.zeros((HEAD_DIM, cols), F32)
    for c, branch in enumerate((o_cmp, o_sel, o_win)):
        gate = jnp.concatenate([gt_ref[pl.ds(c * NSA_HEADS + g * NSA_HPG + j, 1), :] for j in range(NSA_HPG)],
                               axis=1)
        out = out + gate * branch
    o_ref[...] = jnp.concatenate([out[:, j * tq:(j + 1) * tq] for j in range(NSA_HPG)], axis=0).T.astype(o_ref.dtype)


def _nsa_attention(proj, gates, kc_cmp, vc_cmp, batch, seq, n_cmp):
    nq = seq // NSA_TQ
    ncb = kc_cmp.shape[1]
    qcols = NSA_HEADS * HEAD_DIM // NSA_WIDTH
    kv = lambda c: pl.BlockSpec((seq, NSA_WIDTH), lambda b, g, i, c=c: (b, qcols + c))
    cmp_spec = pl.BlockSpec((1, ncb, NSA_WIDTH), lambda b, g, i: (b * NSA_GROUPS + g, 0, 0))
    return pl.pallas_call(
        functools.partial(_nsa_kernel, seq=seq, n_cmp=n_cmp),
        grid=(batch, NSA_GROUPS, nq),
        in_specs=[pl.BlockSpec((NSA_TQ, NSA_WIDTH), lambda b, g, i: (b * nq + i, g)),
                  cmp_spec, cmp_spec, kv(2), kv(3), kv(4), kv(5),
                  pl.BlockSpec((NSA_TQ, LANES), lambda b, g, i: (b * nq + i, 0))],
        out_specs=pl.BlockSpec((NSA_TQ, NSA_WIDTH), lambda b, g, i: (b * nq + i, g)),
        out_shape=jax.ShapeDtypeStruct((batch * seq, NSA_HEADS * HEAD_DIM), BF16),
        scratch_shapes=[pltpu.VMEM((seq, NSA_WIDTH), BF16), pltpu.VMEM((seq, NSA_WIDTH), BF16),
                        pltpu.VMEM((NSA_GROUPS * (HEAD_DIM + ONES_ROWS), seq), BF16),
                        pltpu.VMEM((NSA_GROUPS * (HEAD_DIM + ONES_ROWS), seq), BF16),
                        pltpu.VMEM((HEAD_DIM, ncb), BF16), pltpu.VMEM((LANES, NSA_TQ), F32),
                        pltpu.VMEM((HEAD_DIM, NSA_HPG * NSA_TQ), F32)],
        compiler_params=_params("parallel", "arbitrary", "arbitrary"),
        name="nsa_attention",
    )(proj, kc_cmp, vc_cmp, proj, proj, proj, proj, gates)


def _pad_cols(w, width=LANES):
    return jnp.pad(w, ((0, 0), (0, width - w.shape[1])))


def kernel(x, p, positions, mix_norm, ffn_norm, ple_norm, ple_gate_w, ple_proj_w, fd_w_in, fd_forget_b, fd_w_out, dense_w_gate, dense_w_up, dense_w_down, nsa_w_in, nsa_pos_k, nsa_w1_k, nsa_b1_k, nsa_w2_k, nsa_pos_v, nsa_w1_v, nsa_b1_v, nsa_w2_v, nsa_w_out, moe_w_router, moe_b_router, moe_w_gate, moe_w_up, moe_w_down, final_norm):
    batch, seq, _ = x.shape
    t = batch * seq
    h = x.reshape(t, D_MODEL)
    cos, sin = _rope_tables(positions)

    n_main = 3 * FOX_WIDTH + 3 * DIL_WIDTH
    w_in = fd_w_in[0]
    tiles0 = tuple((c, 512, c in (3 * FOX_WIDTH, 3 * FOX_WIDTH + DIL_WIDTH)) for c in range(0, n_main, 512))
    proj0, log_f = _proj(h, mix_norm[0], w_in[:, :n_main].astype(BF16),
                         _pad_cols(w_in[:, n_main:]).astype(BF16),
                         _pad_cols(fd_forget_b[0].reshape(1, FOX_HEADS)).astype(F32),
                         cos, sin, tiles0, "log_sigmoid")
    ccol, crow = _cumsum(log_f, batch, seq)
    o_fox = _fox_attention(proj0, ccol, crow, batch, seq)
    o_dil = _dilated_attention(proj0, batch, seq)
    h = _outproj(h, [o_fox, o_dil], fd_w_out[0].astype(BF16))
    h = _ffn(h, ffn_norm[0], dense_w_gate[0].astype(BF16), dense_w_up[0].astype(BF16),
             dense_w_down[0].astype(BF16))
    h = _ple(h, ple_norm[0], p[0].reshape(t, -1), ple_gate_w[0].astype(BF16), ple_proj_w[0].astype(BF16),
             final_norm, final=False)

    qw = NSA_HEADS * HEAD_DIM
    n_main1 = qw + 6 * NSA_KV
    w_in1 = nsa_w_in[0]
    rope_cols = set(range(0, qw, 256)) | {qw, qw + 2 * NSA_KV, qw + 4 * NSA_KV}
    tiles1 = tuple((c, 256, c in rope_cols) for c in range(0, n_main1, 256))
    w_gate = w_in1[:, n_main1:].reshape(D_MODEL, NSA_HEADS, 3).transpose(0, 2, 1).reshape(D_MODEL, 3 * NSA_HEADS)
    proj1, gates = _proj(h, mix_norm[1], w_in1[:, :n_main1].astype(BF16), _pad_cols(w_gate).astype(BF16),
                         jnp.zeros((1, LANES), F32), cos, sin, tiles1, "sigmoid")
    n_cmp = (seq - NSA_CMP_LEN) // NSA_CMP_STRIDE + 1
    kc_cmp = _compress(proj1, qw // NSA_WIDTH, nsa_pos_k[0], nsa_w1_k[0], nsa_b1_k[0],
                       jnp.tile(nsa_w2_k[0], (1, NSA_HPG)).astype(BF16), batch, seq)
    vc_cmp = _compress(proj1, qw // NSA_WIDTH + 1, nsa_pos_v[0], nsa_w1_v[0], nsa_b1_v[0],
                       jnp.tile(nsa_w2_v[0], (1, NSA_HPG)).astype(BF16), batch, seq)
    o_nsa = _nsa_attention(proj1, gates, kc_cmp, vc_cmp, batch, seq, n_cmp)
    h = _outproj(h, [o_nsa], nsa_w_out[0].astype(BF16))
    cw, routed, rank_col, rank_row, routed_row = _router(
        h, ffn_norm[1], _pad_cols(moe_w_router[0]).astype(BF16),
        _pad_cols(moe_b_router[0].reshape(1, N_EXPERTS)).astype(F32))
    n_chunks = t // MOE_CHUNK
    counts = rank_col.reshape(n_chunks, MOE_CHUNK, LANES)[:, -1, :N_EXPERTS].astype(jnp.int32).reshape(-1)
    h = _moe_ffn(h, ffn_norm[1], cw, routed, rank_col, rank_row, routed_row, counts,
                 moe_w_gate[0].astype(BF16), moe_w_up[0].astype(BF16), moe_w_down[0].astype(BF16))
    h = _ple(h, ple_norm[1], p[1].reshape(t, -1), ple_gate_w[1].astype(BF16), ple_proj_w[1].astype(BF16),
             final_norm, final=True)
    return h.reshape(batch, seq, D_MODEL)
```

```python
import functools

import jax
import jax.numpy as jnp
from jax import lax
from jax.experimental import pallas as pl
from jax.experimental.pallas import tpu as pltpu

F32 = jnp.float32
BF16 = jnp.bfloat16

D_MODEL = 1024
HEAD_DIM = 64
LANES = 128
FOX_HEADS = 8
DIL_HEADS = 8
FOX_WIDTH = FOX_HEADS * HEAD_DIM
DIL_WIDTH = DIL_HEADS * HEAD_DIM
DIL_PATTERNS = ((128, 1), (512, 4), (2048, 16))
Q_BLOCK = 128
NSA_HEADS = 16
NSA_GROUPS = 4
NSA_HPG = NSA_HEADS // NSA_GROUPS
NSA_KV = NSA_GROUPS * HEAD_DIM
NSA_CMP_LEN = 32
NSA_CMP_STRIDE = 16
NSA_SEL_LEN = 64
NSA_TOP_N = 8
NSA_WINDOW = 512
N_EXPERTS = 8
ROPE_THETA = 10000.0
RMS_EPS = 1e-6
NEG_INF = -1e30
SCALE = HEAD_DIM ** -0.5
LOG2E = 1.4426950408889634
LN2 = 0.6931471805599453
SCALE2 = SCALE * LOG2E
ONES_ROWS = 16

VMEM_LIMIT_BYTES = 52 * 1024 * 1024


def _params(*sem):
    return pltpu.CompilerParams(dimension_semantics=sem, vmem_limit_bytes=VMEM_LIMIT_BYTES)


def _rms(x, g):
    return x * lax.rsqrt(jnp.mean(x * x, axis=-1, keepdims=True) + RMS_EPS) * g


def _dot(a, b):
    return jnp.dot(a, b, preferred_element_type=F32)


def _dot_nt(a, b):
    return lax.dot_general(a, b, (((1,), (1,)), ((), ())), preferred_element_type=F32)


def _split3(x):
    hi = x.astype(BF16)
    r = x - hi.astype(F32)
    mid = r.astype(BF16)
    lo = (r - mid.astype(F32)).astype(BF16)
    return hi, mid, lo


def _lane_col(x, idx):
    lane = lax.broadcasted_iota(jnp.int32, x.shape, 1)
    return jnp.sum(jnp.where(lane == idx, x, 0.0), axis=1, keepdims=True)


def _rope_table_kernel(pos_ref, inv_ref, cos_ref, sin_ref):
    ang = pos_ref[...] * inv_ref[...]
    lane = lax.broadcasted_iota(jnp.int32, ang.shape, 1)
    sign = jnp.where((lane & (HEAD_DIM - 1)) < HEAD_DIM // 2, -1.0, 1.0)
    cos_ref[...] = jnp.cos(ang)
    sin_ref[...] = jnp.sin(ang) * sign


def _rope_tables(positions):
    t = positions.size
    half = HEAD_DIM // 2
    inv_freq = ROPE_THETA ** (-jnp.arange(half, dtype=F32) / half)
    inv = jnp.tile(inv_freq, LANES // half).reshape(1, LANES)
    pos = jnp.broadcast_to(positions.astype(F32).reshape(t, 1), (t, LANES))
    tm = 1024
    return pl.pallas_call(
        _rope_table_kernel,
        grid=(t // tm,),
        in_specs=[pl.BlockSpec((tm, LANES), lambda i: (i, 0)),
                  pl.BlockSpec((1, LANES), lambda i: (0, 0))],
        out_specs=[pl.BlockSpec((tm, LANES), lambda i: (i, 0))] * 2,
        out_shape=[jax.ShapeDtypeStruct((t, LANES), F32)] * 2,
        compiler_params=_params("parallel"),
        name="rope_tables",
    )(pos, inv)


def _proj_kernel(h_ref, g_ref, w_ref, wa_ref, ba_ref, cos_ref, sin_ref, o_ref, oa_ref, *, tiles, aux_act):
    xn = _rms(h_ref[...], g_ref[...]).astype(BF16)
    for c0, width, rope in tiles:
        acc = _dot(xn, w_ref[:, c0:c0 + width])
        if rope:
            reps = width // LANES
            cos = jnp.tile(cos_ref[...], (1, reps))
            sin = jnp.tile(sin_ref[...], (1, reps))
            lane = lax.broadcasted_iota(jnp.int32, acc.shape, 1)
            first_half = (lane & (HEAD_DIM - 1)) < HEAD_DIM // 2
            partner = jnp.where(first_half,
                                pltpu.roll(acc, width - HEAD_DIM // 2, 1),
                                pltpu.roll(acc, HEAD_DIM // 2, 1))
            acc = acc * cos + partner * sin
        o_ref[:, c0:c0 + width] = acc
    aux = _dot(xn, wa_ref[...]) + ba_ref[...]
    if aux_act == "log_sigmoid":
        oa_ref[...] = jnp.minimum(aux, 0.0) - jnp.log1p(jnp.exp(-jnp.abs(aux)))
    else:
        oa_ref[...] = jax.nn.sigmoid(aux)


def _proj(h, gain, w, w_aux, b_aux, cos, sin, tiles, aux_act, tm=512):
    t, n = h.shape[0], w.shape[1]
    return pl.pallas_call(
        functools.partial(_proj_kernel, tiles=tiles, aux_act=aux_act),
        grid=(t // tm,),
        in_specs=[pl.BlockSpec((tm, D_MODEL), lambda i: (i, 0)),
                  pl.BlockSpec((1, D_MODEL), lambda i: (0, 0)),
                  pl.BlockSpec((D_MODEL, n), lambda i: (0, 0)),
                  pl.BlockSpec((D_MODEL, LANES), lambda i: (0, 0)),
                  pl.BlockSpec((1, LANES), lambda i: (0, 0)),
                  pl.BlockSpec((tm, LANES), lambda i: (i, 0)),
                  pl.BlockSpec((tm, LANES), lambda i: (i, 0))],
        out_specs=[pl.BlockSpec((tm, n), lambda i: (i, 0)),
                   pl.BlockSpec((tm, LANES), lambda i: (i, 0))],
        out_shape=[jax.ShapeDtypeStruct((t, n), F32), jax.ShapeDtypeStruct((t, LANES), F32)],
        compiler_params=_params("parallel"),
        name="in_proj",
    )(h, gain.reshape(1, D_MODEL), w, w_aux, b_aux, cos, sin)


CUM_BLOCK = 512


def _cumsum_kernel(x_ref, ccol_ref, crow_ref, carry_ref):
    j = pl.program_id(1)

    @pl.when(j == 0)
    def _():
        carry_ref[...] = jnp.zeros_like(carry_ref)

    x = x_ref[...]
    r = lax.broadcasted_iota(jnp.int32, (CUM_BLOCK, CUM_BLOCK), 0)
    c = lax.broadcasted_iota(jnp.int32, (CUM_BLOCK, CUM_BLOCK), 1)
    tri = jnp.where(r >= c, 1.0, 0.0).astype(BF16)
    hi, mid, lo = _split3(x)
    cum = _dot(tri, hi) + _dot(tri, mid) + _dot(tri, lo) + carry_ref[0:1, :]
    ccol_ref[...] = cum
    crow_ref[0] = cum.T
    carry_ref[...] = jnp.broadcast_to(cum[CUM_BLOCK - 1:CUM_BLOCK, :], carry_ref.shape)


def _cumsum(x, batch, seq):
    nb = seq // CUM_BLOCK
    return pl.pallas_call(
        _cumsum_kernel,
        grid=(batch, nb),
        in_specs=[pl.BlockSpec((CUM_BLOCK, LANES), lambda b, j: (b * nb + j, 0))],
        out_specs=[pl.BlockSpec((CUM_BLOCK, LANES), lambda b, j: (b * nb + j, 0)),
                   pl.BlockSpec((1, LANES, CUM_BLOCK), lambda b, j: (b, 0, j))],
        out_shape=[jax.ShapeDtypeStruct((batch * seq, LANES), F32),
                   jax.ShapeDtypeStruct((batch, LANES, seq), F32)],
        scratch_shapes=[pltpu.VMEM((8, LANES), F32)],
        compiler_params=_params("parallel", "arbitrary"),
        name="token_cumsum",
    )(x)


def _fox_kernel(q_ref, k_ref, v_ref, ccol_ref, crow_ref, o_ref, kb_ref, vt_ref, ck_ref, *, tq, seq):
    pair = pl.program_id(1)
    i = pl.program_id(2)
    tk = tq
    h0 = 2 * pair

    @pl.when(i == 0)
    def _():
        def fill(ci, carry):
            rs = pl.ds(pl.multiple_of(ci * tk, tk), tk)
            kb_ref[rs, :] = k_ref[rs, :].astype(BF16)
            vt_ref[0:LANES, rs] = v_ref[rs, :].T.astype(BF16)
            vt_ref[LANES:, rs] = jnp.ones((ONES_ROWS, tk), BF16)
            cc = ccol_ref[rs, :] * LOG2E
            ck_ref[0, rs, :] = jnp.broadcast_to(_lane_col(cc, h0), (tk, LANES))
            ck_ref[1, rs, :] = jnp.broadcast_to(_lane_col(cc, h0 + 1), (tk, LANES))
            return carry

        lax.fori_loop(0, seq // tk, fill, 0)

    t0 = pl.multiple_of(i * tq, tq)
    low = lax.broadcasted_iota(jnp.int32, (tq, LANES), 1) < HEAD_DIM
    q = q_ref[...] * SCALE2
    qs = jnp.concatenate([jnp.where(low, q, 0.0), jnp.where(low, 0.0, q)], axis=0).astype(BF16)
    cq = jnp.concatenate([crow_ref[0, pl.ds(h0, 1), pl.ds(t0, tq)],
                          crow_ref[0, pl.ds(h0 + 1, 1), pl.ds(t0, tq)]], axis=1) * LOG2E
    reps = tq // LANES

    def scores(k0, nk):
        ck = jnp.concatenate([ck_ref[0, k0:k0 + nk, :]] * reps + [ck_ref[1, k0:k0 + nk, :]] * reps, axis=1)
        return _dot_nt(kb_ref[k0:k0 + nk, :], qs) + cq - ck

    krow = lax.broadcasted_iota(jnp.int32, (tk, tq), 0)
    qlane = lax.broadcasted_iota(jnp.int32, (tk, tq), 1)
    causal = jnp.where(krow <= qlane, 0.0, NEG_INF)

    def branch(n):
        below = (n - 1) * tk
        s_diag = scores(below, tk) + jnp.concatenate([causal, causal], axis=1)
        m = jnp.max(s_diag, axis=0, keepdims=True)
        if below:
            s_below = scores(0, below)
            m = jnp.maximum(m, jnp.max(s_below, axis=0, keepdims=True))
            acc = _dot(vt_ref[:, 0:below], jnp.exp2(s_below - m).astype(BF16))
        else:
            acc = 0.0
        acc = acc + _dot(vt_ref[:, below:below + tk], jnp.exp2(s_diag - m).astype(BF16))
        out = acc[0:LANES] / acc[LANES:LANES + 1]
        o_ref[...] = jnp.concatenate([out[0:HEAD_DIM, 0:tq], out[HEAD_DIM:, tq:]], axis=0).T.astype(o_ref.dtype)

    for n in range(1, seq // tq + 1):
        pl.when(i == n - 1)(functools.partial(branch, n))


def _fox_attention(proj, ccol, crow, batch, seq, tq=512):
    nq = seq // tq
    npair = FOX_WIDTH // LANES
    return pl.pallas_call(
        functools.partial(_fox_kernel, tq=tq, seq=seq),
        grid=(batch, npair, nq),
        in_specs=[pl.BlockSpec((tq, LANES), lambda b, p, i: (b * nq + i, p)),
                  pl.BlockSpec((seq, LANES), lambda b, p, i: (b, npair + p)),
                  pl.BlockSpec((seq, LANES), lambda b, p, i: (b, 2 * npair + p)),
                  pl.BlockSpec((seq, LANES), lambda b, p, i: (b, 0)),
                  pl.BlockSpec((1, 8, seq), lambda b, p, i: (b, 0, 0))],
        out_specs=pl.BlockSpec((tq, LANES), lambda b, p, i: (b * nq + i, p)),
        out_shape=jax.ShapeDtypeStruct((batch * seq, FOX_WIDTH), BF16),
        scratch_shapes=[pltpu.VMEM((seq, LANES), BF16), pltpu.VMEM((LANES + ONES_ROWS, seq), BF16),
                        pltpu.VMEM((2, seq, LANES), F32)],
        compiler_params=_params("parallel", "parallel", "arbitrary"),
        name="fox_attention",
    )(proj, proj, proj, ccol, crow)


DIL_GROUP = 4


def _dil_kernel(q_ref, k_ref, v_ref, o_ref, os_ref, ls_ref, *, seq):
    low = lax.broadcasted_iota(jnp.int32, (Q_BLOCK, LANES), 1) < HEAD_DIM
    kr = lax.broadcasted_iota(jnp.int32, (2 * Q_BLOCK, Q_BLOCK), 0)
    qc = lax.broadcasted_iota(jnp.int32, (2 * Q_BLOCK, Q_BLOCK), 1)
    dist = qc + Q_BLOCK - kr
    def score_group(pi, units):
        window, dil = DIL_PATTERNS[pi]
        span = window // dil
        nb = (seq // dil) // Q_BLOCK
        band_bias = jnp.where((dist >= 0) & (dist <= span), 0.0, NEG_INF)
        scored, dests = [], []

        def rows(start):
            return pl.ds(start, Q_BLOCK, stride=dil) if dil > 1 else pl.ds(start, Q_BLOCK)

        for u in units:
            r, blk = divmod(u, nb)
            cur = r + blk * (Q_BLOCK * dil)
            q = q_ref[rows(cur), :] * SCALE2
            qs = jnp.concatenate([jnp.where(low, q, 0.0), jnp.where(low, 0.0, q)], axis=0).astype(BF16)
            if blk == 0:
                kk = k_ref[rows(cur), :].astype(BF16)
                vt = v_ref[rows(cur), :].T.astype(BF16)
                bias = band_bias[Q_BLOCK:]
            else:
                prev = cur - Q_BLOCK * dil
                kk = jnp.concatenate([k_ref[rows(prev), :], k_ref[rows(cur), :]], axis=0).astype(BF16)
                vt = jnp.concatenate([v_ref[rows(prev), :].T, v_ref[rows(cur), :].T], axis=1).astype(BF16)
                bias = band_bias
            vt = jnp.concatenate([vt, jnp.ones((ONES_ROWS, vt.shape[1]), BF16)], axis=0)
            s = _dot_nt(kk, qs) + jnp.concatenate([bias, bias], axis=1)
            scored.append((s, vt))
            dests.append(rows(cur))
        return scored, dests

    def softmax(s):
        m = jnp.max(s, axis=0, keepdims=True)
        return jnp.exp2(s - m).astype(BF16), m

    def finish(vt, p, m):
        ot = _dot(vt, p)
        den = ot[LANES:LANES + 1]
        ot = ot[0:LANES] * (1.0 / den)
        lse = m * LN2 + jnp.log(den)
        lse_t = jnp.concatenate([jnp.broadcast_to(lse[:, 0:Q_BLOCK], (HEAD_DIM, Q_BLOCK)),
                                 jnp.broadcast_to(lse[:, Q_BLOCK:], (HEAD_DIM, Q_BLOCK))], axis=0)
        return jnp.concatenate([ot[0:HEAD_DIM, 0:Q_BLOCK], ot[HEAD_DIM:, Q_BLOCK:]], axis=0).T, lse_t.T

    def finish_group(pi, scored, dests):
        probs = [softmax(s) for s, _ in scored]
        outs = [finish(vt, p, m) for (_, vt), (p, m) in zip(scored, probs)]
        for dest, (o, l) in zip(dests, outs):
            os_ref[pi, dest, :] = o
            ls_ref[pi, dest, :] = l

    groups = []
    for pi, (window, dil) in enumerate(DIL_PATTERNS):
        n_units = dil * ((seq // dil) // Q_BLOCK)
        groups += [(pi, range(u0, min(u0 + DIL_GROUP, n_units))) for u0 in range(0, n_units, DIL_GROUP)]
    pending = score_group(*groups[0])
    for gi, (pi, _) in enumerate(groups):
        upcoming = score_group(*groups[gi + 1]) if gi + 1 < len(groups) else None
        finish_group(pi, *pending)
        pending = upcoming

    chunk = 256

    def combine(ci, carry):
        rs = pl.ds(pl.multiple_of(ci * chunk, chunk), chunk)
        l0, l1, l2 = ls_ref[0, rs, :], ls_ref[1, rs, :], ls_ref[2, rs, :]
        m = jnp.maximum(jnp.maximum(l0, l1), l2)
        e0, e1, e2 = jnp.exp(l0 - m), jnp.exp(l1 - m), jnp.exp(l2 - m)
        tot = e0 + e1 + e2
        o_ref[rs, :] = ((e0 / tot) * os_ref[0, rs, :] + (e1 / tot) * os_ref[1, rs, :]
                        + (e2 / tot) * os_ref[2, rs, :]).astype(o_ref.dtype)
        return carry

    lax.fori_loop(0, seq // chunk, combine, 0)


def _dilated_attention(proj, batch, seq):
    npair = DIL_WIDTH // LANES
    base = 3 * FOX_WIDTH // LANES
    return pl.pallas_call(
        functools.partial(_dil_kernel, seq=seq),
        grid=(batch, npair),
        in_specs=[pl.BlockSpec((seq, LANES), lambda b, p: (b, base + p)),
                  pl.BlockSpec((seq, LANES), lambda b, p: (b, base + npair + p)),
                  pl.BlockSpec((seq, LANES), lambda b, p: (b, base + 2 * npair + p))],
        out_specs=pl.BlockSpec((seq, LANES), lambda b, p: (b, p)),
        out_shape=jax.ShapeDtypeStruct((batch * seq, DIL_WIDTH), BF16),
        scratch_shapes=[pltpu.VMEM((3, seq, LANES), F32), pltpu.VMEM((3, seq, LANES), F32)],
        compiler_params=_params("parallel", "parallel"),
        name="dilated_attention",
    )(proj, proj, proj)


def _outproj_kernel(*refs, n_in, tn):
    h_ref = refs[0]
    a_refs = refs[1:1 + n_in]
    w_ref = refs[1 + n_in]
    o_ref = refs[2 + n_in]
    acts = [a[...].astype(BF16) for a in a_refs]
    for c0 in range(0, D_MODEL, tn):
        acc = h_ref[:, c0:c0 + tn]
        k0 = 0
        for a in acts:
            acc = acc + _dot(a, w_ref[k0:k0 + a.shape[1], c0:c0 + tn])
            k0 += a.shape[1]
        o_ref[:, c0:c0 + tn] = acc


def _outproj(h, acts, w, tm=1024, tn=512):
    t = h.shape[0]
    return pl.pallas_call(
        functools.partial(_outproj_kernel, n_in=len(acts), tn=tn),
        grid=(t // tm,),
        in_specs=([pl.BlockSpec((tm, D_MODEL), lambda i: (i, 0))]
                  + [pl.BlockSpec((tm, a.shape[1]), lambda i: (i, 0)) for a in acts]
                  + [pl.BlockSpec((D_MODEL, D_MODEL), lambda i: (0, 0))]),
        out_specs=pl.BlockSpec((tm, D_MODEL), lambda i: (i, 0)),
        out_shape=jax.ShapeDtypeStruct((t, D_MODEL), F32),
        compiler_params=_params("parallel"),
        name="out_proj",
    )(h, *acts, w)


def _swiglu_tile(x, wg, wu, wd):
    gate = _dot(x, wg)
    up = _dot(x, wu)
    return _dot((gate * jax.nn.sigmoid(gate) * up).astype(BF16), wd)


def _ffn_kernel(h_ref, g_ref, wg_ref, wu_ref, wd_ref, o_ref, xn_ref, acc_ref, *, n_f):
    f = pl.program_id(1)

    @pl.when(f == 0)
    def _():
        xn_ref[...] = _rms(h_ref[...], g_ref[...]).astype(BF16)
        acc_ref[...] = jnp.zeros_like(acc_ref)

    acc_ref[...] += _swiglu_tile(xn_ref[...], wg_ref[...], wu_ref[...], wd_ref[...])

    @pl.when(f == n_f - 1)
    def _():
        o_ref[...] = h_ref[...] + acc_ref[...]


def _ffn(h, gain, wg, wu, wd, tm=512, tf=1792):
    t, dff = h.shape[0], wg.shape[1]
    n_f = dff // tf
    return pl.pallas_call(
        functools.partial(_ffn_kernel, n_f=n_f),
        grid=(t // tm, n_f),
        in_specs=[pl.BlockSpec((tm, D_MODEL), lambda i, f: (i, 0)),
                  pl.BlockSpec((1, D_MODEL), lambda i, f: (0, 0)),
                  pl.BlockSpec((D_MODEL, tf), lambda i, f: (0, f)),
                  pl.BlockSpec((D_MODEL, tf), lambda i, f: (0, f)),
                  pl.BlockSpec((tf, D_MODEL), lambda i, f: (f, 0))],
        out_specs=pl.BlockSpec((tm, D_MODEL), lambda i, f: (i, 0)),
        out_shape=jax.ShapeDtypeStruct((t, D_MODEL), F32),
        scratch_shapes=[pltpu.VMEM((tm, D_MODEL), BF16), pltpu.VMEM((tm, D_MODEL), F32)],
        compiler_params=_params("parallel", "arbitrary"),
        name="dense_swiglu",
    )(h, gain.reshape(1, D_MODEL), wg, wu, wd)


MOE_CHUNK = 1024
MOE_ROWS = 128
MOE_SCATTER = 256


def _moe_kernel(cnt_ref, h_ref, g_ref, cw_ref, m_ref, rcol_ref, rrow_ref, mrow_ref, wg_ref, wu_ref, wd_ref,
                o_ref, xn_ref, xe_ref, ye_ref, *, n_f):
    c = pl.program_id(0)
    e = pl.program_id(1)
    f = pl.program_id(2)
    chunk = MOE_CHUNK
    n = cnt_ref[c * N_EXPERTS + e]
    n_scatter = (n + MOE_SCATTER - 1) // MOE_SCATTER
    n_tiles = (n + MOE_ROWS - 1) // MOE_ROWS

    @pl.when((e == 0) & (f == 0))
    def _():
        h = h_ref[...]
        xn_ref[...] = _rms(h, g_ref[...]).astype(BF16)
        o_ref[...] = h

    @pl.when(f == 0)
    def _():
        rank = rrow_ref[pl.ds(e, 1), :] * mrow_ref[pl.ds(e, 1), :]
        slot = lax.broadcasted_iota(jnp.int32, (MOE_ROWS, chunk), 0) + 1

        def gather(i, carry):
            rows = pl.ds(pl.multiple_of(i * MOE_ROWS, MOE_ROWS), MOE_ROWS)
            onehot = jnp.where(rank == (slot + i * MOE_ROWS).astype(F32), 1.0, 0.0).astype(BF16)
            xe_ref[rows, :] = _dot(onehot, xn_ref[...]).astype(BF16)
            ye_ref[rows, :] = jnp.zeros((MOE_ROWS, D_MODEL), F32)
            return carry

        def clear(i, carry):
            rows = pl.ds(pl.multiple_of(i * MOE_ROWS, MOE_ROWS), MOE_ROWS)
            ye_ref[rows, :] = jnp.zeros((MOE_ROWS, D_MODEL), F32)
            return carry

        lax.fori_loop(0, n_tiles, gather, 0)
        lax.fori_loop(n_tiles, n_scatter * (MOE_SCATTER // MOE_ROWS), clear, 0)

    def tile(i, carry):
        rows = pl.ds(pl.multiple_of(i * MOE_ROWS, MOE_ROWS), MOE_ROWS)
        ye_ref[rows, :] += _swiglu_tile(xe_ref[rows, :], wg_ref[...], wu_ref[...], wd_ref[...])
        return carry

    lax.fori_loop(0, n_tiles, tile, 0)

    @pl.when(f == n_f - 1)
    def _():
        rank = _lane_col(rcol_ref[...] * m_ref[...], e)
        weight = _lane_col(cw_ref[...], e)
        slot = lax.broadcasted_iota(jnp.int32, (chunk, MOE_SCATTER), 1) + 1

        def scatter(i, carry):
            rows = pl.ds(pl.multiple_of(i * MOE_SCATTER, MOE_SCATTER), MOE_SCATTER)
            onehot = jnp.where(rank == (slot + i * MOE_SCATTER).astype(F32), 1.0, 0.0).astype(BF16)
            o_ref[...] += weight * _dot(onehot, ye_ref[rows, :].astype(BF16))
            return carry

        lax.fori_loop(0, n_scatter, scatter, 0)


def _moe_ffn(h, gain, cw, mask, rcol, rrow, mrow, counts, wg, wu, wd, tf=1792):
    t, dff = h.shape[0], wg.shape[2]
    n_f = dff // tf
    chunk = MOE_CHUNK
    tok = lambda width: pl.BlockSpec((chunk, width), lambda c, e, f, cnt: (c, 0))
    lane_major = pl.BlockSpec((None, N_EXPERTS, chunk), lambda c, e, f, cnt: (c, 0, 0))
    grid_spec = pltpu.PrefetchScalarGridSpec(
        num_scalar_prefetch=1,
        grid=(t // chunk, N_EXPERTS, n_f),
        in_specs=[tok(D_MODEL),
                  pl.BlockSpec((1, D_MODEL), lambda c, e, f, cnt: (0, 0)),
                  tok(LANES), tok(LANES), tok(LANES), lane_major, lane_major,
                  pl.BlockSpec((None, D_MODEL, tf), lambda c, e, f, cnt: (e, 0, f)),
                  pl.BlockSpec((None, D_MODEL, tf), lambda c, e, f, cnt: (e, 0, f)),
                  pl.BlockSpec((None, tf, D_MODEL), lambda c, e, f, cnt: (e, f, 0))],
        out_specs=tok(D_MODEL),
        scratch_shapes=[pltpu.VMEM((chunk, D_MODEL), BF16), pltpu.VMEM((chunk, D_MODEL), BF16),
                        pltpu.VMEM((chunk, D_MODEL), F32)],
    )
    return pl.pallas_call(
        functools.partial(_moe_kernel, n_f=n_f),
        grid_spec=grid_spec,
        out_shape=jax.ShapeDtypeStruct((t, D_MODEL), F32),
        compiler_params=_params("parallel", "arbitrary", "arbitrary"),
        name="moe_swiglu",
    )(counts, h, gain.reshape(1, D_MODEL), cw, mask, rcol, rrow, mrow, wg, wu, wd)


def _router_kernel(h_ref, g_ref, w_ref, b_ref, cw_ref, m_ref, rcol_ref, rrow_ref, mrow_ref):
    xn = _rms(h_ref[...], g_ref[...]).astype(BF16)
    logits = _dot(xn, w_ref[...]) + b_ref[...]
    lane = lax.broadcasted_iota(jnp.int32, logits.shape, 1).astype(F32)
    logits = jnp.where(lane < N_EXPERTS, logits, -jnp.inf)
    m1 = jnp.max(logits, axis=1, keepdims=True)
    i1 = jnp.min(jnp.where(logits == m1, lane, float(LANES)), axis=1, keepdims=True)
    rest = jnp.where(lane == i1, -jnp.inf, logits)
    m2 = jnp.max(rest, axis=1, keepdims=True)
    i2 = jnp.min(jnp.where(rest == m2, lane, float(LANES)), axis=1, keepdims=True)
    e2 = jnp.exp(m2 - m1)
    w1 = 1.0 / (1.0 + e2)
    w2 = e2 / (1.0 + e2)
    cw_ref[...] = jnp.where(lane == i1, w1, jnp.where(lane == i2, w2, 0.0))
    mask = jnp.where((lane == i1) | (lane == i2), 1.0, 0.0)
    m_ref[...] = mask
    r = lax.broadcasted_iota(jnp.int32, (CUM_BLOCK, CUM_BLOCK), 0)
    c = lax.broadcasted_iota(jnp.int32, (CUM_BLOCK, CUM_BLOCK), 1)
    tri = jnp.where(r >= c, 1.0, 0.0).astype(BF16)
    carry = jnp.zeros((1, LANES), F32)
    for r0 in range(0, mask.shape[0], CUM_BLOCK):
        blk = mask[r0:r0 + CUM_BLOCK]
        cum = _dot(tri, blk.astype(BF16)) + carry
        rcol_ref[r0:r0 + CUM_BLOCK, :] = cum
        rrow_ref[:, r0:r0 + CUM_BLOCK] = cum.T
        mrow_ref[:, r0:r0 + CUM_BLOCK] = blk.T
        carry = cum[CUM_BLOCK - 1:CUM_BLOCK, :]


def _router(h, gain, w, b):
    t = h.shape[0]
    tm = MOE_CHUNK
    token_major = pl.BlockSpec((tm, LANES), lambda i: (i, 0))
    lane_major = pl.BlockSpec((None, LANES, tm), lambda i: (i, 0, 0))
    return pl.pallas_call(
        _router_kernel,
        grid=(t // tm,),
        in_specs=[pl.BlockSpec((tm, D_MODEL), lambda i: (i, 0)),
                  pl.BlockSpec((1, D_MODEL), lambda i: (0, 0)),
                  pl.BlockSpec((D_MODEL, LANES), lambda i: (0, 0)),
                  pl.BlockSpec((1, LANES), lambda i: (0, 0))],
        out_specs=[token_major, token_major, token_major, lane_major, lane_major],
        out_shape=[jax.ShapeDtypeStruct((t, LANES), F32)] * 3
                  + [jax.ShapeDtypeStruct((t // tm, LANES, tm), F32)] * 2,
        compiler_params=_params("parallel"),
        name="moe_router",
    )(h, gain.reshape(1, D_MODEL), w, b)


def _ple_kernel(h_ref, g_ref, p_ref, wg_ref, wp_ref, fg_ref, o_ref, *, final, tn):
    h = h_ref[...]
    xn = _rms(h, g_ref[...]).astype(BF16)
    pe = p_ref[...].astype(BF16)
    outs = []
    for c0 in range(0, D_MODEL, tn):
        gate = jax.nn.sigmoid(_dot(xn, wg_ref[:, c0:c0 + tn]))
        outs.append(h[:, c0:c0 + tn] + gate * _dot(pe, wp_ref[:, c0:c0 + tn]))
    new = jnp.concatenate(outs, axis=1)
    o_ref[...] = _rms(new, fg_ref[...]) if final else new


def _ple(h, gain, p, wg, wp, final_gain, final, tm=1024, tn=512):
    t, pd = p.shape
    return pl.pallas_call(
        functools.partial(_ple_kernel, final=final, tn=tn),
        grid=(t // tm,),
        in_specs=[pl.BlockSpec((tm, D_MODEL), lambda i: (i, 0)),
                  pl.BlockSpec((1, D_MODEL), lambda i: (0, 0)),
                  pl.BlockSpec((tm, pd), lambda i: (i, 0)),
                  pl.BlockSpec((D_MODEL, D_MODEL), lambda i: (0, 0)),
                  pl.BlockSpec((pd, D_MODEL), lambda i: (0, 0)),
                  pl.BlockSpec((1, D_MODEL), lambda i: (0, 0))],
        out_specs=pl.BlockSpec((tm, D_MODEL), lambda i: (i, 0)),
        out_shape=jax.ShapeDtypeStruct((t, D_MODEL), F32),
        compiler_params=_params("parallel"),
        name="ple",
    )(h, gain.reshape(1, D_MODEL), p, wg, wp, final_gain.reshape(1, D_MODEL))


def _compress_kernel(x01_ref, x23_ref, pos_ref, w1_ref, b1_ref, w2_ref, o_ref, *, blocks):
    stride = NSA_CMP_STRIDE
    hidden = b1_ref.shape[1]
    low = lax.broadcasted_iota(jnp.int32, (blocks, LANES), 1) < HEAD_DIM
    first = [jnp.zeros((blocks, hidden), F32) for _ in range(NSA_GROUPS)]
    second = [jnp.zeros((blocks, hidden), F32) for _ in range(NSA_GROUPS)]
    for j in range(stride):
        for half, x_ref in enumerate((x01_ref, x23_ref)):
            xs = x_ref[pl.ds(j, blocks, stride=stride), :]
            xa = xs + pos_ref[j:j + 1, :]
            xb = xs + pos_ref[stride + j:stride + j + 1, :]
            for sub in range(2):
                g = 2 * half + sub
                keep = low if sub == 0 else jnp.logical_not(low)
                first[g] = first[g] + _dot(jnp.where(keep, xa, 0.0).astype(BF16), w1_ref[j])
                second[g] = second[g] + _dot(jnp.where(keep, xb, 0.0).astype(BF16), w1_ref[stride + j])
    row = lax.broadcasted_iota(jnp.int32, (blocks, NSA_WIDTH), 0)
    for g in range(NSA_GROUPS):
        hid = first[g] + pltpu.roll(second[g], blocks - 1, 0) + b1_ref[...]
        out = _dot(jax.nn.gelu(hid, approximate=True).astype(BF16), w2_ref[...])
        o_ref[g] = jnp.where(row == blocks - 1, 0.0, out)


def _compress(proj, col_block, pos, w1, b1, w2, batch, seq):
    blocks = seq // NSA_CMP_STRIDE
    hidden = w1.shape[1]
    w1_rep = jnp.tile(w1.reshape(NSA_CMP_LEN, HEAD_DIM, hidden), (1, 2, 1))
    pos_rep = jnp.tile(pos, (1, 2))
    halves = NSA_WIDTH // LANES
    return pl.pallas_call(
        functools.partial(_compress_kernel, blocks=blocks),
        grid=(batch,),
        in_specs=[pl.BlockSpec((seq, LANES), lambda b: (b, halves * col_block)),
                  pl.BlockSpec((seq, LANES), lambda b: (b, halves * col_block + 1)),
                  pl.BlockSpec((NSA_CMP_LEN, LANES), lambda b: (0, 0)),
                  pl.BlockSpec((NSA_CMP_LEN, LANES, hidden), lambda b: (0, 0, 0)),
                  pl.BlockSpec((1, hidden), lambda b: (0, 0)),
                  pl.BlockSpec((hidden, NSA_WIDTH), lambda b: (0, 0))],
        out_specs=pl.BlockSpec((NSA_GROUPS, blocks, NSA_WIDTH), lambda b: (b, 0, 0)),
        out_shape=jax.ShapeDtypeStruct((batch * NSA_GROUPS, blocks, NSA_WIDTH), F32),
        compiler_params=_params("parallel"),
        name="nsa_compress",
    )(proj, proj, pos_rep, w1_rep.astype(BF16), b1.reshape(1, hidden), w2)


NSA_TQ = 256
NSA_TK = 512
NSA_WIDTH = NSA_HPG * HEAD_DIM
LOG_HEAD_DIM = HEAD_DIM.bit_length() - 1
LOG_SEL_LEN = NSA_SEL_LEN.bit_length() - 1


def _nsa_kernel(q_ref, kc_ref, vc_ref, ks_ref, vs_ref, kw_ref, vw_ref, gate_ref, o_ref,
                ksb, kwb, vst_all, vwt_all, vct, gt_ref, osel_ref, *, seq, n_cmp):
    g = pl.program_id(1)
    i = pl.program_id(2)
    tq, tk = NSA_TQ, NSA_TK
    cols = NSA_HPG * tq
    vrows = HEAD_DIM + ONES_ROWS
    grow = pl.multiple_of(g * vrows, vrows)

    @pl.when((i == 0) & (g == 0))
    def _():
        def fill(ci, carry):
            rs = pl.ds(pl.multiple_of(ci * NSA_WIDTH, NSA_WIDTH), NSA_WIDTH)
            ksb[rs, :] = ks_ref[rs, :].astype(BF16)
            kwb[rs, :] = kw_ref[rs, :].astype(BF16)
            for src, dst in ((vs_ref, vst_all), (vw_ref, vwt_all)):
                vt = src[rs, :].T.astype(BF16)
                for grp in range(NSA_GROUPS):
                    dst[grp * vrows:grp * vrows + HEAD_DIM, rs] = vt[grp * HEAD_DIM:(grp + 1) * HEAD_DIM]
                    dst[grp * vrows + HEAD_DIM:(grp + 1) * vrows, rs] = jnp.ones((ONES_ROWS, NSA_WIDTH), BF16)
            return carry

        lax.fori_loop(0, seq // NSA_WIDTH, fill, 0)

    @pl.when(i == 0)
    def _():
        vct[...] = vc_ref[0].T[0:HEAD_DIM, :].astype(BF16)

    vst = vst_all.at[pl.ds(grow, vrows)]
    vwt = vwt_all.at[pl.ds(grow, vrows)]

    t0 = i * tq
    lane_grp = lax.broadcasted_iota(jnp.int32, (tq, NSA_WIDTH), 1) >> LOG_HEAD_DIM
    q = q_ref[...] * SCALE2
    rolled = [q] + [pltpu.roll(q, s * HEAD_DIM, 1) for s in range(1, NSA_HPG)]
    parts = []
    for j in range(NSA_HPG):
        shift = (g - j) & (NSA_HPG - 1)
        moved = jnp.where(shift == 0, rolled[0],
                          jnp.where(shift == 1, rolled[1], jnp.where(shift == 2, rolled[2], rolled[3])))
        parts.append(jnp.where(lane_grp == g, moved, 0.0))
    qs = jnp.concatenate(parts, axis=0).astype(BF16)

    def heads_sum(x):
        out = x[:, 0:tq]
        for j in range(1, NSA_HPG):
            out = out + x[:, j * tq:(j + 1) * tq]
        return out

    def lanes4(x):
        return jnp.concatenate([x] * NSA_HPG, axis=1)

    wk = NSA_WINDOW + tq
    ws = pl.multiple_of(jnp.maximum(t0 - NSA_WINDOW, 0), tq)
    kpos = ws + lax.broadcasted_iota(jnp.int32, (wk, tq), 0)
    qw = t0 + lax.broadcasted_iota(jnp.int32, (wk, tq), 1)
    bias_w = jnp.where((kpos <= qw) & (kpos > qw - NSA_WINDOW), 0.0, NEG_INF)
    sw = _dot_nt(kwb[pl.ds(ws, wk), :], qs) + lanes4(bias_w)

    nrow = lax.broadcasted_iota(jnp.int32, (LANES, cols), 0)
    tcol = t0 + (lax.broadcasted_iota(jnp.int32, (LANES, cols), 1) & (tq - 1))
    valid_c = (nrow * NSA_CMP_STRIDE + NSA_CMP_LEN - 1 <= tcol) & (nrow < n_cmp)
    sc = jnp.where(valid_c, _dot_nt(kc_ref[0].astype(BF16), qs), NEG_INF)
    mc = jnp.max(sc, axis=0, keepdims=True)
    ec = jnp.where(valid_c, jnp.exp2(sc - mc), 0.0)
    dc = jnp.sum(ec, axis=0, keepdims=True)
    pc = ec / jnp.where(dc > 0.0, dc, 1.0)
    o_cmp = _dot(vct[...], pc.astype(BF16))
    pc_sum = heads_sum(pc)

    n_sel_blocks = seq // NSA_SEL_LEN
    jrow = lax.broadcasted_iota(jnp.int32, (LANES, LANES), 0)
    ncol = lax.broadcasted_iota(jnp.int32, (LANES, LANES), 1)
    overlap = ((ncol * NSA_CMP_STRIDE < (jrow + 1) * NSA_SEL_LEN)
               & (ncol * NSA_CMP_STRIDE + NSA_CMP_LEN > jrow * NSA_SEL_LEN)
               & (ncol < n_cmp) & (jrow < n_sel_blocks))
    overlap = jnp.where(overlap, 1.0, 0.0).astype(BF16)
    hi, mid, lo = _split3(pc_sum)
    imp = (_dot(overlap, hi) + _dot(overlap, mid) + _dot(overlap, lo))[0:n_sel_blocks]
    pw = jnp.exp2(sw - jnp.max(sw, axis=0, keepdims=True))
    o_win = _dot(vwt[:, pl.ds(ws, wk)], pw.astype(BF16))
    o_win = o_win[0:HEAD_DIM] / o_win[HEAD_DIM:HEAD_DIM + 1]
    blk = lax.broadcasted_iota(jnp.int32, (n_sel_blocks, tq), 0)
    cur = (t0 + lax.broadcasted_iota(jnp.int32, (n_sel_blocks, tq), 1)) >> LOG_SEL_LEN
    forced = (blk == 0) | (blk == cur) | (blk == cur - 1)
    imp = jnp.where(blk > cur, -1.0, jnp.where(forced, 1e6, imp))
    beaten = jnp.zeros((n_sel_blocks, tq), jnp.int32)
    for c in range(n_sel_blocks):
        row = imp[c:c + 1, :]
        wins = (row > imp) | ((row == imp) & (blk > c))
        beaten = beaten + jnp.where(wins, 1, 0)
    sel_bias = jnp.where(beaten < NSA_TOP_N, 0.0, NEG_INF)
    sel_bias = jnp.concatenate([sel_bias, jnp.zeros((LANES - n_sel_blocks, tq), F32)], axis=0).astype(BF16)

    n_tiles = (t0 + tq - 1) // tk + 1

    def sel_branch(n):
        nk = n * tk
        erow = lax.broadcasted_iota(jnp.int32, (nk, LANES), 0)
        ecol = lax.broadcasted_iota(jnp.int32, (nk, LANES), 1)
        expand = jnp.where((erow >> LOG_SEL_LEN) == ecol, 1.0, 0.0).astype(BF16)
        krow = lax.broadcasted_iota(jnp.int32, (nk, tq), 0)
        qlane = t0 + lax.broadcasted_iota(jnp.int32, (nk, tq), 1)
        bias = jnp.where(krow <= qlane, _dot(expand, sel_bias), NEG_INF)
        s = _dot_nt(ksb[0:nk, :], qs) + lanes4(bias)
        p = jnp.exp2(s - jnp.max(s, axis=0, keepdims=True))
        acc = _dot(vst[:, 0:nk], p.astype(BF16))
        osel_ref[...] = acc[0:HEAD_DIM] / acc[HEAD_DIM:HEAD_DIM + 1]

    for n in range(1, seq // tk + 1):
        pl.when(n_tiles == n)(functools.partial(sel_branch, n))
    o_sel = osel_ref[...]

    gt_ref[...] = gate_ref[...].T
    out = jnp.zeros((HEAD_DIM, cols), F32)
    for c, branch in enumerate((o_cmp, o_sel, o_win)):
        gate = jnp.concatenate([gt_ref[pl.ds(c * NSA_HEADS + g * NSA_HPG + j, 1), :] for j in range(NSA_HPG)],
                               axis=1)
        out = out + gate * branch
    o_ref[...] = jnp.concatenate([out[:, j * tq:(j + 1) * tq] for j in range(NSA_HPG)], axis=0).T.astype(o_ref.dtype)


def _nsa_attention(proj, gates, kc_cmp, vc_cmp, batch, seq, n_cmp):
    nq = seq // NSA_TQ
    ncb = kc_cmp.shape[1]
    qcols = NSA_HEADS * HEAD_DIM // NSA_WIDTH
    kv = lambda c: pl.BlockSpec((seq, NSA_WIDTH), lambda b, g, i, c=c: (b, qcols + c))
    cmp_spec = pl.BlockSpec((1, ncb, NSA_WIDTH), lambda b, g, i: (b * NSA_GROUPS + g, 0, 0))
    return pl.pallas_call(
        functools.partial(_nsa_kernel, seq=seq, n_cmp=n_cmp),
        grid=(batch, NSA_GROUPS, nq),
        in_specs=[pl.BlockSpec((NSA_TQ, NSA_WIDTH), lambda b, g, i: (b * nq + i, g)),
                  cmp_spec, cmp_spec, kv(2), kv(3), kv(4), kv(5),
                  pl.BlockSpec((NSA_TQ, LANES), lambda b, g, i: (b * nq + i, 0))],
        out_specs=pl.BlockSpec((NSA_TQ, NSA_WIDTH), lambda b, g, i: (b * nq + i, g)),
        out_shape=jax.ShapeDtypeStruct((batch * seq, NSA_HEADS * HEAD_DIM), BF16),
        scratch_shapes=[pltpu.VMEM((seq, NSA_WIDTH), BF16), pltpu.VMEM((seq, NSA_WIDTH), BF16),
                        pltpu.VMEM((NSA_GROUPS * (HEAD_DIM + ONES_ROWS), seq), BF16),
                        pltpu.VMEM((NSA_GROUPS * (HEAD_DIM + ONES_ROWS), seq), BF16),
                        pltpu.VMEM((HEAD_DIM, ncb), BF16), pltpu.VMEM((LANES, NSA_TQ), F32),
                        pltpu.VMEM((HEAD_DIM, NSA_HPG * NSA_TQ), F32)],
        compiler_params=_params("parallel", "arbitrary", "arbitrary"),
        name="nsa_attention",
    )(proj, kc_cmp, vc_cmp, proj, proj, proj, proj, gates)


def _pad_cols(w, width=LANES):
    return jnp.pad(w, ((0, 0), (0, width - w.shape[1])))


def kernel(x, p, positions, mix_norm, ffn_norm, ple_norm, ple_gate_w, ple_proj_w, fd_w_in, fd_forget_b, fd_w_out, dense_w_gate, dense_w_up, dense_w_down, nsa_w_in, nsa_pos_k, nsa_w1_k, nsa_b1_k, nsa_w2_k, nsa_pos_v, nsa_w1_v, nsa_b1_v, nsa_w2_v, nsa_w_out, moe_w_router, moe_b_router, moe_w_gate, moe_w_up, moe_w_down, final_norm):
    batch, seq, _ = x.shape
    t = batch * seq
    h = x.reshape(t, D_MODEL)
    cos, sin = _rope_tables(positions)

    n_main = 3 * FOX_WIDTH + 3 * DIL_WIDTH
    w_in = fd_w_in[0]
    tiles0 = tuple((c, 512, c in (3 * FOX_WIDTH, 3 * FOX_WIDTH + DIL_WIDTH)) for c in range(0, n_main, 512))
    proj0, log_f = _proj(h, mix_norm[0], w_in[:, :n_main].astype(BF16),
                         _pad_cols(w_in[:, n_main:]).astype(BF16),
                         _pad_cols(fd_forget_b[0].reshape(1, FOX_HEADS)).astype(F32),
                         cos, sin, tiles0, "log_sigmoid")
    ccol, crow = _cumsum(log_f, batch, seq)
    o_fox = _fox_attention(proj0, ccol, crow, batch, seq)
    o_dil = _dilated_attention(proj0, batch, seq)
    h = _outproj(h, [o_fox, o_dil], fd_w_out[0].astype(BF16))
    h = _ffn(h, ffn_norm[0], dense_w_gate[0].astype(BF16), dense_w_up[0].astype(BF16),
             dense_w_down[0].astype(BF16))
    h = _ple(h, ple_norm[0], p[0].reshape(t, -1), ple_gate_w[0].astype(BF16), ple_proj_w[0].astype(BF16),
             final_norm, final=False)

    qw = NSA_HEADS * HEAD_DIM
    n_main1 = qw + 6 * NSA_KV
    w_in1 = nsa_w_in[0]
    rope_cols = set(range(0, qw, 256)) | {qw, qw + 2 * NSA_KV, qw + 4 * NSA_KV}
    tiles1 = tuple((c, 256, c in rope_cols) for c in range(0, n_main1, 256))
    w_gate = w_in1[:, n_main1:].reshape(D_MODEL, NSA_HEADS, 3).transpose(0, 2, 1).reshape(D_MODEL, 3 * NSA_HEADS)
    proj1, gates = _proj(h, mix_norm[1], w_in1[:, :n_main1].astype(BF16), _pad_cols(w_gate).astype(BF16),
                         jnp.zeros((1, LANES), F32), cos, sin, tiles1, "sigmoid")
    n_cmp = (seq - NSA_CMP_LEN) // NSA_CMP_STRIDE + 1
    kc_cmp = _compress(proj1, qw // NSA_WIDTH, nsa_pos_k[0], nsa_w1_k[0], nsa_b1_k[0],
                       jnp.tile(nsa_w2_k[0], (1, NSA_HPG)).astype(BF16), batch, seq)
    vc_cmp = _compress(proj1, qw // NSA_WIDTH + 1, nsa_pos_v[0], nsa_w1_v[0], nsa_b1_v[0],
                       jnp.tile(nsa_w2_v[0], (1, NSA_HPG)).astype(BF16), batch, seq)
    o_nsa = _nsa_attention(proj1, gates, kc_cmp, vc_cmp, batch, seq, n_cmp)
    h = _outproj(h, [o_nsa], nsa_w_out[0].astype(BF16))
    cw, routed, rank_col, rank_row, routed_row = _router(
        h, ffn_norm[1], _pad_cols(moe_w_router[0]).astype(BF16),
        _pad_cols(moe_b_router[0].reshape(1, N_EXPERTS)).astype(F32))
    n_chunks = t // MOE_CHUNK
    counts = rank_col.reshape(n_chunks, MOE_CHUNK, LANES)[:, -1, :N_EXPERTS].astype(jnp.int32).reshape(-1)
    h = _moe_ffn(h, ffn_norm[1], cw, routed, rank_col, rank_row, routed_row, counts,
                 moe_w_gate[0].astype(BF16), moe_w_up[0].astype(BF16), moe_w_down[0].astype(BF16))
    h = _ple(h, ple_norm[1], p[1].reshape(t, -1), ple_gate_w[1].astype(BF16), ple_proj_w[1].astype(BF16),
             final_norm, final=True)
    return h.reshape(batch, seq, D_MODEL)
```

```python
import functools

import jax
import jax.numpy as jnp
from jax import lax
from jax.experimental import pallas as pl
from jax.experimental.pallas import tpu as pltpu

F32 = jnp.float32
BF16 = jnp.bfloat16

D_MODEL = 1024
HEAD_DIM = 64
LANES = 128
FOX_HEADS = 8
DIL_HEADS = 8
FOX_WIDTH = FOX_HEADS * HEAD_DIM
DIL_WIDTH = DIL_HEADS * HEAD_DIM
DIL_PATTERNS = ((128, 1), (512, 4), (2048, 16))
Q_BLOCK = 128
NSA_HEADS = 16
NSA_GROUPS = 4
NSA_HPG = NSA_HEADS // NSA_GROUPS
NSA_KV = NSA_GROUPS * HEAD_DIM
NSA_CMP_LEN = 32
NSA_CMP_STRIDE = 16
NSA_SEL_LEN = 64
NSA_TOP_N = 8
NSA_WINDOW = 512
N_EXPERTS = 8
ROPE_THETA = 10000.0
RMS_EPS = 1e-6
NEG_INF = -1e30
SCALE = HEAD_DIM ** -0.5
LOG2E = 1.4426950408889634
LN2 = 0.6931471805599453
SCALE2 = SCALE * LOG2E
ONES_ROWS = 16

VMEM_LIMIT_BYTES = 52 * 1024 * 1024


def _params(*sem):
    return pltpu.CompilerParams(dimension_semantics=sem, vmem_limit_bytes=VMEM_LIMIT_BYTES)


def _rms(x, g):
    return x * lax.rsqrt(jnp.mean(x * x, axis=-1, keepdims=True) + RMS_EPS) * g


def _dot(a, b):
    return jnp.dot(a, b, preferred_element_type=F32)


def _dot_nt(a, b):
    return lax.dot_general(a, b, (((1,), (1,)), ((), ())), preferred_element_type=F32)


def _split3(x):
    hi = x.astype(BF16)
    r = x - hi.astype(F32)
    mid = r.astype(BF16)
    lo = (r - mid.astype(F32)).astype(BF16)
    return hi, mid, lo


def _lane_col(x, idx):
    lane = lax.broadcasted_iota(jnp.int32, x.shape, 1)
    return jnp.sum(jnp.where(lane == idx, x, 0.0), axis=1, keepdims=True)


def _rope_table_kernel(pos_ref, inv_ref, cos_ref, sin_ref):
    ang = pos_ref[...] * inv_ref[...]
    lane = lax.broadcasted_iota(jnp.int32, ang.shape, 1)
    sign = jnp.where((lane & (HEAD_DIM - 1)) < HEAD_DIM // 2, -1.0, 1.0)
    cos_ref[...] = jnp.cos(ang)
    sin_ref[...] = jnp.sin(ang) * sign


def _rope_tables(positions):
    t = positions.size
    half = HEAD_DIM // 2
    inv_freq = ROPE_THETA ** (-jnp.arange(half, dtype=F32) / half)
    inv = jnp.tile(inv_freq, LANES // half).reshape(1, LANES)
    pos = jnp.broadcast_to(positions.astype(F32).reshape(t, 1), (t, LANES))
    tm = 1024
    return pl.pallas_call(
        _rope_table_kernel,
        grid=(t // tm,),
        in_specs=[pl.BlockSpec((tm, LANES), lambda i: (i, 0)),
                  pl.BlockSpec((1, LANES), lambda i: (0, 0))],
        out_specs=[pl.BlockSpec((tm, LANES), lambda i: (i, 0))] * 2,
        out_shape=[jax.ShapeDtypeStruct((t, LANES), F32)] * 2,
        compiler_params=_params("parallel"),
        name="rope_tables",
    )(pos, inv)


def _proj_kernel(h_ref, g_ref, w_ref, wa_ref, ba_ref, cos_ref, sin_ref, o_ref, oa_ref, *, tiles, aux_act):
    xn = _rms(h_ref[...], g_ref[...]).astype(BF16)
    for c0, width, rope in tiles:
        acc = _dot(xn, w_ref[:, c0:c0 + width])
        if rope:
            reps = width // LANES
            cos = jnp.tile(cos_ref[...], (1, reps))
            sin = jnp.tile(sin_ref[...], (1, reps))
            lane = lax.broadcasted_iota(jnp.int32, acc.shape, 1)
            first_half = (lane & (HEAD_DIM - 1)) < HEAD_DIM // 2
            partner = jnp.where(first_half,
                                pltpu.roll(acc, width - HEAD_DIM // 2, 1),
                                pltpu.roll(acc, HEAD_DIM // 2, 1))
            acc = acc * cos + partner * sin
        o_ref[:, c0:c0 + width] = acc
    aux = _dot(xn, wa_ref[...]) + ba_ref[...]
    if aux_act == "log_sigmoid":
        oa_ref[...] = jnp.minimum(aux, 0.0) - jnp.log1p(jnp.exp(-jnp.abs(aux)))
    else:
        oa_ref[...] = jax.nn.sigmoid(aux)


def _proj(h, gain, w, w_aux, b_aux, cos, sin, tiles, aux_act, tm=512):
    t, n = h.shape[0], w.shape[1]
    return pl.pallas_call(
        functools.partial(_proj_kernel, tiles=tiles, aux_act=aux_act),
        grid=(t // tm,),
        in_specs=[pl.BlockSpec((tm, D_MODEL), lambda i: (i, 0)),
                  pl.BlockSpec((1, D_MODEL), lambda i: (0, 0)),
                  pl.BlockSpec((D_MODEL, n), lambda i: (0, 0)),
                  pl.BlockSpec((D_MODEL, LANES), lambda i: (0, 0)),
                  pl.BlockSpec((1, LANES), lambda i: (0, 0)),
                  pl.BlockSpec((tm, LANES), lambda i: (i, 0)),
                  pl.BlockSpec((tm, LANES), lambda i: (i, 0))],
        out_specs=[pl.BlockSpec((tm, n), lambda i: (i, 0)),
                   pl.BlockSpec((tm, LANES), lambda i: (i, 0))],
        out_shape=[jax.ShapeDtypeStruct((t, n), F32), jax.ShapeDtypeStruct((t, LANES), F32)],
        compiler_params=_params("parallel"),
        name="in_proj",
    )(h, gain.reshape(1, D_MODEL), w, w_aux, b_aux, cos, sin)


CUM_BLOCK = 512


def _cumsum_kernel(x_ref, ccol_ref, crow_ref, carry_ref):
    j = pl.program_id(1)

    @pl.when(j == 0)
    def _():
        carry_ref[...] = jnp.zeros_like(carry_ref)

    x = x_ref[...]
    r = lax.broadcasted_iota(jnp.int32, (CUM_BLOCK, CUM_BLOCK), 0)
    c = lax.broadcasted_iota(jnp.int32, (CUM_BLOCK, CUM_BLOCK), 1)
    tri = jnp.where(r >= c, 1.0, 0.0).astype(BF16)
    hi, mid, lo = _split3(x)
    cum = _dot(tri, hi) + _dot(tri, mid) + _dot(tri, lo) + carry_ref[0:1, :]
    ccol_ref[...] = cum
    crow_ref[0] = cum.T
    carry_ref[...] = jnp.broadcast_to(cum[CUM_BLOCK - 1:CUM_BLOCK, :], carry_ref.shape)


def _cumsum(x, batch, seq):
    nb = seq // CUM_BLOCK
    return pl.pallas_call(
        _cumsum_kernel,
        grid=(batch, nb),
        in_specs=[pl.BlockSpec((CUM_BLOCK, LANES), lambda b, j: (b * nb + j, 0))],
        out_specs=[pl.BlockSpec((CUM_BLOCK, LANES), lambda b, j: (b * nb + j, 0)),
                   pl.BlockSpec((1, LANES, CUM_BLOCK), lambda b, j: (b, 0, j))],
        out_shape=[jax.ShapeDtypeStruct((batch * seq, LANES), F32),
                   jax.ShapeDtypeStruct((batch, LANES, seq), F32)],
        scratch_shapes=[pltpu.VMEM((8, LANES), F32)],
        compiler_params=_params("parallel", "arbitrary"),
        name="token_cumsum",
    )(x)


def _fox_kernel(q_ref, k_ref, v_ref, ccol_ref, crow_ref, o_ref, kb_ref, vt_ref, ck_ref, *, tq, seq):
    pair = pl.program_id(1)
    i = pl.program_id(2)
    tk = tq
    h0 = 2 * pair

    @pl.when(i == 0)
    def _():
        def fill(ci, carry):
            rs = pl.ds(pl.multiple_of(ci * tk, tk), tk)
            kb_ref[rs, :] = k_ref[rs, :].astype(BF16)
            vt_ref[0:LANES, rs] = v_ref[rs, :].T.astype(BF16)
            vt_ref[LANES:, rs] = jnp.ones((ONES_ROWS, tk), BF16)
            cc = ccol_ref[rs, :] * LOG2E
            ck_ref[0, rs, :] = jnp.broadcast_to(_lane_col(cc, h0), (tk, LANES))
            ck_ref[1, rs, :] = jnp.broadcast_to(_lane_col(cc, h0 + 1), (tk, LANES))
            return carry

        lax.fori_loop(0, seq // tk, fill, 0)

    t0 = pl.multiple_of(i * tq, tq)
    low = lax.broadcasted_iota(jnp.int32, (tq, LANES), 1) < HEAD_DIM
    q = q_ref[...] * SCALE2
    qs = jnp.concatenate([jnp.where(low, q, 0.0), jnp.where(low, 0.0, q)], axis=0).astype(BF16)
    cq = jnp.concatenate([crow_ref[0, pl.ds(h0, 1), pl.ds(t0, tq)],
                          crow_ref[0, pl.ds(h0 + 1, 1), pl.ds(t0, tq)]], axis=1) * LOG2E
    reps = tq // LANES

    def scores(k0, nk):
        ck = jnp.concatenate([ck_ref[0, k0:k0 + nk, :]] * reps + [ck_ref[1, k0:k0 + nk, :]] * reps, axis=1)
        return _dot_nt(kb_ref[k0:k0 + nk, :], qs) + cq - ck

    krow = lax.broadcasted_iota(jnp.int32, (tk, tq), 0)
    qlane = lax.broadcasted_iota(jnp.int32, (tk, tq), 1)
    causal = jnp.where(krow <= qlane, 0.0, NEG_INF)

    def branch(n):
        below = (n - 1) * tk
        s_diag = scores(below, tk) + jnp.concatenate([causal, causal], axis=1)
        m = jnp.max(s_diag, axis=0, keepdims=True)
        if below:
            s_below = scores(0, below)
            m = jnp.maximum(m, jnp.max(s_below, axis=0, keepdims=True))
            acc = _dot(vt_ref[:, 0:below], jnp.exp2(s_below - m).astype(BF16))
        else:
            acc = 0.0
        acc = acc + _dot(vt_ref[:, below:below + tk], jnp.exp2(s_diag - m).astype(BF16))
        out = acc[0:LANES] / acc[LANES:LANES + 1]
        o_ref[...] = jnp.concatenate([out[0:HEAD_DIM, 0:tq], out[HEAD_DIM:, tq:]], axis=0).T.astype(o_ref.dtype)

    for n in range(1, seq // tq + 1):
        pl.when(i == n - 1)(functools.partial(branch, n))


def _fox_attention(proj, ccol, crow, batch, seq, tq=512):
    nq = seq // tq
    npair = FOX_WIDTH // LANES
    return pl.pallas_call(
        functools.partial(_fox_kernel, tq=tq, seq=seq),
        grid=(batch, npair, nq),
        in_specs=[pl.BlockSpec((tq, LANES), lambda b, p, i: (b * nq + i, p)),
                  pl.BlockSpec((seq, LANES), lambda b, p, i: (b, npair + p)),
                  pl.BlockSpec((seq, LANES), lambda b, p, i: (b, 2 * npair + p)),
                  pl.BlockSpec((seq, LANES), lambda b, p, i: (b, 0)),
                  pl.BlockSpec((1, 8, seq), lambda b, p, i: (b, 0, 0))],
        out_specs=pl.BlockSpec((tq, LANES), lambda b, p, i: (b * nq + i, p)),
        out_shape=jax.ShapeDtypeStruct((batch * seq, FOX_WIDTH), BF16),
        scratch_shapes=[pltpu.VMEM((seq, LANES), BF16), pltpu.VMEM((LANES + ONES_ROWS, seq), BF16),
                        pltpu.VMEM((2, seq, LANES), F32)],
        compiler_params=_params("parallel", "parallel", "arbitrary"),
        name="fox_attention",
    )(proj, proj, proj, ccol, crow)


DIL_GROUP = 4


def _dil_kernel(q_ref, k_ref, v_ref, o_ref, os_ref, ls_ref, *, seq):
    low = lax.broadcasted_iota(jnp.int32, (Q_BLOCK, LANES), 1) < HEAD_DIM
    kr = lax.broadcasted_iota(jnp.int32, (2 * Q_BLOCK, Q_BLOCK), 0)
    qc = lax.broadcasted_iota(jnp.int32, (2 * Q_BLOCK, Q_BLOCK), 1)
    dist = qc + Q_BLOCK - kr
    def score_group(pi, units):
        window, dil = DIL_PATTERNS[pi]
        span = window // dil
        nb = (seq // dil) // Q_BLOCK
        band_bias = jnp.where((dist >= 0) & (dist <= span), 0.0, NEG_INF)
        scored, dests = [], []

        def rows(start):
            return pl.ds(start, Q_BLOCK, stride=dil) if dil > 1 else pl.ds(start, Q_BLOCK)

        for u in units:
            r, blk = divmod(u, nb)
            cur = r + blk * (Q_BLOCK * dil)
            q = q_ref[rows(cur), :] * SCALE2
            qs = jnp.concatenate([jnp.where(low, q, 0.0), jnp.where(low, 0.0, q)], axis=0).astype(BF16)
            if blk == 0:
                kk = k_ref[rows(cur), :].astype(BF16)
                vt = v_ref[rows(cur), :].T.astype(BF16)
                bias = band_bias[Q_BLOCK:]
            else:
                prev = cur - Q_BLOCK * dil
                kk = jnp.concatenate([k_ref[rows(prev), :], k_ref[rows(cur), :]], axis=0).astype(BF16)
                vt = jnp.concatenate([v_ref[rows(prev), :].T, v_ref[rows(cur), :].T], axis=1).astype(BF16)
                bias = band_bias
            vt = jnp.concatenate([vt, jnp.ones((ONES_ROWS, vt.shape[1]), BF16)], axis=0)
            s = _dot_nt(kk, qs) + jnp.concatenate([bias, bias], axis=1)
            scored.append((s, vt))
            dests.append(rows(cur))
        return scored, dests

    def softmax(s):
        m = jnp.max(s, axis=0, keepdims=True)
        return jnp.exp2(s - m).astype(BF16), m

    def finish(vt, p, m):
        ot = _dot(vt, p)
        den = ot[LANES:LANES + 1]
        ot = ot[0:LANES] * (1.0 / den)
        lse = m * LN2 + jnp.log(den)
        lse_t = jnp.concatenate([jnp.broadcast_to(lse[:, 0:Q_BLOCK], (HEAD_DIM, Q_BLOCK)),
                                 jnp.broadcast_to(lse[:, Q_BLOCK:], (HEAD_DIM, Q_BLOCK))], axis=0)
        return jnp.concatenate([ot[0:HEAD_DIM, 0:Q_BLOCK], ot[HEAD_DIM:, Q_BLOCK:]], axis=0).T, lse_t.T

    def finish_group(pi, scored, dests):
        probs = [softmax(s) for s, _ in scored]
        outs = [finish(vt, p, m) for (_, vt), (p, m) in zip(scored, probs)]
        for dest, (o, l) in zip(dests, outs):
            os_ref[pi, dest, :] = o
            ls_ref[pi, dest, :] = l

    groups = []
    for pi, (window, dil) in enumerate(DIL_PATTERNS):
        n_units = dil * ((seq // dil) // Q_BLOCK)
        groups += [(pi, range(u0, min(u0 + DIL_GROUP, n_units))) for u0 in range(0, n_units, DIL_GROUP)]
    pending = score_group(*groups[0])
    for gi, (pi, _) in enumerate(groups):
        upcoming = score_group(*groups[gi + 1]) if gi + 1 < len(groups) else None
        finish_group(pi, *pending)
        pending = upcoming

    chunk = 256

    def combine(ci, carry):
        rs = pl.ds(pl.multiple_of(ci * chunk, chunk), chunk)
        l0, l1, l2 = ls_ref[0, rs, :], ls_ref[1, rs, :], ls_ref[2, rs, :]
        m = jnp.maximum(jnp.maximum(l0, l1), l2)
        e0, e1, e2 = jnp.exp(l0 - m), jnp.exp(l1 - m), jnp.exp(l2 - m)
        tot = e0 + e1 + e2
        o_ref[rs, :] = ((e0 / tot) * os_ref[0, rs, :] + (e1 / tot) * os_ref[1, rs, :]
                        + (e2 / tot) * os_ref[2, rs, :]).astype(o_ref.dtype)
        return carry

    lax.fori_loop(0, seq // chunk, combine, 0)


def _dilated_attention(proj, batch, seq):
    npair = DIL_WIDTH // LANES
    base = 3 * FOX_WIDTH // LANES
    return pl.pallas_call(
        functools.partial(_dil_kernel, seq=seq),
        grid=(batch, npair),
        in_specs=[pl.BlockSpec((seq, LANES), lambda b, p: (b, base + p)),
                  pl.BlockSpec((seq, LANES), lambda b, p: (b, base + npair + p)),
                  pl.BlockSpec((seq, LANES), lambda b, p: (b, base + 2 * npair + p))],
        out_specs=pl.BlockSpec((seq, LANES), lambda b, p: (b, p)),
        out_shape=jax.ShapeDtypeStruct((batch * seq, DIL_WIDTH), BF16),
        scratch_shapes=[pltpu.VMEM((3, seq, LANES), F32), pltpu.VMEM((3, seq, LANES), F32)],
        compiler_params=_params("parallel", "parallel"),
        name="dilated_attention",
    )(proj, proj, proj)


def _outproj_kernel(*refs, n_in, tn):
    h_ref = refs[0]
    a_refs = refs[1:1 + n_in]
    w_ref = refs[1 + n_in]
    o_ref = refs[2 + n_in]
    acts = [a[...].astype(BF16) for a in a_refs]
    for c0 in range(0, D_MODEL, tn):
        acc = h_ref[:, c0:c0 + tn]
        k0 = 0
        for a in acts:
            acc = acc + _dot(a, w_ref[k0:k0 + a.shape[1], c0:c0 + tn])
            k0 += a.shape[1]
        o_ref[:, c0:c0 + tn] = acc


def _outproj(h, acts, w, tm=1024, tn=512):
    t = h.shape[0]
    return pl.pallas_call(
        functools.partial(_outproj_kernel, n_in=len(acts), tn=tn),
        grid=(t // tm,),
        in_specs=([pl.BlockSpec((tm, D_MODEL), lambda i: (i, 0))]
                  + [pl.BlockSpec((tm, a.shape[1]), lambda i: (i, 0)) for a in acts]
                  + [pl.BlockSpec((D_MODEL, D_MODEL), lambda i: (0, 0))]),
        out_specs=pl.BlockSpec((tm, D_MODEL), lambda i: (i, 0)),
        out_shape=jax.ShapeDtypeStruct((t, D_MODEL), F32),
        compiler_params=_params("parallel"),
        name="out_proj",
    )(h, *acts, w)


def _swiglu_tile(x, wg, wu, wd):
    gate = _dot(x, wg)
    up = _dot(x, wu)
    return _dot((gate * jax.nn.sigmoid(gate) * up).astype(BF16), wd)


def _ffn_kernel(h_ref, g_ref, wg_ref, wu_ref, wd_ref, o_ref, xn_ref, acc_ref, *, n_f):
    f = pl.program_id(1)

    @pl.when(f == 0)
    def _():
        xn_ref[...] = _rms(h_ref[...], g_ref[...]).astype(BF16)
        acc_ref[...] = jnp.zeros_like(acc_ref)

    acc_ref[...] += _swiglu_tile(xn_ref[...], wg_ref[...], wu_ref[...], wd_ref[...])

    @pl.when(f == n_f - 1)
    def _():
        o_ref[...] = h_ref[...] + acc_ref[...]


def _ffn(h, gain, wg, wu, wd, tm=512, tf=1792):
    t, dff = h.shape[0], wg.shape[1]
    n_f = dff // tf
    return pl.pallas_call(
        functools.partial(_ffn_kernel, n_f=n_f),
        grid=(t // tm, n_f),
        in_specs=[pl.BlockSpec((tm, D_MODEL), lambda i, f: (i, 0)),
                  pl.BlockSpec((1, D_MODEL), lambda i, f: (0, 0)),
                  pl.BlockSpec((D_MODEL, tf), lambda i, f: (0, f)),
                  pl.BlockSpec((D_MODEL, tf), lambda i, f: (0, f)),
                  pl.BlockSpec((tf, D_MODEL), lambda i, f: (f, 0))],
        out_specs=pl.BlockSpec((tm, D_MODEL), lambda i, f: (i, 0)),
        out_shape=jax.ShapeDtypeStruct((t, D_MODEL), F32),
        scratch_shapes=[pltpu.VMEM((tm, D_MODEL), BF16), pltpu.VMEM((tm, D_MODEL), F32)],
        compiler_params=_params("parallel", "arbitrary"),
        name="dense_swiglu",
    )(h, gain.reshape(1, D_MODEL), wg, wu, wd)


MOE_CHUNK = 1024
MOE_ROWS = 128
MOE_SCATTER = 256


def _moe_kernel(cnt_ref, h_ref, g_ref, cw_ref, m_ref, rcol_ref, rrow_ref, mrow_ref, wg_ref, wu_ref, wd_ref,
                o_ref, xn_ref, xe_ref, ye_ref, *, n_f):
    c = pl.program_id(0)
    e = pl.program_id(1)
    f = pl.program_id(2)
    chunk = MOE_CHUNK
    n = cnt_ref[c * N_EXPERTS + e]
    n_scatter = (n + MOE_SCATTER - 1) // MOE_SCATTER
    n_tiles = (n + MOE_ROWS - 1) // MOE_ROWS

    @pl.when((e == 0) & (f == 0))
    def _():
        h = h_ref[...]
        xn_ref[...] = _rms(h, g_ref[...]).astype(BF16)
        o_ref[...] = h

    @pl.when(f == 0)
    def _():
        rank = rrow_ref[pl.ds(e, 1), :] * mrow_ref[pl.ds(e, 1), :]
        slot = lax.broadcasted_iota(jnp.int32, (MOE_ROWS, chunk), 0) + 1

        def gather(i, carry):
            rows = pl.ds(pl.multiple_of(i * MOE_ROWS, MOE_ROWS), MOE_ROWS)
            onehot = jnp.where(rank == (slot + i * MOE_ROWS).astype(F32), 1.0, 0.0).astype(BF16)
            xe_ref[rows, :] = _dot(onehot, xn_ref[...]).astype(BF16)
            ye_ref[rows, :] = jnp.zeros((MOE_ROWS, D_MODEL), F32)
            return carry

        def clear(i, carry):
            rows = pl.ds(pl.multiple_of(i * MOE_ROWS, MOE_ROWS), MOE_ROWS)
            ye_ref[rows, :] = jnp.zeros((MOE_ROWS, D_MODEL), F32)
            return carry

        lax.fori_loop(0, n_tiles, gather, 0)
        lax.fori_loop(n_tiles, n_scatter * (MOE_SCATTER // MOE_ROWS), clear, 0)

    def tile(i, carry):
        rows = pl.ds(pl.multiple_of(i * MOE_ROWS, MOE_ROWS), MOE_ROWS)
        ye_ref[rows, :] += _swiglu_tile(xe_ref[rows, :], wg_ref[...], wu_ref[...], wd_ref[...])
        return carry

    lax.fori_loop(0, n_tiles, tile, 0)

    @pl.when(f == n_f - 1)
    def _():
        rank = _lane_col(rcol_ref[...] * m_ref[...], e)
        weight = _lane_col(cw_ref[...], e)
        slot = lax.broadcasted_iota(jnp.int32, (chunk, MOE_SCATTER), 1) + 1

        def scatter(i, carry):
            rows = pl.ds(pl.multiple_of(i * MOE_SCATTER, MOE_SCATTER), MOE_SCATTER)
            onehot = jnp.where(rank == (slot + i * MOE_SCATTER).astype(F32), 1.0, 0.0).astype(BF16)
            o_ref[...] += weight * _dot(onehot, ye_ref[rows, :].astype(BF16))
            return carry

        lax.fori_loop(0, n_scatter, scatter, 0)


def _moe_ffn(h, gain, cw, mask, rcol, rrow, mrow, counts, wg, wu, wd, tf=1792):
    t, dff = h.shape[0], wg.shape[2]
    n_f = dff // tf
    chunk = MOE_CHUNK
    tok = lambda width: pl.BlockSpec((chunk, width), lambda c, e, f, cnt: (c, 0))
    lane_major = pl.BlockSpec((None, N_EXPERTS, chunk), lambda c, e, f, cnt: (c, 0, 0))
    grid_spec = pltpu.PrefetchScalarGridSpec(
        num_scalar_prefetch=1,
        grid=(t // chunk, N_EXPERTS, n_f),
        in_specs=[tok(D_MODEL),
                  pl.BlockSpec((1, D_MODEL), lambda c, e, f, cnt: (0, 0)),
                  tok(LANES), tok(LANES), tok(LANES), lane_major, lane_major,
                  pl.BlockSpec((None, D_MODEL, tf), lambda c, e, f, cnt: (e, 0, f)),
                  pl.BlockSpec((None, D_MODEL, tf), lambda c, e, f, cnt: (e, 0, f)),
                  pl.BlockSpec((None, tf, D_MODEL), lambda c, e, f, cnt: (e, f, 0))],
        out_specs=tok(D_MODEL),
        scratch_shapes=[pltpu.VMEM((chunk, D_MODEL), BF16), pltpu.VMEM((chunk, D_MODEL), BF16),
                        pltpu.VMEM((chunk, D_MODEL), F32)],
    )
    return pl.pallas_call(
        functools.partial(_moe_kernel, n_f=n_f),
        grid_spec=grid_spec,
        out_shape=jax.ShapeDtypeStruct((t, D_MODEL), F32),
        compiler_params=_params("parallel", "arbitrary", "arbitrary"),
        name="moe_swiglu",
    )(counts, h, gain.reshape(1, D_MODEL), cw, mask, rcol, rrow, mrow, wg, wu, wd)


def _router_kernel(h_ref, g_ref, w_ref, b_ref, cw_ref, m_ref, rcol_ref, rrow_ref, mrow_ref):
    xn = _rms(h_ref[...], g_ref[...]).astype(BF16)
    logits = _dot(xn, w_ref[...]) + b_ref[...]
    lane = lax.broadcasted_iota(jnp.int32, logits.shape, 1).astype(F32)
    logits = jnp.where(lane < N_EXPERTS, logits, -jnp.inf)
    m1 = jnp.max(logits, axis=1, keepdims=True)
    i1 = jnp.min(jnp.where(logits == m1, lane, float(LANES)), axis=1, keepdims=True)
    rest = jnp.where(lane == i1, -jnp.inf, logits)
    m2 = jnp.max(rest, axis=1, keepdims=True)
    i2 = jnp.min(jnp.where(rest == m2, lane, float(LANES)), axis=1, keepdims=True)
    e2 = jnp.exp(m2 - m1)
    w1 = 1.0 / (1.0 + e2)
    w2 = e2 / (1.0 + e2)
    cw_ref[...] = jnp.where(lane == i1, w1, jnp.where(lane == i2, w2, 0.0))
    mask = jnp.where((lane == i1) | (lane == i2), 1.0, 0.0)
    m_ref[...] = mask
    r = lax.broadcasted_iota(jnp.int32, (CUM_BLOCK, CUM_BLOCK), 0)
    c = lax.broadcasted_iota(jnp.int32, (CUM_BLOCK, CUM_BLOCK), 1)
    tri = jnp.where(r >= c, 1.0, 0.0).astype(BF16)
    carry = jnp.zeros((1, LANES), F32)
    for r0 in range(0, mask.shape[0], CUM_BLOCK):
        blk = mask[r0:r0 + CUM_BLOCK]
        cum = _dot(tri, blk.astype(BF16)) + carry
        rcol_ref[r0:r0 + CUM_BLOCK, :] = cum
        rrow_ref[:, r0:r0 + CUM_BLOCK] = cum.T
        mrow_ref[:, r0:r0 + CUM_BLOCK] = blk.T
        carry = cum[CUM_BLOCK - 1:CUM_BLOCK, :]


def _router(h, gain, w, b):
    t = h.shape[0]
    tm = MOE_CHUNK
    token_major = pl.BlockSpec((tm, LANES), lambda i: (i, 0))
    lane_major = pl.BlockSpec((None, LANES, tm), lambda i: (i, 0, 0))
    return pl.pallas_call(
        _router_kernel,
        grid=(t // tm,),
        in_specs=[pl.BlockSpec((tm, D_MODEL), lambda i: (i, 0)),
                  pl.BlockSpec((1, D_MODEL), lambda i: (0, 0)),
                  pl.BlockSpec((D_MODEL, LANES), lambda i: (0, 0)),
                  pl.BlockSpec((1, LANES), lambda i: (0, 0))],
        out_specs=[token_major, token_major, token_major, lane_major, lane_major],
        out_shape=[jax.ShapeDtypeStruct((t, LANES), F32)] * 3
                  + [jax.ShapeDtypeStruct((t // tm, LANES, tm), F32)] * 2,
        compiler_params=_params("parallel"),
        name="moe_router",
    )(h, gain.reshape(1, D_MODEL), w, b)


def _ple_kernel(h_ref, g_ref, p_ref, wg_ref, wp_ref, fg_ref, o_ref, *, final, tn):
    h = h_ref[...]
    xn = _rms(h, g_ref[...]).astype(BF16)
    pe = p_ref[...].astype(BF16)
    outs = []
    for c0 in range(0, D_MODEL, tn):
        gate = jax.nn.sigmoid(_dot(xn, wg_ref[:, c0:c0 + tn]))
        outs.append(h[:, c0:c0 + tn] + gate * _dot(pe, wp_ref[:, c0:c0 + tn]))
    new = jnp.concatenate(outs, axis=1)
    o_ref[...] = _rms(new, fg_ref[...]) if final else new


def _ple(h, gain, p, wg, wp, final_gain, final, tm=1024, tn=512):
    t, pd = p.shape
    return pl.pallas_call(
        functools.partial(_ple_kernel, final=final, tn=tn),
        grid=(t // tm,),
        in_specs=[pl.BlockSpec((tm, D_MODEL), lambda i: (i, 0)),
                  pl.BlockSpec((1, D_MODEL), lambda i: (0, 0)),
                  pl.BlockSpec((tm, pd), lambda i: (i, 0)),
                  pl.BlockSpec((D_MODEL, D_MODEL), lambda i: (0, 0)),
                  pl.BlockSpec((pd, D_MODEL), lambda i: (0, 0)),
                  pl.BlockSpec((1, D_MODEL), lambda i: (0, 0))],
        out_specs=pl.BlockSpec((tm, D_MODEL), lambda i: (i, 0)),
        out_shape=jax.ShapeDtypeStruct((t, D_MODEL), F32),
        compiler_params=_params("parallel"),
        name="ple",
    )(h, gain.reshape(1, D_MODEL), p, wg, wp, final_gain.reshape(1, D_MODEL))


def _compress_kernel(x01_ref, x23_ref, pos_ref, w1_ref, b1_ref, w2_ref, o_ref, *, blocks):
    stride = NSA_CMP_STRIDE
    hidden = b1_ref.shape[1]
    low = lax.broadcasted_iota(jnp.int32, (blocks, LANES), 1) < HEAD_DIM
    first = [jnp.zeros((blocks, hidden), F32) for _ in range(NSA_GROUPS)]
    second = [jnp.zeros((blocks, hidden), F32) for _ in range(NSA_GROUPS)]
    for j in range(stride):
        for half, x_ref in enumerate((x01_ref, x23_ref)):
            xs = x_ref[pl.ds(j, blocks, stride=stride), :]
            xa = xs + pos_ref[j:j + 1, :]
            xb = xs + pos_ref[stride + j:stride + j + 1, :]
            for sub in range(2):
                g = 2 * half + sub
                keep = low if sub == 0 else jnp.logical_not(low)
                first[g] = first[g] + _dot(jnp.where(keep, xa, 0.0).astype(BF16), w1_ref[j])
                second[g] = second[g] + _dot(jnp.where(keep, xb, 0.0).astype(BF16), w1_ref[stride + j])
    row = lax.broadcasted_iota(jnp.int32, (blocks, NSA_WIDTH), 0)
    for g in range(NSA_GROUPS):
        hid = first[g] + pltpu.roll(second[g], blocks - 1, 0) + b1_ref[...]
        out = _dot(jax.nn.gelu(hid, approximate=True).astype(BF16), w2_ref[...])
        o_ref[g] = jnp.where(row == blocks - 1, 0.0, out)


def _compress(proj, col_block, pos, w1, b1, w2, batch, seq):
    blocks = seq // NSA_CMP_STRIDE
    hidden = w1.shape[1]
    w1_rep = jnp.tile(w1.reshape(NSA_CMP_LEN, HEAD_DIM, hidden), (1, 2, 1))
    pos_rep = jnp.tile(pos, (1, 2))
    halves = NSA_WIDTH // LANES
    return pl.pallas_call(
        functools.partial(_compress_kernel, blocks=blocks),
        grid=(batch,),
        in_specs=[pl.BlockSpec((seq, LANES), lambda b: (b, halves * col_block)),
                  pl.BlockSpec((seq, LANES), lambda b: (b, halves * col_block + 1)),
                  pl.BlockSpec((NSA_CMP_LEN, LANES), lambda b: (0, 0)),
                  pl.BlockSpec((NSA_CMP_LEN, LANES, hidden), lambda b: (0, 0, 0)),
                  pl.BlockSpec((1, hidden), lambda b: (0, 0)),
                  pl.BlockSpec((hidden, NSA_WIDTH), lambda b: (0, 0))],
        out_specs=pl.BlockSpec((NSA_GROUPS, blocks, NSA_WIDTH), lambda b: (b, 0, 0)),
        out_shape=jax.ShapeDtypeStruct((batch * NSA_GROUPS, blocks, NSA_WIDTH), F32),
        compiler_params=_params("parallel"),
        name="nsa_compress",
    )(proj, proj, pos_rep, w1_rep.astype(BF16), b1.reshape(1, hidden), w2)


NSA_TQ = 256
NSA_TK = 512
NSA_SUB = 256
NSA_WIDTH = NSA_HPG * HEAD_DIM
LOG_HEAD_DIM = HEAD_DIM.bit_length() - 1
LOG_SEL_LEN = NSA_SEL_LEN.bit_length() - 1


def _nsa_kernel(q_ref, kc_ref, vc_ref, ks_ref, vs_ref, kw_ref, vw_ref, gate_ref, o_ref,
                ksb, kwb, vst_all, vwt_all, vct, gt_ref, osel_ref, *, seq, n_cmp):
    g = pl.program_id(1)
    i = pl.program_id(2)
    tq, tk = NSA_TQ, NSA_TK
    cols = NSA_HPG * tq
    vrows = HEAD_DIM + ONES_ROWS
    grow = pl.multiple_of(g * vrows, vrows)

    @pl.when((i == 0) & (g == 0))
    def _():
        def fill(ci, carry):
            rs = pl.ds(pl.multiple_of(ci * NSA_WIDTH, NSA_WIDTH), NSA_WIDTH)
            ksb[rs, :] = ks_ref[rs, :].astype(BF16)
            kwb[rs, :] = kw_ref[rs, :].astype(BF16)
            for src, dst in ((vs_ref, vst_all), (vw_ref, vwt_all)):
                vt = src[rs, :].T.astype(BF16)
                for grp in range(NSA_GROUPS):
                    dst[grp * vrows:grp * vrows + HEAD_DIM, rs] = vt[grp * HEAD_DIM:(grp + 1) * HEAD_DIM]
                    dst[grp * vrows + HEAD_DIM:(grp + 1) * vrows, rs] = jnp.ones((ONES_ROWS, NSA_WIDTH), BF16)
            return carry

        lax.fori_loop(0, seq // NSA_WIDTH, fill, 0)

    @pl.when(i == 0)
    def _():
        vct[...] = vc_ref[0].T[0:HEAD_DIM, :].astype(BF16)

    vst = vst_all.at[pl.ds(grow, vrows)]
    vwt = vwt_all.at[pl.ds(grow, vrows)]

    t0 = i * tq
    lane_grp = lax.broadcasted_iota(jnp.int32, (tq, NSA_WIDTH), 1) >> LOG_HEAD_DIM
    q = q_ref[...] * SCALE2
    rolled = [q] + [pltpu.roll(q, s * HEAD_DIM, 1) for s in range(1, NSA_HPG)]
    parts = []
    for j in range(NSA_HPG):
        shift = (g - j) & (NSA_HPG - 1)
        moved = jnp.where(shift == 0, rolled[0],
                          jnp.where(shift == 1, rolled[1], jnp.where(shift == 2, rolled[2], rolled[3])))
        parts.append(jnp.where(lane_grp == g, moved, 0.0))
    qs = jnp.concatenate(parts, axis=0).astype(BF16)

    def heads_sum(x):
        out = x[:, 0:tq]
        for j in range(1, NSA_HPG):
            out = out + x[:, j * tq:(j + 1) * tq]
        return out

    def lanes4(x):
        return jnp.concatenate([x] * NSA_HPG, axis=1)

    wk = NSA_WINDOW + tq
    ws = pl.multiple_of(jnp.maximum(t0 - NSA_WINDOW, 0), tq)
    kpos = ws + lax.broadcasted_iota(jnp.int32, (wk, tq), 0)
    qw = t0 + lax.broadcasted_iota(jnp.int32, (wk, tq), 1)
    bias_w = jnp.where((kpos <= qw) & (kpos > qw - NSA_WINDOW), 0.0, NEG_INF)
    sw = _dot_nt(kwb[pl.ds(ws, wk), :], qs) + lanes4(bias_w)

    nrow = lax.broadcasted_iota(jnp.int32, (LANES, cols), 0)
    tcol = t0 + (lax.broadcasted_iota(jnp.int32, (LANES, cols), 1) & (tq - 1))
    valid_c = (nrow * NSA_CMP_STRIDE + NSA_CMP_LEN - 1 <= tcol) & (nrow < n_cmp)
    sc = jnp.where(valid_c, _dot_nt(kc_ref[0].astype(BF16), qs), NEG_INF)
    mc = jnp.max(sc, axis=0, keepdims=True)
    ec = jnp.where(valid_c, jnp.exp2(sc - mc), 0.0)
    dc = jnp.sum(ec, axis=0, keepdims=True)
    pc = ec / jnp.where(dc > 0.0, dc, 1.0)
    o_cmp = _dot(vct[...], pc.astype(BF16))
    pc_sum = heads_sum(pc)

    n_sel_blocks = seq // NSA_SEL_LEN
    jrow = lax.broadcasted_iota(jnp.int32, (LANES, LANES), 0)
    ncol = lax.broadcasted_iota(jnp.int32, (LANES, LANES), 1)
    overlap = ((ncol * NSA_CMP_STRIDE < (jrow + 1) * NSA_SEL_LEN)
               & (ncol * NSA_CMP_STRIDE + NSA_CMP_LEN > jrow * NSA_SEL_LEN)
               & (ncol < n_cmp) & (jrow < n_sel_blocks))
    overlap = jnp.where(overlap, 1.0, 0.0).astype(BF16)
    hi, mid, lo = _split3(pc_sum)
    imp = (_dot(overlap, hi) + _dot(overlap, mid) + _dot(overlap, lo))[0:n_sel_blocks]
    pw = jnp.exp2(sw - jnp.max(sw, axis=0, keepdims=True))
    o_win = _dot(vwt[:, pl.ds(ws, wk)], pw.astype(BF16))
    o_win = o_win[0:HEAD_DIM] / o_win[HEAD_DIM:HEAD_DIM + 1]
    blk = lax.broadcasted_iota(jnp.int32, (n_sel_blocks, tq), 0)
    cur = (t0 + lax.broadcasted_iota(jnp.int32, (n_sel_blocks, tq), 1)) >> LOG_SEL_LEN
    forced = (blk == 0) | (blk == cur) | (blk == cur - 1)
    imp = jnp.where(blk > cur, -1.0, jnp.where(forced, 1e6, imp))
    beaten = jnp.zeros((n_sel_blocks, tq), jnp.int32)
    for c in range(n_sel_blocks):
        row = imp[c:c + 1, :]
        wins = (row > imp) | ((row == imp) & (blk > c))
        beaten = beaten + jnp.where(wins, 1, 0)
    sel_bias = jnp.where(beaten < NSA_TOP_N, 0.0, NEG_INF)
    sel_bias = jnp.concatenate([sel_bias, jnp.zeros((LANES - n_sel_blocks, tq), F32)], axis=0).astype(BF16)

    n_tiles = (t0 + tq - 1) // tk + 1

    def sel_branch(n):
        sub = NSA_SUB
        erow = lax.broadcasted_iota(jnp.int32, (sub, LANES), 0)
        ecol = lax.broadcasted_iota(jnp.int32, (sub, LANES), 1)
        krow = lax.broadcasted_iota(jnp.int32, (sub, tq), 0)
        qlane = t0 + lax.broadcasted_iota(jnp.int32, (sub, tq), 1)

        def scores(k0):
            expand = jnp.where(((k0 + erow) >> LOG_SEL_LEN) == ecol, 1.0, 0.0).astype(BF16)
            bias = jnp.where(k0 + krow <= qlane, _dot(expand, sel_bias), NEG_INF)
            return _dot_nt(ksb[k0:k0 + sub, :], qs) + lanes4(bias)

        starts = list(range(0, n * tk, sub))
        parts = []
        pending = scores(starts[0])
        for i, k0 in enumerate(starts):
            upcoming = scores(starts[i + 1]) if i + 1 < len(starts) else None
            m_sub = jnp.max(pending, axis=0, keepdims=True)
            parts.append((m_sub, _dot(vst[:, k0:k0 + sub], jnp.exp2(pending - m_sub).astype(BF16))))
            pending = upcoming
        m = functools.reduce(jnp.maximum, [m_sub for m_sub, _ in parts])
        acc = sum(a * jnp.exp2(m_sub - m) for m_sub, a in parts)
        osel_ref[...] = acc[0:HEAD_DIM] / acc[HEAD_DIM:HEAD_DIM + 1]

    for n in range(1, seq // tk + 1):
        pl.when(n_tiles == n)(functools.partial(sel_branch, n))
    o_sel = osel_ref[...]

    gt_ref[...] = gate_ref[...].T
    out = jnp.zeros((HEAD_DIM, cols), F32)
    for c, branch in enumerate((o_cmp, o_sel, o_win)):
        gate = jnp.concatenate([gt_ref[pl.ds(c * NSA_HEADS + g * NSA_HPG + j, 1), :] for j in range(NSA_HPG)],
                               axis=1)
        out = out + gate * branch
    o_ref[...] = jnp.concatenate([out[:, j * tq:(j + 1) * tq] for j in range(NSA_HPG)], axis=0).T.astype(o_ref.dtype)


def _nsa_attention(proj, gates, kc_cmp, vc_cmp, batch, seq, n_cmp):
    nq = seq // NSA_TQ
    ncb = kc_cmp.shape[1]
    qcols = NSA_HEADS * HEAD_DIM // NSA_WIDTH
    kv = lambda c: pl.BlockSpec((seq, NSA_WIDTH), lambda b, g, i, c=c: (b, qcols + c))
    cmp_spec = pl.BlockSpec((1, ncb, NSA_WIDTH), lambda b, g, i: (b * NSA_GROUPS + g, 0, 0))
    return pl.pallas_call(
        functools.partial(_nsa_kernel, seq=seq, n_cmp=n_cmp),
        grid=(batch, NSA_GROUPS, nq),
        in_specs=[pl.BlockSpec((NSA_TQ, NSA_WIDTH), lambda b, g, i: (b * nq + i, g)),
                  cmp_spec, cmp_spec, kv(2), kv(3), kv(4), kv(5),
                  pl.BlockSpec((NSA_TQ, LANES), lambda b, g, i: (b * nq + i, 0))],
        out_specs=pl.BlockSpec((NSA_TQ, NSA_WIDTH), lambda b, g, i: (b * nq + i, g)),
        out_shape=jax.ShapeDtypeStruct((batch * seq, NSA_HEADS * HEAD_DIM), BF16),
        scratch_shapes=[pltpu.VMEM((seq, NSA_WIDTH), BF16), pltpu.VMEM((seq, NSA_WIDTH), BF16),
                        pltpu.VMEM((NSA_GROUPS * (HEAD_DIM + ONES_ROWS), seq), BF16),
                        pltpu.VMEM((NSA_GROUPS * (HEAD_DIM + ONES_ROWS), seq), BF16),
                        pltpu.VMEM((HEAD_DIM, ncb), BF16), pltpu.VMEM((LANES, NSA_TQ), F32),
                        pltpu.VMEM((HEAD_DIM, NSA_HPG * NSA_TQ), F32)],
        compiler_params=_params("parallel", "arbitrary", "arbitrary"),
        name="nsa_attention",
    )(proj, kc_cmp, vc_cmp, proj, proj, proj, proj, gates)


def _pad_cols(w, width=LANES):
    return jnp.pad(w, ((0, 0), (0, width - w.shape[1])))


def kernel(x, p, positions, mix_norm, ffn_norm, ple_norm, ple_gate_w, ple_proj_w, fd_w_in, fd_forget_b, fd_w_out, dense_w_gate, dense_w_up, dense_w_down, nsa_w_in, nsa_pos_k, nsa_w1_k, nsa_b1_k, nsa_w2_k, nsa_pos_v, nsa_w1_v, nsa_b1_v, nsa_w2_v, nsa_w_out, moe_w_router, moe_b_router, moe_w_gate, moe_w_up, moe_w_down, final_norm):
    batch, seq, _ = x.shape
    t = batch * seq
    h = x.reshape(t, D_MODEL)
    cos, sin = _rope_tables(positions)

    n_main = 3 * FOX_WIDTH + 3 * DIL_WIDTH
    w_in = fd_w_in[0]
    tiles0 = tuple((c, 512, c in (3 * FOX_WIDTH, 3 * FOX_WIDTH + DIL_WIDTH)) for c in range(0, n_main, 512))
    proj0, log_f = _proj(h, mix_norm[0], w_in[:, :n_main].astype(BF16),
                         _pad_cols(w_in[:, n_main:]).astype(BF16),
                         _pad_cols(fd_forget_b[0].reshape(1, FOX_HEADS)).astype(F32),
                         cos, sin, tiles0, "log_sigmoid")
    ccol, crow = _cumsum(log_f, batch, seq)
    o_fox = _fox_attention(proj0, ccol, crow, batch, seq)
    o_dil = _dilated_attention(proj0, batch, seq)
    h = _outproj(h, [o_fox, o_dil], fd_w_out[0].astype(BF16))
    h = _ffn(h, ffn_norm[0], dense_w_gate[0].astype(BF16), dense_w_up[0].astype(BF16),
             dense_w_down[0].astype(BF16))
    h = _ple(h, ple_norm[0], p[0].reshape(t, -1), ple_gate_w[0].astype(BF16), ple_proj_w[0].astype(BF16),
             final_norm, final=False)

    qw = NSA_HEADS * HEAD_DIM
    n_main1 = qw + 6 * NSA_KV
    w_in1 = nsa_w_in[0]
    rope_cols = set(range(0, qw, 256)) | {qw, qw + 2 * NSA_KV, qw + 4 * NSA_KV}
    tiles1 = tuple((c, 256, c in rope_cols) for c in range(0, n_main1, 256))
    w_gate = w_in1[:, n_main1:].reshape(D_MODEL, NSA_HEADS, 3).transpose(0, 2, 1).reshape(D_MODEL, 3 * NSA_HEADS)
    proj1, gates = _proj(h, mix_norm[1], w_in1[:, :n_main1].astype(BF16), _pad_cols(w_gate).astype(BF16),
                         jnp.zeros((1, LANES), F32), cos, sin, tiles1, "sigmoid")
    n_cmp = (seq - NSA_CMP_LEN) // NSA_CMP_STRIDE + 1
    kc_cmp = _compress(proj1, qw // NSA_WIDTH, nsa_pos_k[0], nsa_w1_k[0], nsa_b1_k[0],
                       jnp.tile(nsa_w2_k[0], (1, NSA_HPG)).astype(BF16), batch, seq)
    vc_cmp = _compress(proj1, qw // NSA_WIDTH + 1, nsa_pos_v[0], nsa_w1_v[0], nsa_b1_v[0],
                       jnp.tile(nsa_w2_v[0], (1, NSA_HPG)).astype(BF16), batch, seq)
    o_nsa = _nsa_attention(proj1, gates, kc_cmp, vc_cmp, batch, seq, n_cmp)
    h = _outproj(h, [o_nsa], nsa_w_out[0].astype(BF16))
    cw, routed, rank_col, rank_row, routed_row = _router(
        h, ffn_norm[1], _pad_cols(moe_w_router[0]).astype(BF16),
        _pad_cols(moe_b_router[0].reshape(1, N_EXPERTS)).astype(F32))
    n_chunks = t // MOE_CHUNK
    counts = rank_col.reshape(n_chunks, MOE_CHUNK, LANES)[:, -1, :N_EXPERTS].astype(jnp.int32).reshape(-1)
    h = _moe_ffn(h, ffn_norm[1], cw, routed, rank_col, rank_row, routed_row, counts,
                 moe_w_gate[0].astype(BF16), moe_w_up[0].astype(BF16), moe_w_down[0].astype(BF16))
    h = _ple(h, ple_norm[1], p[1].reshape(t, -1), ple_gate_w[1].astype(BF16), ple_proj_w[1].astype(BF16),
             final_norm, final=True)
    return h.reshape(batch, seq, D_MODEL)
```

```python
import functools

import jax
import jax.numpy as jnp
from jax import lax
from jax.experimental import pallas as pl
from jax.experimental.pallas import tpu as pltpu

F32 = jnp.float32
BF16 = jnp.bfloat16

D_MODEL = 1024
HEAD_DIM = 64
LANES = 128
FOX_HEADS = 8
DIL_HEADS = 8
FOX_WIDTH = FOX_HEADS * HEAD_DIM
DIL_WIDTH = DIL_HEADS * HEAD_DIM
DIL_PATTERNS = ((128, 1), (512, 4), (2048, 16))
Q_BLOCK = 128
NSA_HEADS = 16
NSA_GROUPS = 4
NSA_HPG = NSA_HEADS // NSA_GROUPS
NSA_KV = NSA_GROUPS * HEAD_DIM
NSA_CMP_LEN = 32
NSA_CMP_STRIDE = 16
NSA_SEL_LEN = 64
NSA_TOP_N = 8
NSA_WINDOW = 512
N_EXPERTS = 8
ROPE_THETA = 10000.0
RMS_EPS = 1e-6
NEG_INF = -1e30
SCALE = HEAD_DIM ** -0.5
LOG2E = 1.4426950408889634
LN2 = 0.6931471805599453
SCALE2 = SCALE * LOG2E
ONES_ROWS = 16

VMEM_LIMIT_BYTES = 52 * 1024 * 1024


def _params(*sem):
    return pltpu.CompilerParams(dimension_semantics=sem, vmem_limit_bytes=VMEM_LIMIT_BYTES)


def _rms(x, g):
    return x * lax.rsqrt(jnp.mean(x * x, axis=-1, keepdims=True) + RMS_EPS) * g


def _dot(a, b):
    return jnp.dot(a, b, preferred_element_type=F32)


def _dot_nt(a, b):
    return lax.dot_general(a, b, (((1,), (1,)), ((), ())), preferred_element_type=F32)


def _split3(x):
    hi = x.astype(BF16)
    r = x - hi.astype(F32)
    mid = r.astype(BF16)
    lo = (r - mid.astype(F32)).astype(BF16)
    return hi, mid, lo


def _lane_col(x, idx):
    lane = lax.broadcasted_iota(jnp.int32, x.shape, 1)
    return jnp.sum(jnp.where(lane == idx, x, 0.0), axis=1, keepdims=True)


def _rope_table_kernel(pos_ref, inv_ref, cos_ref, sin_ref):
    ang = pos_ref[...] * inv_ref[...]
    lane = lax.broadcasted_iota(jnp.int32, ang.shape, 1)
    sign = jnp.where((lane & (HEAD_DIM - 1)) < HEAD_DIM // 2, -1.0, 1.0)
    cos_ref[...] = jnp.cos(ang)
    sin_ref[...] = jnp.sin(ang) * sign


def _rope_tables(positions):
    t = positions.size
    half = HEAD_DIM // 2
    inv_freq = ROPE_THETA ** (-jnp.arange(half, dtype=F32) / half)
    inv = jnp.tile(inv_freq, LANES // half).reshape(1, LANES)
    pos = jnp.broadcast_to(positions.astype(F32).reshape(t, 1), (t, LANES))
    tm = 1024
    return pl.pallas_call(
        _rope_table_kernel,
        grid=(t // tm,),
        in_specs=[pl.BlockSpec((tm, LANES), lambda i: (i, 0)),
                  pl.BlockSpec((1, LANES), lambda i: (0, 0))],
        out_specs=[pl.BlockSpec((tm, LANES), lambda i: (i, 0))] * 2,
        out_shape=[jax.ShapeDtypeStruct((t, LANES), F32)] * 2,
        compiler_params=_params("parallel"),
        name="rope_tables",
    )(pos, inv)


def _proj_kernel(h_ref, g_ref, w_ref, wa_ref, ba_ref, cos_ref, sin_ref, o_ref, oa_ref, *, tiles, aux_act):
    xn = _rms(h_ref[...], g_ref[...]).astype(BF16)
    for c0, width, rope in tiles:
        acc = _dot(xn, w_ref[:, c0:c0 + width])
        if rope:
            reps = width // LANES
            cos = jnp.tile(cos_ref[...], (1, reps))
            sin = jnp.tile(sin_ref[...], (1, reps))
            lane = lax.broadcasted_iota(jnp.int32, acc.shape, 1)
            first_half = (lane & (HEAD_DIM - 1)) < HEAD_DIM // 2
            partner = jnp.where(first_half,
                                pltpu.roll(acc, width - HEAD_DIM // 2, 1),
                                pltpu.roll(acc, HEAD_DIM // 2, 1))
            acc = acc * cos + partner * sin
        o_ref[:, c0:c0 + width] = acc
    aux = _dot(xn, wa_ref[...]) + ba_ref[...]
    if aux_act == "log_sigmoid":
        oa_ref[...] = jnp.minimum(aux, 0.0) - jnp.log1p(jnp.exp(-jnp.abs(aux)))
    else:
        oa_ref[...] = jax.nn.sigmoid(aux)


def _proj(h, gain, w, w_aux, b_aux, cos, sin, tiles, aux_act, tm=512):
    t, n = h.shape[0], w.shape[1]
    return pl.pallas_call(
        functools.partial(_proj_kernel, tiles=tiles, aux_act=aux_act),
        grid=(t // tm,),
        in_specs=[pl.BlockSpec((tm, D_MODEL), lambda i: (i, 0)),
                  pl.BlockSpec((1, D_MODEL), lambda i: (0, 0)),
                  pl.BlockSpec((D_MODEL, n), lambda i: (0, 0)),
                  pl.BlockSpec((D_MODEL, LANES), lambda i: (0, 0)),
                  pl.BlockSpec((1, LANES), lambda i: (0, 0)),
                  pl.BlockSpec((tm, LANES), lambda i: (i, 0)),
                  pl.BlockSpec((tm, LANES), lambda i: (i, 0))],
        out_specs=[pl.BlockSpec((tm, n), lambda i: (i, 0)),
                   pl.BlockSpec((tm, LANES), lambda i: (i, 0))],
        out_shape=[jax.ShapeDtypeStruct((t, n), F32), jax.ShapeDtypeStruct((t, LANES), F32)],
        compiler_params=_params("parallel"),
        name="in_proj",
    )(h, gain.reshape(1, D_MODEL), w, w_aux, b_aux, cos, sin)


CUM_BLOCK = 512


def _cumsum_kernel(x_ref, ccol_ref, crow_ref, carry_ref):
    j = pl.program_id(1)

    @pl.when(j == 0)
    def _():
        carry_ref[...] = jnp.zeros_like(carry_ref)

    x = x_ref[...]
    r = lax.broadcasted_iota(jnp.int32, (CUM_BLOCK, CUM_BLOCK), 0)
    c = lax.broadcasted_iota(jnp.int32, (CUM_BLOCK, CUM_BLOCK), 1)
    tri = jnp.where(r >= c, 1.0, 0.0).astype(BF16)
    hi, mid, lo = _split3(x)
    cum = _dot(tri, hi) + _dot(tri, mid) + _dot(tri, lo) + carry_ref[0:1, :]
    ccol_ref[...] = cum
    crow_ref[0] = cum.T
    carry_ref[...] = jnp.broadcast_to(cum[CUM_BLOCK - 1:CUM_BLOCK, :], carry_ref.shape)


def _cumsum(x, batch, seq):
    nb = seq // CUM_BLOCK
    return pl.pallas_call(
        _cumsum_kernel,
        grid=(batch, nb),
        in_specs=[pl.BlockSpec((CUM_BLOCK, LANES), lambda b, j: (b * nb + j, 0))],
        out_specs=[pl.BlockSpec((CUM_BLOCK, LANES), lambda b, j: (b * nb + j, 0)),
                   pl.BlockSpec((1, LANES, CUM_BLOCK), lambda b, j: (b, 0, j))],
        out_shape=[jax.ShapeDtypeStruct((batch * seq, LANES), F32),
                   jax.ShapeDtypeStruct((batch, LANES, seq), F32)],
        scratch_shapes=[pltpu.VMEM((8, LANES), F32)],
        compiler_params=_params("parallel", "arbitrary"),
        name="token_cumsum",
    )(x)


def _fox_kernel(q_ref, k_ref, v_ref, ccol_ref, crow_ref, o_ref, kb_ref, vt_ref, ck_ref, *, tq, seq):
    pair = pl.program_id(1)
    i = pl.program_id(2)
    tk = tq
    h0 = 2 * pair

    @pl.when(i == 0)
    def _():
        def fill(ci, carry):
            rs = pl.ds(pl.multiple_of(ci * tk, tk), tk)
            kb_ref[rs, :] = k_ref[rs, :].astype(BF16)
            vt_ref[0:LANES, rs] = v_ref[rs, :].T.astype(BF16)
            vt_ref[LANES:, rs] = jnp.ones((ONES_ROWS, tk), BF16)
            cc = ccol_ref[rs, :] * LOG2E
            ck_ref[0, rs, :] = jnp.broadcast_to(_lane_col(cc, h0), (tk, LANES))
            ck_ref[1, rs, :] = jnp.broadcast_to(_lane_col(cc, h0 + 1), (tk, LANES))
            return carry

        lax.fori_loop(0, seq // tk, fill, 0)

    t0 = pl.multiple_of(i * tq, tq)
    low = lax.broadcasted_iota(jnp.int32, (tq, LANES), 1) < HEAD_DIM
    q = q_ref[...] * SCALE2
    qs = jnp.concatenate([jnp.where(low, q, 0.0), jnp.where(low, 0.0, q)], axis=0).astype(BF16)
    cq = jnp.concatenate([crow_ref[0, pl.ds(h0, 1), pl.ds(t0, tq)],
                          crow_ref[0, pl.ds(h0 + 1, 1), pl.ds(t0, tq)]], axis=1) * LOG2E
    reps = tq // LANES

    def scores(k0, nk):
        ck = jnp.concatenate([ck_ref[0, k0:k0 + nk, :]] * reps + [ck_ref[1, k0:k0 + nk, :]] * reps, axis=1)
        return _dot_nt(kb_ref[k0:k0 + nk, :], qs) + cq - ck

    krow = lax.broadcasted_iota(jnp.int32, (tk, tq), 0)
    qlane = lax.broadcasted_iota(jnp.int32, (tk, tq), 1)
    causal = jnp.where(krow <= qlane, 0.0, NEG_INF)

    def branch(n):
        def tile_scores(t):
            s = scores(t * tk, tk)
            return s + jnp.concatenate([causal, causal], axis=1) if t == n - 1 else s

        parts = []
        pending = tile_scores(0)
        for t in range(n):
            upcoming = tile_scores(t + 1) if t + 1 < n else None
            m_t = jnp.max(pending, axis=0, keepdims=True)
            parts.append((m_t, _dot(vt_ref[:, t * tk:(t + 1) * tk], jnp.exp2(pending - m_t).astype(BF16))))
            pending = upcoming
        m = functools.reduce(jnp.maximum, [m_t for m_t, _ in parts])
        acc = sum(a * jnp.exp2(m_t - m) for m_t, a in parts)
        out = acc[0:LANES] / acc[LANES:LANES + 1]
        o_ref[...] = jnp.concatenate([out[0:HEAD_DIM, 0:tq], out[HEAD_DIM:, tq:]], axis=0).T.astype(o_ref.dtype)

    for n in range(1, seq // tq + 1):
        pl.when(i == n - 1)(functools.partial(branch, n))


def _fox_attention(proj, ccol, crow, batch, seq, tq=512):
    nq = seq // tq
    npair = FOX_WIDTH // LANES
    return pl.pallas_call(
        functools.partial(_fox_kernel, tq=tq, seq=seq),
        grid=(batch, npair, nq),
        in_specs=[pl.BlockSpec((tq, LANES), lambda b, p, i: (b * nq + i, p)),
                  pl.BlockSpec((seq, LANES), lambda b, p, i: (b, npair + p)),
                  pl.BlockSpec((seq, LANES), lambda b, p, i: (b, 2 * npair + p)),
                  pl.BlockSpec((seq, LANES), lambda b, p, i: (b, 0)),
                  pl.BlockSpec((1, 8, seq), lambda b, p, i: (b, 0, 0))],
        out_specs=pl.BlockSpec((tq, LANES), lambda b, p, i: (b * nq + i, p)),
        out_shape=jax.ShapeDtypeStruct((batch * seq, FOX_WIDTH), BF16),
        scratch_shapes=[pltpu.VMEM((seq, LANES), BF16), pltpu.VMEM((LANES + ONES_ROWS, seq), BF16),
                        pltpu.VMEM((2, seq, LANES), F32)],
        compiler_params=_params("parallel", "parallel", "arbitrary"),
        name="fox_attention",
    )(proj, proj, proj, ccol, crow)


DIL_GROUP = 4


def _dil_kernel(q_ref, k_ref, v_ref, o_ref, os_ref, ls_ref, *, seq):
    low = lax.broadcasted_iota(jnp.int32, (Q_BLOCK, LANES), 1) < HEAD_DIM
    kr = lax.broadcasted_iota(jnp.int32, (2 * Q_BLOCK, Q_BLOCK), 0)
    qc = lax.broadcasted_iota(jnp.int32, (2 * Q_BLOCK, Q_BLOCK), 1)
    dist = qc + Q_BLOCK - kr
    def score_group(pi, units):
        window, dil = DIL_PATTERNS[pi]
        span = window // dil
        nb = (seq // dil) // Q_BLOCK
        band_bias = jnp.where((dist >= 0) & (dist <= span), 0.0, NEG_INF)
        scored, dests = [], []

        def rows(start):
            return pl.ds(start, Q_BLOCK, stride=dil) if dil > 1 else pl.ds(start, Q_BLOCK)

        for u in units:
            r, blk = divmod(u, nb)
            cur = r + blk * (Q_BLOCK * dil)
            q = q_ref[rows(cur), :] * SCALE2
            qs = jnp.concatenate([jnp.where(low, q, 0.0), jnp.where(low, 0.0, q)], axis=0).astype(BF16)
            if blk == 0:
                kk = k_ref[rows(cur), :].astype(BF16)
                vt = v_ref[rows(cur), :].T.astype(BF16)
                bias = band_bias[Q_BLOCK:]
            else:
                prev = cur - Q_BLOCK * dil
                kk = jnp.concatenate([k_ref[rows(prev), :], k_ref[rows(cur), :]], axis=0).astype(BF16)
                vt = jnp.concatenate([v_ref[rows(prev), :].T, v_ref[rows(cur), :].T], axis=1).astype(BF16)
                bias = band_bias
            vt = jnp.concatenate([vt, jnp.ones((ONES_ROWS, vt.shape[1]), BF16)], axis=0)
            s = _dot_nt(kk, qs) + jnp.concatenate([bias, bias], axis=1)
            scored.append((s, vt))
            dests.append(rows(cur))
        return scored, dests

    def softmax(s):
        m = jnp.max(s, axis=0, keepdims=True)
        return jnp.exp2(s - m).astype(BF16), m

    def finish(vt, p, m):
        ot = _dot(vt, p)
        den = ot[LANES:LANES + 1]
        ot = ot[0:LANES] * (1.0 / den)
        lse = m * LN2 + jnp.log(den)
        lse_t = jnp.concatenate([jnp.broadcast_to(lse[:, 0:Q_BLOCK], (HEAD_DIM, Q_BLOCK)),
                                 jnp.broadcast_to(lse[:, Q_BLOCK:], (HEAD_DIM, Q_BLOCK))], axis=0)
        return jnp.concatenate([ot[0:HEAD_DIM, 0:Q_BLOCK], ot[HEAD_DIM:, Q_BLOCK:]], axis=0).T, lse_t.T

    def finish_group(pi, scored, dests):
        probs = [softmax(s) for s, _ in scored]
        outs = [finish(vt, p, m) for (_, vt), (p, m) in zip(scored, probs)]
        for dest, (o, l) in zip(dests, outs):
            os_ref[pi, dest, :] = o
            ls_ref[pi, dest, :] = l

    groups = []
    for pi, (window, dil) in enumerate(DIL_PATTERNS):
        n_units = dil * ((seq // dil) // Q_BLOCK)
        groups += [(pi, range(u0, min(u0 + DIL_GROUP, n_units))) for u0 in range(0, n_units, DIL_GROUP)]
    pending = score_group(*groups[0])
    for gi, (pi, _) in enumerate(groups):
        upcoming = score_group(*groups[gi + 1]) if gi + 1 < len(groups) else None
        finish_group(pi, *pending)
        pending = upcoming

    chunk = 256

    def combine(ci, carry):
        rs = pl.ds(pl.multiple_of(ci * chunk, chunk), chunk)
        l0, l1, l2 = ls_ref[0, rs, :], ls_ref[1, rs, :], ls_ref[2, rs, :]
        m = jnp.maximum(jnp.maximum(l0, l1), l2)
        e0, e1, e2 = jnp.exp(l0 - m), jnp.exp(l1 - m), jnp.exp(l2 - m)
        tot = e0 + e1 + e2
        o_ref[rs, :] = ((e0 / tot) * os_ref[0, rs, :] + (e1 / tot) * os_ref[1, rs, :]
                        + (e2 / tot) * os_ref[2, rs, :]).astype(o_ref.dtype)
        return carry

    lax.fori_loop(0, seq // chunk, combine, 0)


def _dilated_attention(proj, batch, seq):
    npair = DIL_WIDTH // LANES
    base = 3 * FOX_WIDTH // LANES
    return pl.pallas_call(
        functools.partial(_dil_kernel, seq=seq),
        grid=(batch, npair),
        in_specs=[pl.BlockSpec((seq, LANES), lambda b, p: (b, base + p)),
                  pl.BlockSpec((seq, LANES), lambda b, p: (b, base + npair + p)),
                  pl.BlockSpec((seq, LANES), lambda b, p: (b, base + 2 * npair + p))],
        out_specs=pl.BlockSpec((seq, LANES), lambda b, p: (b, p)),
        out_shape=jax.ShapeDtypeStruct((batch * seq, DIL_WIDTH), BF16),
        scratch_shapes=[pltpu.VMEM((3, seq, LANES), F32), pltpu.VMEM((3, seq, LANES), F32)],
        compiler_params=_params("parallel", "parallel"),
        name="dilated_attention",
    )(proj, proj, proj)


def _outproj_kernel(*refs, n_in, tn):
    h_ref = refs[0]
    a_refs = refs[1:1 + n_in]
    w_ref = refs[1 + n_in]
    o_ref = refs[2 + n_in]
    acts = [a[...].astype(BF16) for a in a_refs]
    for c0 in range(0, D_MODEL, tn):
        acc = h_ref[:, c0:c0 + tn]
        k0 = 0
        for a in acts:
            acc = acc + _dot(a, w_ref[k0:k0 + a.shape[1], c0:c0 + tn])
            k0 += a.shape[1]
        o_ref[:, c0:c0 + tn] = acc


def _outproj(h, acts, w, tm=1024, tn=512):
    t = h.shape[0]
    return pl.pallas_call(
        functools.partial(_outproj_kernel, n_in=len(acts), tn=tn),
        grid=(t // tm,),
        in_specs=([pl.BlockSpec((tm, D_MODEL), lambda i: (i, 0))]
                  + [pl.BlockSpec((tm, a.shape[1]), lambda i: (i, 0)) for a in acts]
                  + [pl.BlockSpec((D_MODEL, D_MODEL), lambda i: (0, 0))]),
        out_specs=pl.BlockSpec((tm, D_MODEL), lambda i: (i, 0)),
        out_shape=jax.ShapeDtypeStruct((t, D_MODEL), F32),
        compiler_params=_params("parallel"),
        name="out_proj",
    )(h, *acts, w)


def _swiglu_tile(x, wg, wu, wd):
    gate = _dot(x, wg)
    up = _dot(x, wu)
    return _dot((gate * jax.nn.sigmoid(gate) * up).astype(BF16), wd)


def _ffn_kernel(h_ref, g_ref, wg_ref, wu_ref, wd_ref, o_ref, xn_ref, acc_ref, *, n_f):
    f = pl.program_id(1)

    @pl.when(f == 0)
    def _():
        xn_ref[...] = _rms(h_ref[...], g_ref[...]).astype(BF16)
        acc_ref[...] = jnp.zeros_like(acc_ref)

    acc_ref[...] += _swiglu_tile(xn_ref[...], wg_ref[...], wu_ref[...], wd_ref[...])

    @pl.when(f == n_f - 1)
    def _():
        o_ref[...] = h_ref[...] + acc_ref[...]


def _ffn(h, gain, wg, wu, wd, tm=512, tf=1792):
    t, dff = h.shape[0], wg.shape[1]
    n_f = dff // tf
    return pl.pallas_call(
        functools.partial(_ffn_kernel, n_f=n_f),
        grid=(t // tm, n_f),
        in_specs=[pl.BlockSpec((tm, D_MODEL), lambda i, f: (i, 0)),
                  pl.BlockSpec((1, D_MODEL), lambda i, f: (0, 0)),
                  pl.BlockSpec((D_MODEL, tf), lambda i, f: (0, f)),
                  pl.BlockSpec((D_MODEL, tf), lambda i, f: (0, f)),
                  pl.BlockSpec((tf, D_MODEL), lambda i, f: (f, 0))],
        out_specs=pl.BlockSpec((tm, D_MODEL), lambda i, f: (i, 0)),
        out_shape=jax.ShapeDtypeStruct((t, D_MODEL), F32),
        scratch_shapes=[pltpu.VMEM((tm, D_MODEL), BF16), pltpu.VMEM((tm, D_MODEL), F32)],
        compiler_params=_params("parallel", "arbitrary"),
        name="dense_swiglu",
    )(h, gain.reshape(1, D_MODEL), wg, wu, wd)


MOE_CHUNK = 1024
MOE_ROWS = 128
MOE_SCATTER = 256


def _moe_kernel(cnt_ref, h_ref, g_ref, cw_ref, m_ref, rcol_ref, rrow_ref, mrow_ref, wg_ref, wu_ref, wd_ref,
                o_ref, xn_ref, xe_ref, ye_ref, *, n_f):
    c = pl.program_id(0)
    e = pl.program_id(1)
    f = pl.program_id(2)
    chunk = MOE_CHUNK
    n = cnt_ref[c * N_EXPERTS + e]
    n_scatter = (n + MOE_SCATTER - 1) // MOE_SCATTER
    n_tiles = (n + MOE_ROWS - 1) // MOE_ROWS

    @pl.when((e == 0) & (f == 0))
    def _():
        h = h_ref[...]
        xn_ref[...] = _rms(h, g_ref[...]).astype(BF16)
        o_ref[...] = h

    @pl.when(f == 0)
    def _():
        rank = rrow_ref[pl.ds(e, 1), :] * mrow_ref[pl.ds(e, 1), :]
        slot = lax.broadcasted_iota(jnp.int32, (MOE_ROWS, chunk), 0) + 1

        def gather(i, carry):
            rows = pl.ds(pl.multiple_of(i * MOE_ROWS, MOE_ROWS), MOE_ROWS)
            onehot = jnp.where(rank == (slot + i * MOE_ROWS).astype(F32), 1.0, 0.0).astype(BF16)
            xe_ref[rows, :] = _dot(onehot, xn_ref[...]).astype(BF16)
            ye_ref[rows, :] = jnp.zeros((MOE_ROWS, D_MODEL), F32)
            return carry

        def clear(i, carry):
            rows = pl.ds(pl.multiple_of(i * MOE_ROWS, MOE_ROWS), MOE_ROWS)
            ye_ref[rows, :] = jnp.zeros((MOE_ROWS, D_MODEL), F32)
            return carry

        lax.fori_loop(0, n_tiles, gather, 0)
        lax.fori_loop(n_tiles, n_scatter * (MOE_SCATTER // MOE_ROWS), clear, 0)

    def tile(i, carry):
        rows = pl.ds(pl.multiple_of(i * MOE_ROWS, MOE_ROWS), MOE_ROWS)
        ye_ref[rows, :] += _swiglu_tile(xe_ref[rows, :], wg_ref[...], wu_ref[...], wd_ref[...])
        return carry

    lax.fori_loop(0, n_tiles, tile, 0)

    @pl.when(f == n_f - 1)
    def _():
        rank = _lane_col(rcol_ref[...] * m_ref[...], e)
        weight = _lane_col(cw_ref[...], e)
        slot = lax.broadcasted_iota(jnp.int32, (chunk, MOE_SCATTER), 1) + 1

        def scatter(i, carry):
            rows = pl.ds(pl.multiple_of(i * MOE_SCATTER, MOE_SCATTER), MOE_SCATTER)
            onehot = jnp.where(rank == (slot + i * MOE_SCATTER).astype(F32), 1.0, 0.0).astype(BF16)
            o_ref[...] += weight * _dot(onehot, ye_ref[rows, :].astype(BF16))
            return carry

        lax.fori_loop(0, n_scatter, scatter, 0)


def _moe_ffn(h, gain, cw, mask, rcol, rrow, mrow, counts, wg, wu, wd, tf=1792):
    t, dff = h.shape[0], wg.shape[2]
    n_f = dff // tf
    chunk = MOE_CHUNK
    tok = lambda width: pl.BlockSpec((chunk, width), lambda c, e, f, cnt: (c, 0))
    lane_major = pl.BlockSpec((None, N_EXPERTS, chunk), lambda c, e, f, cnt: (c, 0, 0))
    grid_spec = pltpu.PrefetchScalarGridSpec(
        num_scalar_prefetch=1,
        grid=(t // chunk, N_EXPERTS, n_f),
        in_specs=[tok(D_MODEL),
                  pl.BlockSpec((1, D_MODEL), lambda c, e, f, cnt: (0, 0)),
                  tok(LANES), tok(LANES), tok(LANES), lane_major, lane_major,
                  pl.BlockSpec((None, D_MODEL, tf), lambda c, e, f, cnt: (e, 0, f)),
                  pl.BlockSpec((None, D_MODEL, tf), lambda c, e, f, cnt: (e, 0, f)),
                  pl.BlockSpec((None, tf, D_MODEL), lambda c, e, f, cnt: (e, f, 0))],
        out_specs=tok(D_MODEL),
        scratch_shapes=[pltpu.VMEM((chunk, D_MODEL), BF16), pltpu.VMEM((chunk, D_MODEL), BF16),
                        pltpu.VMEM((chunk, D_MODEL), F32)],
    )
    return pl.pallas_call(
        functools.partial(_moe_kernel, n_f=n_f),
        grid_spec=grid_spec,
        out_shape=jax.ShapeDtypeStruct((t, D_MODEL), F32),
        compiler_params=_params("parallel", "arbitrary", "arbitrary"),
        name="moe_swiglu",
    )(counts, h, gain.reshape(1, D_MODEL), cw, mask, rcol, rrow, mrow, wg, wu, wd)


def _router_kernel(h_ref, g_ref, w_ref, b_ref, cw_ref, m_ref, rcol_ref, rrow_ref, mrow_ref):
    xn = _rms(h_ref[...], g_ref[...]).astype(BF16)
    logits = _dot(xn, w_ref[...]) + b_ref[...]
    lane = lax.broadcasted_iota(jnp.int32, logits.shape, 1).astype(F32)
    logits = jnp.where(lane < N_EXPERTS, logits, -jnp.inf)
    m1 = jnp.max(logits, axis=1, keepdims=True)
    i1 = jnp.min(jnp.where(logits == m1, lane, float(LANES)), axis=1, keepdims=True)
    rest = jnp.where(lane == i1, -jnp.inf, logits)
    m2 = jnp.max(rest, axis=1, keepdims=True)
    i2 = jnp.min(jnp.where(rest == m2, lane, float(LANES)), axis=1, keepdims=True)
    e2 = jnp.exp(m2 - m1)
    w1 = 1.0 / (1.0 + e2)
    w2 = e2 / (1.0 + e2)
    cw_ref[...] = jnp.where(lane == i1, w1, jnp.where(lane == i2, w2, 0.0))
    mask = jnp.where((lane == i1) | (lane == i2), 1.0, 0.0)
    m_ref[...] = mask
    r = lax.broadcasted_iota(jnp.int32, (CUM_BLOCK, CUM_BLOCK), 0)
    c = lax.broadcasted_iota(jnp.int32, (CUM_BLOCK, CUM_BLOCK), 1)
    tri = jnp.where(r >= c, 1.0, 0.0).astype(BF16)
    carry = jnp.zeros((1, LANES), F32)
    for r0 in range(0, mask.shape[0], CUM_BLOCK):
        blk = mask[r0:r0 + CUM_BLOCK]
        cum = _dot(tri, blk.astype(BF16)) + carry
        rcol_ref[r0:r0 + CUM_BLOCK, :] = cum
        rrow_ref[:, r0:r0 + CUM_BLOCK] = cum.T
        mrow_ref[:, r0:r0 + CUM_BLOCK] = blk.T
        carry = cum[CUM_BLOCK - 1:CUM_BLOCK, :]


def _router(h, gain, w, b):
    t = h.shape[0]
    tm = MOE_CHUNK
    token_major = pl.BlockSpec((tm, LANES), lambda i: (i, 0))
    lane_major = pl.BlockSpec((None, LANES, tm), lambda i: (i, 0, 0))
    return pl.pallas_call(
        _router_kernel,
        grid=(t // tm,),
        in_specs=[pl.BlockSpec((tm, D_MODEL), lambda i: (i, 0)),
                  pl.BlockSpec((1, D_MODEL), lambda i: (0, 0)),
                  pl.BlockSpec((D_MODEL, LANES), lambda i: (0, 0)),
                  pl.BlockSpec((1, LANES), lambda i: (0, 0))],
        out_specs=[token_major, token_major, token_major, lane_major, lane_major],
        out_shape=[jax.ShapeDtypeStruct((t, LANES), F32)] * 3
                  + [jax.ShapeDtypeStruct((t // tm, LANES, tm), F32)] * 2,
        compiler_params=_params("parallel"),
        name="moe_router",
    )(h, gain.reshape(1, D_MODEL), w, b)


def _ple_kernel(h_ref, g_ref, p_ref, wg_ref, wp_ref, fg_ref, o_ref, *, final, tn):
    h = h_ref[...]
    xn = _rms(h, g_ref[...]).astype(BF16)
    pe = p_ref[...].astype(BF16)
    outs = []
    for c0 in range(0, D_MODEL, tn):
        gate = jax.nn.sigmoid(_dot(xn, wg_ref[:, c0:c0 + tn]))
        outs.append(h[:, c0:c0 + tn] + gate * _dot(pe, wp_ref[:, c0:c0 + tn]))
    new = jnp.concatenate(outs, axis=1)
    o_ref[...] = _rms(new, fg_ref[...]) if final else new


def _ple(h, gain, p, wg, wp, final_gain, final, tm=1024, tn=512):
    t, pd = p.shape
    return pl.pallas_call(
        functools.partial(_ple_kernel, final=final, tn=tn),
        grid=(t // tm,),
        in_specs=[pl.BlockSpec((tm, D_MODEL), lambda i: (i, 0)),
                  pl.BlockSpec((1, D_MODEL), lambda i: (0, 0)),
                  pl.BlockSpec((tm, pd), lambda i: (i, 0)),
                  pl.BlockSpec((D_MODEL, D_MODEL), lambda i: (0, 0)),
                  pl.BlockSpec((pd, D_MODEL), lambda i: (0, 0)),
                  pl.BlockSpec((1, D_MODEL), lambda i: (0, 0))],
        out_specs=pl.BlockSpec((tm, D_MODEL), lambda i: (i, 0)),
        out_shape=jax.ShapeDtypeStruct((t, D_MODEL), F32),
        compiler_params=_params("parallel"),
        name="ple",
    )(h, gain.reshape(1, D_MODEL), p, wg, wp, final_gain.reshape(1, D_MODEL))


def _compress_kernel(x01_ref, x23_ref, pos_ref, w1_ref, b1_ref, w2_ref, o_ref, *, blocks):
    stride = NSA_CMP_STRIDE
    hidden = b1_ref.shape[1]
    low = lax.broadcasted_iota(jnp.int32, (blocks, LANES), 1) < HEAD_DIM
    first = [jnp.zeros((blocks, hidden), F32) for _ in range(NSA_GROUPS)]
    second = [jnp.zeros((blocks, hidden), F32) for _ in range(NSA_GROUPS)]
    for j in range(stride):
        for half, x_ref in enumerate((x01_ref, x23_ref)):
            xs = x_ref[pl.ds(j, blocks, stride=stride), :]
            xa = xs + pos_ref[j:j + 1, :]
            xb = xs + pos_ref[stride + j:stride + j + 1, :]
            for sub in range(2):
                g = 2 * half + sub
                keep = low if sub == 0 else jnp.logical_not(low)
                first[g] = first[g] + _dot(jnp.where(keep, xa, 0.0).astype(BF16), w1_ref[j])
                second[g] = second[g] + _dot(jnp.where(keep, xb, 0.0).astype(BF16), w1_ref[stride + j])
    row = lax.broadcasted_iota(jnp.int32, (blocks, NSA_WIDTH), 0)
    for g in range(NSA_GROUPS):
        hid = first[g] + pltpu.roll(second[g], blocks - 1, 0) + b1_ref[...]
        out = _dot(jax.nn.gelu(hid, approximate=True).astype(BF16), w2_ref[...])
        o_ref[g] = jnp.where(row == blocks - 1, 0.0, out)


def _compress(proj, col_block, pos, w1, b1, w2, batch, seq):
    blocks = seq // NSA_CMP_STRIDE
    hidden = w1.shape[1]
    w1_rep = jnp.tile(w1.reshape(NSA_CMP_LEN, HEAD_DIM, hidden), (1, 2, 1))
    pos_rep = jnp.tile(pos, (1, 2))
    halves = NSA_WIDTH // LANES
    return pl.pallas_call(
        functools.partial(_compress_kernel, blocks=blocks),
        grid=(batch,),
        in_specs=[pl.BlockSpec((seq, LANES), lambda b: (b, halves * col_block)),
                  pl.BlockSpec((seq, LANES), lambda b: (b, halves * col_block + 1)),
                  pl.BlockSpec((NSA_CMP_LEN, LANES), lambda b: (0, 0)),
                  pl.BlockSpec((NSA_CMP_LEN, LANES, hidden), lambda b: (0, 0, 0)),
                  pl.BlockSpec((1, hidden), lambda b: (0, 0)),
                  pl.BlockSpec((hidden, NSA_WIDTH), lambda b: (0, 0))],
        out_specs=pl.BlockSpec((NSA_GROUPS, blocks, NSA_WIDTH), lambda b: (b, 0, 0)),
        out_shape=jax.ShapeDtypeStruct((batch * NSA_GROUPS, blocks, NSA_WIDTH), F32),
        compiler_params=_params("parallel"),
        name="nsa_compress",
    )(proj, proj, pos_rep, w1_rep.astype(BF16), b1.reshape(1, hidden), w2)


NSA_TQ = 256
NSA_TK = 512
NSA_SUB = 512
NSA_WIDTH = NSA_HPG * HEAD_DIM
LOG_HEAD_DIM = HEAD_DIM.bit_length() - 1
LOG_SEL_LEN = NSA_SEL_LEN.bit_length() - 1


def _nsa_kernel(q_ref, kc_ref, vc_ref, ks_ref, vs_ref, kw_ref, vw_ref, gate_ref, o_ref,
                ksb, kwb, vst_all, vwt_all, vct, gt_ref, osel_ref, *, seq, n_cmp):
    g = pl.program_id(1)
    i = pl.program_id(2)
    tq, tk = NSA_TQ, NSA_TK
    cols = NSA_HPG * tq
    vrows = HEAD_DIM + ONES_ROWS
    grow = pl.multiple_of(g * vrows, vrows)

    @pl.when((i == 0) & (g == 0))
    def _():
        def fill(ci, carry):
            rs = pl.ds(pl.multiple_of(ci * NSA_WIDTH, NSA_WIDTH), NSA_WIDTH)
            ksb[rs, :] = ks_ref[rs, :].astype(BF16)
            kwb[rs, :] = kw_ref[rs, :].astype(BF16)
            for src, dst in ((vs_ref, vst_all), (vw_ref, vwt_all)):
                vt = src[rs, :].T.astype(BF16)
                for grp in range(NSA_GROUPS):
                    dst[grp * vrows:grp * vrows + HEAD_DIM, rs] = vt[grp * HEAD_DIM:(grp + 1) * HEAD_DIM]
                    dst[grp * vrows + HEAD_DIM:(grp + 1) * vrows, rs] = jnp.ones((ONES_ROWS, NSA_WIDTH), BF16)
            return carry

        lax.fori_loop(0, seq // NSA_WIDTH, fill, 0)

    @pl.when(i == 0)
    def _():
        vct[...] = vc_ref[0].T[0:HEAD_DIM, :].astype(BF16)

    vst = vst_all.at[pl.ds(grow, vrows)]
    vwt = vwt_all.at[pl.ds(grow, vrows)]

    t0 = i * tq
    lane_grp = lax.broadcasted_iota(jnp.int32, (tq, NSA_WIDTH), 1) >> LOG_HEAD_DIM
    q = q_ref[...] * SCALE2
    rolled = [q] + [pltpu.roll(q, s * HEAD_DIM, 1) for s in range(1, NSA_HPG)]
    parts = []
    for j in range(NSA_HPG):
        shift = (g - j) & (NSA_HPG - 1)
        moved = jnp.where(shift == 0, rolled[0],
                          jnp.where(shift == 1, rolled[1], jnp.where(shift == 2, rolled[2], rolled[3])))
        parts.append(jnp.where(lane_grp == g, moved, 0.0))
    qs = jnp.concatenate(parts, axis=0).astype(BF16)

    def heads_sum(x):
        out = x[:, 0:tq]
        for j in range(1, NSA_HPG):
            out = out + x[:, j * tq:(j + 1) * tq]
        return out

    def lanes4(x):
        return jnp.concatenate([x] * NSA_HPG, axis=1)

    wk = NSA_WINDOW + tq
    ws = pl.multiple_of(jnp.maximum(t0 - NSA_WINDOW, 0), tq)
    kpos = ws + lax.broadcasted_iota(jnp.int32, (wk, tq), 0)
    qw = t0 + lax.broadcasted_iota(jnp.int32, (wk, tq), 1)
    bias_w = jnp.where((kpos <= qw) & (kpos > qw - NSA_WINDOW), 0.0, NEG_INF)
    sw = _dot_nt(kwb[pl.ds(ws, wk), :], qs) + lanes4(bias_w)

    nrow = lax.broadcasted_iota(jnp.int32, (LANES, cols), 0)
    tcol = t0 + (lax.broadcasted_iota(jnp.int32, (LANES, cols), 1) & (tq - 1))
    valid_c = (nrow * NSA_CMP_STRIDE + NSA_CMP_LEN - 1 <= tcol) & (nrow < n_cmp)
    sc = jnp.where(valid_c, _dot_nt(kc_ref[0].astype(BF16), qs), NEG_INF)
    mc = jnp.max(sc, axis=0, keepdims=True)
    ec = jnp.where(valid_c, jnp.exp2(sc - mc), 0.0)
    dc = jnp.sum(ec, axis=0, keepdims=True)
    pc = ec / jnp.where(dc > 0.0, dc, 1.0)
    o_cmp = _dot(vct[...], pc.astype(BF16))
    pc_sum = heads_sum(pc)

    n_sel_blocks = seq // NSA_SEL_LEN
    jrow = lax.broadcasted_iota(jnp.int32, (LANES, LANES), 0)
    ncol = lax.broadcasted_iota(jnp.int32, (LANES, LANES), 1)
    overlap = ((ncol * NSA_CMP_STRIDE < (jrow + 1) * NSA_SEL_LEN)
               & (ncol * NSA_CMP_STRIDE + NSA_CMP_LEN > jrow * NSA_SEL_LEN)
               & (ncol < n_cmp) & (jrow < n_sel_blocks))
    overlap = jnp.where(overlap, 1.0, 0.0).astype(BF16)
    hi, mid, lo = _split3(pc_sum)
    imp = (_dot(overlap, hi) + _dot(overlap, mid) + _dot(overlap, lo))[0:n_sel_blocks]
    pw = jnp.exp2(sw - jnp.max(sw, axis=0, keepdims=True))
    o_win = _dot(vwt[:, pl.ds(ws, wk)], pw.astype(BF16))
    o_win = o_win[0:HEAD_DIM] / o_win[HEAD_DIM:HEAD_DIM + 1]
    blk = lax.broadcasted_iota(jnp.int32, (n_sel_blocks, tq), 0)
    cur = (t0 + lax.broadcasted_iota(jnp.int32, (n_sel_blocks, tq), 1)) >> LOG_SEL_LEN
    forced = (blk == 0) | (blk == cur) | (blk == cur - 1)
    imp = jnp.where(blk > cur, -1.0, jnp.where(forced, 1e6, imp))
    beaten = jnp.zeros((n_sel_blocks, tq), jnp.int32)
    for c in range(n_sel_blocks):
        row = imp[c:c + 1, :]
        wins = (row > imp) | ((row == imp) & (blk > c))
        beaten = beaten + jnp.where(wins, 1, 0)
    sel_bias = jnp.where(beaten < NSA_TOP_N, 0.0, NEG_INF)
    sel_bias = jnp.concatenate([sel_bias, jnp.zeros((LANES - n_sel_blocks, tq), F32)], axis=0).astype(BF16)

    n_tiles = (t0 + tq - 1) // tk + 1

    def sel_branch(n):
        sub = NSA_SUB
        erow = lax.broadcasted_iota(jnp.int32, (sub, LANES), 0)
        ecol = lax.broadcasted_iota(jnp.int32, (sub, LANES), 1)
        krow = lax.broadcasted_iota(jnp.int32, (sub, tq), 0)
        qlane = t0 + lax.broadcasted_iota(jnp.int32, (sub, tq), 1)

        def scores(k0):
            expand = jnp.where(((k0 + erow) >> LOG_SEL_LEN) == ecol, 1.0, 0.0).astype(BF16)
            bias = jnp.where(k0 + krow <= qlane, _dot(expand, sel_bias), NEG_INF)
            return _dot_nt(ksb[k0:k0 + sub, :], qs) + lanes4(bias)

        starts = list(range(0, n * tk, sub))
        parts = []
        pending = scores(starts[0])
        for i, k0 in enumerate(starts):
            upcoming = scores(starts[i + 1]) if i + 1 < len(starts) else None
            m_sub = jnp.max(pending, axis=0, keepdims=True)
            parts.append((m_sub, _dot(vst[:, k0:k0 + sub], jnp.exp2(pending - m_sub).astype(BF16))))
            pending = upcoming
        m = functools.reduce(jnp.maximum, [m_sub for m_sub, _ in parts])
        acc = sum(a * jnp.exp2(m_sub - m) for m_sub, a in parts)
        osel_ref[...] = acc[0:HEAD_DIM] / acc[HEAD_DIM:HEAD_DIM + 1]

    for n in range(1, seq // tk + 1):
        pl.when(n_tiles == n)(functools.partial(sel_branch, n))
    o_sel = osel_ref[...]

    gt_ref[...] = gate_ref[...].T
    out = jnp.zeros((HEAD_DIM, cols), F32)
    for c, branch in enumerate((o_cmp, o_sel, o_win)):
        gate = jnp.concatenate([gt_ref[pl.ds(c * NSA_HEADS + g * NSA_HPG + j, 1), :] for j in range(NSA_HPG)],
                               axis=1)
        out = out + gate * branch
    o_ref[...] = jnp.concatenate([out[:, j * tq:(j + 1) * tq] for j in range(NSA_HPG)], axis=0).T.astype(o_ref.dtype)


def _nsa_attention(proj, gates, kc_cmp, vc_cmp, batch, seq, n_cmp):
    nq = seq // NSA_TQ
    ncb = kc_cmp.shape[1]
    qcols = NSA_HEADS * HEAD_DIM // NSA_WIDTH
    kv = lambda c: pl.BlockSpec((seq, NSA_WIDTH), lambda b, g, i, c=c: (b, qcols + c))
    cmp_spec = pl.BlockSpec((1, ncb, NSA_WIDTH), lambda b, g, i: (b * NSA_GROUPS + g, 0, 0))
    return pl.pallas_call(
        functools.partial(_nsa_kernel, seq=seq, n_cmp=n_cmp),
        grid=(batch, NSA_GROUPS, nq),
        in_specs=[pl.BlockSpec((NSA_TQ, NSA_WIDTH), lambda b, g, i: (b * nq + i, g)),
                  cmp_spec, cmp_spec, kv(2), kv(3), kv(4), kv(5),
                  pl.BlockSpec((NSA_TQ, LANES), lambda b, g, i: (b * nq + i, 0))],
        out_specs=pl.BlockSpec((NSA_TQ, NSA_WIDTH), lambda b, g, i: (b * nq + i, g)),
        out_shape=jax.ShapeDtypeStruct((batch * seq, NSA_HEADS * HEAD_DIM), BF16),
        scratch_shapes=[pltpu.VMEM((seq, NSA_WIDTH), BF16), pltpu.VMEM((seq, NSA_WIDTH), BF16),
                        pltpu.VMEM((NSA_GROUPS * (HEAD_DIM + ONES_ROWS), seq), BF16),
                        pltpu.VMEM((NSA_GROUPS * (HEAD_DIM + ONES_ROWS), seq), BF16),
                        pltpu.VMEM((HEAD_DIM, ncb), BF16), pltpu.VMEM((LANES, NSA_TQ), F32),
                        pltpu.VMEM((HEAD_DIM, NSA_HPG * NSA_TQ), F32)],
        compiler_params=_params("parallel", "arbitrary", "arbitrary"),
        name="nsa_attention",
    )(proj, kc_cmp, vc_cmp, proj, proj, proj, proj, gates)


def _pad_cols(w, width=LANES):
    return jnp.pad(w, ((0, 0), (0, width - w.shape[1])))


def kernel(x, p, positions, mix_norm, ffn_norm, ple_norm, ple_gate_w, ple_proj_w, fd_w_in, fd_forget_b, fd_w_out, dense_w_gate, dense_w_up, dense_w_down, nsa_w_in, nsa_pos_k, nsa_w1_k, nsa_b1_k, nsa_w2_k, nsa_pos_v, nsa_w1_v, nsa_b1_v, nsa_w2_v, nsa_w_out, moe_w_router, moe_b_router, moe_w_gate, moe_w_up, moe_w_down, final_norm):
    batch, seq, _ = x.shape
    t = batch * seq
    h = x.reshape(t, D_MODEL)
    cos, sin = _rope_tables(positions)

    n_main = 3 * FOX_WIDTH + 3 * DIL_WIDTH
    w_in = fd_w_in[0]
    tiles0 = tuple((c, 512, c in (3 * FOX_WIDTH, 3 * FOX_WIDTH + DIL_WIDTH)) for c in range(0, n_main, 512))
    proj0, log_f = _proj(h, mix_norm[0], w_in[:, :n_main].astype(BF16),
                         _pad_cols(w_in[:, n_main:]).astype(BF16),
                         _pad_cols(fd_forget_b[0].reshape(1, FOX_HEADS)).astype(F32),
                         cos, sin, tiles0, "log_sigmoid")
    ccol, crow = _cumsum(log_f, batch, seq)
    o_fox = _fox_attention(proj0, ccol, crow, batch, seq)
    o_dil = _dilated_attention(proj0, batch, seq)
    h = _outproj(h, [o_fox, o_dil], fd_w_out[0].astype(BF16))
    h = _ffn(h, ffn_norm[0], dense_w_gate[0].astype(BF16), dense_w_up[0].astype(BF16),
             dense_w_down[0].astype(BF16))
    h = _ple(h, ple_norm[0], p[0].reshape(t, -1), ple_gate_w[0].astype(BF16), ple_proj_w[0].astype(BF16),
             final_norm, final=False)

    qw = NSA_HEADS * HEAD_DIM
    n_main1 = qw + 6 * NSA_KV
    w_in1 = nsa_w_in[0]
    rope_cols = set(range(0, qw, 256)) | {qw, qw + 2 * NSA_KV, qw + 4 * NSA_KV}
    tiles1 = tuple((c, 256, c in rope_cols) for c in range(0, n_main1, 256))
    w_gate = w_in1[:, n_main1:].reshape(D_MODEL, NSA_HEADS, 3).transpose(0, 2, 1).reshape(D_MODEL, 3 * NSA_HEADS)
    proj1, gates = _proj(h, mix_norm[1], w_in1[:, :n_main1].astype(BF16), _pad_cols(w_gate).astype(BF16),
                         jnp.zeros((1, LANES), F32), cos, sin, tiles1, "sigmoid")
    n_cmp = (seq - NSA_CMP_LEN) // NSA_CMP_STRIDE + 1
    kc_cmp = _compress(proj1, qw // NSA_WIDTH, nsa_pos_k[0], nsa_w1_k[0], nsa_b1_k[0],
                       jnp.tile(nsa_w2_k[0], (1, NSA_HPG)).astype(BF16), batch, seq)
    vc_cmp = _compress(proj1, qw // NSA_WIDTH + 1, nsa_pos_v[0], nsa_w1_v[0], nsa_b1_v[0],
                       jnp.tile(nsa_w2_v[0], (1, NSA_HPG)).astype(BF16), batch, seq)
    o_nsa = _nsa_attention(proj1, gates, kc_cmp, vc_cmp, batch, seq, n_cmp)
    h = _outproj(h, [o_nsa], nsa_w_out[0].astype(BF16))
    cw, routed, rank_col, rank_row, routed_row = _router(
        h, ffn_norm[1], _pad_cols(moe_w_router[0]).astype(BF16),
        _pad_cols(moe_b_router[0].reshape(1, N_EXPERTS)).astype(F32))
    n_chunks = t // MOE_CHUNK
    counts = rank_col.reshape(n_chunks, MOE_CHUNK, LANES)[:, -1, :N_EXPERTS].astype(jnp.int32).reshape(-1)
    h = _moe_ffn(h, ffn_norm[1], cw, routed, rank_col, rank_row, routed_row, counts,
                 moe_w_gate[0].astype(BF16), moe_w_up[0].astype(BF16), moe_w_down[0].astype(BF16))
    h = _ple(h, ple_norm[1], p[1].reshape(t, -1), ple_gate_w[1].astype(BF16), ple_proj_w[1].astype(BF16),
             final_norm, final=True)
    return h.reshape(batch, seq, D_MODEL)
```

```python
import functools

import jax
import jax.numpy as jnp
from jax import lax
from jax.experimental import pallas as pl
from jax.experimental.pallas import tpu as pltpu

F32 = jnp.float32
BF16 = jnp.bfloat16

D_MODEL = 1024
HEAD_DIM = 64
LANES = 128
FOX_HEADS = 8
DIL_HEADS = 8
FOX_WIDTH = FOX_HEADS * HEAD_DIM
DIL_WIDTH = DIL_HEADS * HEAD_DIM
DIL_PATTERNS = ((128, 1), (512, 4), (2048, 16))
Q_BLOCK = 128
NSA_HEADS = 16
NSA_GROUPS = 4
NSA_HPG = NSA_HEADS // NSA_GROUPS
NSA_KV = NSA_GROUPS * HEAD_DIM
NSA_CMP_LEN = 32
NSA_CMP_STRIDE = 16
NSA_SEL_LEN = 64
NSA_TOP_N = 8
NSA_WINDOW = 512
N_EXPERTS = 8
ROPE_THETA = 10000.0
RMS_EPS = 1e-6
NEG_INF = -1e30
SCALE = HEAD_DIM ** -0.5
LOG2E = 1.4426950408889634
LN2 = 0.6931471805599453
SCALE2 = SCALE * LOG2E
ONES_ROWS = 16

VMEM_LIMIT_BYTES = 52 * 1024 * 1024


def _params(*sem):
    return pltpu.CompilerParams(dimension_semantics=sem, vmem_limit_bytes=VMEM_LIMIT_BYTES)


def _rms(x, g):
    return x * lax.rsqrt(jnp.mean(x * x, axis=-1, keepdims=True) + RMS_EPS) * g


def _dot(a, b):
    return jnp.dot(a, b, preferred_element_type=F32)


def _dot_nt(a, b):
    return lax.dot_general(a, b, (((1,), (1,)), ((), ())), preferred_element_type=F32)


def _split3(x):
    hi = x.astype(BF16)
    r = x - hi.astype(F32)
    mid = r.astype(BF16)
    lo = (r - mid.astype(F32)).astype(BF16)
    return hi, mid, lo


def _lane_col(x, idx):
    lane = lax.broadcasted_iota(jnp.int32, x.shape, 1)
    return jnp.sum(jnp.where(lane == idx, x, 0.0), axis=1, keepdims=True)


def _rope_table_kernel(pos_ref, inv_ref, cos_ref, sin_ref):
    ang = pos_ref[...] * inv_ref[...]
    lane = lax.broadcasted_iota(jnp.int32, ang.shape, 1)
    sign = jnp.where((lane & (HEAD_DIM - 1)) < HEAD_DIM // 2, -1.0, 1.0)
    cos_ref[...] = jnp.cos(ang)
    sin_ref[...] = jnp.sin(ang) * sign


def _rope_tables(positions):
    t = positions.size
    half = HEAD_DIM // 2
    inv_freq = ROPE_THETA ** (-jnp.arange(half, dtype=F32) / half)
    inv = jnp.tile(inv_freq, LANES // half).reshape(1, LANES)
    pos = jnp.broadcast_to(positions.astype(F32).reshape(t, 1), (t, LANES))
    tm = 1024
    return pl.pallas_call(
        _rope_table_kernel,
        grid=(t // tm,),
        in_specs=[pl.BlockSpec((tm, LANES), lambda i: (i, 0)),
                  pl.BlockSpec((1, LANES), lambda i: (0, 0))],
        out_specs=[pl.BlockSpec((tm, LANES), lambda i: (i, 0))] * 2,
        out_shape=[jax.ShapeDtypeStruct((t, LANES), F32)] * 2,
        compiler_params=_params("parallel"),
        name="rope_tables",
    )(pos, inv)


def _proj_kernel(h_ref, g_ref, w_ref, wa_ref, ba_ref, cos_ref, sin_ref, o_ref, oa_ref, *, tiles, aux_act):
    xn = _rms(h_ref[...], g_ref[...]).astype(BF16)
    for c0, width, rope in tiles:
        acc = _dot(xn, w_ref[:, c0:c0 + width])
        if rope:
            reps = width // LANES
            cos = jnp.tile(cos_ref[...], (1, reps))
            sin = jnp.tile(sin_ref[...], (1, reps))
            lane = lax.broadcasted_iota(jnp.int32, acc.shape, 1)
            first_half = (lane & (HEAD_DIM - 1)) < HEAD_DIM // 2
            partner = jnp.where(first_half,
                                pltpu.roll(acc, width - HEAD_DIM // 2, 1),
                                pltpu.roll(acc, HEAD_DIM // 2, 1))
            acc = acc * cos + partner * sin
        o_ref[:, c0:c0 + width] = acc
    aux = _dot(xn, wa_ref[...]) + ba_ref[...]
    if aux_act == "log_sigmoid":
        oa_ref[...] = jnp.minimum(aux, 0.0) - jnp.log1p(jnp.exp(-jnp.abs(aux)))
    else:
        oa_ref[...] = jax.nn.sigmoid(aux)


def _proj(h, gain, w, w_aux, b_aux, cos, sin, tiles, aux_act, tm=512):
    t, n = h.shape[0], w.shape[1]
    return pl.pallas_call(
        functools.partial(_proj_kernel, tiles=tiles, aux_act=aux_act),
        grid=(t // tm,),
        in_specs=[pl.BlockSpec((tm, D_MODEL), lambda i: (i, 0)),
                  pl.BlockSpec((1, D_MODEL), lambda i: (0, 0)),
                  pl.BlockSpec((D_MODEL, n), lambda i: (0, 0)),
                  pl.BlockSpec((D_MODEL, LANES), lambda i: (0, 0)),
                  pl.BlockSpec((1, LANES), lambda i: (0, 0)),
                  pl.BlockSpec((tm, LANES), lambda i: (i, 0)),
                  pl.BlockSpec((tm, LANES), lambda i: (i, 0))],
        out_specs=[pl.BlockSpec((tm, n), lambda i: (i, 0)),
                   pl.BlockSpec((tm, LANES), lambda i: (i, 0))],
        out_shape=[jax.ShapeDtypeStruct((t, n), F32), jax.ShapeDtypeStruct((t, LANES), F32)],
        compiler_params=_params("parallel"),
        name="in_proj",
    )(h, gain.reshape(1, D_MODEL), w, w_aux, b_aux, cos, sin)


CUM_BLOCK = 512


def _cumsum_kernel(x_ref, ccol_ref, crow_ref, carry_ref):
    j = pl.program_id(1)

    @pl.when(j == 0)
    def _():
        carry_ref[...] = jnp.zeros_like(carry_ref)

    x = x_ref[...]
    r = lax.broadcasted_iota(jnp.int32, (CUM_BLOCK, CUM_BLOCK), 0)
    c = lax.broadcasted_iota(jnp.int32, (CUM_BLOCK, CUM_BLOCK), 1)
    tri = jnp.where(r >= c, 1.0, 0.0).astype(BF16)
    hi, mid, lo = _split3(x)
    cum = _dot(tri, hi) + _dot(tri, mid) + _dot(tri, lo) + carry_ref[0:1, :]
    ccol_ref[...] = cum
    crow_ref[0] = cum.T
    carry_ref[...] = jnp.broadcast_to(cum[CUM_BLOCK - 1:CUM_BLOCK, :], carry_ref.shape)


def _cumsum(x, batch, seq):
    nb = seq // CUM_BLOCK
    return pl.pallas_call(
        _cumsum_kernel,
        grid=(batch, nb),
        in_specs=[pl.BlockSpec((CUM_BLOCK, LANES), lambda b, j: (b * nb + j, 0))],
        out_specs=[pl.BlockSpec((CUM_BLOCK, LANES), lambda b, j: (b * nb + j, 0)),
                   pl.BlockSpec((1, LANES, CUM_BLOCK), lambda b, j: (b, 0, j))],
        out_shape=[jax.ShapeDtypeStruct((batch * seq, LANES), F32),
                   jax.ShapeDtypeStruct((batch, LANES, seq), F32)],
        scratch_shapes=[pltpu.VMEM((8, LANES), F32)],
        compiler_params=_params("parallel", "arbitrary"),
        name="token_cumsum",
    )(x)


def _fox_kernel(q_ref, k_ref, v_ref, ccol_ref, crow_ref, o_ref, kb_ref, vt_ref, ck_ref, *, tq, seq):
    pair = pl.program_id(1)
    i = pl.program_id(2)
    tk = tq
    h0 = 2 * pair

    @pl.when(i == 0)
    def _():
        def fill(ci, carry):
            rs = pl.ds(pl.multiple_of(ci * tk, tk), tk)
            kb_ref[rs, :] = k_ref[rs, :].astype(BF16)
            vt_ref[0:LANES, rs] = v_ref[rs, :].T.astype(BF16)
            vt_ref[LANES:, rs] = jnp.ones((ONES_ROWS, tk), BF16)
            cc = ccol_ref[rs, :] * LOG2E
            ck_ref[0, rs, :] = jnp.broadcast_to(_lane_col(cc, h0), (tk, LANES))
            ck_ref[1, rs, :] = jnp.broadcast_to(_lane_col(cc, h0 + 1), (tk, LANES))
            return carry

        lax.fori_loop(0, seq // tk, fill, 0)

    t0 = pl.multiple_of(i * tq, tq)
    low = lax.broadcasted_iota(jnp.int32, (tq, LANES), 1) < HEAD_DIM
    q = q_ref[...] * SCALE2
    qs = jnp.concatenate([jnp.where(low, q, 0.0), jnp.where(low, 0.0, q)], axis=0).astype(BF16)
    cq = jnp.concatenate([crow_ref[0, pl.ds(h0, 1), pl.ds(t0, tq)],
                          crow_ref[0, pl.ds(h0 + 1, 1), pl.ds(t0, tq)]], axis=1) * LOG2E
    reps = tq // LANES

    def scores(k0, nk):
        ck = jnp.concatenate([ck_ref[0, k0:k0 + nk, :]] * reps + [ck_ref[1, k0:k0 + nk, :]] * reps, axis=1)
        return _dot_nt(kb_ref[k0:k0 + nk, :], qs) + cq - ck

    krow = lax.broadcasted_iota(jnp.int32, (tk, tq), 0)
    qlane = lax.broadcasted_iota(jnp.int32, (tk, tq), 1)
    causal = jnp.where(krow <= qlane, 0.0, NEG_INF)

    def branch(n):
        def tile_scores(t):
            s = scores(t * tk, tk)
            return s + jnp.concatenate([causal, causal], axis=1) if t == n - 1 else s

        parts = []
        pending = tile_scores(0)
        for t in range(n):
            upcoming = tile_scores(t + 1) if t + 1 < n else None
            m_t = jnp.max(pending, axis=0, keepdims=True)
            parts.append((m_t, _dot(vt_ref[:, t * tk:(t + 1) * tk], jnp.exp2(pending - m_t).astype(BF16))))
            pending = upcoming
        m = functools.reduce(jnp.maximum, [m_t for m_t, _ in parts])
        acc = sum(a * jnp.exp2(m_t - m) for m_t, a in parts)
        out = acc[0:LANES] / acc[LANES:LANES + 1]
        o_ref[...] = jnp.concatenate([out[0:HEAD_DIM, 0:tq], out[HEAD_DIM:, tq:]], axis=0).T.astype(o_ref.dtype)

    for n in range(1, seq // tq + 1):
        pl.when(i == n - 1)(functools.partial(branch, n))


def _fox_attention(proj, ccol, crow, batch, seq, tq=512):
    nq = seq // tq
    npair = FOX_WIDTH // LANES
    return pl.pallas_call(
        functools.partial(_fox_kernel, tq=tq, seq=seq),
        grid=(batch, npair, nq),
        in_specs=[pl.BlockSpec((tq, LANES), lambda b, p, i: (b * nq + i, p)),
                  pl.BlockSpec((seq, LANES), lambda b, p, i: (b, npair + p)),
                  pl.BlockSpec((seq, LANES), lambda b, p, i: (b, 2 * npair + p)),
                  pl.BlockSpec((seq, LANES), lambda b, p, i: (b, 0)),
                  pl.BlockSpec((1, 8, seq), lambda b, p, i: (b, 0, 0))],
        out_specs=pl.BlockSpec((tq, LANES), lambda b, p, i: (b * nq + i, p)),
        out_shape=jax.ShapeDtypeStruct((batch * seq, FOX_WIDTH), BF16),
        scratch_shapes=[pltpu.VMEM((seq, LANES), BF16), pltpu.VMEM((LANES + ONES_ROWS, seq), BF16),
                        pltpu.VMEM((2, seq, LANES), F32)],
        compiler_params=_params("parallel", "parallel", "arbitrary"),
        name="fox_attention",
    )(proj, proj, proj, ccol, crow)


DIL_GROUP = 4


def _dil_kernel(q_ref, k_ref, v_ref, o_ref, os_ref, ls_ref, *, seq):
    low = lax.broadcasted_iota(jnp.int32, (Q_BLOCK, LANES), 1) < HEAD_DIM
    kr = lax.broadcasted_iota(jnp.int32, (2 * Q_BLOCK, Q_BLOCK), 0)
    qc = lax.broadcasted_iota(jnp.int32, (2 * Q_BLOCK, Q_BLOCK), 1)
    dist = qc + Q_BLOCK - kr
    def score_group(pi, units):
        window, dil = DIL_PATTERNS[pi]
        span = window // dil
        nb = (seq // dil) // Q_BLOCK
        band_bias = jnp.where((dist >= 0) & (dist <= span), 0.0, NEG_INF)
        scored, dests = [], []

        def rows(start):
            return pl.ds(start, Q_BLOCK, stride=dil) if dil > 1 else pl.ds(start, Q_BLOCK)

        for u in units:
            r, blk = divmod(u, nb)
            cur = r + blk * (Q_BLOCK * dil)
            q = q_ref[rows(cur), :] * SCALE2
            qs = jnp.concatenate([jnp.where(low, q, 0.0), jnp.where(low, 0.0, q)], axis=0).astype(BF16)
            if blk == 0:
                kk = k_ref[rows(cur), :].astype(BF16)
                vt = v_ref[rows(cur), :].T.astype(BF16)
                bias = band_bias[Q_BLOCK:]
            else:
                prev = cur - Q_BLOCK * dil
                kk = jnp.concatenate([k_ref[rows(prev), :], k_ref[rows(cur), :]], axis=0).astype(BF16)
                vt = jnp.concatenate([v_ref[rows(prev), :].T, v_ref[rows(cur), :].T], axis=1).astype(BF16)
                bias = band_bias
            vt = jnp.concatenate([vt, jnp.ones((ONES_ROWS, vt.shape[1]), BF16)], axis=0)
            s = _dot_nt(kk, qs) + jnp.concatenate([bias, bias], axis=1)
            scored.append((s, vt))
            dests.append(rows(cur))
        return scored, dests

    def softmax(s):
        m = jnp.max(s, axis=0, keepdims=True)
        return jnp.exp2(s - m).astype(BF16), m

    def finish(vt, p, m):
        ot = _dot(vt, p)
        den = ot[LANES:LANES + 1]
        ot = ot[0:LANES] * (1.0 / den)
        lse = m * LN2 + jnp.log(den)
        lse_t = jnp.concatenate([jnp.broadcast_to(lse[:, 0:Q_BLOCK], (HEAD_DIM, Q_BLOCK)),
                                 jnp.broadcast_to(lse[:, Q_BLOCK:], (HEAD_DIM, Q_BLOCK))], axis=0)
        return jnp.concatenate([ot[0:HEAD_DIM, 0:Q_BLOCK], ot[HEAD_DIM:, Q_BLOCK:]], axis=0).T, lse_t.T

    def finish_group(pi, scored, dests):
        probs = [softmax(s) for s, _ in scored]
        outs = [finish(vt, p, m) for (_, vt), (p, m) in zip(scored, probs)]
        for dest, (o, l) in zip(dests, outs):
            os_ref[pi, dest, :] = o
            ls_ref[pi, dest, :] = l

    groups = []
    for pi, (window, dil) in enumerate(DIL_PATTERNS):
        n_units = dil * ((seq // dil) // Q_BLOCK)
        groups += [(pi, range(u0, min(u0 + DIL_GROUP, n_units))) for u0 in range(0, n_units, DIL_GROUP)]
    pending = score_group(*groups[0])
    for gi, (pi, _) in enumerate(groups):
        upcoming = score_group(*groups[gi + 1]) if gi + 1 < len(groups) else None
        finish_group(pi, *pending)
        pending = upcoming

    chunk = 256

    def combine(ci, carry):
        rs = pl.ds(pl.multiple_of(ci * chunk, chunk), chunk)
        l0, l1, l2 = ls_ref[0, rs, :], ls_ref[1, rs, :], ls_ref[2, rs, :]
        m = jnp.maximum(jnp.maximum(l0, l1), l2)
        e0, e1, e2 = jnp.exp(l0 - m), jnp.exp(l1 - m), jnp.exp(l2 - m)
        tot = e0 + e1 + e2
        o_ref[rs, :] = ((e0 / tot) * os_ref[0, rs, :] + (e1 / tot) * os_ref[1, rs, :]
                        + (e2 / tot) * os_ref[2, rs, :]).astype(o_ref.dtype)
        return carry

    lax.fori_loop(0, seq // chunk, combine, 0)


def _dilated_attention(proj, batch, seq):
    npair = DIL_WIDTH // LANES
    base = 3 * FOX_WIDTH // LANES
    return pl.pallas_call(
        functools.partial(_dil_kernel, seq=seq),
        grid=(batch, npair),
        in_specs=[pl.BlockSpec((seq, LANES), lambda b, p: (b, base + p)),
                  pl.BlockSpec((seq, LANES), lambda b, p: (b, base + npair + p)),
                  pl.BlockSpec((seq, LANES), lambda b, p: (b, base + 2 * npair + p))],
        out_specs=pl.BlockSpec((seq, LANES), lambda b, p: (b, p)),
        out_shape=jax.ShapeDtypeStruct((batch * seq, DIL_WIDTH), BF16),
        scratch_shapes=[pltpu.VMEM((3, seq, LANES), F32), pltpu.VMEM((3, seq, LANES), F32)],
        compiler_params=_params("parallel", "parallel"),
        name="dilated_attention",
    )(proj, proj, proj)


def _outproj_kernel(*refs, n_in, tn):
    h_ref = refs[0]
    a_refs = refs[1:1 + n_in]
    w_ref = refs[1 + n_in]
    o_ref = refs[2 + n_in]
    acts = [a[...].astype(BF16) for a in a_refs]
    for c0 in range(0, D_MODEL, tn):
        acc = h_ref[:, c0:c0 + tn]
        k0 = 0
        for a in acts:
            acc = acc + _dot(a, w_ref[k0:k0 + a.shape[1], c0:c0 + tn])
            k0 += a.shape[1]
        o_ref[:, c0:c0 + tn] = acc


def _outproj(h, acts, w, tm=1024, tn=512):
    t = h.shape[0]
    return pl.pallas_call(
        functools.partial(_outproj_kernel, n_in=len(acts), tn=tn),
        grid=(t // tm,),
        in_specs=([pl.BlockSpec((tm, D_MODEL), lambda i: (i, 0))]
                  + [pl.BlockSpec((tm, a.shape[1]), lambda i: (i, 0)) for a in acts]
                  + [pl.BlockSpec((D_MODEL, D_MODEL), lambda i: (0, 0))]),
        out_specs=pl.BlockSpec((tm, D_MODEL), lambda i: (i, 0)),
        out_shape=jax.ShapeDtypeStruct((t, D_MODEL), F32),
        compiler_params=_params("parallel"),
        name="out_proj",
    )(h, *acts, w)


def _swiglu_tile(x, wg, wu, wd):
    gate = _dot(x, wg)
    up = _dot(x, wu)
    return _dot((gate * jax.nn.sigmoid(gate) * up).astype(BF16), wd)


def _ffn_kernel(h_ref, g_ref, wg_ref, wu_ref, wd_ref, o_ref, xn_ref, acc_ref, *, n_f):
    f = pl.program_id(1)

    @pl.when(f == 0)
    def _():
        xn_ref[...] = _rms(h_ref[...], g_ref[...]).astype(BF16)
        acc_ref[...] = jnp.zeros_like(acc_ref)

    acc_ref[...] += _swiglu_tile(xn_ref[...], wg_ref[...], wu_ref[...], wd_ref[...])

    @pl.when(f == n_f - 1)
    def _():
        o_ref[...] = h_ref[...] + acc_ref[...]


def _ffn(h, gain, wg, wu, wd, tm=512, tf=1792):
    t, dff = h.shape[0], wg.shape[1]
    n_f = dff // tf
    return pl.pallas_call(
        functools.partial(_ffn_kernel, n_f=n_f),
        grid=(t // tm, n_f),
        in_specs=[pl.BlockSpec((tm, D_MODEL), lambda i, f: (i, 0)),
                  pl.BlockSpec((1, D_MODEL), lambda i, f: (0, 0)),
                  pl.BlockSpec((D_MODEL, tf), lambda i, f: (0, f)),
                  pl.BlockSpec((D_MODEL, tf), lambda i, f: (0, f)),
                  pl.BlockSpec((tf, D_MODEL), lambda i, f: (f, 0))],
        out_specs=pl.BlockSpec((tm, D_MODEL), lambda i, f: (i, 0)),
        out_shape=jax.ShapeDtypeStruct((t, D_MODEL), F32),
        scratch_shapes=[pltpu.VMEM((tm, D_MODEL), BF16), pltpu.VMEM((tm, D_MODEL), F32)],
        compiler_params=_params("parallel", "arbitrary"),
        name="dense_swiglu",
    )(h, gain.reshape(1, D_MODEL), wg, wu, wd)


MOE_CHUNK = 1024
MOE_ROWS = 128
MOE_SCATTER = 256


def _moe_kernel(cnt_ref, h_ref, g_ref, cw_ref, m_ref, rcol_ref, rrow_ref, mrow_ref, wg_ref, wu_ref, wd_ref,
                o_ref, xn_ref, xe_ref, ye_ref, *, n_f):
    c = pl.program_id(0)
    e = pl.program_id(1)
    f = pl.program_id(2)
    chunk = MOE_CHUNK
    n = cnt_ref[c * N_EXPERTS + e]
    n_scatter = (n + MOE_SCATTER - 1) // MOE_SCATTER
    n_tiles = (n + MOE_ROWS - 1) // MOE_ROWS

    @pl.when((e == 0) & (f == 0))
    def _():
        h = h_ref[...]
        xn_ref[...] = _rms(h, g_ref[...]).astype(BF16)
        o_ref[...] = h

    @pl.when(f == 0)
    def _():
        rank = rrow_ref[pl.ds(e, 1), :] * mrow_ref[pl.ds(e, 1), :]
        slot = lax.broadcasted_iota(jnp.int32, (MOE_ROWS, chunk), 0) + 1

        def gather(i, carry):
            rows = pl.ds(pl.multiple_of(i * MOE_ROWS, MOE_ROWS), MOE_ROWS)
            onehot = jnp.where(rank == (slot + i * MOE_ROWS).astype(F32), 1.0, 0.0).astype(BF16)
            xe_ref[rows, :] = _dot(onehot, xn_ref[...]).astype(BF16)
            ye_ref[rows, :] = jnp.zeros((MOE_ROWS, D_MODEL), F32)
            return carry

        def clear(i, carry):
            rows = pl.ds(pl.multiple_of(i * MOE_ROWS, MOE_ROWS), MOE_ROWS)
            ye_ref[rows, :] = jnp.zeros((MOE_ROWS, D_MODEL), F32)
            return carry

        lax.fori_loop(0, n_tiles, gather, 0)
        lax.fori_loop(n_tiles, n_scatter * (MOE_SCATTER // MOE_ROWS), clear, 0)

    def tile(i, carry):
        rows = pl.ds(pl.multiple_of(i * MOE_ROWS, MOE_ROWS), MOE_ROWS)
        ye_ref[rows, :] += _swiglu_tile(xe_ref[rows, :], wg_ref[...], wu_ref[...], wd_ref[...])
        return carry

    lax.fori_loop(0, n_tiles, tile, 0)

    @pl.when(f == n_f - 1)
    def _():
        rank = _lane_col(rcol_ref[...] * m_ref[...], e)
        weight = _lane_col(cw_ref[...], e)
        slot = lax.broadcasted_iota(jnp.int32, (chunk, MOE_SCATTER), 1) + 1

        def scatter(i, carry):
            rows = pl.ds(pl.multiple_of(i * MOE_SCATTER, MOE_SCATTER), MOE_SCATTER)
            onehot = jnp.where(rank == (slot + i * MOE_SCATTER).astype(F32), 1.0, 0.0).astype(BF16)
            o_ref[...] += weight * _dot(onehot, ye_ref[rows, :].astype(BF16))
            return carry

        lax.fori_loop(0, n_scatter, scatter, 0)


def _moe_ffn(h, gain, cw, mask, rcol, rrow, mrow, counts, wg, wu, wd, tf=1792):
    t, dff = h.shape[0], wg.shape[2]
    n_f = dff // tf
    chunk = MOE_CHUNK
    tok = lambda width: pl.BlockSpec((chunk, width), lambda c, e, f, cnt: (c, 0))
    lane_major = pl.BlockSpec((None, N_EXPERTS, chunk), lambda c, e, f, cnt: (c, 0, 0))
    grid_spec = pltpu.PrefetchScalarGridSpec(
        num_scalar_prefetch=1,
        grid=(t // chunk, N_EXPERTS, n_f),
        in_specs=[tok(D_MODEL),
                  pl.BlockSpec((1, D_MODEL), lambda c, e, f, cnt: (0, 0)),
                  tok(LANES), tok(LANES), tok(LANES), lane_major, lane_major,
                  pl.BlockSpec((None, D_MODEL, tf), lambda c, e, f, cnt: (e, 0, f)),
                  pl.BlockSpec((None, D_MODEL, tf), lambda c, e, f, cnt: (e, 0, f)),
                  pl.BlockSpec((None, tf, D_MODEL), lambda c, e, f, cnt: (e, f, 0))],
        out_specs=tok(D_MODEL),
        scratch_shapes=[pltpu.VMEM((chunk, D_MODEL), BF16), pltpu.VMEM((chunk, D_MODEL), BF16),
                        pltpu.VMEM((chunk, D_MODEL), F32)],
    )
    return pl.pallas_call(
        functools.partial(_moe_kernel, n_f=n_f),
        grid_spec=grid_spec,
        out_shape=jax.ShapeDtypeStruct((t, D_MODEL), F32),
        compiler_params=_params("parallel", "arbitrary", "arbitrary"),
        name="moe_swiglu",
    )(counts, h, gain.reshape(1, D_MODEL), cw, mask, rcol, rrow, mrow, wg, wu, wd)


def _router_kernel(h_ref, g_ref, w_ref, b_ref, cw_ref, m_ref, rcol_ref, rrow_ref, mrow_ref):
    xn = _rms(h_ref[...], g_ref[...]).astype(BF16)
    logits = _dot(xn, w_ref[...]) + b_ref[...]
    lane = lax.broadcasted_iota(jnp.int32, logits.shape, 1).astype(F32)
    logits = jnp.where(lane < N_EXPERTS, logits, -jnp.inf)
    m1 = jnp.max(logits, axis=1, keepdims=True)
    i1 = jnp.min(jnp.where(logits == m1, lane, float(LANES)), axis=1, keepdims=True)
    rest = jnp.where(lane == i1, -jnp.inf, logits)
    m2 = jnp.max(rest, axis=1, keepdims=True)
    i2 = jnp.min(jnp.where(rest == m2, lane, float(LANES)), axis=1, keepdims=True)
    e2 = jnp.exp(m2 - m1)
    w1 = 1.0 / (1.0 + e2)
    w2 = e2 / (1.0 + e2)
    cw_ref[...] = jnp.where(lane == i1, w1, jnp.where(lane == i2, w2, 0.0))
    mask = jnp.where((lane == i1) | (lane == i2), 1.0, 0.0)
    m_ref[...] = mask
    r = lax.broadcasted_iota(jnp.int32, (CUM_BLOCK, CUM_BLOCK), 0)
    c = lax.broadcasted_iota(jnp.int32, (CUM_BLOCK, CUM_BLOCK), 1)
    tri = jnp.where(r >= c, 1.0, 0.0).astype(BF16)
    carry = jnp.zeros((1, LANES), F32)
    for r0 in range(0, mask.shape[0], CUM_BLOCK):
        blk = mask[r0:r0 + CUM_BLOCK]
        cum = _dot(tri, blk.astype(BF16)) + carry
        rcol_ref[r0:r0 + CUM_BLOCK, :] = cum
        rrow_ref[:, r0:r0 + CUM_BLOCK] = cum.T
        mrow_ref[:, r0:r0 + CUM_BLOCK] = blk.T
        carry = cum[CUM_BLOCK - 1:CUM_BLOCK, :]


def _router(h, gain, w, b):
    t = h.shape[0]
    tm = MOE_CHUNK
    token_major = pl.BlockSpec((tm, LANES), lambda i: (i, 0))
    lane_major = pl.BlockSpec((None, LANES, tm), lambda i: (i, 0, 0))
    return pl.pallas_call(
        _router_kernel,
        grid=(t // tm,),
        in_specs=[pl.BlockSpec((tm, D_MODEL), lambda i: (i, 0)),
                  pl.BlockSpec((1, D_MODEL), lambda i: (0, 0)),
                  pl.BlockSpec((D_MODEL, LANES), lambda i: (0, 0)),
                  pl.BlockSpec((1, LANES), lambda i: (0, 0))],
        out_specs=[token_major, token_major, token_major, lane_major, lane_major],
        out_shape=[jax.ShapeDtypeStruct((t, LANES), F32)] * 3
                  + [jax.ShapeDtypeStruct((t // tm, LANES, tm), F32)] * 2,
        compiler_params=_params("parallel"),
        name="moe_router",
    )(h, gain.reshape(1, D_MODEL), w, b)


def _ple_kernel(h_ref, g_ref, p_ref, wg_ref, wp_ref, fg_ref, o_ref, *, final, tn):
    h = h_ref[...]
    xn = _rms(h, g_ref[...]).astype(BF16)
    pe = p_ref[...].astype(BF16)
    outs = []
    for c0 in range(0, D_MODEL, tn):
        gate = jax.nn.sigmoid(_dot(xn, wg_ref[:, c0:c0 + tn]))
        outs.append(h[:, c0:c0 + tn] + gate * _dot(pe, wp_ref[:, c0:c0 + tn]))
    new = jnp.concatenate(outs, axis=1)
    o_ref[...] = _rms(new, fg_ref[...]) if final else new


def _ple(h, gain, p, wg, wp, final_gain, final, tm=1024, tn=512):
    t, pd = p.shape
    return pl.pallas_call(
        functools.partial(_ple_kernel, final=final, tn=tn),
        grid=(t // tm,),
        in_specs=[pl.BlockSpec((tm, D_MODEL), lambda i: (i, 0)),
                  pl.BlockSpec((1, D_MODEL), lambda i: (0, 0)),
                  pl.BlockSpec((tm, pd), lambda i: (i, 0)),
                  pl.BlockSpec((D_MODEL, D_MODEL), lambda i: (0, 0)),
                  pl.BlockSpec((pd, D_MODEL), lambda i: (0, 0)),
                  pl.BlockSpec((1, D_MODEL), lambda i: (0, 0))],
        out_specs=pl.BlockSpec((tm, D_MODEL), lambda i: (i, 0)),
        out_shape=jax.ShapeDtypeStruct((t, D_MODEL), F32),
        compiler_params=_params("parallel"),
        name="ple",
    )(h, gain.reshape(1, D_MODEL), p, wg, wp, final_gain.reshape(1, D_MODEL))


def _compress_kernel(x01_ref, x23_ref, pos_ref, w1_ref, b1_ref, w2_ref, o_ref, *, blocks):
    stride = NSA_CMP_STRIDE
    hidden = b1_ref.shape[1]
    low = lax.broadcasted_iota(jnp.int32, (blocks, LANES), 1) < HEAD_DIM
    first = [jnp.zeros((blocks, hidden), F32) for _ in range(NSA_GROUPS)]
    second = [jnp.zeros((blocks, hidden), F32) for _ in range(NSA_GROUPS)]
    for j in range(stride):
        for half, x_ref in enumerate((x01_ref, x23_ref)):
            xs = x_ref[pl.ds(j, blocks, stride=stride), :]
            xa = xs + pos_ref[j:j + 1, :]
            xb = xs + pos_ref[stride + j:stride + j + 1, :]
            for sub in range(2):
                g = 2 * half + sub
                keep = low if sub == 0 else jnp.logical_not(low)
                first[g] = first[g] + _dot(jnp.where(keep, xa, 0.0).astype(BF16), w1_ref[j])
                second[g] = second[g] + _dot(jnp.where(keep, xb, 0.0).astype(BF16), w1_ref[stride + j])
    row = lax.broadcasted_iota(jnp.int32, (blocks, NSA_WIDTH), 0)
    for g in range(NSA_GROUPS):
        hid = first[g] + pltpu.roll(second[g], blocks - 1, 0) + b1_ref[...]
        out = _dot(jax.nn.gelu(hid, approximate=True).astype(BF16), w2_ref[...])
        o_ref[g] = jnp.where(row == blocks - 1, 0.0, out)


def _compress(proj, col_block, pos, w1, b1, w2, batch, seq):
    blocks = seq // NSA_CMP_STRIDE
    hidden = w1.shape[1]
    w1_rep = jnp.tile(w1.reshape(NSA_CMP_LEN, HEAD_DIM, hidden), (1, 2, 1))
    pos_rep = jnp.tile(pos, (1, 2))
    halves = NSA_WIDTH // LANES
    return pl.pallas_call(
        functools.partial(_compress_kernel, blocks=blocks),
        grid=(batch,),
        in_specs=[pl.BlockSpec((seq, LANES), lambda b: (b, halves * col_block)),
                  pl.BlockSpec((seq, LANES), lambda b: (b, halves * col_block + 1)),
                  pl.BlockSpec((NSA_CMP_LEN, LANES), lambda b: (0, 0)),
                  pl.BlockSpec((NSA_CMP_LEN, LANES, hidden), lambda b: (0, 0, 0)),
                  pl.BlockSpec((1, hidden), lambda b: (0, 0)),
                  pl.BlockSpec((hidden, NSA_WIDTH), lambda b: (0, 0))],
        out_specs=pl.BlockSpec((NSA_GROUPS, blocks, NSA_WIDTH), lambda b: (b, 0, 0)),
        out_shape=jax.ShapeDtypeStruct((batch * NSA_GROUPS, blocks, NSA_WIDTH), F32),
        compiler_params=_params("parallel"),
        name="nsa_compress",
    )(proj, proj, pos_rep, w1_rep.astype(BF16), b1.reshape(1, hidden), w2)


NSA_TQ = 256
NSA_TK = 256
NSA_SUB = 256
NSA_WIDTH = NSA_HPG * HEAD_DIM
LOG_HEAD_DIM = HEAD_DIM.bit_length() - 1
LOG_SEL_LEN = NSA_SEL_LEN.bit_length() - 1


def _nsa_kernel(q_ref, kc_ref, vc_ref, ks_ref, vs_ref, kw_ref, vw_ref, gate_ref, o_ref,
                ksb, kwb, vst_all, vwt_all, vct, gt_ref, osel_ref, *, seq, n_cmp):
    g = pl.program_id(1)
    i = pl.program_id(2)
    tq, tk = NSA_TQ, NSA_TK
    cols = NSA_HPG * tq
    vrows = HEAD_DIM + ONES_ROWS
    grow = pl.multiple_of(g * vrows, vrows)

    @pl.when((i == 0) & (g == 0))
    def _():
        def fill(ci, carry):
            rs = pl.ds(pl.multiple_of(ci * NSA_WIDTH, NSA_WIDTH), NSA_WIDTH)
            ksb[rs, :] = ks_ref[rs, :].astype(BF16)
            kwb[rs, :] = kw_ref[rs, :].astype(BF16)
            for src, dst in ((vs_ref, vst_all), (vw_ref, vwt_all)):
                vt = src[rs, :].T.astype(BF16)
                for grp in range(NSA_GROUPS):
                    dst[grp * vrows:grp * vrows + HEAD_DIM, rs] = vt[grp * HEAD_DIM:(grp + 1) * HEAD_DIM]
                    dst[grp * vrows + HEAD_DIM:(grp + 1) * vrows, rs] = jnp.ones((ONES_ROWS, NSA_WIDTH), BF16)
            return carry

        lax.fori_loop(0, seq // NSA_WIDTH, fill, 0)

    @pl.when(i == 0)
    def _():
        vct[...] = vc_ref[0].T[0:HEAD_DIM, :].astype(BF16)

    vst = vst_all.at[pl.ds(grow, vrows)]
    vwt = vwt_all.at[pl.ds(grow, vrows)]

    t0 = i * tq
    lane_grp = lax.broadcasted_iota(jnp.int32, (tq, NSA_WIDTH), 1) >> LOG_HEAD_DIM
    q = q_ref[...] * SCALE2
    rolled = [q] + [pltpu.roll(q, s * HEAD_DIM, 1) for s in range(1, NSA_HPG)]
    parts = []
    for j in range(NSA_HPG):
        shift = (g - j) & (NSA_HPG - 1)
        moved = jnp.where(shift == 0, rolled[0],
                          jnp.where(shift == 1, rolled[1], jnp.where(shift == 2, rolled[2], rolled[3])))
        parts.append(jnp.where(lane_grp == g, moved, 0.0))
    qs = jnp.concatenate(parts, axis=0).astype(BF16)

    def heads_sum(x):
        out = x[:, 0:tq]
        for j in range(1, NSA_HPG):
            out = out + x[:, j * tq:(j + 1) * tq]
        return out

    def lanes4(x):
        return jnp.concatenate([x] * NSA_HPG, axis=1)

    wk = NSA_WINDOW + tq
    ws = pl.multiple_of(jnp.maximum(t0 - NSA_WINDOW, 0), tq)
    kpos = ws + lax.broadcasted_iota(jnp.int32, (wk, tq), 0)
    qw = t0 + lax.broadcasted_iota(jnp.int32, (wk, tq), 1)
    bias_w = jnp.where((kpos <= qw) & (kpos > qw - NSA_WINDOW), 0.0, NEG_INF)
    sw = _dot_nt(kwb[pl.ds(ws, wk), :], qs) + lanes4(bias_w)

    nrow = lax.broadcasted_iota(jnp.int32, (LANES, cols), 0)
    tcol = t0 + (lax.broadcasted_iota(jnp.int32, (LANES, cols), 1) & (tq - 1))
    valid_c = (nrow * NSA_CMP_STRIDE + NSA_CMP_LEN - 1 <= tcol) & (nrow < n_cmp)
    sc = jnp.where(valid_c, _dot_nt(kc_ref[0].astype(BF16), qs), NEG_INF)
    mc = jnp.max(sc, axis=0, keepdims=True)
    ec = jnp.where(valid_c, jnp.exp2(sc - mc), 0.0)
    dc = jnp.sum(ec, axis=0, keepdims=True)
    pc = ec / jnp.where(dc > 0.0, dc, 1.0)
    o_cmp = _dot(vct[...], pc.astype(BF16))
    pc_sum = heads_sum(pc)

    n_sel_blocks = seq // NSA_SEL_LEN
    jrow = lax.broadcasted_iota(jnp.int32, (LANES, LANES), 0)
    ncol = lax.broadcasted_iota(jnp.int32, (LANES, LANES), 1)
    overlap = ((ncol * NSA_CMP_STRIDE < (jrow + 1) * NSA_SEL_LEN)
               & (ncol * NSA_CMP_STRIDE + NSA_CMP_LEN > jrow * NSA_SEL_LEN)
               & (ncol < n_cmp) & (jrow < n_sel_blocks))
    overlap = jnp.where(overlap, 1.0, 0.0).astype(BF16)
    hi, mid, lo = _split3(pc_sum)
    imp = (_dot(overlap, hi) + _dot(overlap, mid) + _dot(overlap, lo))[0:n_sel_blocks]
    pw = jnp.exp2(sw - jnp.max(sw, axis=0, keepdims=True))
    o_win = _dot(vwt[:, pl.ds(ws, wk)], pw.astype(BF16))
    o_win = o_win[0:HEAD_DIM] / o_win[HEAD_DIM:HEAD_DIM + 1]
    blk = lax.broadcasted_iota(jnp.int32, (n_sel_blocks, tq), 0)
    cur = (t0 + lax.broadcasted_iota(jnp.int32, (n_sel_blocks, tq), 1)) >> LOG_SEL_LEN
    forced = (blk == 0) | (blk == cur) | (blk == cur - 1)
    imp = jnp.where(blk > cur, -1.0, jnp.where(forced, 1e6, imp))
    beaten = jnp.zeros((n_sel_blocks, tq), jnp.int32)
    for c in range(n_sel_blocks):
        row = imp[c:c + 1, :]
        wins = (row > imp) | ((row == imp) & (blk > c))
        beaten = beaten + jnp.where(wins, 1, 0)
    sel_bias = jnp.where(beaten < NSA_TOP_N, 0.0, NEG_INF)
    sel_bias = jnp.concatenate([sel_bias, jnp.zeros((LANES - n_sel_blocks, tq), F32)], axis=0).astype(BF16)

    n_tiles = (t0 + tq - 1) // tk + 1

    def sel_branch(n):
        sub = NSA_SUB
        erow = lax.broadcasted_iota(jnp.int32, (sub, LANES), 0)
        ecol = lax.broadcasted_iota(jnp.int32, (sub, LANES), 1)
        krow = lax.broadcasted_iota(jnp.int32, (sub, tq), 0)
        qlane = t0 + lax.broadcasted_iota(jnp.int32, (sub, tq), 1)

        def scores(k0):
            expand = jnp.where(((k0 + erow) >> LOG_SEL_LEN) == ecol, 1.0, 0.0).astype(BF16)
            bias = jnp.where(k0 + krow <= qlane, _dot(expand, sel_bias), NEG_INF)
            return _dot_nt(ksb[k0:k0 + sub, :], qs) + lanes4(bias)

        starts = list(range(0, n * tk, sub))
        parts = []
        pending = scores(starts[0])
        for i, k0 in enumerate(starts):
            upcoming = scores(starts[i + 1]) if i + 1 < len(starts) else None
            m_sub = jnp.max(pending, axis=0, keepdims=True)
            parts.append((m_sub, _dot(vst[:, k0:k0 + sub], jnp.exp2(pending - m_sub).astype(BF16))))
            pending = upcoming
        m = functools.reduce(jnp.maximum, [m_sub for m_sub, _ in parts])
        acc = sum(a * jnp.exp2(m_sub - m) for m_sub, a in parts)
        osel_ref[...] = acc[0:HEAD_DIM] / acc[HEAD_DIM:HEAD_DIM + 1]

    for n in range(1, seq // tk + 1):
        pl.when(n_tiles == n)(functools.partial(sel_branch, n))
    o_sel = osel_ref[...]

    gt_ref[...] = gate_ref[...].T
    out = jnp.zeros((HEAD_DIM, cols), F32)
    for c, branch in enumerate((o_cmp, o_sel, o_win)):
        gate = jnp.concatenate([gt_ref[pl.ds(c * NSA_HEADS + g * NSA_HPG + j, 1), :] for j in range(NSA_HPG)],
                               axis=1)
        out = out + gate * branch
    o_ref[...] = jnp.concatenate([out[:, j * tq:(j + 1) * tq] for j in range(NSA_HPG)], axis=0).T.astype(o_ref.dtype)


def _nsa_attention(proj, gates, kc_cmp, vc_cmp, batch, seq, n_cmp):
    nq = seq // NSA_TQ
    ncb = kc_cmp.shape[1]
    qcols = NSA_HEADS * HEAD_DIM // NSA_WIDTH
    kv = lambda c: pl.BlockSpec((seq, NSA_WIDTH), lambda b, g, i, c=c: (b, qcols + c))
    cmp_spec = pl.BlockSpec((1, ncb, NSA_WIDTH), lambda b, g, i: (b * NSA_GROUPS + g, 0, 0))
    return pl.pallas_call(
        functools.partial(_nsa_kernel, seq=seq, n_cmp=n_cmp),
        grid=(batch, NSA_GROUPS, nq),
        in_specs=[pl.BlockSpec((NSA_TQ, NSA_WIDTH), lambda b, g, i: (b * nq + i, g)),
                  cmp_spec, cmp_spec, kv(2), kv(3), kv(4), kv(5),
                  pl.BlockSpec((NSA_TQ, LANES), lambda b, g, i: (b * nq + i, 0))],
        out_specs=pl.BlockSpec((NSA_TQ, NSA_WIDTH), lambda b, g, i: (b * nq + i, g)),
        out_shape=jax.ShapeDtypeStruct((batch * seq, NSA_HEADS * HEAD_DIM), BF16),
        scratch_shapes=[pltpu.VMEM((seq, NSA_WIDTH), BF16), pltpu.VMEM((seq, NSA_WIDTH), BF16),
                        pltpu.VMEM((NSA_GROUPS * (HEAD_DIM + ONES_ROWS), seq), BF16),
                        pltpu.VMEM((NSA_GROUPS * (HEAD_DIM + ONES_ROWS), seq), BF16),
                        pltpu.VMEM((HEAD_DIM, ncb), BF16), pltpu.VMEM((LANES, NSA_TQ), F32),
                        pltpu.VMEM((HEAD_DIM, NSA_HPG * NSA_TQ), F32)],
        compiler_params=_params("parallel", "arbitrary", "arbitrary"),
        name="nsa_attention",
    )(proj, kc_cmp, vc_cmp, proj, proj, proj, proj, gates)


def _pad_cols(w, width=LANES):
    return jnp.pad(w, ((0, 0), (0, width - w.shape[1])))


def kernel(x, p, positions, mix_norm, ffn_norm, ple_norm, ple_gate_w, ple_proj_w, fd_w_in, fd_forget_b, fd_w_out, dense_w_gate, dense_w_up, dense_w_down, nsa_w_in, nsa_pos_k, nsa_w1_k, nsa_b1_k, nsa_w2_k, nsa_pos_v, nsa_w1_v, nsa_b1_v, nsa_w2_v, nsa_w_out, moe_w_router, moe_b_router, moe_w_gate, moe_w_up, moe_w_down, final_norm):
    batch, seq, _ = x.shape
    t = batch * seq
    h = x.reshape(t, D_MODEL)
    cos, sin = _rope_tables(positions)

    n_main = 3 * FOX_WIDTH + 3 * DIL_WIDTH
    w_in = fd_w_in[0]
    tiles0 = tuple((c, 512, c in (3 * FOX_WIDTH, 3 * FOX_WIDTH + DIL_WIDTH)) for c in range(0, n_main, 512))
    proj0, log_f = _proj(h, mix_norm[0], w_in[:, :n_main].astype(BF16),
                         _pad_cols(w_in[:, n_main:]).astype(BF16),
                         _pad_cols(fd_forget_b[0].reshape(1, FOX_HEADS)).astype(F32),
                         cos, sin, tiles0, "log_sigmoid")
    ccol, crow = _cumsum(log_f, batch, seq)
    o_fox = _fox_attention(proj0, ccol, crow, batch, seq)
    o_dil = _dilated_attention(proj0, batch, seq)
    h = _outproj(h, [o_fox, o_dil], fd_w_out[0].astype(BF16))
    h = _ffn(h, ffn_norm[0], dense_w_gate[0].astype(BF16), dense_w_up[0].astype(BF16),
             dense_w_down[0].astype(BF16))
    h = _ple(h, ple_norm[0], p[0].reshape(t, -1), ple_gate_w[0].astype(BF16), ple_proj_w[0].astype(BF16),
             final_norm, final=False)

    qw = NSA_HEADS * HEAD_DIM
    n_main1 = qw + 6 * NSA_KV
    w_in1 = nsa_w_in[0]
    rope_cols = set(range(0, qw, 256)) | {qw, qw + 2 * NSA_KV, qw + 4 * NSA_KV}
    tiles1 = tuple((c, 256, c in rope_cols) for c in range(0, n_main1, 256))
    w_gate = w_in1[:, n_main1:].reshape(D_MODEL, NSA_HEADS, 3).transpose(0, 2, 1).reshape(D_MODEL, 3 * NSA_HEADS)
    proj1, gates = _proj(h, mix_norm[1], w_in1[:, :n_main1].astype(BF16), _pad_cols(w_gate).astype(BF16),
                         jnp.zeros((1, LANES), F32), cos, sin, tiles1, "sigmoid")
    n_cmp = (seq - NSA_CMP_LEN) // NSA_CMP_STRIDE + 1
    kc_cmp = _compress(proj1, qw // NSA_WIDTH, nsa_pos_k[0], nsa_w1_k[0], nsa_b1_k[0],
                       jnp.tile(nsa_w2_k[0], (1, NSA_HPG)).astype(BF16), batch, seq)
    vc_cmp = _compress(proj1, qw // NSA_WIDTH + 1, nsa_pos_v[0], nsa_w1_v[0], nsa_b1_v[0],
                       jnp.tile(nsa_w2_v[0], (1, NSA_HPG)).astype(BF16), batch, seq)
    o_nsa = _nsa_attention(proj1, gates, kc_cmp, vc_cmp, batch, seq, n_cmp)
    h = _outproj(h, [o_nsa], nsa_w_out[0].astype(BF16))
    cw, routed, rank_col, rank_row, routed_row = _router(
        h, ffn_norm[1], _pad_cols(moe_w_router[0]).astype(BF16),
        _pad_cols(moe_b_router[0].reshape(1, N_EXPERTS)).astype(F32))
    n_chunks = t // MOE_CHUNK
    counts = rank_col.reshape(n_chunks, MOE_CHUNK, LANES)[:, -1, :N_EXPERTS].astype(jnp.int32).reshape(-1)
    h = _moe_ffn(h, ffn_norm[1], cw, routed, rank_col, rank_row, routed_row, counts,
                 moe_w_gate[0].astype(BF16), moe_w_up[0].astype(BF16), moe_w_down[0].astype(BF16))
    h = _ple(h, ple_norm[1], p[1].reshape(t, -1), ple_gate_w[1].astype(BF16), ple_proj_w[1].astype(BF16),
             final_norm, final=True)
    return h.reshape(batch, seq, D_MODEL)
```

```python
import functools

import jax
import jax.numpy as jnp
from jax import lax
from jax.experimental import pallas as pl
from jax.experimental.pallas import tpu as pltpu

F32 = jnp.float32
BF16 = jnp.bfloat16

D_MODEL = 1024
HEAD_DIM = 64
LANES = 128
FOX_HEADS = 8
DIL_HEADS = 8
FOX_WIDTH = FOX_HEADS * HEAD_DIM
DIL_WIDTH = DIL_HEADS * HEAD_DIM
DIL_PATTERNS = ((128, 1), (512, 4), (2048, 16))
Q_BLOCK = 128
NSA_HEADS = 16
NSA_GROUPS = 4
NSA_HPG = NSA_HEADS // NSA_GROUPS
NSA_KV = NSA_GROUPS * HEAD_DIM
NSA_CMP_LEN = 32
NSA_CMP_STRIDE = 16
NSA_SEL_LEN = 64
NSA_TOP_N = 8
NSA_WINDOW = 512
N_EXPERTS = 8
ROPE_THETA = 10000.0
RMS_EPS = 1e-6
NEG_INF = -1e30
SCALE = HEAD_DIM ** -0.5
LOG2E = 1.4426950408889634
LN2 = 0.6931471805599453
SCALE2 = SCALE * LOG2E
ONES_ROWS = 16

VMEM_LIMIT_BYTES = 52 * 1024 * 1024


def _params(*sem):
    return pltpu.CompilerParams(dimension_semantics=sem, vmem_limit_bytes=VMEM_LIMIT_BYTES)


def _rms(x, g):
    return x * lax.rsqrt(jnp.mean(x * x, axis=-1, keepdims=True) + RMS_EPS) * g


def _dot(a, b):
    return jnp.dot(a, b, preferred_element_type=F32)


def _dot_nt(a, b):
    return lax.dot_general(a, b, (((1,), (1,)), ((), ())), preferred_element_type=F32)


def _split3(x):
    hi = x.astype(BF16)
    r = x - hi.astype(F32)
    mid = r.astype(BF16)
    lo = (r - mid.astype(F32)).astype(BF16)
    return hi, mid, lo


def _lane_col(x, idx):
    lane = lax.broadcasted_iota(jnp.int32, x.shape, 1)
    return jnp.sum(jnp.where(lane == idx, x, 0.0), axis=1, keepdims=True)


def _rope_table_kernel(pos_ref, inv_ref, cos_ref, sin_ref):
    ang = pos_ref[...] * inv_ref[...]
    lane = lax.broadcasted_iota(jnp.int32, ang.shape, 1)
    sign = jnp.where((lane & (HEAD_DIM - 1)) < HEAD_DIM // 2, -1.0, 1.0)
    cos_ref[...] = jnp.cos(ang)
    sin_ref[...] = jnp.sin(ang) * sign


def _rope_tables(positions):
    t = positions.size
    half = HEAD_DIM // 2
    inv_freq = ROPE_THETA ** (-jnp.arange(half, dtype=F32) / half)
    inv = jnp.tile(inv_freq, LANES // half).reshape(1, LANES)
    pos = jnp.broadcast_to(positions.astype(F32).reshape(t, 1), (t, LANES))
    tm = 1024
    return pl.pallas_call(
        _rope_table_kernel,
        grid=(t // tm,),
        in_specs=[pl.BlockSpec((tm, LANES), lambda i: (i, 0)),
                  pl.BlockSpec((1, LANES), lambda i: (0, 0))],
        out_specs=[pl.BlockSpec((tm, LANES), lambda i: (i, 0))] * 2,
        out_shape=[jax.ShapeDtypeStruct((t, LANES), F32)] * 2,
        compiler_params=_params("parallel"),
        name="rope_tables",
    )(pos, inv)


def _proj_kernel(h_ref, g_ref, w_ref, wa_ref, ba_ref, cos_ref, sin_ref, o_ref, oa_ref, *, tiles, aux_act):
    xn = _rms(h_ref[...], g_ref[...]).astype(BF16)
    for c0, width, rope in tiles:
        acc = _dot(xn, w_ref[:, c0:c0 + width])
        if rope:
            reps = width // LANES
            cos = jnp.tile(cos_ref[...], (1, reps))
            sin = jnp.tile(sin_ref[...], (1, reps))
            lane = lax.broadcasted_iota(jnp.int32, acc.shape, 1)
            first_half = (lane & (HEAD_DIM - 1)) < HEAD_DIM // 2
            partner = jnp.where(first_half,
                                pltpu.roll(acc, width - HEAD_DIM // 2, 1),
                                pltpu.roll(acc, HEAD_DIM // 2, 1))
            acc = acc * cos + partner * sin
        o_ref[:, c0:c0 + width] = acc
    aux = _dot(xn, wa_ref[...]) + ba_ref[...]
    if aux_act == "log_sigmoid":
        oa_ref[...] = jnp.minimum(aux, 0.0) - jnp.log1p(jnp.exp(-jnp.abs(aux)))
    else:
        oa_ref[...] = jax.nn.sigmoid(aux)


def _proj(h, gain, w, w_aux, b_aux, cos, sin, tiles, aux_act, tm=512):
    t, n = h.shape[0], w.shape[1]
    return pl.pallas_call(
        functools.partial(_proj_kernel, tiles=tiles, aux_act=aux_act),
        grid=(t // tm,),
        in_specs=[pl.BlockSpec((tm, D_MODEL), lambda i: (i, 0)),
                  pl.BlockSpec((1, D_MODEL), lambda i: (0, 0)),
                  pl.BlockSpec((D_MODEL, n), lambda i: (0, 0)),
                  pl.BlockSpec((D_MODEL, LANES), lambda i: (0, 0)),
                  pl.BlockSpec((1, LANES), lambda i: (0, 0)),
                  pl.BlockSpec((tm, LANES), lambda i: (i, 0)),
                  pl.BlockSpec((tm, LANES), lambda i: (i, 0))],
        out_specs=[pl.BlockSpec((tm, n), lambda i: (i, 0)),
                   pl.BlockSpec((tm, LANES), lambda i: (i, 0))],
        out_shape=[jax.ShapeDtypeStruct((t, n), F32), jax.ShapeDtypeStruct((t, LANES), F32)],
        compiler_params=_params("parallel"),
        name="in_proj",
    )(h, gain.reshape(1, D_MODEL), w, w_aux, b_aux, cos, sin)


CUM_BLOCK = 512


def _cumsum_kernel(x_ref, ccol_ref, crow_ref, carry_ref):
    j = pl.program_id(1)

    @pl.when(j == 0)
    def _():
        carry_ref[...] = jnp.zeros_like(carry_ref)

    x = x_ref[...]
    r = lax.broadcasted_iota(jnp.int32, (CUM_BLOCK, CUM_BLOCK), 0)
    c = lax.broadcasted_iota(jnp.int32, (CUM_BLOCK, CUM_BLOCK), 1)
    tri = jnp.where(r >= c, 1.0, 0.0).astype(BF16)
    hi, mid, lo = _split3(x)
    cum = _dot(tri, hi) + _dot(tri, mid) + _dot(tri, lo) + carry_ref[0:1, :]
    ccol_ref[...] = cum
    crow_ref[0] = cum.T
    carry_ref[...] = jnp.broadcast_to(cum[CUM_BLOCK - 1:CUM_BLOCK, :], carry_ref.shape)


def _cumsum(x, batch, seq):
    nb = seq // CUM_BLOCK
    return pl.pallas_call(
        _cumsum_kernel,
        grid=(batch, nb),
        in_specs=[pl.BlockSpec((CUM_BLOCK, LANES), lambda b, j: (b * nb + j, 0))],
        out_specs=[pl.BlockSpec((CUM_BLOCK, LANES), lambda b, j: (b * nb + j, 0)),
                   pl.BlockSpec((1, LANES, CUM_BLOCK), lambda b, j: (b, 0, j))],
        out_shape=[jax.ShapeDtypeStruct((batch * seq, LANES), F32),
                   jax.ShapeDtypeStruct((batch, LANES, seq), F32)],
        scratch_shapes=[pltpu.VMEM((8, LANES), F32)],
        compiler_params=_params("parallel", "arbitrary"),
        name="token_cumsum",
    )(x)


def _fox_kernel(q_ref, k_ref, v_ref, ccol_ref, crow_ref, o_ref, kb_ref, vt_ref, ck_ref, *, tq, seq):
    pair = pl.program_id(1)
    i = pl.program_id(2)
    tk = tq
    h0 = 2 * pair

    @pl.when(i == 0)
    def _():
        def fill(ci, carry):
            rs = pl.ds(pl.multiple_of(ci * tk, tk), tk)
            kb_ref[rs, :] = k_ref[rs, :].astype(BF16)
            vt_ref[0:LANES, rs] = v_ref[rs, :].T.astype(BF16)
            vt_ref[LANES:, rs] = jnp.ones((ONES_ROWS, tk), BF16)
            cc = ccol_ref[rs, :] * LOG2E
            ck_ref[0, rs, :] = jnp.broadcast_to(_lane_col(cc, h0), (tk, LANES))
            ck_ref[1, rs, :] = jnp.broadcast_to(_lane_col(cc, h0 + 1), (tk, LANES))
            return carry

        lax.fori_loop(0, seq // tk, fill, 0)

    t0 = pl.multiple_of(i * tq, tq)
    low = lax.broadcasted_iota(jnp.int32, (tq, LANES), 1) < HEAD_DIM
    q = q_ref[...] * SCALE2
    qs = jnp.concatenate([jnp.where(low, q, 0.0), jnp.where(low, 0.0, q)], axis=0).astype(BF16)
    cq = jnp.concatenate([crow_ref[0, pl.ds(h0, 1), pl.ds(t0, tq)],
                          crow_ref[0, pl.ds(h0 + 1, 1), pl.ds(t0, tq)]], axis=1) * LOG2E
    reps = tq // LANES

    def scores(k0, nk):
        ck = jnp.concatenate([ck_ref[0, k0:k0 + nk, :]] * reps + [ck_ref[1, k0:k0 + nk, :]] * reps, axis=1)
        return _dot_nt(kb_ref[k0:k0 + nk, :], qs) + cq - ck

    krow = lax.broadcasted_iota(jnp.int32, (tk, tq), 0)
    qlane = lax.broadcasted_iota(jnp.int32, (tk, tq), 1)
    causal = jnp.where(krow <= qlane, 0.0, NEG_INF)

    def branch(n):
        def tile_scores(t):
            s = scores(t * tk, tk)
            return s + jnp.concatenate([causal, causal], axis=1) if t == n - 1 else s

        parts = []
        pending = tile_scores(0)
        for t in range(n):
            upcoming = tile_scores(t + 1) if t + 1 < n else None
            m_t = jnp.max(pending, axis=0, keepdims=True)
            parts.append((m_t, _dot(vt_ref[:, t * tk:(t + 1) * tk], jnp.exp2(pending - m_t).astype(BF16))))
            pending = upcoming
        m = functools.reduce(jnp.maximum, [m_t for m_t, _ in parts])
        acc = sum(a * jnp.exp2(m_t - m) for m_t, a in parts)
        out = acc[0:LANES] / acc[LANES:LANES + 1]
        o_ref[...] = jnp.concatenate([out[0:HEAD_DIM, 0:tq], out[HEAD_DIM:, tq:]], axis=0).T.astype(o_ref.dtype)

    for n in range(1, seq // tq + 1):
        pl.when(i == n - 1)(functools.partial(branch, n))


def _fox_attention(proj, ccol, crow, batch, seq, tq=512):
    nq = seq // tq
    npair = FOX_WIDTH // LANES
    return pl.pallas_call(
        functools.partial(_fox_kernel, tq=tq, seq=seq),
        grid=(batch, npair, nq),
        in_specs=[pl.BlockSpec((tq, LANES), lambda b, p, i: (b * nq + i, p)),
                  pl.BlockSpec((seq, LANES), lambda b, p, i: (b, npair + p)),
                  pl.BlockSpec((seq, LANES), lambda b, p, i: (b, 2 * npair + p)),
                  pl.BlockSpec((seq, LANES), lambda b, p, i: (b, 0)),
                  pl.BlockSpec((1, 8, seq), lambda b, p, i: (b, 0, 0))],
        out_specs=pl.BlockSpec((tq, LANES), lambda b, p, i: (b * nq + i, p)),
        out_shape=jax.ShapeDtypeStruct((batch * seq, FOX_WIDTH), BF16),
        scratch_shapes=[pltpu.VMEM((seq, LANES), BF16), pltpu.VMEM((LANES + ONES_ROWS, seq), BF16),
                        pltpu.VMEM((2, seq, LANES), F32)],
        compiler_params=_params("parallel", "parallel", "arbitrary"),
        name="fox_attention",
    )(proj, proj, proj, ccol, crow)


DIL_GROUP = 4


def _dil_kernel(q_ref, k_ref, v_ref, o_ref, os_ref, ls_ref, *, seq):
    low = lax.broadcasted_iota(jnp.int32, (Q_BLOCK, LANES), 1) < HEAD_DIM
    kr = lax.broadcasted_iota(jnp.int32, (2 * Q_BLOCK, Q_BLOCK), 0)
    qc = lax.broadcasted_iota(jnp.int32, (2 * Q_BLOCK, Q_BLOCK), 1)
    dist = qc + Q_BLOCK - kr
    def score_group(pi, units):
        window, dil = DIL_PATTERNS[pi]
        span = window // dil
        nb = (seq // dil) // Q_BLOCK
        band_bias = jnp.where((dist >= 0) & (dist <= span), 0.0, NEG_INF)
        scored, dests = [], []

        def rows(start):
            return pl.ds(start, Q_BLOCK, stride=dil) if dil > 1 else pl.ds(start, Q_BLOCK)

        for u in units:
            r, blk = divmod(u, nb)
            cur = r + blk * (Q_BLOCK * dil)
            q = q_ref[rows(cur), :] * SCALE2
            qs = jnp.concatenate([jnp.where(low, q, 0.0), jnp.where(low, 0.0, q)], axis=0).astype(BF16)
            if blk == 0:
                kk = k_ref[rows(cur), :].astype(BF16)
                vt = v_ref[rows(cur), :].T.astype(BF16)
                bias = band_bias[Q_BLOCK:]
            else:
                prev = cur - Q_BLOCK * dil
                kk = jnp.concatenate([k_ref[rows(prev), :], k_ref[rows(cur), :]], axis=0).astype(BF16)
                vt = jnp.concatenate([v_ref[rows(prev), :].T, v_ref[rows(cur), :].T], axis=1).astype(BF16)
                bias = band_bias
            vt = jnp.concatenate([vt, jnp.ones((ONES_ROWS, vt.shape[1]), BF16)], axis=0)
            s = _dot_nt(kk, qs) + jnp.concatenate([bias, bias], axis=1)
            scored.append((s, vt))
            dests.append(rows(cur))
        return scored, dests

    def softmax(s):
        m = jnp.max(s, axis=0, keepdims=True)
        return jnp.exp2(s - m).astype(BF16), m

    def finish(vt, p, m):
        ot = _dot(vt, p)
        den = ot[LANES:LANES + 1]
        ot = ot[0:LANES] * (1.0 / den)
        lse = m * LN2 + jnp.log(den)
        lse_t = jnp.concatenate([jnp.broadcast_to(lse[:, 0:Q_BLOCK], (HEAD_DIM, Q_BLOCK)),
                                 jnp.broadcast_to(lse[:, Q_BLOCK:], (HEAD_DIM, Q_BLOCK))], axis=0)
        return jnp.concatenate([ot[0:HEAD_DIM, 0:Q_BLOCK], ot[HEAD_DIM:, Q_BLOCK:]], axis=0).T, lse_t.T

    def finish_group(pi, scored, dests):
        probs = [softmax(s) for s, _ in scored]
        outs = [finish(vt, p, m) for (_, vt), (p, m) in zip(scored, probs)]
        for dest, (o, l) in zip(dests, outs):
            os_ref[pi, dest, :] = o
            ls_ref[pi, dest, :] = l

    groups = []
    for pi, (window, dil) in enumerate(DIL_PATTERNS):
        n_units = dil * ((seq // dil) // Q_BLOCK)
        groups += [(pi, range(u0, min(u0 + DIL_GROUP, n_units))) for u0 in range(0, n_units, DIL_GROUP)]
    pending = score_group(*groups[0])
    for gi, (pi, _) in enumerate(groups):
        upcoming = score_group(*groups[gi + 1]) if gi + 1 < len(groups) else None
        finish_group(pi, *pending)
        pending = upcoming

    chunk = 256

    def combine(ci, carry):
        rs = pl.ds(pl.multiple_of(ci * chunk, chunk), chunk)
        l0, l1, l2 = ls_ref[0, rs, :], ls_ref[1, rs, :], ls_ref[2, rs, :]
        m = jnp.maximum(jnp.maximum(l0, l1), l2)
        e0, e1, e2 = jnp.exp(l0 - m), jnp.exp(l1 - m), jnp.exp(l2 - m)
        tot = e0 + e1 + e2
        o_ref[rs, :] = ((e0 / tot) * os_ref[0, rs, :] + (e1 / tot) * os_ref[1, rs, :]
                        + (e2 / tot) * os_ref[2, rs, :]).astype(o_ref.dtype)
        return carry

    lax.fori_loop(0, seq // chunk, combine, 0)


def _dilated_attention(proj, batch, seq):
    npair = DIL_WIDTH // LANES
    base = 3 * FOX_WIDTH // LANES
    return pl.pallas_call(
        functools.partial(_dil_kernel, seq=seq),
        grid=(batch, npair),
        in_specs=[pl.BlockSpec((seq, LANES), lambda b, p: (b, base + p)),
                  pl.BlockSpec((seq, LANES), lambda b, p: (b, base + npair + p)),
                  pl.BlockSpec((seq, LANES), lambda b, p: (b, base + 2 * npair + p))],
        out_specs=pl.BlockSpec((seq, LANES), lambda b, p: (b, p)),
        out_shape=jax.ShapeDtypeStruct((batch * seq, DIL_WIDTH), BF16),
        scratch_shapes=[pltpu.VMEM((3, seq, LANES), F32), pltpu.VMEM((3, seq, LANES), F32)],
        compiler_params=_params("parallel", "parallel"),
        name="dilated_attention",
    )(proj, proj, proj)


def _outproj_kernel(*refs, n_in, tn):
    h_ref = refs[0]
    a_refs = refs[1:1 + n_in]
    w_ref = refs[1 + n_in]
    o_ref = refs[2 + n_in]
    acts = [a[...].astype(BF16) for a in a_refs]
    for c0 in range(0, D_MODEL, tn):
        acc = h_ref[:, c0:c0 + tn]
        k0 = 0
        for a in acts:
            acc = acc + _dot(a, w_ref[k0:k0 + a.shape[1], c0:c0 + tn])
            k0 += a.shape[1]
        o_ref[:, c0:c0 + tn] = acc


def _outproj(h, acts, w, tm=1024, tn=512):
    t = h.shape[0]
    return pl.pallas_call(
        functools.partial(_outproj_kernel, n_in=len(acts), tn=tn),
        grid=(t // tm,),
        in_specs=([pl.BlockSpec((tm, D_MODEL), lambda i: (i, 0))]
                  + [pl.BlockSpec((tm, a.shape[1]), lambda i: (i, 0)) for a in acts]
                  + [pl.BlockSpec((D_MODEL, D_MODEL), lambda i: (0, 0))]),
        out_specs=pl.BlockSpec((tm, D_MODEL), lambda i: (i, 0)),
        out_shape=jax.ShapeDtypeStruct((t, D_MODEL), F32),
        compiler_params=_params("parallel"),
        name="out_proj",
    )(h, *acts, w)


def _swiglu_tile(x, wg, wu, wd):
    gate = _dot(x, wg)
    up = _dot(x, wu)
    return _dot((gate * jax.nn.sigmoid(gate) * up).astype(BF16), wd)


def _ffn_kernel(h_ref, g_ref, wg_ref, wu_ref, wd_ref, pg_ref, p_ref, pwg_ref, pwp_ref, o_ref, xn_ref, acc_ref, *, n_f):
    f = pl.program_id(1)

    @pl.when(f == 0)
    def _():
        xn_ref[...] = _rms(h_ref[...], g_ref[...]).astype(BF16)
        acc_ref[...] = jnp.zeros_like(acc_ref)

    acc_ref[...] += _swiglu_tile(xn_ref[...], wg_ref[...], wu_ref[...], wd_ref[...])

    @pl.when(f == n_f - 1)
    def _():
        h = h_ref[...] + acc_ref[...]
        gate = jax.nn.sigmoid(_dot(_rms(h, pg_ref[...]).astype(BF16), pwg_ref[...]))
        o_ref[...] = h + gate * _dot(p_ref[...].astype(BF16), pwp_ref[...])


def _ffn_ple(h, gain, wg, wu, wd, ple_gain, p, ple_wg, ple_wp, tm=512, tf=1792):
    t, dff = h.shape[0], wg.shape[1]
    n_f = dff // tf
    pd = p.shape[1]
    return pl.pallas_call(
        functools.partial(_ffn_kernel, n_f=n_f),
        grid=(t // tm, n_f),
        in_specs=[pl.BlockSpec((tm, D_MODEL), lambda i, f: (i, 0)),
                  pl.BlockSpec((1, D_MODEL), lambda i, f: (0, 0)),
                  pl.BlockSpec((D_MODEL, tf), lambda i, f: (0, f)),
                  pl.BlockSpec((D_MODEL, tf), lambda i, f: (0, f)),
                  pl.BlockSpec((tf, D_MODEL), lambda i, f: (f, 0)),
                  pl.BlockSpec((1, D_MODEL), lambda i, f: (0, 0)),
                  pl.BlockSpec((tm, pd), lambda i, f: (i, 0)),
                  pl.BlockSpec((D_MODEL, D_MODEL), lambda i, f: (0, 0)),
                  pl.BlockSpec((pd, D_MODEL), lambda i, f: (0, 0))],
        out_specs=pl.BlockSpec((tm, D_MODEL), lambda i, f: (i, 0)),
        out_shape=jax.ShapeDtypeStruct((t, D_MODEL), F32),
        scratch_shapes=[pltpu.VMEM((tm, D_MODEL), BF16), pltpu.VMEM((tm, D_MODEL), F32)],
        compiler_params=_params("parallel", "arbitrary"),
        name="dense_swiglu_ple",
    )(h, gain.reshape(1, D_MODEL), wg, wu, wd, ple_gain.reshape(1, D_MODEL), p, ple_wg, ple_wp)


MOE_CHUNK = 1024
MOE_ROWS = 128
MOE_SCATTER = 256


def _moe_kernel(cnt_ref, h_ref, g_ref, cw_ref, m_ref, rcol_ref, rrow_ref, mrow_ref, wg_ref, wu_ref, wd_ref,
                o_ref, xn_ref, xe_ref, ye_ref, *, n_f):
    c = pl.program_id(0)
    e = pl.program_id(1)
    f = pl.program_id(2)
    chunk = MOE_CHUNK
    n = cnt_ref[c * N_EXPERTS + e]
    n_scatter = (n + MOE_SCATTER - 1) // MOE_SCATTER
    n_tiles = (n + MOE_ROWS - 1) // MOE_ROWS

    @pl.when((e == 0) & (f == 0))
    def _():
        h = h_ref[...]
        xn_ref[...] = _rms(h, g_ref[...]).astype(BF16)
        o_ref[...] = h

    @pl.when(f == 0)
    def _():
        rank = rrow_ref[pl.ds(e, 1), :] * mrow_ref[pl.ds(e, 1), :]
        slot = lax.broadcasted_iota(jnp.int32, (MOE_ROWS, chunk), 0) + 1

        def gather(i, carry):
            rows = pl.ds(pl.multiple_of(i * MOE_ROWS, MOE_ROWS), MOE_ROWS)
            onehot = jnp.where(rank == (slot + i * MOE_ROWS).astype(F32), 1.0, 0.0).astype(BF16)
            xe_ref[rows, :] = _dot(onehot, xn_ref[...]).astype(BF16)
            ye_ref[rows, :] = jnp.zeros((MOE_ROWS, D_MODEL), F32)
            return carry

        def clear(i, carry):
            rows = pl.ds(pl.multiple_of(i * MOE_ROWS, MOE_ROWS), MOE_ROWS)
            ye_ref[rows, :] = jnp.zeros((MOE_ROWS, D_MODEL), F32)
            return carry

        lax.fori_loop(0, n_tiles, gather, 0)
        lax.fori_loop(n_tiles, n_scatter * (MOE_SCATTER // MOE_ROWS), clear, 0)

    def tile(i, carry):
        rows = pl.ds(pl.multiple_of(i * MOE_ROWS, MOE_ROWS), MOE_ROWS)
        ye_ref[rows, :] += _swiglu_tile(xe_ref[rows, :], wg_ref[...], wu_ref[...], wd_ref[...])
        return carry

    lax.fori_loop(0, n_tiles, tile, 0)

    @pl.when(f == n_f - 1)
    def _():
        rank = _lane_col(rcol_ref[...] * m_ref[...], e)
        weight = _lane_col(cw_ref[...], e)
        slot = lax.broadcasted_iota(jnp.int32, (chunk, MOE_SCATTER), 1) + 1

        def scatter(i, carry):
            rows = pl.ds(pl.multiple_of(i * MOE_SCATTER, MOE_SCATTER), MOE_SCATTER)
            onehot = jnp.where(rank == (slot + i * MOE_SCATTER).astype(F32), 1.0, 0.0).astype(BF16)
            o_ref[...] += weight * _dot(onehot, ye_ref[rows, :].astype(BF16))
            return carry

        lax.fori_loop(0, n_scatter, scatter, 0)


def _moe_ffn(h, gain, cw, mask, rcol, rrow, mrow, counts, wg, wu, wd, tf=1792):
    t, dff = h.shape[0], wg.shape[2]
    n_f = dff // tf
    chunk = MOE_CHUNK
    tok = lambda width: pl.BlockSpec((chunk, width), lambda c, e, f, cnt: (c, 0))
    lane_major = pl.BlockSpec((None, N_EXPERTS, chunk), lambda c, e, f, cnt: (c, 0, 0))
    grid_spec = pltpu.PrefetchScalarGridSpec(
        num_scalar_prefetch=1,
        grid=(t // chunk, N_EXPERTS, n_f),
        in_specs=[tok(D_MODEL),
                  pl.BlockSpec((1, D_MODEL), lambda c, e, f, cnt: (0, 0)),
                  tok(LANES), tok(LANES), tok(LANES), lane_major, lane_major,
                  pl.BlockSpec((None, D_MODEL, tf), lambda c, e, f, cnt: (e, 0, f)),
                  pl.BlockSpec((None, D_MODEL, tf), lambda c, e, f, cnt: (e, 0, f)),
                  pl.BlockSpec((None, tf, D_MODEL), lambda c, e, f, cnt: (e, f, 0))],
        out_specs=tok(D_MODEL),
        scratch_shapes=[pltpu.VMEM((chunk, D_MODEL), BF16), pltpu.VMEM((chunk, D_MODEL), BF16),
                        pltpu.VMEM((chunk, D_MODEL), F32)],
    )
    return pl.pallas_call(
        functools.partial(_moe_kernel, n_f=n_f),
        grid_spec=grid_spec,
        out_shape=jax.ShapeDtypeStruct((t, D_MODEL), F32),
        compiler_params=_params("parallel", "arbitrary", "arbitrary"),
        name="moe_swiglu",
    )(counts, h, gain.reshape(1, D_MODEL), cw, mask, rcol, rrow, mrow, wg, wu, wd)


def _router_kernel(h_ref, g_ref, w_ref, b_ref, cw_ref, m_ref, rcol_ref, rrow_ref, mrow_ref):
    xn = _rms(h_ref[...], g_ref[...]).astype(BF16)
    logits = _dot(xn, w_ref[...]) + b_ref[...]
    lane = lax.broadcasted_iota(jnp.int32, logits.shape, 1).astype(F32)
    logits = jnp.where(lane < N_EXPERTS, logits, -jnp.inf)
    m1 = jnp.max(logits, axis=1, keepdims=True)
    i1 = jnp.min(jnp.where(logits == m1, lane, float(LANES)), axis=1, keepdims=True)
    rest = jnp.where(lane == i1, -jnp.inf, logits)
    m2 = jnp.max(rest, axis=1, keepdims=True)
    i2 = jnp.min(jnp.where(rest == m2, lane, float(LANES)), axis=1, keepdims=True)
    e2 = jnp.exp(m2 - m1)
    w1 = 1.0 / (1.0 + e2)
    w2 = e2 / (1.0 + e2)
    cw_ref[...] = jnp.where(lane == i1, w1, jnp.where(lane == i2, w2, 0.0))
    mask = jnp.where((lane == i1) | (lane == i2), 1.0, 0.0)
    m_ref[...] = mask
    r = lax.broadcasted_iota(jnp.int32, (CUM_BLOCK, CUM_BLOCK), 0)
    c = lax.broadcasted_iota(jnp.int32, (CUM_BLOCK, CUM_BLOCK), 1)
    tri = jnp.where(r >= c, 1.0, 0.0).astype(BF16)
    carry = jnp.zeros((1, LANES), F32)
    for r0 in range(0, mask.shape[0], CUM_BLOCK):
        blk = mask[r0:r0 + CUM_BLOCK]
        cum = _dot(tri, blk.astype(BF16)) + carry
        rcol_ref[r0:r0 + CUM_BLOCK, :] = cum
        rrow_ref[:, r0:r0 + CUM_BLOCK] = cum.T
        mrow_ref[:, r0:r0 + CUM_BLOCK] = blk.T
        carry = cum[CUM_BLOCK - 1:CUM_BLOCK, :]


def _router(h, gain, w, b):
    t = h.shape[0]
    tm = MOE_CHUNK
    token_major = pl.BlockSpec((tm, LANES), lambda i: (i, 0))
    lane_major = pl.BlockSpec((None, LANES, tm), lambda i: (i, 0, 0))
    return pl.pallas_call(
        _router_kernel,
        grid=(t // tm,),
        in_specs=[pl.BlockSpec((tm, D_MODEL), lambda i: (i, 0)),
                  pl.BlockSpec((1, D_MODEL), lambda i: (0, 0)),
                  pl.BlockSpec((D_MODEL, LANES), lambda i: (0, 0)),
                  pl.BlockSpec((1, LANES), lambda i: (0, 0))],
        out_specs=[token_major, token_major, token_major, lane_major, lane_major],
        out_shape=[jax.ShapeDtypeStruct((t, LANES), F32)] * 3
                  + [jax.ShapeDtypeStruct((t // tm, LANES, tm), F32)] * 2,
        compiler_params=_params("parallel"),
        name="moe_router",
    )(h, gain.reshape(1, D_MODEL), w, b)


def _ple_kernel(h_ref, g_ref, p_ref, wg_ref, wp_ref, fg_ref, o_ref, *, final, tn):
    h = h_ref[...]
    xn = _rms(h, g_ref[...]).astype(BF16)
    pe = p_ref[...].astype(BF16)
    outs = []
    for c0 in range(0, D_MODEL, tn):
        gate = jax.nn.sigmoid(_dot(xn, wg_ref[:, c0:c0 + tn]))
        outs.append(h[:, c0:c0 + tn] + gate * _dot(pe, wp_ref[:, c0:c0 + tn]))
    new = jnp.concatenate(outs, axis=1)
    o_ref[...] = _rms(new, fg_ref[...]) if final else new


def _ple(h, gain, p, wg, wp, final_gain, final, tm=1024, tn=512):
    t, pd = p.shape
    return pl.pallas_call(
        functools.partial(_ple_kernel, final=final, tn=tn),
        grid=(t // tm,),
        in_specs=[pl.BlockSpec((tm, D_MODEL), lambda i: (i, 0)),
                  pl.BlockSpec((1, D_MODEL), lambda i: (0, 0)),
                  pl.BlockSpec((tm, pd), lambda i: (i, 0)),
                  pl.BlockSpec((D_MODEL, D_MODEL), lambda i: (0, 0)),
                  pl.BlockSpec((pd, D_MODEL), lambda i: (0, 0)),
                  pl.BlockSpec((1, D_MODEL), lambda i: (0, 0))],
        out_specs=pl.BlockSpec((tm, D_MODEL), lambda i: (i, 0)),
        out_shape=jax.ShapeDtypeStruct((t, D_MODEL), F32),
        compiler_params=_params("parallel"),
        name="ple",
    )(h, gain.reshape(1, D_MODEL), p, wg, wp, final_gain.reshape(1, D_MODEL))


def _compress_kernel(x01_ref, x23_ref, pos_ref, w1_ref, b1_ref, w2_ref, o_ref, *, blocks):
    stride = NSA_CMP_STRIDE
    hidden = b1_ref.shape[1]
    low = lax.broadcasted_iota(jnp.int32, (blocks, LANES), 1) < HEAD_DIM
    first = [jnp.zeros((blocks, hidden), F32) for _ in range(NSA_GROUPS)]
    second = [jnp.zeros((blocks, hidden), F32) for _ in range(NSA_GROUPS)]
    for j in range(stride):
        for half, x_ref in enumerate((x01_ref, x23_ref)):
            xs = x_ref[pl.ds(j, blocks, stride=stride), :]
            xa = xs + pos_ref[j:j + 1, :]
            xb = xs + pos_ref[stride + j:stride + j + 1, :]
            for sub in range(2):
                g = 2 * half + sub
                keep = low if sub == 0 else jnp.logical_not(low)
                first[g] = first[g] + _dot(jnp.where(keep, xa, 0.0).astype(BF16), w1_ref[j])
                second[g] = second[g] + _dot(jnp.where(keep, xb, 0.0).astype(BF16), w1_ref[stride + j])
    row = lax.broadcasted_iota(jnp.int32, (blocks, NSA_WIDTH), 0)
    for g in range(NSA_GROUPS):
        hid = first[g] + pltpu.roll(second[g], blocks - 1, 0) + b1_ref[...]
        out = _dot(jax.nn.gelu(hid, approximate=True).astype(BF16), w2_ref[...])
        o_ref[g] = jnp.where(row == blocks - 1, 0.0, out)


def _compress(proj, col_block, pos, w1, b1, w2, batch, seq):
    blocks = seq // NSA_CMP_STRIDE
    hidden = w1.shape[1]
    w1_rep = jnp.tile(w1.reshape(NSA_CMP_LEN, HEAD_DIM, hidden), (1, 2, 1))
    pos_rep = jnp.tile(pos, (1, 2))
    halves = NSA_WIDTH // LANES
    return pl.pallas_call(
        functools.partial(_compress_kernel, blocks=blocks),
        grid=(batch,),
        in_specs=[pl.BlockSpec((seq, LANES), lambda b: (b, halves * col_block)),
                  pl.BlockSpec((seq, LANES), lambda b: (b, halves * col_block + 1)),
                  pl.BlockSpec((NSA_CMP_LEN, LANES), lambda b: (0, 0)),
                  pl.BlockSpec((NSA_CMP_LEN, LANES, hidden), lambda b: (0, 0, 0)),
                  pl.BlockSpec((1, hidden), lambda b: (0, 0)),
                  pl.BlockSpec((hidden, NSA_WIDTH), lambda b: (0, 0))],
        out_specs=pl.BlockSpec((NSA_GROUPS, blocks, NSA_WIDTH), lambda b: (b, 0, 0)),
        out_shape=jax.ShapeDtypeStruct((batch * NSA_GROUPS, blocks, NSA_WIDTH), F32),
        compiler_params=_params("parallel"),
        name="nsa_compress",
    )(proj, proj, pos_rep, w1_rep.astype(BF16), b1.reshape(1, hidden), w2)


NSA_TQ = 256
NSA_TK = 256
NSA_SUB = 256
NSA_WIDTH = NSA_HPG * HEAD_DIM
LOG_HEAD_DIM = HEAD_DIM.bit_length() - 1
LOG_SEL_LEN = NSA_SEL_LEN.bit_length() - 1


def _nsa_kernel(q_ref, kc_ref, vc_ref, ks_ref, vs_ref, kw_ref, vw_ref, gate_ref, o_ref,
                ksb, kwb, vst_all, vwt_all, vct, gt_ref, osel_ref, *, seq, n_cmp):
    g = pl.program_id(1)
    i = pl.program_id(2)
    tq, tk = NSA_TQ, NSA_TK
    cols = NSA_HPG * tq
    vrows = HEAD_DIM + ONES_ROWS
    grow = pl.multiple_of(g * vrows, vrows)

    @pl.when((i == 0) & (g == 0))
    def _():
        def fill(ci, carry):
            rs = pl.ds(pl.multiple_of(ci * NSA_WIDTH, NSA_WIDTH), NSA_WIDTH)
            ksb[rs, :] = ks_ref[rs, :].astype(BF16)
            kwb[rs, :] = kw_ref[rs, :].astype(BF16)
            for src, dst in ((vs_ref, vst_all), (vw_ref, vwt_all)):
                vt = src[rs, :].T.astype(BF16)
                for grp in range(NSA_GROUPS):
                    dst[grp * vrows:grp * vrows + HEAD_DIM, rs] = vt[grp * HEAD_DIM:(grp + 1) * HEAD_DIM]
                    dst[grp * vrows + HEAD_DIM:(grp + 1) * vrows, rs] = jnp.ones((ONES_ROWS, NSA_WIDTH), BF16)
            return carry

        lax.fori_loop(0, seq // NSA_WIDTH, fill, 0)

    @pl.when(i == 0)
    def _():
        vct[...] = vc_ref[0].T[0:HEAD_DIM, :].astype(BF16)

    vst = vst_all.at[pl.ds(grow, vrows)]
    vwt = vwt_all.at[pl.ds(grow, vrows)]

    t0 = i * tq
    lane_grp = lax.broadcasted_iota(jnp.int32, (tq, NSA_WIDTH), 1) >> LOG_HEAD_DIM
    q = q_ref[...] * SCALE2
    rolled = [q] + [pltpu.roll(q, s * HEAD_DIM, 1) for s in range(1, NSA_HPG)]
    parts = []
    for j in range(NSA_HPG):
        shift = (g - j) & (NSA_HPG - 1)
        moved = jnp.where(shift == 0, rolled[0],
                          jnp.where(shift == 1, rolled[1], jnp.where(shift == 2, rolled[2], rolled[3])))
        parts.append(jnp.where(lane_grp == g, moved, 0.0))
    qs = jnp.concatenate(parts, axis=0).astype(BF16)

    def heads_sum(x):
        out = x[:, 0:tq]
        for j in range(1, NSA_HPG):
            out = out + x[:, j * tq:(j + 1) * tq]
        return out

    def lanes4(x):
        return jnp.concatenate([x] * NSA_HPG, axis=1)

    wk = NSA_WINDOW + tq
    ws = pl.multiple_of(jnp.maximum(t0 - NSA_WINDOW, 0), tq)
    kpos = ws + lax.broadcasted_iota(jnp.int32, (wk, tq), 0)
    qw = t0 + lax.broadcasted_iota(jnp.int32, (wk, tq), 1)
    bias_w = jnp.where((kpos <= qw) & (kpos > qw - NSA_WINDOW), 0.0, NEG_INF)
    sw = _dot_nt(kwb[pl.ds(ws, wk), :], qs) + lanes4(bias_w)

    nrow = lax.broadcasted_iota(jnp.int32, (LANES, cols), 0)
    tcol = t0 + (lax.broadcasted_iota(jnp.int32, (LANES, cols), 1) & (tq - 1))
    valid_c = (nrow * NSA_CMP_STRIDE + NSA_CMP_LEN - 1 <= tcol) & (nrow < n_cmp)
    sc = jnp.where(valid_c, _dot_nt(kc_ref[0].astype(BF16), qs), NEG_INF)
    mc = jnp.max(sc, axis=0, keepdims=True)
    ec = jnp.where(valid_c, jnp.exp2(sc - mc), 0.0)
    dc = jnp.sum(ec, axis=0, keepdims=True)
    pc = ec / jnp.where(dc > 0.0, dc, 1.0)
    o_cmp = _dot(vct[...], pc.astype(BF16))
    pc_sum = heads_sum(pc)

    n_sel_blocks = seq // NSA_SEL_LEN
    jrow = lax.broadcasted_iota(jnp.int32, (LANES, LANES), 0)
    ncol = lax.broadcasted_iota(jnp.int32, (LANES, LANES), 1)
    overlap = ((ncol * NSA_CMP_STRIDE < (jrow + 1) * NSA_SEL_LEN)
               & (ncol * NSA_CMP_STRIDE + NSA_CMP_LEN > jrow * NSA_SEL_LEN)
               & (ncol < n_cmp) & (jrow < n_sel_blocks))
    overlap = jnp.where(overlap, 1.0, 0.0).astype(BF16)
    hi, mid, lo = _split3(pc_sum)
    imp = (_dot(overlap, hi) + _dot(overlap, mid) + _dot(overlap, lo))[0:n_sel_blocks]
    pw = jnp.exp2(sw - jnp.max(sw, axis=0, keepdims=True))
    o_win = _dot(vwt[:, pl.ds(ws, wk)], pw.astype(BF16))
    o_win = o_win[0:HEAD_DIM] / o_win[HEAD_DIM:HEAD_DIM + 1]
    blk = lax.broadcasted_iota(jnp.int32, (n_sel_blocks, tq), 0)
    cur = (t0 + lax.broadcasted_iota(jnp.int32, (n_sel_blocks, tq), 1)) >> LOG_SEL_LEN
    forced = (blk == 0) | (blk == cur) | (blk == cur - 1)
    imp = jnp.where(blk > cur, -1.0, jnp.where(forced, 1e6, imp))
    beaten = jnp.zeros((n_sel_blocks, tq), jnp.int32)
    for c in range(n_sel_blocks):
        row = imp[c:c + 1, :]
        wins = (row > imp) | ((row == imp) & (blk > c))
        beaten = beaten + jnp.where(wins, 1, 0)
    sel_bias = jnp.where(beaten < NSA_TOP_N, 0.0, NEG_INF)
    sel_bias = jnp.concatenate([sel_bias, jnp.zeros((LANES - n_sel_blocks, tq), F32)], axis=0).astype(BF16)

    n_tiles = (t0 + tq - 1) // tk + 1

    def sel_branch(n):
        sub = NSA_SUB
        erow = lax.broadcasted_iota(jnp.int32, (sub, LANES), 0)
        ecol = lax.broadcasted_iota(jnp.int32, (sub, LANES), 1)
        krow = lax.broadcasted_iota(jnp.int32, (sub, tq), 0)
        qlane = t0 + lax.broadcasted_iota(jnp.int32, (sub, tq), 1)

        def scores(k0):
            expand = jnp.where(((k0 + erow) >> LOG_SEL_LEN) == ecol, 1.0, 0.0).astype(BF16)
            bias = jnp.where(k0 + krow <= qlane, _dot(expand, sel_bias), NEG_INF)
            return _dot_nt(ksb[k0:k0 + sub, :], qs) + lanes4(bias)

        starts = list(range(0, n * tk, sub))
        parts = []
        pending = scores(starts[0])
        for i, k0 in enumerate(starts):
            upcoming = scores(starts[i + 1]) if i + 1 < len(starts) else None
            m_sub = jnp.max(pending, axis=0, keepdims=True)
            parts.append((m_sub, _dot(vst[:, k0:k0 + sub], jnp.exp2(pending - m_sub).astype(BF16))))
            pending = upcoming
        m = functools.reduce(jnp.maximum, [m_sub for m_sub, _ in parts])
        acc = sum(a * jnp.exp2(m_sub - m) for m_sub, a in parts)
        osel_ref[...] = acc[0:HEAD_DIM] / acc[HEAD_DIM:HEAD_DIM + 1]

    for n in range(1, seq // tk + 1):
        pl.when(n_tiles == n)(functools.partial(sel_branch, n))
    o_sel = osel_ref[...]

    gt_ref[...] = gate_ref[...].T
    out = jnp.zeros((HEAD_DIM, cols), F32)
    for c, branch in enumerate((o_cmp, o_sel, o_win)):
        gate = jnp.concatenate([gt_ref[pl.ds(c * NSA_HEADS + g * NSA_HPG + j, 1), :] for j in range(NSA_HPG)],
                               axis=1)
        out = out + gate * branch
    o_ref[...] = jnp.concatenate([out[:, j * tq:(j + 1) * tq] for j in range(NSA_HPG)], axis=0).T.astype(o_ref.dtype)


def _nsa_attention(proj, gates, kc_cmp, vc_cmp, batch, seq, n_cmp):
    nq = seq // NSA_TQ
    ncb = kc_cmp.shape[1]
    qcols = NSA_HEADS * HEAD_DIM // NSA_WIDTH
    kv = lambda c: pl.BlockSpec((seq, NSA_WIDTH), lambda b, g, i, c=c: (b, qcols + c))
    cmp_spec = pl.BlockSpec((1, ncb, NSA_WIDTH), lambda b, g, i: (b * NSA_GROUPS + g, 0, 0))
    return pl.pallas_call(
        functools.partial(_nsa_kernel, seq=seq, n_cmp=n_cmp),
        grid=(batch, NSA_GROUPS, nq),
        in_specs=[pl.BlockSpec((NSA_TQ, NSA_WIDTH), lambda b, g, i: (b * nq + i, g)),
                  cmp_spec, cmp_spec, kv(2), kv(3), kv(4), kv(5),
                  pl.BlockSpec((NSA_TQ, LANES), lambda b, g, i: (b * nq + i, 0))],
        out_specs=pl.BlockSpec((NSA_TQ, NSA_WIDTH), lambda b, g, i: (b * nq + i, g)),
        out_shape=jax.ShapeDtypeStruct((batch * seq, NSA_HEADS * HEAD_DIM), BF16),
        scratch_shapes=[pltpu.VMEM((seq, NSA_WIDTH), BF16), pltpu.VMEM((seq, NSA_WIDTH), BF16),
                        pltpu.VMEM((NSA_GROUPS * (HEAD_DIM + ONES_ROWS), seq), BF16),
                        pltpu.VMEM((NSA_GROUPS * (HEAD_DIM + ONES_ROWS), seq), BF16),
                        pltpu.VMEM((HEAD_DIM, ncb), BF16), pltpu.VMEM((LANES, NSA_TQ), F32),
                        pltpu.VMEM((HEAD_DIM, NSA_HPG * NSA_TQ), F32)],
        compiler_params=_params("parallel", "arbitrary", "arbitrary"),
        name="nsa_attention",
    )(proj, kc_cmp, vc_cmp, proj, proj, proj, proj, gates)


def _pad_cols(w, width=LANES):
    return jnp.pad(w, ((0, 0), (0, width - w.shape[1])))


def kernel(x, p, positions, mix_norm, ffn_norm, ple_norm, ple_gate_w, ple_proj_w, fd_w_in, fd_forget_b, fd_w_out, dense_w_gate, dense_w_up, dense_w_down, nsa_w_in, nsa_pos_k, nsa_w1_k, nsa_b1_k, nsa_w2_k, nsa_pos_v, nsa_w1_v, nsa_b1_v, nsa_w2_v, nsa_w_out, moe_w_router, moe_b_router, moe_w_gate, moe_w_up, moe_w_down, final_norm):
    batch, seq, _ = x.shape
    t = batch * seq
    h = x.reshape(t, D_MODEL)
    cos, sin = _rope_tables(positions)

    n_main = 3 * FOX_WIDTH + 3 * DIL_WIDTH
    w_in = fd_w_in[0]
    tiles0 = tuple((c, 512, c in (3 * FOX_WIDTH, 3 * FOX_WIDTH + DIL_WIDTH)) for c in range(0, n_main, 512))
    proj0, log_f = _proj(h, mix_norm[0], w_in[:, :n_main].astype(BF16),
                         _pad_cols(w_in[:, n_main:]).astype(BF16),
                         _pad_cols(fd_forget_b[0].reshape(1, FOX_HEADS)).astype(F32),
                         cos, sin, tiles0, "log_sigmoid")
    ccol, crow = _cumsum(log_f, batch, seq)
    o_fox = _fox_attention(proj0, ccol, crow, batch, seq)
    o_dil = _dilated_attention(proj0, batch, seq)
    h = _outproj(h, [o_fox, o_dil], fd_w_out[0].astype(BF16))
    h = _ffn_ple(h, ffn_norm[0], dense_w_gate[0].astype(BF16), dense_w_up[0].astype(BF16),
                 dense_w_down[0].astype(BF16), ple_norm[0], p[0].reshape(t, -1),
                 ple_gate_w[0].astype(BF16), ple_proj_w[0].astype(BF16))

    qw = NSA_HEADS * HEAD_DIM
    n_main1 = qw + 6 * NSA_KV
    w_in1 = nsa_w_in[0]
    rope_cols = set(range(0, qw, 256)) | {qw, qw + 2 * NSA_KV, qw + 4 * NSA_KV}
    tiles1 = tuple((c, 256, c in rope_cols) for c in range(0, n_main1, 256))
    w_gate = w_in1[:, n_main1:].reshape(D_MODEL, NSA_HEADS, 3).transpose(0, 2, 1).reshape(D_MODEL, 3 * NSA_HEADS)
    proj1, gates = _proj(h, mix_norm[1], w_in1[:, :n_main1].astype(BF16), _pad_cols(w_gate).astype(BF16),
                         jnp.zeros((1, LANES), F32), cos, sin, tiles1, "sigmoid")
    n_cmp = (seq - NSA_CMP_LEN) // NSA_CMP_STRIDE + 1
    kc_cmp = _compress(proj1, qw // NSA_WIDTH, nsa_pos_k[0], nsa_w1_k[0], nsa_b1_k[0],
                       jnp.tile(nsa_w2_k[0], (1, NSA_HPG)).astype(BF16), batch, seq)
    vc_cmp = _compress(proj1, qw // NSA_WIDTH + 1, nsa_pos_v[0], nsa_w1_v[0], nsa_b1_v[0],
                       jnp.tile(nsa_w2_v[0], (1, NSA_HPG)).astype(BF16), batch, seq)
    o_nsa = _nsa_attention(proj1, gates, kc_cmp, vc_cmp, batch, seq, n_cmp)
    h = _outproj(h, [o_nsa], nsa_w_out[0].astype(BF16))
    cw, routed, rank_col, rank_row, routed_row = _router(
        h, ffn_norm[1], _pad_cols(moe_w_router[0]).astype(BF16),
        _pad_cols(moe_b_router[0].reshape(1, N_EXPERTS)).astype(F32))
    n_chunks = t // MOE_CHUNK
    counts = rank_col.reshape(n_chunks, MOE_CHUNK, LANES)[:, -1, :N_EXPERTS].astype(jnp.int32).reshape(-1)
    h = _moe_ffn(h, ffn_norm[1], cw, routed, rank_col, rank_row, routed_row, counts,
                 moe_w_gate[0].astype(BF16), moe_w_up[0].astype(BF16), moe_w_down[0].astype(BF16))
    h = _ple(h, ple_norm[1], p[1].reshape(t, -1), ple_gate_w[1].astype(BF16), ple_proj_w[1].astype(BF16),
             final_norm, final=True)
    return h.reshape(batch, seq, D_MODEL)
```
